```python
import math
import jax, jax.numpy as jnp
from jax import lax
import numpy as np

D_MODEL = 1024
BATCH = 2
SEQ = 8192
DEPTH = 1

N_HEADS_ATTN = 8
HEAD_DIM = 64
D_ATTN = N_HEADS_ATTN * HEAD_DIM
Q_BLOCK = 128
D_SSM = D_MODEL // 2
SSM_GROUP = 16
N_SSM_GROUPS = D_SSM // SSM_GROUP
SSM_STATE = 64
DT_MIN = 1e-3
DT_MAX = 1e-1
N_EXPERT_GROUPS = 4
EXPERTS_PER_GROUP = 8
N_EXPERTS = N_EXPERT_GROUPS * EXPERTS_PER_GROUP
TOP_K = 2
D_EXPERT = D_MODEL // 4
D_IN_PROJ = 3 * D_ATTN + N_HEADS_ATTN + D_SSM + 2 * D_MODEL
EPS = 1e-6
NEG_INF = -1e30

kernel_name = "hybrid_fox_s5_hier_moe_block"


def _rmsnorm(x, g):
    xf = x.astype(jnp.float32)
    y = xf * lax.rsqrt(jnp.mean(xf * xf, axis=-1, keepdims=True) + EPS)
    return (y * g.astype(jnp.float32)).astype(x.dtype)


def _forgetting_attention(q, k, v, f_logit):
    b, s, h, d = q.shape
    n_blk = s // Q_BLOCK
    log_f = jax.nn.log_sigmoid(f_logit.astype(jnp.float32))
    cum = jnp.cumsum(log_f, axis=1).transpose(0, 2, 1)
    qf = q.astype(jnp.float32).transpose(0, 2, 1, 3) * (d ** -0.5)
    kf = k.astype(jnp.float32).transpose(0, 2, 1, 3)
    vf = v.astype(jnp.float32).transpose(0, 2, 1, 3)
    q_blk = qf.reshape(b, h, n_blk, Q_BLOCK, d).transpose(2, 0, 1, 3, 4)
    c_blk = cum.reshape(b, h, n_blk, Q_BLOCK).transpose(2, 0, 1, 3)
    pos = jnp.arange(s, dtype=jnp.int32)
    p_blk = pos.reshape(n_blk, Q_BLOCK)

    def one_block(args):
        qb, cb, pb = args
        logits = jnp.einsum('bhqd,bhkd->bhqk', qb, kf) + cb[..., :, None] - cum[:, :, None, :]
        causal = pos[None, :] <= pb[:, None]
        logits = jnp.where(causal, logits, NEG_INF)
        p = jax.nn.softmax(logits, axis=-1)
        return jnp.einsum('bhqk,bhkd->bhqd', p, vf)

    o = lax.map(one_block, (q_blk, c_blk, p_blk))
    return o.transpose(1, 0, 3, 2, 4).reshape(b, s, h, d).astype(v.dtype)


def _linear_combine(a, b):
    a1, b1 = a
    a2, b2 = b
    return a2 * a1, a2 * b1 + b2


def _s5(u, A_re, A_im, log_dt, B_re, B_im, C_re, C_im, D_skip, w_glu):
    f32 = jnp.float32
    b, s, _ = u.shape
    uf = u.astype(f32).reshape(b, s, N_SSM_GROUPS, SSM_GROUP)
    A = lax.complex(A_re.astype(f32), A_im.astype(f32))
    dt = jnp.exp(log_dt.astype(f32))[:, None]
    A_bar = jnp.exp(A * dt)
    Bm = lax.complex(B_re.astype(f32), B_im.astype(f32))
    B_bar = ((A_bar - 1.0) / A)[..., None] * Bm
    Bu = jnp.einsum('bsgc,gpc->bsgp', uf.astype(jnp.complex64), B_bar)
    A_seq = jnp.broadcast_to(A_bar, Bu.shape)
    _, state = lax.associative_scan(_linear_combine, (A_seq, Bu), axis=1)
    Cm = lax.complex(C_re.astype(f32), C_im.astype(f32))
    y = jnp.real(jnp.einsum('gcp,bsgp->bsgc', Cm, state))
    y = y + D_skip.astype(f32).reshape(N_SSM_GROUPS, SSM_GROUP) * uf
    y = jax.nn.gelu(y.reshape(b, s, D_SSM))
    y = y * jax.nn.sigmoid(y @ w_glu.astype(f32))
    return y.astype(u.dtype)


def _hier_moe(x, w_rg, b_rg, w_re, b_re, w_gate, w_up, w_down):
    f32 = jnp.float32
    b, s, d = x.shape
    t = b * s
    xt = x.reshape(t, d)
    g_prob = jax.nn.softmax((xt @ w_rg).astype(f32) + b_rg.astype(f32), axis=-1)
    g_top, g_sel = lax.top_k(g_prob, 1)
    e_logits = ((xt @ w_re).astype(f32) + b_re.astype(f32)).reshape(t, N_EXPERT_GROUPS, EXPERTS_PER_GROUP)
    idx = jnp.broadcast_to(g_sel[:, :, None], (t, 1, EXPERTS_PER_GROUP))
    e_in = jnp.take_along_axis(e_logits, idx, axis=1)[:, 0]
    e_prob = jax.nn.softmax(e_in, axis=-1)
    e_top, e_idx = lax.top_k(e_prob, TOP_K)
    w = g_top * e_top / jnp.sum(e_top, axis=-1, keepdims=True)
    expert_id = g_sel * EXPERTS_PER_GROUP + e_idx
    combine = jnp.sum(jax.nn.one_hot(expert_id, N_EXPERTS, dtype=f32) * w[..., None], axis=1).astype(x.dtype)
    y = jnp.zeros_like(xt)
    for e in range(N_EXPERTS):
        h = jax.nn.silu(xt @ w_gate[e]) * (xt @ w_up[e])
        y = y + combine[:, e:e + 1] * (h @ w_down[e])
    return y.reshape(b, s, d)


def setup_inputs(seed: int = 0) -> dict:
    key = jax.random.key(seed)
    ks = jax.random.split(key, 32)
    f32 = jnp.float32
    nrm = lambda k, shape, scale: (jax.random.normal(k, shape, f32) * scale)
    L = DEPTH
    x = jax.random.normal(ks[0], (BATCH, SEQ, D_MODEL), f32)
    norm_mix_g = 1.0 + nrm(ks[1], (L, D_MODEL), 0.02)
    w_in = nrm(ks[2], (L, D_MODEL, D_IN_PROJ), D_MODEL ** -0.5)
    b_forget = jnp.broadcast_to(jnp.linspace(1.0, 6.0, N_HEADS_ATTN, dtype=f32), (L, N_HEADS_ATTN)) + nrm(ks[3], (L, N_HEADS_ATTN), 0.1)
    q_norm_g = 1.0 + nrm(ks[4], (L, HEAD_DIM), 0.02)
    k_norm_g = 1.0 + nrm(ks[5], (L, HEAD_DIM), 0.02)
    ssm_A_re = -0.5 + nrm(ks[6], (L, N_SSM_GROUPS, SSM_STATE), 0.01)
    ssm_A_im = jnp.pi * jnp.arange(SSM_STATE, dtype=f32)[None, None, :] + nrm(ks[7], (L, N_SSM_GROUPS, SSM_STATE), 0.01)
    ssm_log_dt = jax.random.uniform(ks[8], (L, N_SSM_GROUPS), f32, math.log(DT_MIN), math.log(DT_MAX))
    ssm_B_re = nrm(ks[9], (L, N_SSM_GROUPS, SSM_STATE, SSM_GROUP), (2 * SSM_GROUP) ** -0.5)
    ssm_B_im = nrm(ks[10], (L, N_SSM_GROUPS, SSM_STATE, SSM_GROUP), (2 * SSM_GROUP) ** -0.5)
    ssm_C_re = nrm(ks[11], (L, N_SSM_GROUPS, SSM_GROUP, SSM_STATE), (2 * SSM_STATE) ** -0.5)
    ssm_C_im = nrm(ks[12], (L, N_SSM_GROUPS, SSM_GROUP, SSM_STATE), (2 * SSM_STATE) ** -0.5)
    ssm_D = nrm(ks[13], (L, D_SSM), 1.0)
    w_glu = nrm(ks[14], (L, D_SSM, D_SSM), D_SSM ** -0.5)
    w_proj_attn = nrm(ks[15], (L, D_ATTN, D_MODEL), D_ATTN ** -0.5)
    w_proj_ssm = nrm(ks[16], (L, D_SSM, D_MODEL), D_SSM ** -0.5)
    w_out = nrm(ks[17], (L, D_MODEL, D_MODEL), D_MODEL ** -0.5)
    norm_ffn_g = 1.0 + nrm(ks[18], (L, D_MODEL), 0.02)
    w_router_group = nrm(ks[19], (L, D_MODEL, N_EXPERT_GROUPS), D_MODEL ** -0.5)
    b_router_group = nrm(ks[20], (L, N_EXPERT_GROUPS), 0.01)
    w_router_expert = nrm(ks[21], (L, D_MODEL, N_EXPERTS), D_MODEL ** -0.5)
    b_router_expert = nrm(ks[22], (L, N_EXPERTS), 0.01)
    w_expert_gate = nrm(ks[23], (L, N_EXPERTS, D_MODEL, D_EXPERT), D_MODEL ** -0.5)
    w_expert_up = nrm(ks[24], (L, N_EXPERTS, D_MODEL, D_EXPERT), D_MODEL ** -0.5)
    w_expert_down = nrm(ks[25], (L, N_EXPERTS, D_EXPERT, D_MODEL), D_EXPERT ** -0.5)
    return {"x": x, "norm_mix_g": norm_mix_g, "w_in": w_in, "b_forget": b_forget,
            "q_norm_g": q_norm_g, "k_norm_g": k_norm_g,
            "ssm_A_re": ssm_A_re, "ssm_A_im": ssm_A_im, "ssm_log_dt": ssm_log_dt,
            "ssm_B_re": ssm_B_re, "ssm_B_im": ssm_B_im, "ssm_C_re": ssm_C_re, "ssm_C_im": ssm_C_im,
            "ssm_D": ssm_D, "w_glu": w_glu, "w_proj_attn": w_proj_attn, "w_proj_ssm": w_proj_ssm,
            "w_out": w_out, "norm_ffn_g": norm_ffn_g,
            "w_router_group": w_router_group, "b_router_group": b_router_group,
            "w_router_expert": w_router_expert, "b_router_expert": b_router_expert,
            "w_expert_gate": w_expert_gate, "w_expert_up": w_expert_up, "w_expert_down": w_expert_down}


def reference(x, norm_mix_g, w_in, b_forget, q_norm_g, k_norm_g,
              ssm_A_re, ssm_A_im, ssm_log_dt, ssm_B_re, ssm_B_im, ssm_C_re, ssm_C_im,
              ssm_D, w_glu, w_proj_attn, w_proj_ssm, w_out, norm_ffn_g,
              w_router_group, b_router_group, w_router_expert, b_router_expert,
              w_expert_gate, w_expert_up, w_expert_down):
    b, s, _ = x.shape
    splits = np.cumsum([D_ATTN, D_ATTN, D_ATTN, N_HEADS_ATTN, D_SSM, D_MODEL]).tolist()
    for l in range(DEPTH):
        h = _rmsnorm(x, norm_mix_g[l])
        z = h @ w_in[l]
        q, k, v, f, u_ssm, g_attn, g_ssm = jnp.split(z, splits, axis=-1)
        q = _rmsnorm(q.reshape(b, s, N_HEADS_ATTN, HEAD_DIM), q_norm_g[l])
        k = _rmsnorm(k.reshape(b, s, N_HEADS_ATTN, HEAD_DIM), k_norm_g[l])
        v = v.reshape(b, s, N_HEADS_ATTN, HEAD_DIM)
        y_attn = _forgetting_attention(q, k, v, f + b_forget[l]).reshape(b, s, D_ATTN)
        y_ssm = _s5(u_ssm, ssm_A_re[l], ssm_A_im[l], ssm_log_dt[l], ssm_B_re[l], ssm_B_im[l],
                    ssm_C_re[l], ssm_C_im[l], ssm_D[l], w_glu[l])
        mixed = (jax.nn.sigmoid(g_attn) * (y_attn @ w_proj_attn[l])
                 + jax.nn.sigmoid(g_ssm) * (y_ssm @ w_proj_ssm[l]))
        x = x + mixed @ w_out[l]
        x = x + _hier_moe(_rmsnorm(x, norm_ffn_g[l]), w_router_group[l], b_router_group[l],
                          w_router_expert[l], b_router_expert[l],
                          w_expert_gate[l], w_expert_up[l], w_expert_down[l])
    return x
```

```python
import functools
import math

import jax
import jax.numpy as jnp
from jax import lax
from jax.experimental import pallas as pl
from jax.experimental.pallas import tpu as pltpu

F32 = jnp.float32
BF16 = jnp.bfloat16
I32 = jnp.int32

LANES = 128
SUBLANES = 8
MXU_DIM = 256

N_HEADS = 8
HEAD_DIM = 64
SSM_GROUP = 16
SSM_STATE = 64
N_EXPERT_GROUPS = 4
EXPERTS_PER_GROUP = 8
N_EXPERTS = N_EXPERT_GROUPS * EXPERTS_PER_GROUP
EPS = 1e-6
NEG_INF = -1e30

CHUNK = MXU_DIM // SSM_GROUP
CHUNKS_PER_SUPER = 8
SUPER = CHUNK * CHUNKS_PER_SUPER

TM_PROJ = 512
TQ_ATTN = 512
TM_MIX = 512
TM_EXPERT = 256
TM_DISPATCH = 512
TM_COMBINE = 512
SORT_CHUNK = 512
ROUTER_ROWS = 48

VMEM_LIMIT = 48 * 1024 * 1024


def _nt_dot(a, b):
    return lax.dot_general(a, b, (((1,), (1,)), ((), ())), preferred_element_type=F32)


def _dot(a, b):
    return jnp.dot(a, b, preferred_element_type=F32)


def _lane_tile(x, n):
    return x if n == 1 else jnp.concatenate([x] * n, axis=1)


def _rmsnorm_rows(x, g):
    ms = jnp.mean(x * x, axis=-1, keepdims=True)
    return x * lax.rsqrt(ms + EPS) * g


def _sigmoid(x):
    return 1.0 / (1.0 + jnp.exp(-x))


def _proj_kernel(x_ref, g_ref, w_ref, wf_ref, bf_ref, qg_ref, kg_ref, bd_ref, tri_ref,
                 q_ref, k_ref, v_ref, u_ref, kb_ref, carry_ref, *, tiles_per_seq):
    i = pl.program_id(0)

    @pl.when(i % tiles_per_seq == 0)
    def _():
        carry_ref[...] = jnp.zeros_like(carry_ref)

    tm = x_ref.shape[0]
    da = q_ref.shape[1]
    xb = _rmsnorm_rows(x_ref[...], g_ref[...]).astype(BF16)
    z = _dot(xb, w_ref[...])
    zq, zk = z[:, :da], z[:, da:2 * da]
    bd = bd_ref[...]
    msq = _dot((zq * zq).astype(BF16), bd)
    msk = _dot((zk * zk).astype(BF16), bd)
    q_ref[...] = (zq * lax.rsqrt(msq + EPS) * qg_ref[...]).astype(BF16)
    k_ref[...] = (zk * lax.rsqrt(msk + EPS) * kg_ref[...]).astype(BF16)
    v_ref[...] = z[:, 2 * da:3 * da].astype(BF16)
    u_ref[...] = z[:, 3 * da:].astype(BF16)

    a = _nt_dot(wf_ref[...], xb) + bf_ref[...]
    logf = jnp.minimum(a, 0.0) - jnp.log(1.0 + jnp.exp(-jnp.abs(a)))
    hi = logf.astype(BF16).astype(F32)
    r1 = logf - hi
    mid = r1.astype(BF16).astype(F32)
    lo = r1 - mid
    parts = jnp.concatenate([hi, mid, lo], axis=0).astype(BF16)
    cs = _dot(parts, tri_ref[...])
    h = logf.shape[0]
    cum = cs[0:h] + cs[h:2 * h] + cs[2 * h:3 * h] + carry_ref[:, 0:1]
    kb_ref[...] = -cum
    carry_ref[...] = jnp.broadcast_to(cum[:, tm - 1:tm], carry_ref.shape)


def _proj(x2, norm_g, w_main, wf_t, b_f, qg, kg, bd, tri, *, seq):
    t, d = x2.shape
    tm = TM_PROJ
    da = N_HEADS * HEAD_DIM
    du = w_main.shape[1] - 3 * da
    tiles_per_seq = seq // tm
    nb = t // seq
    full = lambda shape: pl.BlockSpec(shape, lambda i: (0,) * len(shape))
    row = lambda width: pl.BlockSpec((tm, width), lambda i: (i, 0))
    return pl.pallas_call(
        functools.partial(_proj_kernel, tiles_per_seq=tiles_per_seq),
        grid=(t // tm,),
        in_specs=[row(d), full((1, d)), full(w_main.shape), full(wf_t.shape), full(b_f.shape),
                  full((1, da)), full((1, da)), full(bd.shape), full(tri.shape)],
        out_specs=[row(da), row(da), row(da), row(du),
                   pl.BlockSpec((None, N_HEADS, tm), lambda i: (i // tiles_per_seq, 0, i % tiles_per_seq))],
        out_shape=[jax.ShapeDtypeStruct((t, da), BF16)] * 3
        + [jax.ShapeDtypeStruct((t, du), BF16), jax.ShapeDtypeStruct((nb, N_HEADS, seq), F32)],
        scratch_shapes=[pltpu.VMEM((N_HEADS, LANES), F32)],
        compiler_params=pltpu.CompilerParams(dimension_semantics=("arbitrary",), vmem_limit_bytes=VMEM_LIMIT),
        name="proj",
    )(x2, norm_g, w_main, wf_t, b_f, qg, kg, bd, tri)


def _attn_kernel(q_ref, k_ref, v_ref, kb_ref, o_ref, m_scr, l_scr, acc_scr, *, tq):
    i = pl.program_id(2)
    tk = tq
    lane = lax.broadcasted_iota(I32, (tq, LANES), 1)
    q = q_ref[...]
    zero = jnp.zeros_like(q)
    qh = (jnp.where(lane < HEAD_DIM, q, zero), jnp.where(lane >= HEAD_DIM, q, zero))
    m_scr[...] = jnp.full(m_scr.shape, NEG_INF, F32)
    l_scr[...] = jnp.zeros(l_scr.shape, F32)
    acc_scr[...] = jnp.zeros(acc_scr.shape, F32)

    def step(j, masked):
        off = pl.multiple_of(j * tk, tk)
        kblk = k_ref[pl.ds(off, tk), :]
        vblk = v_ref[pl.ds(off, tk), :]
        for h in range(2):
            s = _nt_dot(qh[h], kblk) + kb_ref[h, pl.ds(j, 1), :]
            if masked:
                rows = lax.broadcasted_iota(I32, (tq, tk), 0)
                cols = lax.broadcasted_iota(I32, (tq, tk), 1)
                s = jnp.where(cols <= rows, s, NEG_INF)
            m_prev = m_scr[h]
            m_new = jnp.maximum(m_prev, jnp.max(s, axis=1, keepdims=True))
            alpha = jnp.exp(m_prev - m_new)
            p = jnp.exp(s - _lane_tile(m_new, tk // LANES))
            l_scr[h] = alpha * l_scr[h] + jnp.sum(p, axis=1, keepdims=True)
            acc_scr[h] = alpha * acc_scr[h] + _dot(p.astype(BF16), vblk)
            m_scr[h] = m_new

    def body(j, carry):
        step(j, False)
        return carry

    lax.fori_loop(0, i, body, 0)
    step(i, True)
    o = jnp.where(lane < HEAD_DIM, acc_scr[0] / l_scr[0], acc_scr[1] / l_scr[1])
    o_ref[...] = o.astype(o_ref.dtype)


def _attention(q, k, v, kb):
    b, s, da = q.shape
    tq = TQ_ATTN
    pairs = da // LANES
    nk = s // tq
    kb5 = kb.reshape(b, pairs, 2, nk, tq)
    return pl.pallas_call(
        functools.partial(_attn_kernel, tq=tq),
        grid=(b, pairs, s // tq),
        in_specs=[pl.BlockSpec((None, tq, LANES), lambda bi, p, i: (bi, i, p)),
                  pl.BlockSpec((None, s, LANES), lambda bi, p, i: (bi, 0, p)),
                  pl.BlockSpec((None, s, LANES), lambda bi, p, i: (bi, 0, p)),
                  pl.BlockSpec((None, None, 2, nk, tq), lambda bi, p, i: (bi, p, 0, 0, 0))],
        out_specs=pl.BlockSpec((None, tq, LANES), lambda bi, p, i: (bi, i, p)),
        out_shape=jax.ShapeDtypeStruct((b, s, da), BF16),
        scratch_shapes=[pltpu.VMEM((2, tq, LANES), F32)] * 3,
        compiler_params=pltpu.CompilerParams(
            dimension_semantics=("arbitrary", "arbitrary", "arbitrary"), vmem_limit_bytes=VMEM_LIMIT),
        name="attn",
    )(q, k, v, kb5)


def _s5_kernel(ug_ref, win_ref, tw_ref, coef_ref, y_ref, *, nk):
    nj, r, w = ug_ref.shape
    half = w // 2
    u = ug_ref[...].reshape(nj * r, w)
    s1 = _dot(u, win_ref[...]).reshape(nj, r, w)

    def swap(val):
        return jnp.concatenate([val[:, half:], val[:, :half]], axis=1)

    def cmul(val, idx):
        return coef_ref[idx:idx + 1, :] * val + coef_ref[idx + 1:idx + 2, :] * swap(val)

    e = jnp.zeros((r, w), F32)
    local = []
    for j in range(nj):
        local.append(e)
        e = cmul(e, 0) + s1[j]
    kidx = lax.broadcasted_iota(I32, (r, w), 0) & (nk - 1)
    x = e
    step, d = 0, 1
    while d < nk:
        shifted = jnp.where(kidx >= d, pltpu.roll(x, d, axis=0), 0.0)
        x = x + cmul(shifted, 2 + 2 * nj + 2 * step)
        step, d = step + 1, d * 2
    x_start = jnp.where(kidx >= 1, pltpu.roll(x, 1, axis=0), 0.0)
    starts = [local[j] + cmul(x_start, 2 + 2 * j) for j in range(nj)]
    p = jnp.concatenate([st[:, :half] for st in starts], axis=0)
    hi = p.astype(BF16)
    lo = (p - hi.astype(F32)).astype(BF16)
    lhs = jnp.concatenate([u, hi, lo], axis=1)
    y = _dot(lhs, tw_ref[...])
    y_ref[...] = y.reshape(nj, r, w).astype(y_ref.dtype)


def _s5_params(a_re, a_im, log_dt, b_re, b_im, c_re, c_im, nk):
    g, p = a_re.shape
    c = SSM_GROUP
    a = lax.complex(a_re, a_im)
    dt = jnp.exp(log_dt)[:, None]
    adt = a * dt
    lam = jnp.exp(adt)
    bbar = ((lam - 1.0) / a)[..., None] * lax.complex(b_re, b_im)
    cm = lax.complex(c_re, c_im)

    def lam_pow(n):
        return jnp.exp(adt[:, None, :] * n.astype(F32)[None, :, None])

    lags = jnp.arange(CHUNK + 1)
    lp = lam_pow(lags)
    kern = jnp.real(jnp.einsum('gcp,gtp,gpd->gtcd', cm, lp[:, :CHUNK], bbar, precision=lax.Precision.HIGHEST))
    s_idx = jnp.arange(CHUNK)[:, None]
    t_idx = jnp.arange(CHUNK)[None, :]
    lag = t_idx - s_idx
    toep = kern[:, jnp.clip(lag, 0, CHUNK - 1)]
    toep = jnp.where((lag >= 0)[None, :, :, None, None], toep, 0.0)
    toep = toep.transpose(0, 1, 4, 2, 3).reshape(g, CHUNK * c, CHUNK * c)
    cl = cm[:, None, :, :] * lp[:, 1:, None, :]
    w_re = jnp.real(cl).transpose(0, 3, 1, 2).reshape(g, p, CHUNK * c)
    w_im = (-jnp.imag(cl)).transpose(0, 3, 1, 2).reshape(g, p, CHUNK * c)
    wout = jnp.concatenate([w_re, w_im], axis=1)
    tw = jnp.concatenate([toep, wout, wout], axis=1).astype(BF16)
    inj = lp[:, :CHUNK][:, ::-1, :, None] * bbar[:, None, :, :]
    inj = inj.transpose(0, 1, 3, 2).reshape(g, CHUNK * c, p)
    ir, ii = jnp.real(inj), jnp.imag(inj)
    win = jnp.concatenate([ir, ii, ii, ir], axis=2).astype(BF16)

    n_steps = max(nk.bit_length() - 1, 0)
    powers = [CHUNK] + [CHUNK * j for j in range(CHUNKS_PER_SUPER)] + [SUPER * (1 << i) for i in range(n_steps)]
    mult = lam_pow(jnp.array(powers, dtype=jnp.int32))
    ar, ai = jnp.real(mult), jnp.imag(mult)
    c1 = jnp.concatenate([ar, ar, ar, ar], axis=2)
    c2 = jnp.concatenate([-ai, ai, ai, -ai], axis=2)
    coef = jnp.stack([c1, c2], axis=2).reshape(g, 2 * len(powers), 4 * p)
    pad = (-coef.shape[1]) % SUBLANES
    coef = jnp.pad(coef, ((0, 0), (0, pad), (0, 0)))
    return win, tw, coef


def _s5(u, win, tw, coef):
    b, s, dssm = u.shape
    g = dssm // SSM_GROUP
    nk = s // SUPER
    r = b * nk
    nj = CHUNKS_PER_SUPER
    w = CHUNK * SSM_GROUP
    ug = u.reshape(b, nk, nj, CHUNK, g, SSM_GROUP).transpose(4, 2, 0, 1, 3, 5).reshape(g, nj, r, w)
    yg = pl.pallas_call(
        functools.partial(_s5_kernel, nk=nk),
        grid=(g,),
        in_specs=[pl.BlockSpec((None, nj, r, w), lambda i: (i, 0, 0, 0)),
                  pl.BlockSpec((None,) + win.shape[1:], lambda i: (i, 0, 0)),
                  pl.BlockSpec((None,) + tw.shape[1:], lambda i: (i, 0, 0)),
                  pl.BlockSpec((None,) + coef.shape[1:], lambda i: (i, 0, 0))],
        out_specs=pl.BlockSpec((None, nj, r, w), lambda i: (i, 0, 0, 0)),
        out_shape=jax.ShapeDtypeStruct((g, nj, r, w), BF16),
        compiler_params=pltpu.CompilerParams(dimension_semantics=("arbitrary",), vmem_limit_bytes=VMEM_LIMIT),
        name="s5",
    )(ug, win, tw, coef)
    return yg.reshape(g, nj, b, nk, CHUNK, SSM_GROUP).transpose(2, 3, 1, 4, 0, 5).reshape(b, s, dssm)


def _route(logits):
    e_all = logits[0:N_EXPERTS]
    gl = logits[N_EXPERTS:N_EXPERTS + SUBLANES]
    tm = logits.shape[1]
    ridx = lax.broadcasted_iota(I32, (SUBLANES, tm), 0)
    ge = jnp.exp(gl - jnp.max(gl, axis=0, keepdims=True))
    gp = ge / jnp.sum(ge, axis=0, keepdims=True)
    g_top = jnp.max(gp, axis=0, keepdims=True)
    g_sel = jnp.min(jnp.where(gp == g_top, ridx, SUBLANES), axis=0, keepdims=True)
    e_in = e_all[(N_EXPERT_GROUPS - 1) * SUBLANES:]
    for gi in range(N_EXPERT_GROUPS - 2, -1, -1):
        e_in = jnp.where(g_sel == gi, e_all[gi * SUBLANES:(gi + 1) * SUBLANES], e_in)
    ee = jnp.exp(e_in - jnp.max(e_in, axis=0, keepdims=True))
    ep = ee / jnp.sum(ee, axis=0, keepdims=True)
    v1 = jnp.max(ep, axis=0, keepdims=True)
    i1 = jnp.min(jnp.where(ep == v1, ridx, SUBLANES), axis=0, keepdims=True)
    ep2 = jnp.where(ridx == i1, -1.0, ep)
    v2 = jnp.max(ep2, axis=0, keepdims=True)
    i2 = jnp.min(jnp.where(ep2 == v2, ridx, SUBLANES), axis=0, keepdims=True)
    den = v1 + v2
    w1 = g_top * v1 / den
    w2 = g_top * v2 / den
    e1 = g_sel * EXPERTS_PER_GROUP + i1
    e2 = g_sel * EXPERTS_PER_GROUP + i2
    ids = jnp.where(ridx == 0, e1, jnp.where(ridx == 1, e2, 0))
    wts = jnp.where(ridx == 0, w1, jnp.where(ridx == 1, w2, 0.0))
    return ids, wts


def _mix_kernel(x_ref, ya_ref, ys_ref, u_ref, g1_ref, wgate_ref, dsk_ref, wglu_ref, wpa_ref, wps_ref,
                wout_ref, g2_ref, wrh_ref, wrl_ref, br_ref, x1_ref, xp_ref, ids_ref, wts_ref):
    d = x_ref.shape[1]
    x = x_ref[...]
    xb = _rmsnorm_rows(x, g1_ref[...]).astype(BF16)
    gates = _dot(xb, wgate_ref[...])
    y = ys_ref[...].astype(F32) + dsk_ref[...] * u_ref[...].astype(F32)
    y = y * (0.5 * (1.0 + jnp.tanh(math.sqrt(2.0 / math.pi) * (y + 0.044715 * (y * y * y)))))
    y = y * _sigmoid(_dot(y.astype(BF16), wglu_ref[...]))
    mixed = (_sigmoid(gates[:, :d]) * _dot(ya_ref[...], wpa_ref[...])
             + _sigmoid(gates[:, d:]) * _dot(y.astype(BF16), wps_ref[...]))
    x1 = x + _dot(mixed.astype(BF16), wout_ref[...])
    x1_ref[...] = x1
    xn = _rmsnorm_rows(x1, g2_ref[...])
    xp_ref[...] = pltpu.pack_elementwise([xn[:, :d // 2], xn[:, d // 2:]], packed_dtype=BF16)
    xh = xn.astype(BF16)
    xl = (xn - xh.astype(F32)).astype(BF16)
    wrh = wrh_ref[...]
    logits = _nt_dot(wrh, xh) + _nt_dot(wrl_ref[...], xh) + _nt_dot(wrh, xl) + br_ref[...]
    ids, wts = _route(logits)
    ids_ref[...] = ids
    wts_ref[...] = wts


def _mix(x2, ya, ys, u, g1, wgate, dsk, wglu, wpa, wps, wout, g2, wrh, wrl, br):
    t, d = x2.shape
    tm = TM_MIX
    dh = ya.shape[1]
    full = lambda a: pl.BlockSpec(a.shape, lambda i: (0,) * a.ndim)
    row = lambda width: pl.BlockSpec((tm, width), lambda i: (i, 0))
    col = pl.BlockSpec((SUBLANES, tm), lambda i: (0, i))
    return pl.pallas_call(
        _mix_kernel,
        grid=(t // tm,),
        in_specs=[row(d), row(dh), row(dh), row(dh), full(g1), full(wgate), full(dsk), full(wglu), full(wpa),
                  full(wps), full(wout), full(g2), full(wrh), full(wrl), full(br)],
        out_specs=[row(d), row(d // 2), col, col],
        out_shape=[jax.ShapeDtypeStruct((t, d), F32), jax.ShapeDtypeStruct((t, d // 2), I32),
                   jax.ShapeDtypeStruct((SUBLANES, t), I32), jax.ShapeDtypeStruct((SUBLANES, t), F32)],
        compiler_params=pltpu.CompilerParams(dimension_semantics=("arbitrary",), vmem_limit_bytes=VMEM_LIMIT),
        name="mix",
    )(x2, ya, ys, u, g1, wgate, dsk, wglu, wpa, wps, wout, g2, wrh, wrl, br)


def _meta_kernel(ids_ref, tri_ref, dest_ref, tile_ref, *, tile_rows):
    nk, nc, c = ids_ref.shape
    ne = N_EXPERTS
    erow = lax.broadcasted_iota(I32, (ne, c), 0)
    ones = jnp.ones((c, LANES), BF16)

    def onehot(k, ci):
        mask = erow == ids_ref[k, pl.ds(ci, 1), :]
        return mask, jnp.where(mask, 1.0, 0.0).astype(BF16)

    def count_body(n, acc):
        return acc + _dot(onehot(n // nc, n % nc)[1], ones)

    cnt = lax.fori_loop(0, nk * nc, count_body, jnp.zeros((ne, LANES), F32))
    ntiles = jnp.floor((cnt + (tile_rows - 1)) * (1.0 / tile_rows))
    lower = jnp.where(lax.broadcasted_iota(I32, (ne, ne), 1) < lax.broadcasted_iota(I32, (ne, ne), 0), 1.0, 0.0)
    start_tiles = _dot(lower.astype(BF16), ntiles.astype(BF16))
    base = start_tiles * tile_rows

    tri = tri_ref[...]

    def dest_body(n, carry):
        k, ci = n // nc, n % nc
        mask, oh = onehot(k, ci)
        prefix = _dot(oh, tri)
        slot = _lane_tile(base + carry, c // LANES) + prefix - 1.0
        dest = jnp.sum(jnp.where(mask, slot, 0.0), axis=0, keepdims=True)
        dest_ref[k, pl.ds(ci, 1), :] = dest.astype(I32)
        return carry + _dot(oh, ones)

    lax.fori_loop(0, nk * nc, dest_body, jnp.zeros((ne, LANES), F32))

    nt_lanes = tile_ref.shape[1]
    end_tiles = _lane_tile(start_tiles + ntiles, nt_lanes // LANES)
    tidx = lax.broadcasted_iota(I32, (ne, nt_lanes), 1).astype(F32)
    texp = jnp.sum(jnp.where(tidx >= end_tiles, 1.0, 0.0), axis=0, keepdims=True)
    valid = jnp.where(texp < ne, 1, 0)
    texp = jnp.minimum(texp, ne - 1.0).astype(I32)
    ridx = lax.broadcasted_iota(I32, tile_ref.shape, 0)
    tile_ref[...] = jnp.where(ridx == 0, texp, jnp.where(ridx == 1, valid, 0))


def _meta(ids2, tri, n_tiles):
    nk, t = ids2.shape
    c = SORT_CHUNK
    nt_lanes = pl.cdiv(n_tiles, LANES) * LANES
    ids3 = ids2.reshape(nk, t // c, c)
    dest, tile = pl.pallas_call(
        functools.partial(_meta_kernel, tile_rows=TM_EXPERT),
        out_shape=[jax.ShapeDtypeStruct(ids3.shape, I32), jax.ShapeDtypeStruct((SUBLANES, nt_lanes), I32)],
        compiler_params=pltpu.CompilerParams(vmem_limit_bytes=VMEM_LIMIT),
        name="meta",
    )(ids3, tri)
    return dest.reshape(nk * t), tile[0, :n_tiles], tile[1, :n_tiles]


def _row_copy(src_ref, src_row, dst_ref, dst_row, sem):
    return pltpu.make_async_copy(src_ref.at[pl.ds(src_row, 1)], dst_ref.at[pl.ds(dst_row, 1)], sem)


def _dispatch_kernel(dest_ref, xp_ref, xs_in_ref, xs_ref, sem, *, n_tokens):
    del xs_in_ref
    tm = xp_ref.shape[0]
    base = pl.program_id(0) * tm

    def body(r, carry):
        for k in range(2):
            _row_copy(xp_ref, r, xs_ref, dest_ref[k * n_tokens + base + r], sem).start()
        return carry

    lax.fori_loop(0, tm, body, 0)
    for k in range(2):
        pltpu.make_async_copy(xp_ref, xs_ref.at[pl.ds(0, tm)], sem).wait()


def _dispatch(dest, xp, n_slots):
    t, w = xp.shape
    tm = TM_DISPATCH
    xs0 = jnp.zeros((n_slots, w), xp.dtype)
    return pl.pallas_call(
        functools.partial(_dispatch_kernel, n_tokens=t),
        grid_spec=pltpu.PrefetchScalarGridSpec(
            num_scalar_prefetch=1,
            grid=(t // tm,),
            in_specs=[pl.BlockSpec((tm, w), lambda i, d: (i, 0)), pl.BlockSpec(memory_space=pl.ANY)],
            out_specs=pl.BlockSpec(memory_space=pl.ANY),
            scratch_shapes=[pltpu.SemaphoreType.DMA(())]),
        out_shape=jax.ShapeDtypeStruct((n_slots, w), xp.dtype),
        input_output_aliases={2: 0},
        compiler_params=pltpu.CompilerParams(dimension_semantics=("arbitrary",), has_side_effects=True),
        name="dispatch",
    )(dest, xp, xs0)


def _unpack_rows(packed):
    lo = pltpu.unpack_elementwise(packed, index=0, packed_dtype=BF16, unpacked_dtype=F32)
    hi = pltpu.unpack_elementwise(packed, index=1, packed_dtype=BF16, unpacked_dtype=F32)
    return jnp.concatenate([lo, hi], axis=1)


def _expert_kernel(te_ref, tv_ref, xs_ref, wg_ref, wu_ref, wd_ref, ys_ref):
    del te_ref
    i = pl.program_id(0)
    half = xs_ref.shape[1]

    @pl.when(tv_ref[i] > 0)
    def _():
        x = _unpack_rows(xs_ref[...]).astype(BF16)
        hg = _dot(x, wg_ref[...])
        hu = _dot(x, wu_ref[...])
        h = (hg * _sigmoid(hg) * hu).astype(BF16)
        y = _dot(h, wd_ref[...])
        ys_ref[...] = pltpu.pack_elementwise([y[:, :half], y[:, half:]], packed_dtype=BF16)

    @pl.when(tv_ref[i] == 0)
    def _():
        ys_ref[...] = jnp.zeros(ys_ref.shape, ys_ref.dtype)


def _experts(tile_expert, tile_valid, xs, wg, wu, wd):
    n_slots, w = xs.shape
    tm = TM_EXPERT
    wspec = lambda a: pl.BlockSpec((None,) + a.shape[1:], lambda i, te, tv: (te[i], 0, 0))
    return pl.pallas_call(
        _expert_kernel,
        grid_spec=pltpu.PrefetchScalarGridSpec(
            num_scalar_prefetch=2,
            grid=(n_slots // tm,),
            in_specs=[pl.BlockSpec((tm, w), lambda i, te, tv: (i, 0)), wspec(wg), wspec(wu), wspec(wd)],
            out_specs=pl.BlockSpec((tm, w), lambda i, te, tv: (i, 0))),
        out_shape=jax.ShapeDtypeStruct((n_slots, w), xs.dtype),
        compiler_params=pltpu.CompilerParams(dimension_semantics=("arbitrary",), vmem_limit_bytes=VMEM_LIMIT),
        name="experts",
    )(tile_expert, tile_valid, xs, wg, wu, wd)


def _combine_kernel(dest_ref, x1_ref, w0_ref, w1_ref, ys_ref, out_ref, buf, sem, *, n_tokens):
    tm, d = x1_ref.shape
    base = pl.program_id(0) * tm

    def body(r, carry):
        for k in range(2):
            _row_copy(ys_ref, dest_ref[k * n_tokens + base + r], buf.at[k], r, sem).start()
        return carry

    lax.fori_loop(0, tm, body, 0)
    for k in range(2):
        pltpu.make_async_copy(ys_ref.at[pl.ds(0, tm)], buf.at[k], sem).wait()
    reps = d // LANES
    out_ref[...] = (x1_ref[...]
                    + _lane_tile(w0_ref[...], reps) * _unpack_rows(buf[0])
                    + _lane_tile(w1_ref[...], reps) * _unpack_rows(buf[1]))


def _combine(dest, x1, w0, w1, ys):
    t, d = x1.shape
    tm = TM_COMBINE
    row = lambda width: pl.BlockSpec((tm, width), lambda i, dref: (i, 0))
    return pl.pallas_call(
        functools.partial(_combine_kernel, n_tokens=t),
        grid_spec=pltpu.PrefetchScalarGridSpec(
            num_scalar_prefetch=1,
            grid=(t // tm,),
            in_specs=[row(d), row(LANES), row(LANES), pl.BlockSpec(memory_space=pl.ANY)],
            out_specs=row(d),
            scratch_shapes=[pltpu.VMEM((2, tm, ys.shape[1]), ys.dtype), pltpu.SemaphoreType.DMA(())]),
        out_shape=jax.ShapeDtypeStruct((t, d), x1.dtype),
        compiler_params=pltpu.CompilerParams(dimension_semantics=("arbitrary",), vmem_limit_bytes=VMEM_LIMIT),
        name="combine",
    )(dest, x1, w0, w1, ys)


def _upper_tri(n):
    return jnp.where(jnp.arange(n)[:, None] <= jnp.arange(n)[None, :], 1.0, 0.0).astype(BF16)


def _split_bf16(w):
    hi = w.astype(BF16)
    lo = (w - hi.astype(F32)).astype(BF16)
    return hi, lo


def _layer(x, norm_mix_g, w_in, b_forget, q_norm_g, k_norm_g, ssm_A_re, ssm_A_im, ssm_log_dt, ssm_B_re, ssm_B_im,
           ssm_C_re, ssm_C_im, ssm_D, w_glu, w_proj_attn, w_proj_ssm, w_out, norm_ffn_g, w_router_group,
           b_router_group, w_router_expert, b_router_expert, w_expert_gate, w_expert_up, w_expert_down):
    b, s, d = x.shape
    t = b * s
    da = N_HEADS * HEAD_DIM
    dssm = ssm_D.shape[0]
    assert s % SUPER == 0 and s % TQ_ATTN == 0 and t % TM_PROJ == 0 and (s // SUPER) & (s // SUPER - 1) == 0
    x2 = x.reshape(t, d)

    o_f = 3 * da
    o_u = o_f + N_HEADS
    o_g = o_u + dssm
    w_main = jnp.concatenate([w_in[:, :o_f], w_in[:, o_u:o_g]], axis=1).astype(BF16)
    wf_t = w_in[:, o_f:o_u].T.astype(BF16)
    w_gates = w_in[:, o_g:].astype(BF16)
    qg = (jnp.tile(q_norm_g, N_HEADS) * HEAD_DIM ** -0.5)[None, :]
    kg = jnp.tile(k_norm_g, N_HEADS)[None, :]
    head_of = jnp.arange(da) // HEAD_DIM
    bd = jnp.where(head_of[:, None] == head_of[None, :], 1.0 / HEAD_DIM, 0.0).astype(BF16)

    q, k, v, u, kb = _proj(x2, norm_mix_g[None, :], w_main, wf_t, b_forget[:, None], qg, kg, bd,
                           _upper_tri(TM_PROJ), seq=s)
    y_attn = _attention(q.reshape(b, s, da), k.reshape(b, s, da), v.reshape(b, s, da), kb)

    win, tw, coef = _s5_params(ssm_A_re, ssm_A_im, ssm_log_dt, ssm_B_re, ssm_B_im, ssm_C_re, ssm_C_im, s // SUPER)
    y_ssm = _s5(u.reshape(b, s, dssm), win, tw, coef)

    wr = jnp.concatenate([w_router_expert.T, w_router_group.T,
                          jnp.zeros((ROUTER_ROWS - N_EXPERTS - N_EXPERT_GROUPS, d), F32)], axis=0)
    br = jnp.concatenate([b_router_expert, b_router_group,
                          jnp.full((SUBLANES - N_EXPERT_GROUPS,), NEG_INF, F32),
                          jnp.zeros((ROUTER_ROWS - N_EXPERTS - SUBLANES,), F32)])[:, None]
    x1, xp, ids, wts = _mix(x2, y_attn.reshape(t, da), y_ssm.reshape(t, dssm), u, norm_mix_g[None, :], w_gates,
                            ssm_D[None, :], w_glu.astype(BF16), w_proj_attn.astype(BF16), w_proj_ssm.astype(BF16),
                            w_out.astype(BF16), norm_ffn_g[None, :], *_split_bf16(wr), br)

    n_tiles = (2 * t) // TM_EXPERT + N_EXPERTS
    dest, tile_expert, tile_valid = _meta(ids[:2], _upper_tri(SORT_CHUNK), n_tiles)
    xs = _dispatch(dest, xp, n_tiles * TM_EXPERT)
    ys = _experts(tile_expert, tile_valid, xs, w_expert_gate.astype(BF16), w_expert_up.astype(BF16),
                  w_expert_down.astype(BF16))
    w0 = jnp.broadcast_to(wts[0][:, None], (t, LANES))
    w1 = jnp.broadcast_to(wts[1][:, None], (t, LANES))
    out = _combine(dest, x1, w0, w1, ys)
    return out.reshape(b, s, d)


def kernel(x, norm_mix_g, w_in, b_forget, q_norm_g, k_norm_g, ssm_A_re, ssm_A_im, ssm_log_dt, ssm_B_re, ssm_B_im,
           ssm_C_re, ssm_C_im, ssm_D, w_glu, w_proj_attn, w_proj_ssm, w_out, norm_ffn_g, w_router_group,
           b_router_group, w_router_expert, b_router_expert, w_expert_gate, w_expert_up, w_expert_down):
    layer_params = (norm_mix_g, w_in, b_forget, q_norm_g, k_norm_g, ssm_A_re, ssm_A_im, ssm_log_dt, ssm_B_re,
                    ssm_B_im, ssm_C_re, ssm_C_im, ssm_D, w_glu, w_proj_attn, w_proj_ssm, w_out, norm_ffn_g,
                    w_router_group, b_router_group, w_router_expert, b_router_expert, w_expert_gate, w_expert_up,
                    w_expert_down)
    for layer in range(norm_mix_g.shape[0]):
        x = _layer(x, *[p[layer] for p in layer_params])
    return x
```

```python
import functools
import math

import jax
import jax.numpy as jnp
from jax import lax
from jax.experimental import pallas as pl
from jax.experimental.pallas import tpu as pltpu

F32 = jnp.float32
BF16 = jnp.bfloat16
I32 = jnp.int32

LANES = 128
SUBLANES = 8
MXU_DIM = 256

N_HEADS = 8
HEAD_DIM = 64
SSM_GROUP = 16
SSM_STATE = 64
N_EXPERT_GROUPS = 4
EXPERTS_PER_GROUP = 8
N_EXPERTS = N_EXPERT_GROUPS * EXPERTS_PER_GROUP
EPS = 1e-6
NEG_INF = -1e30

CHUNK = MXU_DIM // SSM_GROUP
CHUNKS_PER_SUPER = 8
SUPER = CHUNK * CHUNKS_PER_SUPER
GROUPS_PER_BLOCK = LANES // SSM_GROUP

TM_PROJ = 512
TQ_ATTN = 512
TM_MIX = 512
TM_EXPERT = 256
TM_DISPATCH = 512
TM_COMBINE = 512
SORT_CHUNK = 512
ROUTER_ROWS = 48

VMEM_LIMIT = 48 * 1024 * 1024


def _nt_dot(a, b):
    return lax.dot_general(a, b, (((1,), (1,)), ((), ())), preferred_element_type=F32)


def _dot(a, b):
    return jnp.dot(a, b, preferred_element_type=F32)


def _lane_tile(x, n):
    return x if n == 1 else jnp.concatenate([x] * n, axis=1)


def _rmsnorm_rows(x, g):
    ms = jnp.mean(x * x, axis=-1, keepdims=True)
    return x * lax.rsqrt(ms + EPS) * g


def _sigmoid(x):
    return 1.0 / (1.0 + jnp.exp(-x))


def _proj_kernel(x_ref, g_ref, w_ref, wf_ref, bf_ref, qg_ref, kg_ref, bd_ref, tri_ref,
                 q_ref, k_ref, v_ref, u_ref, kb_ref, carry_ref, *, tiles_per_seq):
    i = pl.program_id(0)

    @pl.when(i % tiles_per_seq == 0)
    def _():
        carry_ref[...] = jnp.zeros_like(carry_ref)

    tm = x_ref.shape[0]
    da = q_ref.shape[1]
    xb = _rmsnorm_rows(x_ref[...], g_ref[...]).astype(BF16)
    z = _dot(xb, w_ref[...])
    zq, zk = z[:, :da], z[:, da:2 * da]
    bd = bd_ref[...]
    msq = _dot((zq * zq).astype(BF16), bd)
    msk = _dot((zk * zk).astype(BF16), bd)
    q_ref[...] = (zq * lax.rsqrt(msq + EPS) * qg_ref[...]).astype(BF16)
    k_ref[...] = (zk * lax.rsqrt(msk + EPS) * kg_ref[...]).astype(BF16)
    v_ref[...] = z[:, 2 * da:3 * da].astype(BF16)
    u_ref[...] = z[:, 3 * da:]

    a = _nt_dot(wf_ref[...], xb) + bf_ref[...]
    logf = jnp.minimum(a, 0.0) - jnp.log(1.0 + jnp.exp(-jnp.abs(a)))
    hi = logf.astype(BF16).astype(F32)
    r1 = logf - hi
    mid = r1.astype(BF16).astype(F32)
    lo = r1 - mid
    parts = jnp.concatenate([hi, mid, lo], axis=0).astype(BF16)
    cs = _dot(parts, tri_ref[...])
    h = logf.shape[0]
    cum = cs[0:h] + cs[h:2 * h] + cs[2 * h:3 * h] + carry_ref[:, 0:1]
    kb_ref[...] = -cum
    carry_ref[...] = jnp.broadcast_to(cum[:, tm - 1:tm], carry_ref.shape)


def _proj(x2, norm_g, w_main, wf_t, b_f, qg, kg, bd, tri, *, seq):
    t, d = x2.shape
    tm = TM_PROJ
    da = N_HEADS * HEAD_DIM
    du = w_main.shape[1] - 3 * da
    tiles_per_seq = seq // tm
    nb = t // seq
    full = lambda shape: pl.BlockSpec(shape, lambda i: (0,) * len(shape))
    row = lambda width: pl.BlockSpec((tm, width), lambda i: (i, 0))
    return pl.pallas_call(
        functools.partial(_proj_kernel, tiles_per_seq=tiles_per_seq),
        grid=(t // tm,),
        in_specs=[row(d), full((1, d)), full(w_main.shape), full(wf_t.shape), full(b_f.shape),
                  full((1, da)), full((1, da)), full(bd.shape), full(tri.shape)],
        out_specs=[row(da), row(da), row(da), row(du),
                   pl.BlockSpec((None, N_HEADS, tm), lambda i: (i // tiles_per_seq, 0, i % tiles_per_seq))],
        out_shape=[jax.ShapeDtypeStruct((t, da), BF16)] * 3
        + [jax.ShapeDtypeStruct((t, du), F32), jax.ShapeDtypeStruct((nb, N_HEADS, seq), F32)],
        scratch_shapes=[pltpu.VMEM((N_HEADS, LANES), F32)],
        compiler_params=pltpu.CompilerParams(dimension_semantics=("arbitrary",), vmem_limit_bytes=VMEM_LIMIT),
        name="proj",
    )(x2, norm_g, w_main, wf_t, b_f, qg, kg, bd, tri)


def _attn_kernel(q_ref, k_ref, v_ref, kb_ref, o_ref, m_scr, l_scr, acc_scr, *, tq):
    i = pl.program_id(2)
    tk = tq
    lane = lax.broadcasted_iota(I32, (tq, LANES), 1)
    q = q_ref[...]
    zero = jnp.zeros_like(q)
    qh = (jnp.where(lane < HEAD_DIM, q, zero), jnp.where(lane >= HEAD_DIM, q, zero))
    m_scr[...] = jnp.full(m_scr.shape, NEG_INF, F32)
    l_scr[...] = jnp.zeros(l_scr.shape, F32)
    acc_scr[...] = jnp.zeros(acc_scr.shape, F32)

    def step(j, masked):
        off = pl.multiple_of(j * tk, tk)
        kblk = k_ref[pl.ds(off, tk), :]
        vblk = v_ref[pl.ds(off, tk), :]
        for h in range(2):
            s = _nt_dot(qh[h], kblk) + kb_ref[h, pl.ds(j, 1), :]
            if masked:
                rows = lax.broadcasted_iota(I32, (tq, tk), 0)
                cols = lax.broadcasted_iota(I32, (tq, tk), 1)
                s = jnp.where(cols <= rows, s, NEG_INF)
            m_prev = m_scr[h]
            m_new = jnp.maximum(m_prev, jnp.max(s, axis=1, keepdims=True))
            alpha = jnp.exp(m_prev - m_new)
            p = jnp.exp(s - _lane_tile(m_new, tk // LANES))
            l_scr[h] = alpha * l_scr[h] + jnp.sum(p, axis=1, keepdims=True)
            acc_scr[h] = alpha * acc_scr[h] + _dot(p.astype(BF16), vblk)
            m_scr[h] = m_new

    def body(j, carry):
        step(j, False)
        return carry

    lax.fori_loop(0, i, body, 0)
    step(i, True)
    o = jnp.where(lane < HEAD_DIM, acc_scr[0] / l_scr[0], acc_scr[1] / l_scr[1])
    o_ref[...] = o.astype(o_ref.dtype)


def _attention(q, k, v, kb):
    b, s, da = q.shape
    tq = TQ_ATTN
    pairs = da // LANES
    nk = s // tq
    kb5 = kb.reshape(b, pairs, 2, nk, tq)
    return pl.pallas_call(
        functools.partial(_attn_kernel, tq=tq),
        grid=(b, pairs, s // tq),
        in_specs=[pl.BlockSpec((None, tq, LANES), lambda bi, p, i: (bi, i, p)),
                  pl.BlockSpec((None, s, LANES), lambda bi, p, i: (bi, 0, p)),
                  pl.BlockSpec((None, s, LANES), lambda bi, p, i: (bi, 0, p)),
                  pl.BlockSpec((None, None, 2, nk, tq), lambda bi, p, i: (bi, p, 0, 0, 0))],
        out_specs=pl.BlockSpec((None, tq, LANES), lambda bi, p, i: (bi, i, p)),
        out_shape=jax.ShapeDtypeStruct((b, s, da), BF16),
        scratch_shapes=[pltpu.VMEM((2, tq, LANES), F32)] * 3,
        compiler_params=pltpu.CompilerParams(
            dimension_semantics=("arbitrary", "arbitrary", "arbitrary"), vmem_limit_bytes=VMEM_LIMIT),
        name="attn",
    )(q, k, v, kb5)


def _window_select(pieces, first_group):
    window = lax.broadcasted_iota(I32, pieces[0].shape, 1) // SSM_GROUP
    out = pieces[0]
    for m in range(1, GROUPS_PER_BLOCK):
        out = jnp.where(window == (first_group + m) % GROUPS_PER_BLOCK, pieces[m], out)
    return out


def _s5_kernel(u_ref, win_ref, tw_ref, coef_ref, y_ref, ub_scr, yb_scr, *, nk):
    nj = CHUNKS_PER_SUPER
    gpb = GROUPS_PER_BLOCK
    w = CHUNK * SSM_GROUP
    half = w // 2
    nkk = nk // SUBLANES
    row_stride = SUPER

    def gather_body(it, carry):
        j, kk = it // nkk, it % nkk
        for sp in range(CHUNK // gpb):
            rot = []
            for m in range(gpb):
                row0 = j * CHUNK + sp * gpb + m + kk * (SUBLANES * row_stride)
                src = u_ref[pl.ds(row0, SUBLANES, stride=row_stride), :]
                rot.append(src if m == 0 else pltpu.roll(src, m * SSM_GROUP, axis=1))
            for gl in range(gpb):
                ub_scr[gl, j, pl.ds(kk * SUBLANES, SUBLANES), sp * LANES:(sp + 1) * LANES] = _window_select(rot, gl)
        return carry

    lax.fori_loop(0, nj * nkk, gather_body, 0)

    def swap(val):
        return jnp.concatenate([val[:, half:], val[:, :half]], axis=1)

    def group_body(gl, carry):
        def cmul(val, idx):
            return coef_ref[gl, idx:idx + 1, :] * val + coef_ref[gl, idx + 1:idx + 2, :] * swap(val)

        u = ub_scr[gl].reshape(nj * nk, w).astype(BF16)
        s1 = _dot(u, win_ref[gl]).reshape(nj, nk, w)
        e = jnp.zeros((nk, w), F32)
        local = []
        for j in range(nj):
            local.append(e)
            e = cmul(e, 0) + s1[j]
        kidx = lax.broadcasted_iota(I32, (nk, w), 0)
        x = e
        step, d = 0, 1
        while d < nk:
            shifted = jnp.where(kidx >= d, pltpu.roll(x, d, axis=0), 0.0)
            x = x + cmul(shifted, 2 + 2 * nj + 2 * step)
            step, d = step + 1, d * 2
        x_start = jnp.where(kidx >= 1, pltpu.roll(x, 1, axis=0), 0.0)
        starts = [local[j] + cmul(x_start, 2 + 2 * j) for j in range(nj)]
        p = jnp.concatenate([st[:, :half] for st in starts], axis=0)
        hi = p.astype(BF16)
        lo = (p - hi.astype(F32)).astype(BF16)
        y = _dot(jnp.concatenate([u, hi, lo], axis=1), tw_ref[gl])
        yb_scr[gl] = y.reshape(nj, nk, w)
        return carry

    lax.fori_loop(0, gpb, group_body, 0)

    def scatter_body(it, carry):
        j, kk = it // nkk, it % nkk
        for tp in range(CHUNK // gpb):
            src = [yb_scr[gl, j, pl.ds(kk * SUBLANES, SUBLANES), tp * LANES:(tp + 1) * LANES] for gl in range(gpb)]
            for m in range(gpb):
                by_window = [src[(wi - m) % gpb] for wi in range(gpb)]
                window = lax.broadcasted_iota(I32, by_window[0].shape, 1) // SSM_GROUP
                merged = by_window[0]
                for wi in range(1, gpb):
                    merged = jnp.where(window == wi, by_window[wi], merged)
                nat = merged if m == 0 else pltpu.roll(merged, LANES - m * SSM_GROUP, axis=1)
                row0 = j * CHUNK + tp * gpb + m + kk * (SUBLANES * row_stride)
                y_ref[pl.ds(row0, SUBLANES, stride=row_stride), :] = nat
        return carry

    lax.fori_loop(0, nj * nkk, scatter_body, 0)


def _s5_params(a_re, a_im, log_dt, b_re, b_im, c_re, c_im, nk):
    g, p = a_re.shape
    c = SSM_GROUP
    hp = lax.Precision.HIGHEST
    dt = jnp.exp(log_dt)[:, None]
    adt_r, adt_i = a_re * dt, a_im * dt

    def lam_pow(n):
        nf = jnp.asarray(n, F32)[None, :, None]
        mag = jnp.exp(adt_r[:, None, :] * nf)
        ang = adt_i[:, None, :] * nf
        return mag * jnp.cos(ang), mag * jnp.sin(ang)

    lr, li = lam_pow(jnp.arange(CHUNK + 1))
    den = a_re * a_re + a_im * a_im
    nr, ni = lr[:, 1] - 1.0, li[:, 1]
    qr = (nr * a_re + ni * a_im) / den
    qi = (ni * a_re - nr * a_im) / den
    bbr = qr[..., None] * b_re - qi[..., None] * b_im
    bbi = qr[..., None] * b_im + qi[..., None] * b_re

    mr = lr[:, :CHUNK, :, None] * bbr[:, None] - li[:, :CHUNK, :, None] * bbi[:, None]
    mi = lr[:, :CHUNK, :, None] * bbi[:, None] + li[:, :CHUNK, :, None] * bbr[:, None]
    kern = (jnp.einsum('gcp,gtpd->gtcd', c_re, mr, precision=hp)
            - jnp.einsum('gcp,gtpd->gtcd', c_im, mi, precision=hp))
    s_idx = jnp.arange(CHUNK)[:, None]
    t_idx = jnp.arange(CHUNK)[None, :]
    lag = t_idx - s_idx
    toep = kern[:, jnp.clip(lag, 0, CHUNK - 1)]
    toep = jnp.where((lag >= 0)[None, :, :, None, None], toep, 0.0)
    toep = toep.transpose(0, 1, 4, 2, 3).reshape(g, CHUNK * c, CHUNK * c)
    clr = c_re[:, None] * lr[:, 1:, None, :] - c_im[:, None] * li[:, 1:, None, :]
    cli = c_re[:, None] * li[:, 1:, None, :] + c_im[:, None] * lr[:, 1:, None, :]
    w_re = clr.transpose(0, 3, 1, 2).reshape(g, p, CHUNK * c)
    w_im = (-cli).transpose(0, 3, 1, 2).reshape(g, p, CHUNK * c)
    wout = jnp.concatenate([w_re, w_im], axis=1)
    pr, pi = lr[:, :CHUNK][:, ::-1], li[:, :CHUNK][:, ::-1]
    ir = pr[..., None] * bbr[:, None] - pi[..., None] * bbi[:, None]
    ii = pr[..., None] * bbi[:, None] + pi[..., None] * bbr[:, None]
    ir = ir.transpose(0, 1, 3, 2).reshape(g, CHUNK * c, p)
    ii = ii.transpose(0, 1, 3, 2).reshape(g, CHUNK * c, p)
    win = jnp.concatenate([ir, ii, ii, ir], axis=2)

    gi = jnp.arange(g)[:, None, None] % GROUPS_PER_BLOCK
    si = jnp.arange(CHUNK)[None, :, None]
    ci = jnp.arange(c)[None, None, :]
    lane_of = ((si // GROUPS_PER_BLOCK) * LANES + ((gi + si) % GROUPS_PER_BLOCK) * c + ci).reshape(g, CHUNK * c)
    src_of = jnp.argsort(lane_of, axis=1)
    win = jnp.take_along_axis(win, src_of[:, :, None], axis=1)
    toep = jnp.take_along_axis(toep, src_of[:, :, None], axis=1)
    tw = jnp.concatenate([toep, wout, wout], axis=1)
    tw = jnp.take_along_axis(tw, src_of[:, None, :], axis=2)

    n_steps = max(nk.bit_length() - 1, 0)
    powers = [CHUNK] + [CHUNK * j for j in range(CHUNKS_PER_SUPER)] + [SUPER * (1 << i) for i in range(n_steps)]
    ar, ai = lam_pow(jnp.array(powers, dtype=jnp.int32))
    c1 = jnp.concatenate([ar, ar, ar, ar], axis=2)
    c2 = jnp.concatenate([-ai, ai, ai, -ai], axis=2)
    coef = jnp.stack([c1, c2], axis=2).reshape(g, 2 * len(powers), 4 * p)
    pad = (-coef.shape[1]) % SUBLANES
    coef = jnp.pad(coef, ((0, 0), (0, pad), (0, 0)))
    return win.astype(BF16), tw.astype(BF16), coef


def _s5(u, win, tw, coef):
    b, s, dssm = u.shape
    nk = s // SUPER
    nj = CHUNKS_PER_SUPER
    w = CHUNK * SSM_GROUP
    gpb = GROUPS_PER_BLOCK
    nblk = dssm // LANES
    wspec = lambda a: pl.BlockSpec((gpb,) + a.shape[1:], lambda bi, li: (li, 0, 0))
    return pl.pallas_call(
        functools.partial(_s5_kernel, nk=nk),
        grid=(b, nblk),
        in_specs=[pl.BlockSpec((None, s, LANES), lambda bi, li: (bi, 0, li)), wspec(win), wspec(tw), wspec(coef)],
        out_specs=pl.BlockSpec((None, s, LANES), lambda bi, li: (bi, 0, li)),
        out_shape=jax.ShapeDtypeStruct((b, s, dssm), F32),
        scratch_shapes=[pltpu.VMEM((gpb, nj, nk, w), F32)] * 2,
        compiler_params=pltpu.CompilerParams(dimension_semantics=("arbitrary", "arbitrary"),
                                             vmem_limit_bytes=VMEM_LIMIT),
        name="s5",
    )(u, win, tw, coef)


def _route(logits):
    e_all = logits[0:N_EXPERTS]
    gl = logits[N_EXPERTS:N_EXPERTS + SUBLANES]
    tm = logits.shape[1]
    ridx = lax.broadcasted_iota(I32, (SUBLANES, tm), 0)
    ge = jnp.exp(gl - jnp.max(gl, axis=0, keepdims=True))
    gp = ge / jnp.sum(ge, axis=0, keepdims=True)
    g_top = jnp.max(gp, axis=0, keepdims=True)
    g_sel = jnp.min(jnp.where(gp == g_top, ridx, SUBLANES), axis=0, keepdims=True)
    e_in = e_all[(N_EXPERT_GROUPS - 1) * SUBLANES:]
    for gi in range(N_EXPERT_GROUPS - 2, -1, -1):
        e_in = jnp.where(g_sel == gi, e_all[gi * SUBLANES:(gi + 1) * SUBLANES], e_in)
    ee = jnp.exp(e_in - jnp.max(e_in, axis=0, keepdims=True))
    ep = ee / jnp.sum(ee, axis=0, keepdims=True)
    v1 = jnp.max(ep, axis=0, keepdims=True)
    i1 = jnp.min(jnp.where(ep == v1, ridx, SUBLANES), axis=0, keepdims=True)
    ep2 = jnp.where(ridx == i1, -1.0, ep)
    v2 = jnp.max(ep2, axis=0, keepdims=True)
    i2 = jnp.min(jnp.where(ep2 == v2, ridx, SUBLANES), axis=0, keepdims=True)
    den = v1 + v2
    w1 = g_top * v1 / den
    w2 = g_top * v2 / den
    e1 = g_sel * EXPERTS_PER_GROUP + i1
    e2 = g_sel * EXPERTS_PER_GROUP + i2
    ids = jnp.where(ridx == 0, e1, jnp.where(ridx == 1, e2, 0))
    wts = jnp.where(ridx == 0, w1, jnp.where(ridx == 1, w2, 0.0))
    return ids, wts


def _mix_kernel(x_ref, ya_ref, ys_ref, u_ref, g1_ref, wgate_ref, dsk_ref, wglu_ref, wpa_ref, wps_ref,
                wout_ref, g2_ref, wrh_ref, wrl_ref, br_ref, x1_ref, xp_ref, ids_ref, wts_ref):
    d = x_ref.shape[1]
    x = x_ref[...]
    xb = _rmsnorm_rows(x, g1_ref[...]).astype(BF16)
    gates = _dot(xb, wgate_ref[...])
    y = ys_ref[...].astype(F32) + dsk_ref[...] * u_ref[...].astype(F32)
    y = y * (0.5 * (1.0 + jnp.tanh(math.sqrt(2.0 / math.pi) * (y + 0.044715 * (y * y * y)))))
    y = y * _sigmoid(_dot(y.astype(BF16), wglu_ref[...]))
    mixed = (_sigmoid(gates[:, :d]) * _dot(ya_ref[...], wpa_ref[...])
             + _sigmoid(gates[:, d:]) * _dot(y.astype(BF16), wps_ref[...]))
    x1 = x + _dot(mixed.astype(BF16), wout_ref[...])
    x1_ref[...] = x1
    xn = _rmsnorm_rows(x1, g2_ref[...])
    xp_ref[...] = pltpu.pack_elementwise([xn[:, :d // 2], xn[:, d // 2:]], packed_dtype=BF16)
    xh = xn.astype(BF16)
    xl = (xn - xh.astype(F32)).astype(BF16)
    wrh = wrh_ref[...]
    logits = _nt_dot(wrh, xh) + _nt_dot(wrl_ref[...], xh) + _nt_dot(wrh, xl) + br_ref[...]
    ids, wts = _route(logits)
    ids_ref[...] = ids
    wts_ref[...] = wts


def _mix(x2, ya, ys, u, g1, wgate, dsk, wglu, wpa, wps, wout, g2, wrh, wrl, br):
    t, d = x2.shape
    tm = TM_MIX
    dh = ya.shape[1]
    full = lambda a: pl.BlockSpec(a.shape, lambda i: (0,) * a.ndim)
    row = lambda width: pl.BlockSpec((tm, width), lambda i: (i, 0))
    col = pl.BlockSpec((SUBLANES, tm), lambda i: (0, i))
    return pl.pallas_call(
        _mix_kernel,
        grid=(t // tm,),
        in_specs=[row(d), row(dh), row(dh), row(dh), full(g1), full(wgate), full(dsk), full(wglu), full(wpa),
                  full(wps), full(wout), full(g2), full(wrh), full(wrl), full(br)],
        out_specs=[row(d), row(d // 2), col, col],
        out_shape=[jax.ShapeDtypeStruct((t, d), F32), jax.ShapeDtypeStruct((t, d // 2), jnp.uint32),
                   jax.ShapeDtypeStruct((SUBLANES, t), I32), jax.ShapeDtypeStruct((SUBLANES, t), F32)],
        compiler_params=pltpu.CompilerParams(dimension_semantics=("arbitrary",), vmem_limit_bytes=VMEM_LIMIT),
        name="mix",
    )(x2, ya, ys, u, g1, wgate, dsk, wglu, wpa, wps, wout, g2, wrh, wrl, br)


def _meta_kernel(ids_ref, tri_ref, dest_ref, tile_ref, *, tile_rows):
    nk, nc, c = ids_ref.shape
    ne = N_EXPERTS
    erow = lax.broadcasted_iota(I32, (ne, c), 0)
    ones = jnp.ones((c, LANES), BF16)

    def onehot(k, ci):
        mask = erow == ids_ref[k, pl.ds(ci, 1), :]
        return mask, jnp.where(mask, 1.0, 0.0).astype(BF16)

    def count_body(n, acc):
        return acc + _dot(onehot(n // nc, n % nc)[1], ones)

    cnt = lax.fori_loop(0, nk * nc, count_body, jnp.zeros((ne, LANES), F32))
    ntiles = jnp.floor((cnt + (tile_rows - 1)) * (1.0 / tile_rows))
    lower = jnp.where(lax.broadcasted_iota(I32, (ne, ne), 1) < lax.broadcasted_iota(I32, (ne, ne), 0), 1.0, 0.0)
    start_tiles = _dot(lower.astype(BF16), ntiles.astype(BF16))
    base = start_tiles * tile_rows

    tri = tri_ref[...]

    def dest_body(n, carry):
        k, ci = n // nc, n % nc
        mask, oh = onehot(k, ci)
        prefix = _dot(oh, tri)
        slot = _lane_tile(base + carry, c // LANES) + prefix - 1.0
        dest = jnp.sum(jnp.where(mask, slot, 0.0), axis=0, keepdims=True)
        dest_ref[k, pl.ds(ci, 1), :] = dest.astype(I32)
        return carry + _dot(oh, ones)

    lax.fori_loop(0, nk * nc, dest_body, jnp.zeros((ne, LANES), F32))

    nt_lanes = tile_ref.shape[1]
    end_tiles = _lane_tile(start_tiles + ntiles, nt_lanes // LANES)
    tidx = lax.broadcasted_iota(I32, (ne, nt_lanes), 1).astype(F32)
    texp = jnp.sum(jnp.where(tidx >= end_tiles, 1.0, 0.0), axis=0, keepdims=True)
    valid = jnp.where(texp < ne, 1, 0)
    texp = jnp.minimum(texp, ne - 1.0).astype(I32)
    ridx = lax.broadcasted_iota(I32, tile_ref.shape, 0)
    tile_ref[...] = jnp.where(ridx == 0, texp, jnp.where(ridx == 1, valid, 0))


def _meta(ids2, tri, n_tiles):
    nk, t = ids2.shape
    c = SORT_CHUNK
    nt_lanes = pl.cdiv(n_tiles, LANES) * LANES
    ids3 = ids2.reshape(nk, t // c, c)
    dest, tile = pl.pallas_call(
        functools.partial(_meta_kernel, tile_rows=TM_EXPERT),
        out_shape=[jax.ShapeDtypeStruct(ids3.shape, I32), jax.ShapeDtypeStruct((SUBLANES, nt_lanes), I32)],
        compiler_params=pltpu.CompilerParams(vmem_limit_bytes=VMEM_LIMIT),
        name="meta",
    )(ids3, tri)
    return dest.reshape(nk * t), tile[0, :n_tiles], tile[1, :n_tiles]


def _row_copy(src_ref, src_row, dst_ref, dst_row, sem):
    return pltpu.make_async_copy(src_ref.at[pl.ds(src_row, 1)], dst_ref.at[pl.ds(dst_row, 1)], sem)


def _dispatch_kernel(dest_ref, xp_ref, xs_in_ref, xs_ref, sem, *, n_tokens):
    del xs_in_ref
    tm = xp_ref.shape[0]
    base = pl.program_id(0) * tm

    def body(r, carry):
        for k in range(2):
            _row_copy(xp_ref, r, xs_ref, dest_ref[k * n_tokens + base + r], sem).start()
        return carry

    lax.fori_loop(0, tm, body, 0)
    for k in range(2):
        pltpu.make_async_copy(xp_ref, xs_ref.at[pl.ds(0, tm)], sem).wait()


def _dispatch(dest, xp, n_slots):
    t, w = xp.shape
    tm = TM_DISPATCH
    xs0 = jnp.zeros((n_slots, w), xp.dtype)
    return pl.pallas_call(
        functools.partial(_dispatch_kernel, n_tokens=t),
        grid_spec=pltpu.PrefetchScalarGridSpec(
            num_scalar_prefetch=1,
            grid=(t // tm,),
            in_specs=[pl.BlockSpec((tm, w), lambda i, d: (i, 0)), pl.BlockSpec(memory_space=pl.ANY)],
            out_specs=pl.BlockSpec(memory_space=pl.ANY),
            scratch_shapes=[pltpu.SemaphoreType.DMA(())]),
        out_shape=jax.ShapeDtypeStruct((n_slots, w), xp.dtype),
        input_output_aliases={2: 0},
        compiler_params=pltpu.CompilerParams(dimension_semantics=("arbitrary",), has_side_effects=True),
        name="dispatch",
    )(dest, xp, xs0)


def _unpack_rows(packed):
    lo = pltpu.unpack_elementwise(packed, index=0, packed_dtype=BF16, unpacked_dtype=F32)
    hi = pltpu.unpack_elementwise(packed, index=1, packed_dtype=BF16, unpacked_dtype=F32)
    return jnp.concatenate([lo, hi], axis=1)


def _expert_kernel(te_ref, tv_ref, xs_ref, wg_ref, wu_ref, wd_ref, ys_ref):
    del te_ref
    i = pl.program_id(0)
    half = xs_ref.shape[1]

    @pl.when(tv_ref[i] > 0)
    def _():
        x = _unpack_rows(xs_ref[...]).astype(BF16)
        hg = _dot(x, wg_ref[...])
        hu = _dot(x, wu_ref[...])
        h = (hg * _sigmoid(hg) * hu).astype(BF16)
        y = _dot(h, wd_ref[...])
        ys_ref[...] = pltpu.pack_elementwise([y[:, :half], y[:, half:]], packed_dtype=BF16)

    @pl.when(tv_ref[i] == 0)
    def _():
        zero = jnp.zeros(ys_ref.shape, F32)
        ys_ref[...] = pltpu.pack_elementwise([zero, zero], packed_dtype=BF16)


def _experts(tile_expert, tile_valid, xs, wg, wu, wd):
    n_slots, w = xs.shape
    tm = TM_EXPERT
    wspec = lambda a: pl.BlockSpec((None,) + a.shape[1:], lambda i, te, tv: (te[i], 0, 0))
    return pl.pallas_call(
        _expert_kernel,
        grid_spec=pltpu.PrefetchScalarGridSpec(
            num_scalar_prefetch=2,
            grid=(n_slots // tm,),
            in_specs=[pl.BlockSpec((tm, w), lambda i, te, tv: (i, 0)), wspec(wg), wspec(wu), wspec(wd)],
            out_specs=pl.BlockSpec((tm, w), lambda i, te, tv: (i, 0))),
        out_shape=jax.ShapeDtypeStruct((n_slots, w), xs.dtype),
        compiler_params=pltpu.CompilerParams(dimension_semantics=("arbitrary",), vmem_limit_bytes=VMEM_LIMIT),
        name="experts",
    )(tile_expert, tile_valid, xs, wg, wu, wd)


def _combine_kernel(dest_ref, x1_ref, w0_ref, w1_ref, ys_ref, out_ref, buf, sem, *, n_tokens):
    tm, d = x1_ref.shape
    base = pl.program_id(0) * tm

    def body(r, carry):
        for k in range(2):
            _row_copy(ys_ref, dest_ref[k * n_tokens + base + r], buf.at[k], r, sem).start()
        return carry

    lax.fori_loop(0, tm, body, 0)
    for k in range(2):
        pltpu.make_async_copy(ys_ref.at[pl.ds(0, tm)], buf.at[k], sem).wait()
    reps = d // LANES
    out_ref[...] = (x1_ref[...]
                    + _lane_tile(w0_ref[...], reps) * _unpack_rows(buf[0])
                    + _lane_tile(w1_ref[...], reps) * _unpack_rows(buf[1]))


def _combine(dest, x1, w0, w1, ys):
    t, d = x1.shape
    tm = TM_COMBINE
    row = lambda width: pl.BlockSpec((tm, width), lambda i, dref: (i, 0))
    return pl.pallas_call(
        functools.partial(_combine_kernel, n_tokens=t),
        grid_spec=pltpu.PrefetchScalarGridSpec(
            num_scalar_prefetch=1,
            grid=(t // tm,),
            in_specs=[row(d), row(LANES), row(LANES), pl.BlockSpec(memory_space=pl.ANY)],
            out_specs=row(d),
            scratch_shapes=[pltpu.VMEM((2, tm, ys.shape[1]), ys.dtype), pltpu.SemaphoreType.DMA(())]),
        out_shape=jax.ShapeDtypeStruct((t, d), x1.dtype),
        compiler_params=pltpu.CompilerParams(dimension_semantics=("arbitrary",), vmem_limit_bytes=VMEM_LIMIT),
        name="combine",
    )(dest, x1, w0, w1, ys)


def _upper_tri(n):
    return jnp.where(jnp.arange(n)[:, None] <= jnp.arange(n)[None, :], 1.0, 0.0).astype(BF16)


def _split_bf16(w):
    hi = w.astype(BF16)
    lo = (w - hi.astype(F32)).astype(BF16)
    return hi, lo


def _layer(x, norm_mix_g, w_in, b_forget, q_norm_g, k_norm_g, ssm_A_re, ssm_A_im, ssm_log_dt, ssm_B_re, ssm_B_im,
           ssm_C_re, ssm_C_im, ssm_D, w_glu, w_proj_attn, w_proj_ssm, w_out, norm_ffn_g, w_router_group,
           b_router_group, w_router_expert, b_router_expert, w_expert_gate, w_expert_up, w_expert_down):
    b, s, d = x.shape
    t = b * s
    da = N_HEADS * HEAD_DIM
    dssm = ssm_D.shape[0]
    nk = s // SUPER
    assert s % SUPER == 0 and nk % SUBLANES == 0 and nk & (nk - 1) == 0 and s % TQ_ATTN == 0 and t % TM_PROJ == 0
    x2 = x.reshape(t, d)

    o_f = 3 * da
    o_u = o_f + N_HEADS
    o_g = o_u + dssm
    w_main = jnp.concatenate([w_in[:, :o_f], w_in[:, o_u:o_g]], axis=1).astype(BF16)
    wf_t = w_in[:, o_f:o_u].T.astype(BF16)
    w_gates = w_in[:, o_g:].astype(BF16)
    qg = (jnp.tile(q_norm_g, N_HEADS) * HEAD_DIM ** -0.5)[None, :]
    kg = jnp.tile(k_norm_g, N_HEADS)[None, :]
    head_of = jnp.arange(da) // HEAD_DIM
    bd = jnp.where(head_of[:, None] == head_of[None, :], 1.0 / HEAD_DIM, 0.0).astype(BF16)

    q, k, v, u, kb = _proj(x2, norm_mix_g[None, :], w_main, wf_t, b_forget[:, None], qg, kg, bd,
                           _upper_tri(TM_PROJ), seq=s)
    y_attn = _attention(q.reshape(b, s, da), k.reshape(b, s, da), v.reshape(b, s, da), kb)

    win, tw, coef = _s5_params(ssm_A_re, ssm_A_im, ssm_log_dt, ssm_B_re, ssm_B_im, ssm_C_re, ssm_C_im, s // SUPER)
    y_ssm = _s5(u.reshape(b, s, dssm), win, tw, coef)

    wr = jnp.concatenate([w_router_expert.T, w_router_group.T,
                          jnp.zeros((ROUTER_ROWS - N_EXPERTS - N_EXPERT_GROUPS, d), F32)], axis=0)
    br = jnp.concatenate([b_router_expert, b_router_group,
                          jnp.full((SUBLANES - N_EXPERT_GROUPS,), NEG_INF, F32),
                          jnp.zeros((ROUTER_ROWS - N_EXPERTS - SUBLANES,), F32)])[:, None]
    x1, xp, ids, wts = _mix(x2, y_attn.reshape(t, da), y_ssm.reshape(t, dssm), u, norm_mix_g[None, :], w_gates,
                            ssm_D[None, :], w_glu.astype(BF16), w_proj_attn.astype(BF16), w_proj_ssm.astype(BF16),
                            w_out.astype(BF16), norm_ffn_g[None, :], *_split_bf16(wr), br)

    n_tiles = (2 * t) // TM_EXPERT + N_EXPERTS
    dest, tile_expert, tile_valid = _meta(ids[:2], _upper_tri(SORT_CHUNK), n_tiles)
    xs = _dispatch(dest, xp, n_tiles * TM_EXPERT)
    ys = _experts(tile_expert, tile_valid, xs, w_expert_gate.astype(BF16), w_expert_up.astype(BF16),
                  w_expert_down.astype(BF16))
    w0 = jnp.broadcast_to(wts[0][:, None], (t, LANES))
    w1 = jnp.broadcast_to(wts[1][:, None], (t, LANES))
    out = _combine(dest, x1, w0, w1, ys)
    return out.reshape(b, s, d)


def kernel(x, norm_mix_g, w_in, b_forget, q_norm_g, k_norm_g, ssm_A_re, ssm_A_im, ssm_log_dt, ssm_B_re, ssm_B_im,
           ssm_C_re, ssm_C_im, ssm_D, w_glu, w_proj_attn, w_proj_ssm, w_out, norm_ffn_g, w_router_group,
           b_router_group, w_router_expert, b_router_expert, w_expert_gate, w_expert_up, w_expert_down):
    layer_params = (norm_mix_g, w_in, b_forget, q_norm_g, k_norm_g, ssm_A_re, ssm_A_im, ssm_log_dt, ssm_B_re,
                    ssm_B_im, ssm_C_re, ssm_C_im, ssm_D, w_glu, w_proj_attn, w_proj_ssm, w_out, norm_ffn_g,
                    w_router_group, b_router_group, w_router_expert, b_router_expert, w_expert_gate, w_expert_up,
                    w_expert_down)
    for layer in range(norm_mix_g.shape[0]):
        x = _layer(x, *[p[layer] for p in layer_params])
    return x
```

```python
import functools
import math

import jax
import jax.numpy as jnp
from jax import lax
from jax.experimental import pallas as pl
from jax.experimental.pallas import tpu as pltpu

F32 = jnp.float32
BF16 = jnp.bfloat16
I32 = jnp.int32

LANES = 128
SUBLANES = 8
MXU_DIM = 256

N_HEADS = 8
HEAD_DIM = 64
SSM_GROUP = 16
SSM_STATE = 64
N_EXPERT_GROUPS = 4
EXPERTS_PER_GROUP = 8
N_EXPERTS = N_EXPERT_GROUPS * EXPERTS_PER_GROUP
EPS = 1e-6
NEG_INF = -1e30
LOG2E = math.log2(math.e)
PLAIN_SOFTMAX_MAX_LOG2 = 60.0
BIAS_K_LANE = HEAD_DIM
BIAS_Q_LANE = HEAD_DIM + 3

CHUNK = MXU_DIM // SSM_GROUP
CHUNKS_PER_SUPER = 8
SUPER = CHUNK * CHUNKS_PER_SUPER
GROUPS_PER_BLOCK = LANES // SSM_GROUP

TM_PROJ = 512
TQ_ATTN = 1024
TS_ATTN = 512
TM_MIX = 512
TM_EXPERT = 256
TM_DISPATCH = 512
TM_COMBINE = 512
SORT_CHUNK = 512
ROUTER_ROWS = 48

VMEM_LIMIT = 48 * 1024 * 1024


def _nt_dot(a, b):
    return lax.dot_general(a, b, (((1,), (1,)), ((), ())), preferred_element_type=F32)


def _dot(a, b):
    return jnp.dot(a, b, preferred_element_type=F32)


def _lane_tile(x, n):
    return x if n == 1 else jnp.concatenate([x] * n, axis=1)


def _rmsnorm_rows(x, g):
    ms = jnp.mean(x * x, axis=-1, keepdims=True)
    return x * lax.rsqrt(ms + EPS) * g


def _sigmoid(x):
    return 1.0 / (1.0 + jnp.exp(-x))


def _split3(x):
    hi = x.astype(BF16)
    r1 = x - hi.astype(F32)
    mid = r1.astype(BF16)
    lo = (r1 - mid.astype(F32)).astype(BF16)
    return [hi, mid, lo]


def _expand_heads(z):
    lane = lax.broadcasted_iota(I32, (z.shape[0], LANES), 1)
    blocks = []
    for p in range(z.shape[1] // LANES):
        blk = z[:, p * LANES:(p + 1) * LANES]
        blocks.append(jnp.where(lane < HEAD_DIM, blk, 0.0))
        blocks.append(jnp.where(lane < HEAD_DIM, pltpu.roll(blk, HEAD_DIM, axis=1), 0.0))
    return jnp.concatenate(blocks, axis=1)


def _proj_kernel(x_ref, g_ref, w_ref, wf_ref, bf_ref, qg_ref, kg_ref, bd_ref, ltri_ref, place_ref, ones_ref,
                 shift_ref, q_ref, k_ref, v_ref, u_ref, carry_ref, *, tiles_per_seq):
    i = pl.program_id(0)

    @pl.when(i % tiles_per_seq == 0)
    def _():
        carry_ref[...] = jnp.zeros_like(carry_ref)

    tm = x_ref.shape[0]
    da = N_HEADS * HEAD_DIM
    hw = N_HEADS * LANES
    xb = _rmsnorm_rows(x_ref[...], g_ref[...]).astype(BF16)
    z = _dot(xb, w_ref[...])
    zq, zk = z[:, :da], z[:, da:2 * da]
    bd = bd_ref[...]
    msq = _dot((zq * zq).astype(BF16), bd)
    msk = _dot((zk * zk).astype(BF16), bd)
    qn = zq * lax.rsqrt(msq + EPS) * qg_ref[...]
    kn = zk * lax.rsqrt(msk + EPS) * kg_ref[...]
    u_ref[...] = z[:, 3 * da:]

    a = _dot(xb, wf_ref[...]) + bf_ref[...]
    logf = jnp.minimum(a, 0.0) - jnp.log(1.0 + jnp.exp(-jnp.abs(a)))
    cs = _dot(ltri_ref[...], jnp.concatenate(_split3(logf), axis=1))
    cum = cs[:, :LANES] + cs[:, LANES:2 * LANES] + cs[:, 2 * LANES:] + carry_ref[0:1, :]
    carry_ref[...] = jnp.broadcast_to(cum[tm - 1:tm, :], carry_ref.shape)
    q_bias = (cum - shift_ref[...]) * LOG2E
    k_bias = cum * (-LOG2E)
    pieces = jnp.concatenate(_split3(q_bias) + _split3(k_bias), axis=1)
    extras = _dot(pieces, place_ref[...])
    q_ref[...] = (_expand_heads(qn) + extras[:, :hw] + ones_ref[0:1, :]).astype(BF16)
    k_ref[...] = (_expand_heads(kn) + extras[:, hw:] + ones_ref[1:2, :]).astype(BF16)
    v_ref[...] = (_expand_heads(z[:, 2 * da:3 * da]) + ones_ref[2:3, :]).astype(BF16)


def _proj(x2, norm_g, w_main, wf, b_f, qg, kg, bd, ltri, place, ones, shift, *, seq):
    t, d = x2.shape
    tm = TM_PROJ
    hw = N_HEADS * LANES
    du = w_main.shape[1] - 3 * N_HEADS * HEAD_DIM
    tiles_per_seq = seq // tm
    full = lambda a: pl.BlockSpec(a.shape, lambda i: (0,) * a.ndim)
    row = lambda width: pl.BlockSpec((tm, width), lambda i: (i, 0))
    consts = (norm_g, w_main, wf, b_f, qg, kg, bd, ltri, place, ones, shift)
    return pl.pallas_call(
        functools.partial(_proj_kernel, tiles_per_seq=tiles_per_seq),
        grid=(t // tm,),
        in_specs=[row(d)] + [full(a) for a in consts],
        out_specs=[row(hw), row(hw), row(hw), row(du)],
        out_shape=[jax.ShapeDtypeStruct((t, hw), BF16)] * 3 + [jax.ShapeDtypeStruct((t, du), F32)],
        scratch_shapes=[pltpu.VMEM((SUBLANES, LANES), F32)],
        compiler_params=pltpu.CompilerParams(dimension_semantics=("arbitrary",), vmem_limit_bytes=VMEM_LIMIT),
        name="proj",
    )(x2, *consts)


def _attn_kernel(q_ref, k_ref, v_ref, o_ref, acc_scr, m_scr, *, ts, safe):
    i = pl.program_id(2)
    nsub = q_ref.shape[0] // ts
    acc_scr[...] = jnp.zeros(acc_scr.shape, F32)
    if safe:
        m_scr[...] = jnp.full(m_scr.shape, NEG_INF, F32)

    def block(sub, j, masked):
        off = pl.multiple_of(j * ts, ts)
        s = _nt_dot(q_ref[sub * ts:(sub + 1) * ts, :], k_ref[pl.ds(off, ts), :])
        if masked:
            rows = lax.broadcasted_iota(I32, (ts, ts), 0)
            cols = lax.broadcasted_iota(I32, (ts, ts), 1)
            s = jnp.where(cols <= rows, s, NEG_INF)
        vblk = v_ref[pl.ds(off, ts), :]
        if safe:
            m_prev = m_scr[sub]
            m_new = jnp.maximum(m_prev, jnp.max(s, axis=1, keepdims=True))
            p = jnp.exp2(s - _lane_tile(m_new, ts // LANES)).astype(BF16)
            acc_scr[sub] = jnp.exp2(m_prev - m_new) * acc_scr[sub] + _dot(p, vblk)
            m_scr[sub] = m_new
        else:
            acc_scr[sub] += _dot(jnp.exp2(s).astype(BF16), vblk)

    def body(j, carry):
        for sub in range(nsub):
            block(sub, j, False)
        return carry

    lax.fori_loop(0, nsub * i, body, 0)
    for jj in range(nsub):
        for sub in range(jj, nsub):
            block(sub, nsub * i + jj, sub == jj)
    for sub in range(nsub):
        acc = acc_scr[sub]
        o_ref[sub * ts:(sub + 1) * ts, :] = (acc / acc[:, HEAD_DIM:HEAD_DIM + 1]).astype(o_ref.dtype)


def _attention(q, k, v, plain_ok):
    b, s, hw = q.shape
    tq, ts = TQ_ATTN, TS_ATTN

    def call(safe):
        return pl.pallas_call(
            functools.partial(_attn_kernel, ts=ts, safe=safe),
            grid=(b, hw // LANES, s // tq),
            in_specs=[pl.BlockSpec((None, tq, LANES), lambda bi, h, i: (bi, i, h)),
                      pl.BlockSpec((None, s, LANES), lambda bi, h, i: (bi, 0, h)),
                      pl.BlockSpec((None, s, LANES), lambda bi, h, i: (bi, 0, h))],
            out_specs=pl.BlockSpec((None, tq, LANES), lambda bi, h, i: (bi, i, h)),
            out_shape=jax.ShapeDtypeStruct((b, s, hw), BF16),
            scratch_shapes=[pltpu.VMEM((tq // ts, ts, LANES), F32)] * 2,
            compiler_params=pltpu.CompilerParams(
                dimension_semantics=("arbitrary", "arbitrary", "arbitrary"), vmem_limit_bytes=VMEM_LIMIT),
            name="attn_safe" if safe else "attn",
        )(q, k, v)

    return lax.cond(plain_ok, lambda: call(False), lambda: call(True))


def _window_select(pieces, first_group):
    window = lax.broadcasted_iota(I32, pieces[0].shape, 1) // SSM_GROUP
    out = pieces[0]
    for m in range(1, GROUPS_PER_BLOCK):
        out = jnp.where(window == (first_group + m) % GROUPS_PER_BLOCK, pieces[m], out)
    return out


def _s5_kernel(u_ref, win_ref, tw_ref, coef_ref, y_ref, ub_scr, yb_scr, *, nk):
    nj = CHUNKS_PER_SUPER
    gpb = GROUPS_PER_BLOCK
    w = CHUNK * SSM_GROUP
    half = w // 2
    nkk = nk // SUBLANES
    row_stride = SUPER

    def gather_body(it, carry):
        j, kk = it // nkk, it % nkk
        for sp in range(CHUNK // gpb):
            rot = []
            for m in range(gpb):
                row0 = j * CHUNK + sp * gpb + m + kk * (SUBLANES * row_stride)
                src = u_ref[pl.ds(row0, SUBLANES, stride=row_stride), :]
                rot.append(src if m == 0 else pltpu.roll(src, m * SSM_GROUP, axis=1))
            for gl in range(gpb):
                ub_scr[gl, j, pl.ds(kk * SUBLANES, SUBLANES), sp * LANES:(sp + 1) * LANES] = _window_select(rot, gl)
        return carry

    lax.fori_loop(0, nj * nkk, gather_body, 0)

    def swap(val):
        return jnp.concatenate([val[:, half:], val[:, :half]], axis=1)

    def group_body(gl, carry):
        def cmul(val, idx):
            return coef_ref[gl, idx:idx + 1, :] * val + coef_ref[gl, idx + 1:idx + 2, :] * swap(val)

        u = ub_scr[gl].reshape(nj * nk, w).astype(BF16)
        s1 = _dot(u, win_ref[gl]).reshape(nj, nk, w)
        e = jnp.zeros((nk, w), F32)
        local = []
        for j in range(nj):
            local.append(e)
            e = cmul(e, 0) + s1[j]
        kidx = lax.broadcasted_iota(I32, (nk, w), 0)
        x = e
        step, d = 0, 1
        while d < nk:
            shifted = jnp.where(kidx >= d, pltpu.roll(x, d, axis=0), 0.0)
            x = x + cmul(shifted, 2 + 2 * nj + 2 * step)
            step, d = step + 1, d * 2
        x_start = jnp.where(kidx >= 1, pltpu.roll(x, 1, axis=0), 0.0)
        starts = [local[j] + cmul(x_start, 2 + 2 * j) for j in range(nj)]
        p = jnp.concatenate([st[:, :half] for st in starts], axis=0)
        hi = p.astype(BF16)
        lo = (p - hi.astype(F32)).astype(BF16)
        y = _dot(jnp.concatenate([u, hi, lo], axis=1), tw_ref[gl])
        yb_scr[gl] = y.reshape(nj, nk, w)
        return carry

    lax.fori_loop(0, gpb, group_body, 0)

    def scatter_body(it, carry):
        j, kk = it // nkk, it % nkk
        for tp in range(CHUNK // gpb):
            src = [yb_scr[gl, j, pl.ds(kk * SUBLANES, SUBLANES), tp * LANES:(tp + 1) * LANES] for gl in range(gpb)]
            for m in range(gpb):
                by_window = [src[(wi - m) % gpb] for wi in range(gpb)]
                window = lax.broadcasted_iota(I32, by_window[0].shape, 1) // SSM_GROUP
                merged = by_window[0]
                for wi in range(1, gpb):
                    merged = jnp.where(window == wi, by_window[wi], merged)
                nat = merged if m == 0 else pltpu.roll(merged, LANES - m * SSM_GROUP, axis=1)
                row0 = j * CHUNK + tp * gpb + m + kk * (SUBLANES * row_stride)
                y_ref[pl.ds(row0, SUBLANES, stride=row_stride), :] = nat
        return carry

    lax.fori_loop(0, nj * nkk, scatter_body, 0)


def _s5_params(a_re, a_im, log_dt, b_re, b_im, c_re, c_im, nk):
    g, p = a_re.shape
    c = SSM_GROUP
    hp = lax.Precision.HIGHEST
    dt = jnp.exp(log_dt)[:, None]
    adt_r, adt_i = a_re * dt, a_im * dt

    def lam_pow(n):
        nf = jnp.asarray(n, F32)[None, :, None]
        mag = jnp.exp(adt_r[:, None, :] * nf)
        ang = adt_i[:, None, :] * nf
        return mag * jnp.cos(ang), mag * jnp.sin(ang)

    lr, li = lam_pow(jnp.arange(CHUNK + 1))
    den = a_re * a_re + a_im * a_im
    nr, ni = lr[:, 1] - 1.0, li[:, 1]
    qr = (nr * a_re + ni * a_im) / den
    qi = (ni * a_re - nr * a_im) / den
    bbr = qr[..., None] * b_re - qi[..., None] * b_im
    bbi = qr[..., None] * b_im + qi[..., None] * b_re

    mr = lr[:, :CHUNK, :, None] * bbr[:, None] - li[:, :CHUNK, :, None] * bbi[:, None]
    mi = lr[:, :CHUNK, :, None] * bbi[:, None] + li[:, :CHUNK, :, None] * bbr[:, None]
    kern = (jnp.einsum('gcp,gtpd->gtcd', c_re, mr, precision=hp)
            - jnp.einsum('gcp,gtpd->gtcd', c_im, mi, precision=hp))
    toep = jnp.stack([jnp.pad(kern[:, :CHUNK - s], ((0, 0), (s, 0), (0, 0), (0, 0))) for s in range(CHUNK)],
                     axis=1)
    toep = toep.transpose(0, 1, 4, 2, 3).reshape(g, CHUNK * c, CHUNK * c)
    clr = c_re[:, None] * lr[:, 1:, None, :] - c_im[:, None] * li[:, 1:, None, :]
    cli = c_re[:, None] * li[:, 1:, None, :] + c_im[:, None] * lr[:, 1:, None, :]
    w_re = clr.transpose(0, 3, 1, 2).reshape(g, p, CHUNK * c)
    w_im = (-cli).transpose(0, 3, 1, 2).reshape(g, p, CHUNK * c)
    wout = jnp.concatenate([w_re, w_im], axis=1)
    pr, pi = lr[:, :CHUNK][:, ::-1], li[:, :CHUNK][:, ::-1]
    ir = pr[..., None] * bbr[:, None] - pi[..., None] * bbi[:, None]
    ii = pr[..., None] * bbi[:, None] + pi[..., None] * bbr[:, None]
    ir = ir.transpose(0, 1, 3, 2).reshape(g, CHUNK * c, p)
    ii = ii.transpose(0, 1, 3, 2).reshape(g, CHUNK * c, p)
    win = jnp.concatenate([ir, ii, ii, ir], axis=2)

    def to_lane_order(arr, axis):
        gpb = GROUPS_PER_BLOCK
        shp = arr.shape
        split = arr.reshape(shp[:axis] + (CHUNK // gpb, gpb, c) + shp[axis + 1:])
        rolled = jnp.stack([jnp.roll(split[r::gpb], r, axis=axis + 1) for r in range(gpb)], axis=1)
        return rolled.reshape(shp)

    win = to_lane_order(win, 1)
    tw = to_lane_order(jnp.concatenate([to_lane_order(toep, 1), wout, wout], axis=1), 2)

    n_steps = max(nk.bit_length() - 1, 0)
    powers = [CHUNK] + [CHUNK * j for j in range(CHUNKS_PER_SUPER)] + [SUPER * (1 << i) for i in range(n_steps)]
    ar, ai = lam_pow(jnp.array(powers, dtype=jnp.int32))
    c1 = jnp.concatenate([ar, ar, ar, ar], axis=2)
    c2 = jnp.concatenate([-ai, ai, ai, -ai], axis=2)
    coef = jnp.stack([c1, c2], axis=2).reshape(g, 2 * len(powers), 4 * p)
    pad = (-coef.shape[1]) % SUBLANES
    coef = jnp.pad(coef, ((0, 0), (0, pad), (0, 0)))
    return win.astype(BF16), tw.astype(BF16), coef


def _s5(u, win, tw, coef):
    b, s, dssm = u.shape
    nk = s // SUPER
    nj = CHUNKS_PER_SUPER
    w = CHUNK * SSM_GROUP
    gpb = GROUPS_PER_BLOCK
    nblk = dssm // LANES
    wspec = lambda a: pl.BlockSpec((gpb,) + a.shape[1:], lambda bi, li: (li, 0, 0))
    return pl.pallas_call(
        functools.partial(_s5_kernel, nk=nk),
        grid=(b, nblk),
        in_specs=[pl.BlockSpec((None, s, LANES), lambda bi, li: (bi, 0, li)), wspec(win), wspec(tw), wspec(coef)],
        out_specs=pl.BlockSpec((None, s, LANES), lambda bi, li: (bi, 0, li)),
        out_shape=jax.ShapeDtypeStruct((b, s, dssm), F32),
        scratch_shapes=[pltpu.VMEM((gpb, nj, nk, w), F32)] * 2,
        compiler_params=pltpu.CompilerParams(dimension_semantics=("arbitrary", "arbitrary"),
                                             vmem_limit_bytes=VMEM_LIMIT),
        name="s5",
    )(u, win, tw, coef)


def _route(logits):
    e_all = logits[0:N_EXPERTS]
    gl = logits[N_EXPERTS:N_EXPERTS + SUBLANES]
    tm = logits.shape[1]
    ridx = lax.broadcasted_iota(I32, (SUBLANES, tm), 0)
    ge = jnp.exp(gl - jnp.max(gl, axis=0, keepdims=True))
    gp = ge / jnp.sum(ge, axis=0, keepdims=True)
    g_top = jnp.max(gp, axis=0, keepdims=True)
    g_sel = jnp.min(jnp.where(gp == g_top, ridx, SUBLANES), axis=0, keepdims=True)
    e_in = e_all[(N_EXPERT_GROUPS - 1) * SUBLANES:]
    for gi in range(N_EXPERT_GROUPS - 2, -1, -1):
        e_in = jnp.where(g_sel == gi, e_all[gi * SUBLANES:(gi + 1) * SUBLANES], e_in)
    ee = jnp.exp(e_in - jnp.max(e_in, axis=0, keepdims=True))
    ep = ee / jnp.sum(ee, axis=0, keepdims=True)
    v1 = jnp.max(ep, axis=0, keepdims=True)
    i1 = jnp.min(jnp.where(ep == v1, ridx, SUBLANES), axis=0, keepdims=True)
    ep2 = jnp.where(ridx == i1, -1.0, ep)
    v2 = jnp.max(ep2, axis=0, keepdims=True)
    i2 = jnp.min(jnp.where(ep2 == v2, ridx, SUBLANES), axis=0, keepdims=True)
    den = v1 + v2
    w1 = g_top * v1 / den
    w2 = g_top * v2 / den
    e1 = g_sel * EXPERTS_PER_GROUP + i1
    e2 = g_sel * EXPERTS_PER_GROUP + i2
    ids = jnp.where(ridx == 0, e1, jnp.where(ridx == 1, e2, 0))
    wts = jnp.where(ridx == 0, w1, jnp.where(ridx == 1, w2, 0.0))
    return ids, wts


def _mix_kernel(x_ref, ya_ref, ys_ref, u_ref, g1_ref, wgate_ref, dsk_ref, wglu_ref, wpa_ref, wps_ref,
                wout_ref, g2_ref, wrh_ref, wrl_ref, br_ref, x1_ref, xp_ref, ids_ref, wts_ref):
    d = x_ref.shape[1]
    x = x_ref[...]
    xb = _rmsnorm_rows(x, g1_ref[...]).astype(BF16)
    gates = _dot(xb, wgate_ref[...])
    y = ys_ref[...].astype(F32) + dsk_ref[...] * u_ref[...].astype(F32)
    y = y * (0.5 * (1.0 + jnp.tanh(math.sqrt(2.0 / math.pi) * (y + 0.044715 * (y * y * y)))))
    y = y * _sigmoid(_dot(y.astype(BF16), wglu_ref[...]))
    mixed = (_sigmoid(gates[:, :d]) * _dot(ya_ref[...], wpa_ref[...])
             + _sigmoid(gates[:, d:]) * _dot(y.astype(BF16), wps_ref[...]))
    x1 = x + _dot(mixed.astype(BF16), wout_ref[...])
    x1_ref[...] = x1
    xn = _rmsnorm_rows(x1, g2_ref[...])
    xp_ref[...] = pltpu.pack_elementwise([xn[:, :d // 2], xn[:, d // 2:]], packed_dtype=BF16)
    xh = xn.astype(BF16)
    xl = (xn - xh.astype(F32)).astype(BF16)
    wrh = wrh_ref[...]
    logits = _nt_dot(wrh, xh) + _nt_dot(wrl_ref[...], xh) + _nt_dot(wrh, xl) + br_ref[...]
    ids, wts = _route(logits)
    ids_ref[...] = ids
    wts_ref[...] = wts


def _mix(x2, ya, ys, u, g1, wgate, dsk, wglu, wpa, wps, wout, g2, wrh, wrl, br):
    t, d = x2.shape
    tm = TM_MIX
    full = lambda a: pl.BlockSpec(a.shape, lambda i: (0,) * a.ndim)
    row = lambda width: pl.BlockSpec((tm, width), lambda i: (i, 0))
    col = pl.BlockSpec((SUBLANES, tm), lambda i: (0, i))
    return pl.pallas_call(
        _mix_kernel,
        grid=(t // tm,),
        in_specs=[row(d), row(ya.shape[1]), row(ys.shape[1]), row(u.shape[1]), full(g1), full(wgate), full(dsk),
                  full(wglu), full(wpa),
                  full(wps), full(wout), full(g2), full(wrh), full(wrl), full(br)],
        out_specs=[row(d), row(d // 2), col, col],
        out_shape=[jax.ShapeDtypeStruct((t, d), F32), jax.ShapeDtypeStruct((t, d // 2), jnp.uint32),
                   jax.ShapeDtypeStruct((SUBLANES, t), I32), jax.ShapeDtypeStruct((SUBLANES, t), F32)],
        compiler_params=pltpu.CompilerParams(dimension_semantics=("arbitrary",), vmem_limit_bytes=VMEM_LIMIT),
        name="mix",
    )(x2, ya, ys, u, g1, wgate, dsk, wglu, wpa, wps, wout, g2, wrh, wrl, br)


def _meta_kernel(ids_ref, tri_ref, dest_ref, tile_ref, *, tile_rows):
    nk, nc, c = ids_ref.shape
    ne = N_EXPERTS
    erow = lax.broadcasted_iota(I32, (ne, c), 0)
    ones = jnp.ones((c, LANES), BF16)

    def onehot(k, ci):
        mask = erow == ids_ref[k, pl.ds(ci, 1), :]
        return mask, jnp.where(mask, 1.0, 0.0).astype(BF16)

    def count_body(n, acc):
        return acc + _dot(onehot(n // nc, n % nc)[1], ones)

    cnt = lax.fori_loop(0, nk * nc, count_body, jnp.zeros((ne, LANES), F32))
    ntiles = jnp.floor((cnt + (tile_rows - 1)) * (1.0 / tile_rows))
    lower = jnp.where(lax.broadcasted_iota(I32, (ne, ne), 1) < lax.broadcasted_iota(I32, (ne, ne), 0), 1.0, 0.0)
    start_tiles = _dot(lower.astype(BF16), ntiles.astype(BF16))
    base = start_tiles * tile_rows

    tri = tri_ref[...]

    def dest_body(n, carry):
        k, ci = n // nc, n % nc
        mask, oh = onehot(k, ci)
        prefix = _dot(oh, tri)
        slot = _lane_tile(base + carry, c // LANES) + prefix - 1.0
        dest = jnp.sum(jnp.where(mask, slot, 0.0), axis=0, keepdims=True)
        dest_ref[k, pl.ds(ci, 1), :] = dest.astype(I32)
        return carry + _dot(oh, ones)

    lax.fori_loop(0, nk * nc, dest_body, jnp.zeros((ne, LANES), F32))

    nt_lanes = tile_ref.shape[1]
    end_tiles = _lane_tile(start_tiles + ntiles, nt_lanes // LANES)
    tidx = lax.broadcasted_iota(I32, (ne, nt_lanes), 1).astype(F32)
    texp = jnp.sum(jnp.where(tidx >= end_tiles, 1.0, 0.0), axis=0, keepdims=True)
    valid = jnp.where(texp < ne, 1, 0)
    texp = jnp.minimum(texp, ne - 1.0).astype(I32)
    ridx = lax.broadcasted_iota(I32, tile_ref.shape, 0)
    tile_ref[...] = jnp.where(ridx == 0, texp, jnp.where(ridx == 1, valid, 0))


def _meta(ids2, tri, n_tiles):
    nk, t = ids2.shape
    c = SORT_CHUNK
    nt_lanes = pl.cdiv(n_tiles, LANES) * LANES
    ids3 = ids2.reshape(nk, t // c, c)
    dest, tile = pl.pallas_call(
        functools.partial(_meta_kernel, tile_rows=TM_EXPERT),
        out_shape=[jax.ShapeDtypeStruct(ids3.shape, I32), jax.ShapeDtypeStruct((SUBLANES, nt_lanes), I32)],
        compiler_params=pltpu.CompilerParams(vmem_limit_bytes=VMEM_LIMIT),
        name="meta",
    )(ids3, tri)
    return dest.reshape(nk * t), tile[0, :n_tiles], tile[1, :n_tiles]


def _row_copy(src_ref, src_row, dst_ref, dst_row, sem):
    return pltpu.make_async_copy(src_ref.at[pl.ds(src_row, 1)], dst_ref.at[pl.ds(dst_row, 1)], sem)


def _dispatch_kernel(dest_ref, xp_ref, xs_in_ref, xs_ref, sem, *, n_tokens):
    del xs_in_ref
    tm = xp_ref.shape[0]
    base = pl.program_id(0) * tm

    def body(r, carry):
        for k in range(2):
            _row_copy(xp_ref, r, xs_ref, dest_ref[k * n_tokens + base + r], sem).start()
        return carry

    lax.fori_loop(0, tm, body, 0)
    for k in range(2):
        pltpu.make_async_copy(xp_ref, xs_ref.at[pl.ds(0, tm)], sem).wait()


def _dispatch(dest, xp, n_slots):
    t, w = xp.shape
    tm = TM_DISPATCH
    xs0 = jnp.zeros((n_slots, w), xp.dtype)
    return pl.pallas_call(
        functools.partial(_dispatch_kernel, n_tokens=t),
        grid_spec=pltpu.PrefetchScalarGridSpec(
            num_scalar_prefetch=1,
            grid=(t // tm,),
            in_specs=[pl.BlockSpec((tm, w), lambda i, d: (i, 0)), pl.BlockSpec(memory_space=pl.ANY)],
            out_specs=pl.BlockSpec(memory_space=pl.ANY),
            scratch_shapes=[pltpu.SemaphoreType.DMA(())]),
        out_shape=jax.ShapeDtypeStruct((n_slots, w), xp.dtype),
        input_output_aliases={2: 0},
        compiler_params=pltpu.CompilerParams(dimension_semantics=("arbitrary",), has_side_effects=True),
        name="dispatch",
    )(dest, xp, xs0)


def _unpack_rows(packed):
    lo = pltpu.unpack_elementwise(packed, index=0, packed_dtype=BF16, unpacked_dtype=F32)
    hi = pltpu.unpack_elementwise(packed, index=1, packed_dtype=BF16, unpacked_dtype=F32)
    return jnp.concatenate([lo, hi], axis=1)


def _expert_kernel(te_ref, tv_ref, xs_ref, wg_ref, wu_ref, wd_ref, ys_ref):
    del te_ref
    i = pl.program_id(0)
    half = xs_ref.shape[1]

    @pl.when(tv_ref[i] > 0)
    def _():
        x = _unpack_rows(xs_ref[...]).astype(BF16)
        hg = _dot(x, wg_ref[...])
        hu = _dot(x, wu_ref[...])
        h = (hg * _sigmoid(hg) * hu).astype(BF16)
        y = _dot(h, wd_ref[...])
        ys_ref[...] = pltpu.pack_elementwise([y[:, :half], y[:, half:]], packed_dtype=BF16)

    @pl.when(tv_ref[i] == 0)
    def _():
        zero = jnp.zeros(ys_ref.shape, F32)
        ys_ref[...] = pltpu.pack_elementwise([zero, zero], packed_dtype=BF16)


def _experts(tile_expert, tile_valid, xs, wg, wu, wd):
    n_slots, w = xs.shape
    tm = TM_EXPERT
    wspec = lambda a: pl.BlockSpec((None,) + a.shape[1:], lambda i, te, tv: (te[i], 0, 0))
    return pl.pallas_call(
        _expert_kernel,
        grid_spec=pltpu.PrefetchScalarGridSpec(
            num_scalar_prefetch=2,
            grid=(n_slots // tm,),
            in_specs=[pl.BlockSpec((tm, w), lambda i, te, tv: (i, 0)), wspec(wg), wspec(wu), wspec(wd)],
            out_specs=pl.BlockSpec((tm, w), lambda i, te, tv: (i, 0))),
        out_shape=jax.ShapeDtypeStruct((n_slots, w), xs.dtype),
        compiler_params=pltpu.CompilerParams(dimension_semantics=("arbitrary",), vmem_limit_bytes=VMEM_LIMIT),
        name="experts",
    )(tile_expert, tile_valid, xs, wg, wu, wd)


def _combine_kernel(dest_ref, x1_ref, w0_ref, w1_ref, ys_ref, out_ref, buf, sem, *, n_tokens):
    tm, d = x1_ref.shape
    base = pl.program_id(0) * tm

    def body(r, carry):
        for k in range(2):
            _row_copy(ys_ref, dest_ref[k * n_tokens + base + r], buf.at[k], r, sem).start()
        return carry

    lax.fori_loop(0, tm, body, 0)
    for k in range(2):
        pltpu.make_async_copy(ys_ref.at[pl.ds(0, tm)], buf.at[k], sem).wait()
    reps = d // LANES
    out_ref[...] = (x1_ref[...]
                    + _lane_tile(w0_ref[...], reps) * _unpack_rows(buf[0])
                    + _lane_tile(w1_ref[...], reps) * _unpack_rows(buf[1]))


def _combine(dest, x1, w0, w1, ys):
    t, d = x1.shape
    tm = TM_COMBINE
    row = lambda width: pl.BlockSpec((tm, width), lambda i, dref: (i, 0))
    return pl.pallas_call(
        functools.partial(_combine_kernel, n_tokens=t),
        grid_spec=pltpu.PrefetchScalarGridSpec(
            num_scalar_prefetch=1,
            grid=(t // tm,),
            in_specs=[row(d), row(LANES), row(LANES), pl.BlockSpec(memory_space=pl.ANY)],
            out_specs=row(d),
            scratch_shapes=[pltpu.VMEM((2, tm, ys.shape[1]), ys.dtype), pltpu.SemaphoreType.DMA(())]),
        out_shape=jax.ShapeDtypeStruct((t, d), x1.dtype),
        compiler_params=pltpu.CompilerParams(dimension_semantics=("arbitrary",), vmem_limit_bytes=VMEM_LIMIT),
        name="combine",
    )(dest, x1, w0, w1, ys)


def _lower_tri(n):
    return jnp.where(jnp.arange(n)[:, None] >= jnp.arange(n)[None, :], 1.0, 0.0).astype(BF16)


def _upper_tri(n):
    return jnp.where(jnp.arange(n)[:, None] <= jnp.arange(n)[None, :], 1.0, 0.0).astype(BF16)


def _split_bf16(w):
    hi = w.astype(BF16)
    lo = (w - hi.astype(F32)).astype(BF16)
    return hi, lo


def _layer(x, norm_mix_g, w_in, b_forget, q_norm_g, k_norm_g, ssm_A_re, ssm_A_im, ssm_log_dt, ssm_B_re, ssm_B_im,
           ssm_C_re, ssm_C_im, ssm_D, w_glu, w_proj_attn, w_proj_ssm, w_out, norm_ffn_g, w_router_group,
           b_router_group, w_router_expert, b_router_expert, w_expert_gate, w_expert_up, w_expert_down):
    b, s, d = x.shape
    t = b * s
    da = N_HEADS * HEAD_DIM
    dssm = ssm_D.shape[0]
    nk = s // SUPER
    assert s % SUPER == 0 and nk % SUBLANES == 0 and nk & (nk - 1) == 0 and s % TQ_ATTN == 0 and t % TM_PROJ == 0
    x2 = x.reshape(t, d)

    o_f = 3 * da
    o_u = o_f + N_HEADS
    o_g = o_u + dssm
    w_main = jnp.concatenate([w_in[:, :o_f], w_in[:, o_u:o_g]], axis=1).astype(BF16)
    wf = jnp.pad(w_in[:, o_f:o_u], ((0, 0), (0, LANES - N_HEADS))).astype(BF16)
    bf = jnp.pad(b_forget, (0, LANES - N_HEADS))[None, :]
    w_gates = w_in[:, o_g:].astype(BF16)
    qg = (jnp.tile(q_norm_g, N_HEADS) * (HEAD_DIM ** -0.5 * LOG2E))[None, :]
    kg = jnp.tile(k_norm_g, N_HEADS)[None, :]
    head_of = jnp.arange(da) // HEAD_DIM
    bd = jnp.where(head_of[:, None] == head_of[None, :], 1.0 / HEAD_DIM, 0.0).astype(BF16)
    logit_bound = 1.02 * HEAD_DIM ** 0.5 * jnp.max(jnp.abs(q_norm_g)) * jnp.max(jnp.abs(k_norm_g))
    plain_ok = LOG2E * logit_bound <= PLAIN_SOFTMAX_MAX_LOG2
    shift = jnp.full((1, LANES), logit_bound, F32)
    hw = N_HEADS * LANES
    lane_in_head = jnp.arange(hw) % LANES
    ones = jnp.stack([
        (lane_in_head >= BIAS_K_LANE) & (lane_in_head < BIAS_K_LANE + 3),
        (lane_in_head >= BIAS_Q_LANE) & (lane_in_head < BIAS_Q_LANE + 3),
        lane_in_head == HEAD_DIM]
        + [jnp.zeros((hw,), bool)] * (SUBLANES - 3)).astype(F32)
    prow = jnp.arange(6 * LANES)
    kind, piece, head = prow // (3 * LANES), (prow // LANES) % 3, prow % LANES
    target = jnp.where(kind == 0, head * LANES + BIAS_Q_LANE + piece, hw + head * LANES + BIAS_K_LANE + piece)
    place = ((jnp.arange(2 * hw)[None, :] == target[:, None]) & (head < N_HEADS)[:, None]).astype(BF16)

    q, k, v, u = _proj(x2, norm_mix_g[None, :], w_main, wf, bf, qg, kg, bd, _lower_tri(TM_PROJ), place, ones,
                       shift, seq=s)
    y_attn = _attention(q.reshape(b, s, hw), k.reshape(b, s, hw), v.reshape(b, s, hw), plain_ok)
    w_pa = jnp.pad(w_proj_attn.reshape(N_HEADS, HEAD_DIM, d), ((0, 0), (0, LANES - HEAD_DIM), (0, 0)))
    w_pa = w_pa.reshape(hw, d).astype(BF16)

    win, tw, coef = _s5_params(ssm_A_re, ssm_A_im, ssm_log_dt, ssm_B_re, ssm_B_im, ssm_C_re, ssm_C_im, s // SUPER)
    y_ssm = _s5(u.reshape(b, s, dssm), win, tw, coef)

    wr = jnp.concatenate([w_router_expert.T, w_router_group.T,
                          jnp.zeros((ROUTER_ROWS - N_EXPERTS - N_EXPERT_GROUPS, d), F32)], axis=0)
    br = jnp.concatenate([b_router_expert, b_router_group,
                          jnp.full((SUBLANES - N_EXPERT_GROUPS,), NEG_INF, F32),
                          jnp.zeros((ROUTER_ROWS - N_EXPERTS - SUBLANES,), F32)])[:, None]
    x1, xp, ids, wts = _mix(x2, y_attn.reshape(t, hw), y_ssm.reshape(t, dssm), u, norm_mix_g[None, :], w_gates,
                            ssm_D[None, :], w_glu.astype(BF16), w_pa, w_proj_ssm.astype(BF16),
                            w_out.astype(BF16), norm_ffn_g[None, :], *_split_bf16(wr), br)

    n_tiles = (2 * t) // TM_EXPERT + N_EXPERTS
    dest, tile_expert, tile_valid = _meta(ids[:2], _upper_tri(SORT_CHUNK), n_tiles)
    xs = _dispatch(dest, xp, n_tiles * TM_EXPERT)
    ys = _experts(tile_expert, tile_valid, xs, w_expert_gate.astype(BF16), w_expert_up.astype(BF16),
                  w_expert_down.astype(BF16))
    w0 = jnp.broadcast_to(wts[0][:, None], (t, LANES))
    w1 = jnp.broadcast_to(wts[1][:, None], (t, LANES))
    out = _combine(dest, x1, w0, w1, ys)
    return out.reshape(b, s, d)


def kernel(x, norm_mix_g, w_in, b_forget, q_norm_g, k_norm_g, ssm_A_re, ssm_A_im, ssm_log_dt, ssm_B_re, ssm_B_im,
           ssm_C_re, ssm_C_im, ssm_D, w_glu, w_proj_attn, w_proj_ssm, w_out, norm_ffn_g, w_router_group,
           b_router_group, w_router_expert, b_router_expert, w_expert_gate, w_expert_up, w_expert_down):
    layer_params = (norm_mix_g, w_in, b_forget, q_norm_g, k_norm_g, ssm_A_re, ssm_A_im, ssm_log_dt, ssm_B_re,
                    ssm_B_im, ssm_C_re, ssm_C_im, ssm_D, w_glu, w_proj_attn, w_proj_ssm, w_out, norm_ffn_g,
                    w_router_group, b_router_group, w_router_expert, b_router_expert, w_expert_gate, w_expert_up,
                    w_expert_down)
    for layer in range(norm_mix_g.shape[0]):
        x = _layer(x, *[p[layer] for p in layer_params])
    return x
```

```python
import functools
import math

import jax
import jax.numpy as jnp
from jax import lax
from jax.experimental import pallas as pl
from jax.experimental.pallas import tpu as pltpu

F32 = jnp.float32
BF16 = jnp.bfloat16
I32 = jnp.int32

LANES = 128
SUBLANES = 8
MXU_DIM = 256

N_HEADS = 8
HEAD_DIM = 64
SSM_GROUP = 16
SSM_STATE = 64
N_EXPERT_GROUPS = 4
EXPERTS_PER_GROUP = 8
N_EXPERTS = N_EXPERT_GROUPS * EXPERTS_PER_GROUP
EPS = 1e-6
NEG_INF = -1e30
LOG2E = math.log2(math.e)
PLAIN_SOFTMAX_MAX_LOG2 = 60.0
BIAS_K_LANE = HEAD_DIM
BIAS_Q_LANE = HEAD_DIM + 3

CHUNK = MXU_DIM // SSM_GROUP
CHUNKS_PER_SUPER = 8
SUPER = CHUNK * CHUNKS_PER_SUPER
GROUPS_PER_BLOCK = LANES // SSM_GROUP

TM_PROJ = 512
TQ_ATTN = 1024
TS_ATTN = 512
TM_MIX = 512
TM_EXPERT = 256
TM_DISPATCH = 512
TM_COMBINE = 512
SORT_CHUNK = 512
ROUTER_ROWS = 48

VMEM_LIMIT = 48 * 1024 * 1024


def _nt_dot(a, b):
    return lax.dot_general(a, b, (((1,), (1,)), ((), ())), preferred_element_type=F32)


def _dot(a, b):
    return jnp.dot(a, b, preferred_element_type=F32)


def _lane_tile(x, n):
    return x if n == 1 else jnp.concatenate([x] * n, axis=1)


def _rmsnorm_rows(x, g):
    ms = jnp.mean(x * x, axis=-1, keepdims=True)
    return x * lax.rsqrt(ms + EPS) * g


def _sigmoid(x):
    return 1.0 / (1.0 + jnp.exp(-x))


def _split3(x):
    hi = x.astype(BF16)
    r1 = x - hi.astype(F32)
    mid = r1.astype(BF16)
    lo = (r1 - mid.astype(F32)).astype(BF16)
    return [hi, mid, lo]


def _expand_heads(z):
    lane = lax.broadcasted_iota(I32, (z.shape[0], LANES), 1)
    blocks = []
    for p in range(z.shape[1] // LANES):
        blk = z[:, p * LANES:(p + 1) * LANES]
        blocks.append(jnp.where(lane < HEAD_DIM, blk, 0.0))
        blocks.append(jnp.where(lane < HEAD_DIM, pltpu.roll(blk, HEAD_DIM, axis=1), 0.0))
    return jnp.concatenate(blocks, axis=1)


def _proj_kernel(x_ref, g_ref, w_ref, wf_ref, bf_ref, qg_ref, kg_ref, bd_ref, ltri_ref, place_ref, ones_ref,
                 shift_ref, q_ref, k_ref, v_ref, u_ref, carry_ref, *, tiles_per_seq):
    i = pl.program_id(0)

    @pl.when(i % tiles_per_seq == 0)
    def _():
        carry_ref[...] = jnp.zeros_like(carry_ref)

    tm = x_ref.shape[0]
    da = N_HEADS * HEAD_DIM
    hw = N_HEADS * LANES
    xb = _rmsnorm_rows(x_ref[...], g_ref[...]).astype(BF16)
    z = _dot(xb, w_ref[...])
    zq, zk = z[:, :da], z[:, da:2 * da]
    bd = bd_ref[...]
    msq = _dot((zq * zq).astype(BF16), bd)
    msk = _dot((zk * zk).astype(BF16), bd)
    qn = zq * lax.rsqrt(msq + EPS) * qg_ref[...]
    kn = zk * lax.rsqrt(msk + EPS) * kg_ref[...]
    u_ref[...] = z[:, 3 * da:]

    a = _dot(xb, wf_ref[...]) + bf_ref[...]
    logf = jnp.minimum(a, 0.0) - jnp.log(1.0 + jnp.exp(-jnp.abs(a)))
    cs = _dot(ltri_ref[...], jnp.concatenate(_split3(logf), axis=1))
    cum = cs[:, :LANES] + cs[:, LANES:2 * LANES] + cs[:, 2 * LANES:] + carry_ref[0:1, :]
    carry_ref[...] = jnp.broadcast_to(cum[tm - 1:tm, :], carry_ref.shape)
    q_bias = (cum - shift_ref[...]) * LOG2E
    k_bias = cum * (-LOG2E)
    pieces = jnp.concatenate(_split3(q_bias) + _split3(k_bias), axis=1)
    extras = _dot(pieces, place_ref[...])
    q_ref[...] = (_expand_heads(qn) + extras[:, :hw] + ones_ref[0:1, :]).astype(BF16)
    k_ref[...] = (_expand_heads(kn) + extras[:, hw:] + ones_ref[1:2, :]).astype(BF16)
    v_ref[...] = (_expand_heads(z[:, 2 * da:3 * da]) + ones_ref[2:3, :]).astype(BF16)


def _proj(x2, norm_g, w_main, wf, b_f, qg, kg, bd, ltri, place, ones, shift, *, seq):
    t, d = x2.shape
    tm = TM_PROJ
    hw = N_HEADS * LANES
    du = w_main.shape[1] - 3 * N_HEADS * HEAD_DIM
    tiles_per_seq = seq // tm
    full = lambda a: pl.BlockSpec(a.shape, lambda i: (0,) * a.ndim)
    row = lambda width: pl.BlockSpec((tm, width), lambda i: (i, 0))
    consts = (norm_g, w_main, wf, b_f, qg, kg, bd, ltri, place, ones, shift)
    return pl.pallas_call(
        functools.partial(_proj_kernel, tiles_per_seq=tiles_per_seq),
        grid=(t // tm,),
        in_specs=[row(d)] + [full(a) for a in consts],
        out_specs=[row(hw), row(hw), row(hw), row(du)],
        out_shape=[jax.ShapeDtypeStruct((t, hw), BF16)] * 3 + [jax.ShapeDtypeStruct((t, du), F32)],
        scratch_shapes=[pltpu.VMEM((SUBLANES, LANES), F32)],
        compiler_params=pltpu.CompilerParams(dimension_semantics=("arbitrary",), vmem_limit_bytes=VMEM_LIMIT),
        name="proj",
    )(x2, *consts)


def _attn_kernel(q_ref, k_ref, v_ref, o_ref, acc_scr, m_scr, *, ts, safe):
    i = pl.program_id(2)
    nsub = q_ref.shape[0] // ts
    acc_scr[...] = jnp.zeros(acc_scr.shape, F32)
    if safe:
        m_scr[...] = jnp.full(m_scr.shape, NEG_INF, F32)

    def block(sub, j, masked):
        off = pl.multiple_of(j * ts, ts)
        s = _nt_dot(q_ref[sub * ts:(sub + 1) * ts, :], k_ref[pl.ds(off, ts), :])
        if masked:
            rows = lax.broadcasted_iota(I32, (ts, ts), 0)
            cols = lax.broadcasted_iota(I32, (ts, ts), 1)
            s = jnp.where(cols <= rows, s, NEG_INF)
        vblk = v_ref[pl.ds(off, ts), :]
        if safe:
            m_prev = m_scr[sub]
            m_new = jnp.maximum(m_prev, jnp.max(s, axis=1, keepdims=True))
            p = jnp.exp2(s - _lane_tile(m_new, ts // LANES)).astype(BF16)
            acc_scr[sub] = jnp.exp2(m_prev - m_new) * acc_scr[sub] + _dot(p, vblk)
            m_scr[sub] = m_new
        else:
            acc_scr[sub] += _dot(jnp.exp2(s).astype(BF16), vblk)

    def body(j, carry):
        for sub in range(nsub):
            block(sub, j, False)
        return carry

    lax.fori_loop(0, nsub * i, body, 0)
    for jj in range(nsub):
        for sub in range(jj, nsub):
            block(sub, nsub * i + jj, sub == jj)
    for sub in range(nsub):
        acc = acc_scr[sub]
        o_ref[sub * ts:(sub + 1) * ts, :] = (acc / acc[:, HEAD_DIM:HEAD_DIM + 1]).astype(o_ref.dtype)


def _attention(q, k, v, plain_ok):
    b, s, hw = q.shape
    tq, ts = TQ_ATTN, TS_ATTN

    def call(safe):
        return pl.pallas_call(
            functools.partial(_attn_kernel, ts=ts, safe=safe),
            grid=(b, hw // LANES, s // tq),
            in_specs=[pl.BlockSpec((None, tq, LANES), lambda bi, h, i: (bi, i, h)),
                      pl.BlockSpec((None, s, LANES), lambda bi, h, i: (bi, 0, h)),
                      pl.BlockSpec((None, s, LANES), lambda bi, h, i: (bi, 0, h))],
            out_specs=pl.BlockSpec((None, tq, LANES), lambda bi, h, i: (bi, i, h)),
            out_shape=jax.ShapeDtypeStruct((b, s, hw), BF16),
            scratch_shapes=[pltpu.VMEM((tq // ts, ts, LANES), F32)] * 2,
            compiler_params=pltpu.CompilerParams(
                dimension_semantics=("arbitrary", "arbitrary", "arbitrary"), vmem_limit_bytes=VMEM_LIMIT),
            name="attn_safe" if safe else "attn",
        )(q, k, v)

    return lax.cond(plain_ok, lambda: call(False), lambda: call(True))


def _window_select(pieces, first_group):
    window = lax.broadcasted_iota(I32, pieces[0].shape, 1) // SSM_GROUP
    out = pieces[0]
    for m in range(1, GROUPS_PER_BLOCK):
        out = jnp.where(window == (first_group + m) % GROUPS_PER_BLOCK, pieces[m], out)
    return out


def _s5_kernel(u_ref, win_ref, tw_ref, coef_ref, y_ref, ub_scr, yb_scr, *, nk):
    nj = CHUNKS_PER_SUPER
    gpb = GROUPS_PER_BLOCK
    w = CHUNK * SSM_GROUP
    half = w // 2
    nkk = nk // SUBLANES
    row_stride = SUPER

    def gather_body(it, carry):
        j, kk = it // nkk, it % nkk
        for sp in range(CHUNK // gpb):
            rot = []
            for m in range(gpb):
                row0 = j * CHUNK + sp * gpb + m + kk * (SUBLANES * row_stride)
                src = u_ref[pl.ds(row0, SUBLANES, stride=row_stride), :]
                rot.append(src if m == 0 else pltpu.roll(src, m * SSM_GROUP, axis=1))
            for gl in range(gpb):
                ub_scr[gl, j, pl.ds(kk * SUBLANES, SUBLANES), sp * LANES:(sp + 1) * LANES] = _window_select(rot, gl)
        return carry

    lax.fori_loop(0, nj * nkk, gather_body, 0)

    def swap(val):
        return jnp.concatenate([val[:, half:], val[:, :half]], axis=1)

    def group_body(gl, carry):
        def cmul(val, idx):
            return coef_ref[gl, idx:idx + 1, :] * val + coef_ref[gl, idx + 1:idx + 2, :] * swap(val)

        u = ub_scr[gl].reshape(nj * nk, w).astype(BF16)
        s1 = _dot(u, win_ref[gl]).reshape(nj, nk, w)
        e = jnp.zeros((nk, w), F32)
        local = []
        for j in range(nj):
            local.append(e)
            e = cmul(e, 0) + s1[j]
        kidx = lax.broadcasted_iota(I32, (nk, w), 0)
        x = e
        step, d = 0, 1
        while d < nk:
            shifted = jnp.where(kidx >= d, pltpu.roll(x, d, axis=0), 0.0)
            x = x + cmul(shifted, 2 + 2 * nj + 2 * step)
            step, d = step + 1, d * 2
        x_start = jnp.where(kidx >= 1, pltpu.roll(x, 1, axis=0), 0.0)
        starts = [local[j] + cmul(x_start, 2 + 2 * j) for j in range(nj)]
        p = jnp.concatenate([st[:, :half] for st in starts], axis=0)
        hi = p.astype(BF16)
        lo = (p - hi.astype(F32)).astype(BF16)
        y = _dot(jnp.concatenate([u, hi, lo], axis=1), tw_ref[gl])
        yb_scr[gl] = y.reshape(nj, nk, w)
        return carry

    lax.fori_loop(0, gpb, group_body, 0)

    def scatter_body(it, carry):
        j, kk = it // nkk, it % nkk
        for tp in range(CHUNK // gpb):
            src = [yb_scr[gl, j, pl.ds(kk * SUBLANES, SUBLANES), tp * LANES:(tp + 1) * LANES] for gl in range(gpb)]
            for m in range(gpb):
                by_window = [src[(wi - m) % gpb] for wi in range(gpb)]
                window = lax.broadcasted_iota(I32, by_window[0].shape, 1) // SSM_GROUP
                merged = by_window[0]
                for wi in range(1, gpb):
                    merged = jnp.where(window == wi, by_window[wi], merged)
                nat = merged if m == 0 else pltpu.roll(merged, LANES - m * SSM_GROUP, axis=1)
                row0 = j * CHUNK + tp * gpb + m + kk * (SUBLANES * row_stride)
                y_ref[pl.ds(row0, SUBLANES, stride=row_stride), :] = nat
        return carry

    lax.fori_loop(0, nj * nkk, scatter_body, 0)


def _s5_params(a_re, a_im, log_dt, b_re, b_im, c_re, c_im, nk):
    g, p = a_re.shape
    c = SSM_GROUP
    hp = lax.Precision.HIGHEST
    dt = jnp.exp(log_dt)[:, None]
    adt_r, adt_i = a_re * dt, a_im * dt

    def lam_pow(n):
        nf = jnp.asarray(n, F32)[None, :, None]
        mag = jnp.exp(adt_r[:, None, :] * nf)
        ang = adt_i[:, None, :] * nf
        return mag * jnp.cos(ang), mag * jnp.sin(ang)

    lr, li = lam_pow(jnp.arange(CHUNK + 1))
    den = a_re * a_re + a_im * a_im
    nr, ni = lr[:, 1] - 1.0, li[:, 1]
    qr = (nr * a_re + ni * a_im) / den
    qi = (ni * a_re - nr * a_im) / den
    bbr = qr[..., None] * b_re - qi[..., None] * b_im
    bbi = qr[..., None] * b_im + qi[..., None] * b_re

    mr = lr[:, :CHUNK, :, None] * bbr[:, None] - li[:, :CHUNK, :, None] * bbi[:, None]
    mi = lr[:, :CHUNK, :, None] * bbi[:, None] + li[:, :CHUNK, :, None] * bbr[:, None]
    kern = (jnp.einsum('gcp,gtpd->gtcd', c_re, mr, precision=hp)
            - jnp.einsum('gcp,gtpd->gtcd', c_im, mi, precision=hp))
    toep = jnp.stack([jnp.pad(kern[:, :CHUNK - s], ((0, 0), (s, 0), (0, 0), (0, 0))) for s in range(CHUNK)],
                     axis=1)
    toep = toep.transpose(0, 1, 4, 2, 3).reshape(g, CHUNK * c, CHUNK * c)
    clr = c_re[:, None] * lr[:, 1:, None, :] - c_im[:, None] * li[:, 1:, None, :]
    cli = c_re[:, None] * li[:, 1:, None, :] + c_im[:, None] * lr[:, 1:, None, :]
    w_re = clr.transpose(0, 3, 1, 2).reshape(g, p, CHUNK * c)
    w_im = (-cli).transpose(0, 3, 1, 2).reshape(g, p, CHUNK * c)
    wout = jnp.concatenate([w_re, w_im], axis=1)
    pr, pi = lr[:, :CHUNK][:, ::-1], li[:, :CHUNK][:, ::-1]
    ir = pr[..., None] * bbr[:, None] - pi[..., None] * bbi[:, None]
    ii = pr[..., None] * bbi[:, None] + pi[..., None] * bbr[:, None]
    ir = ir.transpose(0, 1, 3, 2).reshape(g, CHUNK * c, p)
    ii = ii.transpose(0, 1, 3, 2).reshape(g, CHUNK * c, p)
    win = jnp.concatenate([ir, ii, ii, ir], axis=2)

    gi = jnp.arange(g)[:, None, None] % GROUPS_PER_BLOCK
    si = jnp.arange(CHUNK)[None, :, None]
    ci = jnp.arange(c)[None, None, :]
    lane_of = ((si // GROUPS_PER_BLOCK) * LANES + ((gi + si) % GROUPS_PER_BLOCK) * c + ci).reshape(g, CHUNK * c)
    perm = (lane_of[:, None, :] == jnp.arange(CHUNK * c)[None, :, None]).astype(BF16)
    win = jnp.einsum('gln,gnk->glk', perm, win.astype(BF16))
    toep = jnp.einsum('gln,gnk->glk', perm, toep.astype(BF16))
    tw = jnp.concatenate([toep, wout.astype(BF16), wout.astype(BF16)], axis=1)
    tw = jnp.einsum('grn,gln->grl', tw, perm)

    n_steps = max(nk.bit_length() - 1, 0)
    powers = [CHUNK] + [CHUNK * j for j in range(CHUNKS_PER_SUPER)] + [SUPER * (1 << i) for i in range(n_steps)]
    ar, ai = lam_pow(jnp.array(powers, dtype=jnp.int32))
    c1 = jnp.concatenate([ar, ar, ar, ar], axis=2)
    c2 = jnp.concatenate([-ai, ai, ai, -ai], axis=2)
    coef = jnp.stack([c1, c2], axis=2).reshape(g, 2 * len(powers), 4 * p)
    pad = (-coef.shape[1]) % SUBLANES
    coef = jnp.pad(coef, ((0, 0), (0, pad), (0, 0)))
    return win.astype(BF16), tw.astype(BF16), coef


def _s5(u, win, tw, coef):
    b, s, dssm = u.shape
    nk = s // SUPER
    nj = CHUNKS_PER_SUPER
    w = CHUNK * SSM_GROUP
    gpb = GROUPS_PER_BLOCK
    nblk = dssm // LANES
    wspec = lambda a: pl.BlockSpec((gpb,) + a.shape[1:], lambda bi, li: (li, 0, 0))
    return pl.pallas_call(
        functools.partial(_s5_kernel, nk=nk),
        grid=(b, nblk),
        in_specs=[pl.BlockSpec((None, s, LANES), lambda bi, li: (bi, 0, li)), wspec(win), wspec(tw), wspec(coef)],
        out_specs=pl.BlockSpec((None, s, LANES), lambda bi, li: (bi, 0, li)),
        out_shape=jax.ShapeDtypeStruct((b, s, dssm), F32),
        scratch_shapes=[pltpu.VMEM((gpb, nj, nk, w), F32)] * 2,
        compiler_params=pltpu.CompilerParams(dimension_semantics=("arbitrary", "arbitrary"),
                                             vmem_limit_bytes=VMEM_LIMIT),
        name="s5",
    )(u, win, tw, coef)


def _route(logits):
    e_all = logits[0:N_EXPERTS]
    gl = logits[N_EXPERTS:N_EXPERTS + SUBLANES]
    tm = logits.shape[1]
    ridx = lax.broadcasted_iota(I32, (SUBLANES, tm), 0)
    ge = jnp.exp(gl - jnp.max(gl, axis=0, keepdims=True))
    gp = ge / jnp.sum(ge, axis=0, keepdims=True)
    g_top = jnp.max(gp, axis=0, keepdims=True)
    g_sel = jnp.min(jnp.where(gp == g_top, ridx, SUBLANES), axis=0, keepdims=True)
    e_in = e_all[(N_EXPERT_GROUPS - 1) * SUBLANES:]
    for gi in range(N_EXPERT_GROUPS - 2, -1, -1):
        e_in = jnp.where(g_sel == gi, e_all[gi * SUBLANES:(gi + 1) * SUBLANES], e_in)
    ee = jnp.exp(e_in - jnp.max(e_in, axis=0, keepdims=True))
    ep = ee / jnp.sum(ee, axis=0, keepdims=True)
    v1 = jnp.max(ep, axis=0, keepdims=True)
    i1 = jnp.min(jnp.where(ep == v1, ridx, SUBLANES), axis=0, keepdims=True)
    ep2 = jnp.where(ridx == i1, -1.0, ep)
    v2 = jnp.max(ep2, axis=0, keepdims=True)
    i2 = jnp.min(jnp.where(ep2 == v2, ridx, SUBLANES), axis=0, keepdims=True)
    den = v1 + v2
    w1 = g_top * v1 / den
    w2 = g_top * v2 / den
    e1 = g_sel * EXPERTS_PER_GROUP + i1
    e2 = g_sel * EXPERTS_PER_GROUP + i2
    ids = jnp.where(ridx == 0, e1, jnp.where(ridx == 1, e2, 0))
    wts = jnp.where(ridx == 0, w1, jnp.where(ridx == 1, w2, 0.0))
    return ids, wts


def _mix_kernel(x_ref, ya_ref, ys_ref, u_ref, g1_ref, wgate_ref, dsk_ref, wglu_ref, wpa_ref, wps_ref,
                wout_ref, g2_ref, wrh_ref, wrl_ref, br_ref, x1_ref, xp_ref, ids_ref, wts_ref):
    d = x_ref.shape[1]
    x = x_ref[...]
    xb = _rmsnorm_rows(x, g1_ref[...]).astype(BF16)
    gates = _dot(xb, wgate_ref[...])
    y = ys_ref[...].astype(F32) + dsk_ref[...] * u_ref[...].astype(F32)
    y = y * (0.5 * (1.0 + jnp.tanh(math.sqrt(2.0 / math.pi) * (y + 0.044715 * (y * y * y)))))
    y = y * _sigmoid(_dot(y.astype(BF16), wglu_ref[...]))
    mixed = (_sigmoid(gates[:, :d]) * _dot(ya_ref[...], wpa_ref[...])
             + _sigmoid(gates[:, d:]) * _dot(y.astype(BF16), wps_ref[...]))
    x1 = x + _dot(mixed.astype(BF16), wout_ref[...])
    x1_ref[...] = x1
    xn = _rmsnorm_rows(x1, g2_ref[...])
    xp_ref[...] = pltpu.pack_elementwise([xn[:, :d // 2], xn[:, d // 2:]], packed_dtype=BF16)
    xh = xn.astype(BF16)
    xl = (xn - xh.astype(F32)).astype(BF16)
    wrh = wrh_ref[...]
    logits = _nt_dot(wrh, xh) + _nt_dot(wrl_ref[...], xh) + _nt_dot(wrh, xl) + br_ref[...]
    ids, wts = _route(logits)
    ids_ref[...] = ids
    wts_ref[...] = wts


def _mix(x2, ya, ys, u, g1, wgate, dsk, wglu, wpa, wps, wout, g2, wrh, wrl, br):
    t, d = x2.shape
    tm = TM_MIX
    full = lambda a: pl.BlockSpec(a.shape, lambda i: (0,) * a.ndim)
    row = lambda width: pl.BlockSpec((tm, width), lambda i: (i, 0))
    col = pl.BlockSpec((SUBLANES, tm), lambda i: (0, i))
    return pl.pallas_call(
        _mix_kernel,
        grid=(t // tm,),
        in_specs=[row(d), row(ya.shape[1]), row(ys.shape[1]), row(u.shape[1]), full(g1), full(wgate), full(dsk),
                  full(wglu), full(wpa),
                  full(wps), full(wout), full(g2), full(wrh), full(wrl), full(br)],
        out_specs=[row(d), row(d // 2), col, col],
        out_shape=[jax.ShapeDtypeStruct((t, d), F32), jax.ShapeDtypeStruct((t, d // 2), jnp.uint32),
                   jax.ShapeDtypeStruct((SUBLANES, t), I32), jax.ShapeDtypeStruct((SUBLANES, t), F32)],
        compiler_params=pltpu.CompilerParams(dimension_semantics=("arbitrary",), vmem_limit_bytes=VMEM_LIMIT),
        name="mix",
    )(x2, ya, ys, u, g1, wgate, dsk, wglu, wpa, wps, wout, g2, wrh, wrl, br)


def _meta_kernel(ids_ref, tri_ref, dest_ref, tile_ref, *, tile_rows):
    nk, nc, c = ids_ref.shape
    ne = N_EXPERTS
    erow = lax.broadcasted_iota(I32, (ne, c), 0)
    ones = jnp.ones((c, LANES), BF16)

    def onehot(k, ci):
        mask = erow == ids_ref[k, pl.ds(ci, 1), :]
        return mask, jnp.where(mask, 1.0, 0.0).astype(BF16)

    def count_body(n, acc):
        return acc + _dot(onehot(n // nc, n % nc)[1], ones)

    cnt = lax.fori_loop(0, nk * nc, count_body, jnp.zeros((ne, LANES), F32))
    ntiles = jnp.floor((cnt + (tile_rows - 1)) * (1.0 / tile_rows))
    lower = jnp.where(lax.broadcasted_iota(I32, (ne, ne), 1) < lax.broadcasted_iota(I32, (ne, ne), 0), 1.0, 0.0)
    start_tiles = _dot(lower.astype(BF16), ntiles.astype(BF16))
    base = start_tiles * tile_rows

    tri = tri_ref[...]

    def dest_body(n, carry):
        k, ci = n // nc, n % nc
        mask, oh = onehot(k, ci)
        prefix = _dot(oh, tri)
        slot = _lane_tile(base + carry, c // LANES) + prefix - 1.0
        dest = jnp.sum(jnp.where(mask, slot, 0.0), axis=0, keepdims=True)
        dest_ref[k, pl.ds(ci, 1), :] = dest.astype(I32)
        return carry + _dot(oh, ones)

    lax.fori_loop(0, nk * nc, dest_body, jnp.zeros((ne, LANES), F32))

    nt_lanes = tile_ref.shape[1]
    end_tiles = _lane_tile(start_tiles + ntiles, nt_lanes // LANES)
    tidx = lax.broadcasted_iota(I32, (ne, nt_lanes), 1).astype(F32)
    texp = jnp.sum(jnp.where(tidx >= end_tiles, 1.0, 0.0), axis=0, keepdims=True)
    valid = jnp.where(texp < ne, 1, 0)
    texp = jnp.minimum(texp, ne - 1.0).astype(I32)
    ridx = lax.broadcasted_iota(I32, tile_ref.shape, 0)
    tile_ref[...] = jnp.where(ridx == 0, texp, jnp.where(ridx == 1, valid, 0))


def _meta(ids2, tri, n_tiles):
    nk, t = ids2.shape
    c = SORT_CHUNK
    nt_lanes = pl.cdiv(n_tiles, LANES) * LANES
    ids3 = ids2.reshape(nk, t // c, c)
    dest, tile = pl.pallas_call(
        functools.partial(_meta_kernel, tile_rows=TM_EXPERT),
        out_shape=[jax.ShapeDtypeStruct(ids3.shape, I32), jax.ShapeDtypeStruct((SUBLANES, nt_lanes), I32)],
        compiler_params=pltpu.CompilerParams(vmem_limit_bytes=VMEM_LIMIT),
        name="meta",
    )(ids3, tri)
    return dest.reshape(nk * t), tile[0, :n_tiles], tile[1, :n_tiles]


def _row_copy(src_ref, src_row, dst_ref, dst_row, sem):
    return pltpu.make_async_copy(src_ref.at[pl.ds(src_row, 1)], dst_ref.at[pl.ds(dst_row, 1)], sem)


def _dispatch_kernel(dest_ref, xp_ref, xs_in_ref, xs_ref, sem, *, n_tokens):
    del xs_in_ref
    tm = xp_ref.shape[0]
    base = pl.program_id(0) * tm

    def body(r, carry):
        for k in range(2):
            _row_copy(xp_ref, r, xs_ref, dest_ref[k * n_tokens + base + r], sem).start()
        return carry

    lax.fori_loop(0, tm, body, 0)
    for k in range(2):
        pltpu.make_async_copy(xp_ref, xs_ref.at[pl.ds(0, tm)], sem).wait()


def _dispatch(dest, xp, n_slots):
    t, w = xp.shape
    tm = TM_DISPATCH
    xs0 = jnp.zeros((n_slots, w), xp.dtype)
    return pl.pallas_call(
        functools.partial(_dispatch_kernel, n_tokens=t),
        grid_spec=pltpu.PrefetchScalarGridSpec(
            num_scalar_prefetch=1,
            grid=(t // tm,),
            in_specs=[pl.BlockSpec((tm, w), lambda i, d: (i, 0)), pl.BlockSpec(memory_space=pl.ANY)],
            out_specs=pl.BlockSpec(memory_space=pl.ANY),
            scratch_shapes=[pltpu.SemaphoreType.DMA(())]),
        out_shape=jax.ShapeDtypeStruct((n_slots, w), xp.dtype),
        input_output_aliases={2: 0},
        compiler_params=pltpu.CompilerParams(dimension_semantics=("arbitrary",), has_side_effects=True),
        name="dispatch",
    )(dest, xp, xs0)


def _unpack_rows(packed):
    lo = pltpu.unpack_elementwise(packed, index=0, packed_dtype=BF16, unpacked_dtype=F32)
    hi = pltpu.unpack_elementwise(packed, index=1, packed_dtype=BF16, unpacked_dtype=F32)
    return jnp.concatenate([lo, hi], axis=1)


def _expert_kernel(te_ref, tv_ref, xs_ref, wg_ref, wu_ref, wd_ref, ys_ref):
    del te_ref
    i = pl.program_id(0)
    half = xs_ref.shape[1]

    @pl.when(tv_ref[i] > 0)
    def _():
        x = _unpack_rows(xs_ref[...]).astype(BF16)
        hg = _dot(x, wg_ref[...])
        hu = _dot(x, wu_ref[...])
        h = (hg * _sigmoid(hg) * hu).astype(BF16)
        y = _dot(h, wd_ref[...])
        ys_ref[...] = pltpu.pack_elementwise([y[:, :half], y[:, half:]], packed_dtype=BF16)

    @pl.when(tv_ref[i] == 0)
    def _():
        zero = jnp.zeros(ys_ref.shape, F32)
        ys_ref[...] = pltpu.pack_elementwise([zero, zero], packed_dtype=BF16)


def _experts(tile_expert, tile_valid, xs, wg, wu, wd):
    n_slots, w = xs.shape
    tm = TM_EXPERT
    wspec = lambda a: pl.BlockSpec((None,) + a.shape[1:], lambda i, te, tv: (te[i], 0, 0))
    return pl.pallas_call(
        _expert_kernel,
        grid_spec=pltpu.PrefetchScalarGridSpec(
            num_scalar_prefetch=2,
            grid=(n_slots // tm,),
            in_specs=[pl.BlockSpec((tm, w), lambda i, te, tv: (i, 0)), wspec(wg), wspec(wu), wspec(wd)],
            out_specs=pl.BlockSpec((tm, w), lambda i, te, tv: (i, 0))),
        out_shape=jax.ShapeDtypeStruct((n_slots, w), xs.dtype),
        compiler_params=pltpu.CompilerParams(dimension_semantics=("arbitrary",), vmem_limit_bytes=VMEM_LIMIT),
        name="experts",
    )(tile_expert, tile_valid, xs, wg, wu, wd)


def _combine_kernel(dest_ref, x1_ref, w0_ref, w1_ref, ys_ref, out_ref, buf, sem, *, n_tokens):
    tm, d = x1_ref.shape
    base = pl.program_id(0) * tm

    def body(r, carry):
        for k in range(2):
            _row_copy(ys_ref, dest_ref[k * n_tokens + base + r], buf.at[k], r, sem).start()
        return carry

    lax.fori_loop(0, tm, body, 0)
    for k in range(2):
        pltpu.make_async_copy(ys_ref.at[pl.ds(0, tm)], buf.at[k], sem).wait()
    reps = d // LANES
    out_ref[...] = (x1_ref[...]
                    + _lane_tile(w0_ref[...], reps) * _unpack_rows(buf[0])
                    + _lane_tile(w1_ref[...], reps) * _unpack_rows(buf[1]))


def _combine(dest, x1, w0, w1, ys):
    t, d = x1.shape
    tm = TM_COMBINE
    row = lambda width: pl.BlockSpec((tm, width), lambda i, dref: (i, 0))
    return pl.pallas_call(
        functools.partial(_combine_kernel, n_tokens=t),
        grid_spec=pltpu.PrefetchScalarGridSpec(
            num_scalar_prefetch=1,
            grid=(t // tm,),
            in_specs=[row(d), row(LANES), row(LANES), pl.BlockSpec(memory_space=pl.ANY)],
            out_specs=row(d),
            scratch_shapes=[pltpu.VMEM((2, tm, ys.shape[1]), ys.dtype), pltpu.SemaphoreType.DMA(())]),
        out_shape=jax.ShapeDtypeStruct((t, d), x1.dtype),
        compiler_params=pltpu.CompilerParams(dimension_semantics=("arbitrary",), vmem_limit_bytes=VMEM_LIMIT),
        name="combine",
    )(dest, x1, w0, w1, ys)


def _lower_tri(n):
    return jnp.where(jnp.arange(n)[:, None] >= jnp.arange(n)[None, :], 1.0, 0.0).astype(BF16)


def _upper_tri(n):
    return jnp.where(jnp.arange(n)[:, None] <= jnp.arange(n)[None, :], 1.0, 0.0).astype(BF16)


def _split_bf16(w):
    hi = w.astype(BF16)
    lo = (w - hi.astype(F32)).astype(BF16)
    return hi, lo


def _layer(x, norm_mix_g, w_in, b_forget, q_norm_g, k_norm_g, ssm_A_re, ssm_A_im, ssm_log_dt, ssm_B_re, ssm_B_im,
           ssm_C_re, ssm_C_im, ssm_D, w_glu, w_proj_attn, w_proj_ssm, w_out, norm_ffn_g, w_router_group,
           b_router_group, w_router_expert, b_router_expert, w_expert_gate, w_expert_up, w_expert_down):
    b, s, d = x.shape
    t = b * s
    da = N_HEADS * HEAD_DIM
    dssm = ssm_D.shape[0]
    nk = s // SUPER
    assert s % SUPER == 0 and nk % SUBLANES == 0 and nk & (nk - 1) == 0 and s % TQ_ATTN == 0 and t % TM_PROJ == 0
    x2 = x.reshape(t, d)

    o_f = 3 * da
    o_u = o_f + N_HEADS
    o_g = o_u + dssm
    w_main = jnp.concatenate([w_in[:, :o_f], w_in[:, o_u:o_g]], axis=1).astype(BF16)
    wf = jnp.pad(w_in[:, o_f:o_u], ((0, 0), (0, LANES - N_HEADS))).astype(BF16)
    bf = jnp.pad(b_forget, (0, LANES - N_HEADS))[None, :]
    w_gates = w_in[:, o_g:].astype(BF16)
    qg = (jnp.tile(q_norm_g, N_HEADS) * (HEAD_DIM ** -0.5 * LOG2E))[None, :]
    kg = jnp.tile(k_norm_g, N_HEADS)[None, :]
    head_of = jnp.arange(da) // HEAD_DIM
    bd = jnp.where(head_of[:, None] == head_of[None, :], 1.0 / HEAD_DIM, 0.0).astype(BF16)
    logit_bound = 1.02 * HEAD_DIM ** 0.5 * jnp.max(jnp.abs(q_norm_g)) * jnp.max(jnp.abs(k_norm_g))
    plain_ok = LOG2E * logit_bound <= PLAIN_SOFTMAX_MAX_LOG2
    shift = jnp.full((1, LANES), logit_bound, F32)
    hw = N_HEADS * LANES
    lane_in_head = jnp.arange(hw) % LANES
    ones = jnp.stack([
        (lane_in_head >= BIAS_K_LANE) & (lane_in_head < BIAS_K_LANE + 3),
        (lane_in_head >= BIAS_Q_LANE) & (lane_in_head < BIAS_Q_LANE + 3),
        lane_in_head == HEAD_DIM]
        + [jnp.zeros((hw,), bool)] * (SUBLANES - 3)).astype(F32)
    prow = jnp.arange(6 * LANES)
    kind, piece, head = prow // (3 * LANES), (prow // LANES) % 3, prow % LANES
    target = jnp.where(kind == 0, head * LANES + BIAS_Q_LANE + piece, hw + head * LANES + BIAS_K_LANE + piece)
    place = ((jnp.arange(2 * hw)[None, :] == target[:, None]) & (head < N_HEADS)[:, None]).astype(BF16)

    q, k, v, u = _proj(x2, norm_mix_g[None, :], w_main, wf, bf, qg, kg, bd, _lower_tri(TM_PROJ), place, ones,
                       shift, seq=s)
    y_attn = _attention(q.reshape(b, s, hw), k.reshape(b, s, hw), v.reshape(b, s, hw), plain_ok)
    w_pa = jnp.pad(w_proj_attn.reshape(N_HEADS, HEAD_DIM, d), ((0, 0), (0, LANES - HEAD_DIM), (0, 0)))
    w_pa = w_pa.reshape(hw, d).astype(BF16)

    win, tw, coef = _s5_params(ssm_A_re, ssm_A_im, ssm_log_dt, ssm_B_re, ssm_B_im, ssm_C_re, ssm_C_im, s // SUPER)
    y_ssm = _s5(u.reshape(b, s, dssm), win, tw, coef)

    wr = jnp.concatenate([w_router_expert.T, w_router_group.T,
                          jnp.zeros((ROUTER_ROWS - N_EXPERTS - N_EXPERT_GROUPS, d), F32)], axis=0)
    br = jnp.concatenate([b_router_expert, b_router_group,
                          jnp.full((SUBLANES - N_EXPERT_GROUPS,), NEG_INF, F32),
                          jnp.zeros((ROUTER_ROWS - N_EXPERTS - SUBLANES,), F32)])[:, None]
    x1, xp, ids, wts = _mix(x2, y_attn.reshape(t, hw), y_ssm.reshape(t, dssm), u, norm_mix_g[None, :], w_gates,
                            ssm_D[None, :], w_glu.astype(BF16), w_pa, w_proj_ssm.astype(BF16),
                            w_out.astype(BF16), norm_ffn_g[None, :], *_split_bf16(wr), br)

    n_tiles = (2 * t) // TM_EXPERT + N_EXPERTS
    dest, tile_expert, tile_valid = _meta(ids[:2], _upper_tri(SORT_CHUNK), n_tiles)
    xs = _dispatch(dest, xp, n_tiles * TM_EXPERT)
    ys = _experts(tile_expert, tile_valid, xs, w_expert_gate.astype(BF16), w_expert_up.astype(BF16),
                  w_expert_down.astype(BF16))
    w0 = jnp.broadcast_to(wts[0][:, None], (t, LANES))
    w1 = jnp.broadcast_to(wts[1][:, None], (t, LANES))
    out = _combine(dest, x1, w0, w1, ys)
    return out.reshape(b, s, d)


def kernel(x, norm_mix_g, w_in, b_forget, q_norm_g, k_norm_g, ssm_A_re, ssm_A_im, ssm_log_dt, ssm_B_re, ssm_B_im,
           ssm_C_re, ssm_C_im, ssm_D, w_glu, w_proj_attn, w_proj_ssm, w_out, norm_ffn_g, w_router_group,
           b_router_group, w_router_expert, b_router_expert, w_expert_gate, w_expert_up, w_expert_down):
    layer_params = (norm_mix_g, w_in, b_forget, q_norm_g, k_norm_g, ssm_A_re, ssm_A_im, ssm_log_dt, ssm_B_re,
                    ssm_B_im, ssm_C_re, ssm_C_im, ssm_D, w_glu, w_proj_attn, w_proj_ssm, w_out, norm_ffn_g,
                    w_router_group, b_router_group, w_router_expert, b_router_expert, w_expert_gate, w_expert_up,
                    w_expert_down)
    for layer in range(norm_mix_g.shape[0]):
        x = _layer(x, *[p[layer] for p in layer_params])
    return x
```

```python
import functools
import math

import jax
import jax.numpy as jnp
from jax import lax
from jax.experimental import pallas as pl
from jax.experimental.pallas import tpu as pltpu

F32 = jnp.float32
BF16 = jnp.bfloat16
I32 = jnp.int32

LANES = 128
SUBLANES = 8
MXU_DIM = 256

N_HEADS = 8
HEAD_DIM = 64
SSM_GROUP = 16
SSM_STATE = 64
N_EXPERT_GROUPS = 4
EXPERTS_PER_GROUP = 8
N_EXPERTS = N_EXPERT_GROUPS * EXPERTS_PER_GROUP
EPS = 1e-6
NEG_INF = -1e30
LOG2E = math.log2(math.e)
PLAIN_SOFTMAX_MAX_LOG2 = 60.0
BIAS_K_LANE = HEAD_DIM
BIAS_Q_LANE = HEAD_DIM + 3
FORGET_COPIES_STRIDE = LANES // N_HEADS

CHUNK = MXU_DIM // SSM_GROUP
CHUNKS_PER_SUPER = 8
SUPER = CHUNK * CHUNKS_PER_SUPER
GROUPS_PER_BLOCK = LANES // SSM_GROUP
RELAYOUT_UNROLL = 4

TM_PROJ = 512
TQ_ATTN = 1024
TS_ATTN = 512
TM_MIX = 512
TM_EXPERT = 256
TM_DISPATCH = 512
TM_COMBINE = 512
SORT_CHUNK = 512
ZERO_FILL_ROWS = 2048
ROUTER_ROWS = 48

VMEM_LIMIT = 48 * 1024 * 1024


def _nt_dot(a, b):
    return lax.dot_general(a, b, (((1,), (1,)), ((), ())), preferred_element_type=F32)


def _dot(a, b):
    return jnp.dot(a, b, preferred_element_type=F32)


def _lane_tile(x, n):
    return x if n == 1 else jnp.concatenate([x] * n, axis=1)


def _rmsnorm_rows(x, g):
    ms = jnp.mean(x * x, axis=-1, keepdims=True)
    return x * lax.rsqrt(ms + EPS) * g


def _sigmoid(x):
    return 1.0 / (1.0 + jnp.exp(-x))


def _split3(x):
    hi = x.astype(BF16)
    r1 = x - hi.astype(F32)
    mid = r1.astype(BF16)
    lo = (r1 - mid.astype(F32)).astype(BF16)
    return [hi, mid, lo]


def _expand_heads(z):
    lane = lax.broadcasted_iota(I32, (z.shape[0], LANES), 1)
    blocks = []
    for p in range(z.shape[1] // LANES):
        blk = z[:, p * LANES:(p + 1) * LANES]
        blocks.append(jnp.where(lane < HEAD_DIM, blk, 0.0))
        blocks.append(jnp.where(lane < HEAD_DIM, pltpu.roll(blk, HEAD_DIM, axis=1), 0.0))
    return jnp.concatenate(blocks, axis=1)


def _proj_kernel(x_ref, g_ref, w_ref, wf_ref, bf_ref, qg_ref, kg_ref, bd_ref, ltri_ref, place_ref, ones_ref,
                 shift_ref, q_ref, k_ref, v_ref, u_ref, carry_ref, *, tiles_per_seq):
    i = pl.program_id(0)

    @pl.when(i % tiles_per_seq == 0)
    def _():
        carry_ref[...] = jnp.zeros_like(carry_ref)

    tm = x_ref.shape[0]
    da = N_HEADS * HEAD_DIM
    hw = N_HEADS * LANES
    xb = _rmsnorm_rows(x_ref[...], g_ref[...]).astype(BF16)
    z = _dot(xb, w_ref[...])
    zq, zk = z[:, :da], z[:, da:2 * da]
    bd = bd_ref[...]
    msq = _dot((zq * zq).astype(BF16), bd)
    msk = _dot((zk * zk).astype(BF16), bd)
    qn = zq * lax.rsqrt(msq + EPS) * qg_ref[...]
    kn = zk * lax.rsqrt(msk + EPS) * kg_ref[...]
    u_ref[...] = z[:, 3 * da:]

    a = _dot(xb, wf_ref[...]) + bf_ref[...]
    logf = jnp.minimum(a, 0.0) - jnp.log(1.0 + jnp.exp(-jnp.abs(a)))
    cs = _dot(ltri_ref[...], jnp.concatenate(_split3(logf), axis=1))
    cum = cs[:, :LANES] + cs[:, LANES:2 * LANES] + cs[:, 2 * LANES:] + carry_ref[0:1, :]
    carry_ref[...] = jnp.broadcast_to(cum[tm - 1:tm, :], carry_ref.shape)
    copy = lax.broadcasted_iota(I32, cum.shape, 1) % FORGET_COPIES_STRIDE
    bias = jnp.where(copy < 3, (cum - shift_ref[...]) * LOG2E, cum * (-LOG2E))
    hi, mid, lo = [p.astype(F32) for p in _split3(bias)]
    piece = jnp.where(copy % 3 == 0, hi, jnp.where(copy % 3 == 1, mid, lo)).astype(BF16)
    extras = _dot(piece, place_ref[...])
    q_ref[...] = (_expand_heads(qn) + extras[:, :hw] + ones_ref[0:1, :]).astype(BF16)
    k_ref[...] = (_expand_heads(kn) + extras[:, hw:] + ones_ref[1:2, :]).astype(BF16)
    v_ref[...] = (_expand_heads(z[:, 2 * da:3 * da]) + ones_ref[2:3, :]).astype(BF16)


def _proj(x2, norm_g, w_main, wf, b_f, qg, kg, bd, ltri, place, ones, shift, *, seq):
    t, d = x2.shape
    tm = TM_PROJ
    hw = N_HEADS * LANES
    du = w_main.shape[1] - 3 * N_HEADS * HEAD_DIM
    tiles_per_seq = seq // tm
    full = lambda a: pl.BlockSpec(a.shape, lambda i: (0,) * a.ndim)
    row = lambda width: pl.BlockSpec((tm, width), lambda i: (i, 0))
    consts = (norm_g, w_main, wf, b_f, qg, kg, bd, ltri, place, ones, shift)
    return pl.pallas_call(
        functools.partial(_proj_kernel, tiles_per_seq=tiles_per_seq),
        grid=(t // tm,),
        in_specs=[row(d)] + [full(a) for a in consts],
        out_specs=[row(hw), row(hw), row(hw), row(du)],
        out_shape=[jax.ShapeDtypeStruct((t, hw), BF16)] * 3 + [jax.ShapeDtypeStruct((t, du), F32)],
        scratch_shapes=[pltpu.VMEM((SUBLANES, LANES), F32)],
        compiler_params=pltpu.CompilerParams(dimension_semantics=("arbitrary",), vmem_limit_bytes=VMEM_LIMIT),
        name="proj",
    )(x2, *consts)


def _attn_kernel(q_ref, k_ref, v_ref, o_ref, acc_scr, m_scr, *, ts, safe):
    i = pl.program_id(2)
    nsub = q_ref.shape[0] // ts
    acc_scr[...] = jnp.zeros(acc_scr.shape, F32)
    if safe:
        m_scr[...] = jnp.full(m_scr.shape, NEG_INF, F32)

    def block(sub, j, masked):
        off = pl.multiple_of(j * ts, ts)
        s = _nt_dot(q_ref[sub * ts:(sub + 1) * ts, :], k_ref[pl.ds(off, ts), :])
        if masked:
            rows = lax.broadcasted_iota(I32, (ts, ts), 0)
            cols = lax.broadcasted_iota(I32, (ts, ts), 1)
            s = jnp.where(cols <= rows, s, NEG_INF)
        vblk = v_ref[pl.ds(off, ts), :]
        if safe:
            m_prev = m_scr[sub]
            m_new = jnp.maximum(m_prev, jnp.max(s, axis=1, keepdims=True))
            p = jnp.exp2(s - _lane_tile(m_new, ts // LANES)).astype(BF16)
            acc_scr[sub] = jnp.exp2(m_prev - m_new) * acc_scr[sub] + _dot(p, vblk)
            m_scr[sub] = m_new
        else:
            acc_scr[sub] += _dot(jnp.exp2(s).astype(BF16), vblk)

    def body(j, carry):
        for sub in range(nsub):
            block(sub, j, False)
        return carry

    lax.fori_loop(0, nsub * i, body, 0)
    for jj in range(nsub):
        for sub in range(jj, nsub):
            block(sub, nsub * i + jj, sub == jj)
    for sub in range(nsub):
        acc = acc_scr[sub]
        o_ref[sub * ts:(sub + 1) * ts, :] = (acc / acc[:, HEAD_DIM:HEAD_DIM + 1]).astype(o_ref.dtype)


def _attention(q, k, v, plain_ok):
    b, s, hw = q.shape
    tq, ts = TQ_ATTN, TS_ATTN

    def call(safe):
        return pl.pallas_call(
            functools.partial(_attn_kernel, ts=ts, safe=safe),
            grid=(b, hw // LANES, s // tq),
            in_specs=[pl.BlockSpec((None, tq, LANES), lambda bi, h, i: (bi, i, h)),
                      pl.BlockSpec((None, s, LANES), lambda bi, h, i: (bi, 0, h)),
                      pl.BlockSpec((None, s, LANES), lambda bi, h, i: (bi, 0, h))],
            out_specs=pl.BlockSpec((None, tq, LANES), lambda bi, h, i: (bi, i, h)),
            out_shape=jax.ShapeDtypeStruct((b, s, hw), BF16),
            scratch_shapes=[pltpu.VMEM((tq // ts, ts, LANES), F32)] * 2,
            compiler_params=pltpu.CompilerParams(
                dimension_semantics=("arbitrary", "arbitrary", "arbitrary"), vmem_limit_bytes=VMEM_LIMIT),
            name="attn_safe" if safe else "attn",
        )(q, k, v)

    return lax.cond(plain_ok, lambda: call(False), lambda: call(True))


def _window_select(pieces, first_group):
    window = lax.broadcasted_iota(I32, pieces[0].shape, 1) // SSM_GROUP
    out = pieces[0]
    for m in range(1, GROUPS_PER_BLOCK):
        out = jnp.where(window == (first_group + m) % GROUPS_PER_BLOCK, pieces[m], out)
    return out


def _s5_kernel(u_ref, win_ref, tw_ref, coef_ref, y_ref, ub_scr, yb_scr, *, nk):
    nj = CHUNKS_PER_SUPER
    gpb = GROUPS_PER_BLOCK
    w = CHUNK * SSM_GROUP
    half = w // 2
    nkk = nk // SUBLANES
    row_stride = SUPER

    def gather_body(it, carry):
        j, kk = it // nkk, it % nkk
        for sp in range(CHUNK // gpb):
            rot = []
            for m in range(gpb):
                row0 = j * CHUNK + sp * gpb + m + kk * (SUBLANES * row_stride)
                src = u_ref[pl.ds(row0, SUBLANES, stride=row_stride), :]
                rot.append(src if m == 0 else pltpu.roll(src, m * SSM_GROUP, axis=1))
            for gl in range(gpb):
                ub_scr[gl, j, pl.ds(kk * SUBLANES, SUBLANES), sp * LANES:(sp + 1) * LANES] = _window_select(rot, gl)
        return carry

    lax.fori_loop(0, nj * nkk, gather_body, 0, unroll=RELAYOUT_UNROLL)

    def swap(val):
        return jnp.concatenate([val[:, half:], val[:, :half]], axis=1)

    def group_body(gl, carry):
        def cmul(val, idx):
            return coef_ref[gl, idx:idx + 1, :] * val + coef_ref[gl, idx + 1:idx + 2, :] * swap(val)

        u = ub_scr[gl].reshape(nj * nk, w).astype(BF16)
        s1 = _dot(u, win_ref[gl]).reshape(nj, nk, w)
        e = jnp.zeros((nk, w), F32)
        local = []
        for j in range(nj):
            local.append(e)
            e = cmul(e, 0) + s1[j]
        kidx = lax.broadcasted_iota(I32, (nk, w), 0)
        x = e
        step, d = 0, 1
        while d < nk:
            shifted = jnp.where(kidx >= d, pltpu.roll(x, d, axis=0), 0.0)
            x = x + cmul(shifted, 2 + 2 * nj + 2 * step)
            step, d = step + 1, d * 2
        x_start = jnp.where(kidx >= 1, pltpu.roll(x, 1, axis=0), 0.0)
        starts = [local[j] + cmul(x_start, 2 + 2 * j) for j in range(nj)]
        p = jnp.concatenate([st[:, :half] for st in starts], axis=0)
        hi = p.astype(BF16)
        lo = (p - hi.astype(F32)).astype(BF16)
        y = _dot(jnp.concatenate([u, hi, lo], axis=1), tw_ref[gl])
        yb_scr[gl] = y.reshape(nj, nk, w)
        return carry

    lax.fori_loop(0, gpb, group_body, 0)

    def scatter_body(it, carry):
        j, kk = it // nkk, it % nkk
        for tp in range(CHUNK // gpb):
            src = [yb_scr[gl, j, pl.ds(kk * SUBLANES, SUBLANES), tp * LANES:(tp + 1) * LANES] for gl in range(gpb)]
            for m in range(gpb):
                by_window = [src[(wi - m) % gpb] for wi in range(gpb)]
                window = lax.broadcasted_iota(I32, by_window[0].shape, 1) // SSM_GROUP
                merged = by_window[0]
                for wi in range(1, gpb):
                    merged = jnp.where(window == wi, by_window[wi], merged)
                nat = merged if m == 0 else pltpu.roll(merged, LANES - m * SSM_GROUP, axis=1)
                row0 = j * CHUNK + tp * gpb + m + kk * (SUBLANES * row_stride)
                y_ref[pl.ds(row0, SUBLANES, stride=row_stride), :] = nat
        return carry

    lax.fori_loop(0, nj * nkk, scatter_body, 0, unroll=RELAYOUT_UNROLL)


def _s5_params(a_re, a_im, log_dt, b_re, b_im, c_re, c_im, nk):
    g, p = a_re.shape
    c = SSM_GROUP
    hp = lax.Precision.HIGHEST
    dt = jnp.exp(log_dt)[:, None]
    adt_r, adt_i = a_re * dt, a_im * dt

    def lam_pow(n):
        nf = jnp.asarray(n, F32)[None, :, None]
        mag = jnp.exp(adt_r[:, None, :] * nf)
        ang = adt_i[:, None, :] * nf
        return mag * jnp.cos(ang), mag * jnp.sin(ang)

    lr, li = lam_pow(jnp.arange(CHUNK + 1))
    den = a_re * a_re + a_im * a_im
    nr, ni = lr[:, 1] - 1.0, li[:, 1]
    qr = (nr * a_re + ni * a_im) / den
    qi = (ni * a_re - nr * a_im) / den
    bbr = qr[..., None] * b_re - qi[..., None] * b_im
    bbi = qr[..., None] * b_im + qi[..., None] * b_re

    mr = lr[:, :CHUNK, :, None] * bbr[:, None] - li[:, :CHUNK, :, None] * bbi[:, None]
    mi = lr[:, :CHUNK, :, None] * bbi[:, None] + li[:, :CHUNK, :, None] * bbr[:, None]
    kern = (jnp.einsum('gcp,gtpd->gtcd', c_re, mr, precision=hp)
            - jnp.einsum('gcp,gtpd->gtcd', c_im, mi, precision=hp))
    toep = jnp.stack([jnp.pad(kern[:, :CHUNK - s], ((0, 0), (s, 0), (0, 0), (0, 0))) for s in range(CHUNK)],
                     axis=1)
    toep = toep.transpose(0, 1, 4, 2, 3).reshape(g, CHUNK * c, CHUNK * c)
    clr = c_re[:, None] * lr[:, 1:, None, :] - c_im[:, None] * li[:, 1:, None, :]
    cli = c_re[:, None] * li[:, 1:, None, :] + c_im[:, None] * lr[:, 1:, None, :]
    w_re = clr.transpose(0, 3, 1, 2).reshape(g, p, CHUNK * c)
    w_im = (-cli).transpose(0, 3, 1, 2).reshape(g, p, CHUNK * c)
    wout = jnp.concatenate([w_re, w_im], axis=1)
    pr, pi = lr[:, :CHUNK][:, ::-1], li[:, :CHUNK][:, ::-1]
    ir = pr[..., None] * bbr[:, None] - pi[..., None] * bbi[:, None]
    ii = pr[..., None] * bbi[:, None] + pi[..., None] * bbr[:, None]
    ir = ir.transpose(0, 1, 3, 2).reshape(g, CHUNK * c, p)
    ii = ii.transpose(0, 1, 3, 2).reshape(g, CHUNK * c, p)
    win = jnp.concatenate([ir, ii, ii, ir], axis=2)

    gi = jnp.arange(g)[:, None, None] % GROUPS_PER_BLOCK
    si = jnp.arange(CHUNK)[None, :, None]
    ci = jnp.arange(c)[None, None, :]
    lane_of = ((si // GROUPS_PER_BLOCK) * LANES + ((gi + si) % GROUPS_PER_BLOCK) * c + ci).reshape(g, CHUNK * c)
    perm = (lane_of[:, None, :] == jnp.arange(CHUNK * c)[None, :, None]).astype(BF16)
    win = jnp.einsum('gln,gnk->glk', perm, win.astype(BF16))
    toep = jnp.einsum('gln,gnk->glk', perm, toep.astype(BF16))
    tw = jnp.concatenate([toep, wout.astype(BF16), wout.astype(BF16)], axis=1)
    tw = jnp.einsum('grn,gln->grl', tw, perm)

    n_steps = max(nk.bit_length() - 1, 0)
    powers = [CHUNK] + [CHUNK * j for j in range(CHUNKS_PER_SUPER)] + [SUPER * (1 << i) for i in range(n_steps)]
    ar, ai = lam_pow(jnp.array(powers, dtype=jnp.int32))
    c1 = jnp.concatenate([ar, ar, ar, ar], axis=2)
    c2 = jnp.concatenate([-ai, ai, ai, -ai], axis=2)
    coef = jnp.stack([c1, c2], axis=2).reshape(g, 2 * len(powers), 4 * p)
    pad = (-coef.shape[1]) % SUBLANES
    coef = jnp.pad(coef, ((0, 0), (0, pad), (0, 0)))
    return win.astype(BF16), tw.astype(BF16), coef


def _s5(u, win, tw, coef):
    b, s, dssm = u.shape
    nk = s // SUPER
    nj = CHUNKS_PER_SUPER
    w = CHUNK * SSM_GROUP
    gpb = GROUPS_PER_BLOCK
    nblk = dssm // LANES
    wspec = lambda a: pl.BlockSpec((gpb,) + a.shape[1:], lambda bi, li: (li, 0, 0))
    return pl.pallas_call(
        functools.partial(_s5_kernel, nk=nk),
        grid=(b, nblk),
        in_specs=[pl.BlockSpec((None, s, LANES), lambda bi, li: (bi, 0, li)), wspec(win), wspec(tw), wspec(coef)],
        out_specs=pl.BlockSpec((None, s, LANES), lambda bi, li: (bi, 0, li)),
        out_shape=jax.ShapeDtypeStruct((b, s, dssm), F32),
        scratch_shapes=[pltpu.VMEM((gpb, nj, nk, w), F32)] * 2,
        compiler_params=pltpu.CompilerParams(dimension_semantics=("arbitrary", "arbitrary"),
                                             vmem_limit_bytes=VMEM_LIMIT),
        name="s5",
    )(u, win, tw, coef)


def _route(logits):
    e_all = logits[0:N_EXPERTS]
    gl = logits[N_EXPERTS:N_EXPERTS + SUBLANES]
    tm = logits.shape[1]
    ridx = lax.broadcasted_iota(I32, (SUBLANES, tm), 0)
    ge = jnp.exp(gl - jnp.max(gl, axis=0, keepdims=True))
    gp = ge / jnp.sum(ge, axis=0, keepdims=True)
    g_top = jnp.max(gp, axis=0, keepdims=True)
    g_sel = jnp.min(jnp.where(gp == g_top, ridx, SUBLANES), axis=0, keepdims=True)
    e_in = e_all[(N_EXPERT_GROUPS - 1) * SUBLANES:]
    for gi in range(N_EXPERT_GROUPS - 2, -1, -1):
        e_in = jnp.where(g_sel == gi, e_all[gi * SUBLANES:(gi + 1) * SUBLANES], e_in)
    ee = jnp.exp(e_in - jnp.max(e_in, axis=0, keepdims=True))
    ep = ee / jnp.sum(ee, axis=0, keepdims=True)
    v1 = jnp.max(ep, axis=0, keepdims=True)
    i1 = jnp.min(jnp.where(ep == v1, ridx, SUBLANES), axis=0, keepdims=True)
    ep2 = jnp.where(ridx == i1, -1.0, ep)
    v2 = jnp.max(ep2, axis=0, keepdims=True)
    i2 = jnp.min(jnp.where(ep2 == v2, ridx, SUBLANES), axis=0, keepdims=True)
    den = v1 + v2
    w1 = g_top * v1 / den
    w2 = g_top * v2 / den
    e1 = g_sel * EXPERTS_PER_GROUP + i1
    e2 = g_sel * EXPERTS_PER_GROUP + i2
    ids = jnp.where(ridx == 0, e1, jnp.where(ridx == 1, e2, 0))
    wts = jnp.where(ridx == 0, w1, jnp.where(ridx == 1, w2, 0.0))
    return ids, wts


def _mix_kernel(x_ref, ya_ref, ys_ref, u_ref, g1_ref, wgate_ref, dsk_ref, wglu_ref, wpa_ref, wps_ref,
                wout_ref, g2_ref, wrh_ref, wrl_ref, br_ref, x1_ref, xp_ref, ids_ref, wts_ref):
    d = x_ref.shape[1]
    x = x_ref[...]
    xb = _rmsnorm_rows(x, g1_ref[...]).astype(BF16)
    gates = _dot(xb, wgate_ref[...])
    y = ys_ref[...].astype(F32) + dsk_ref[...] * u_ref[...].astype(F32)
    y = y * (0.5 * (1.0 + jnp.tanh(math.sqrt(2.0 / math.pi) * (y + 0.044715 * (y * y * y)))))
    y = y * _sigmoid(_dot(y.astype(BF16), wglu_ref[...]))
    mixed = (_sigmoid(gates[:, :d]) * _dot(ya_ref[...], wpa_ref[...])
             + _sigmoid(gates[:, d:]) * _dot(y.astype(BF16), wps_ref[...]))
    x1 = x + _dot(mixed.astype(BF16), wout_ref[...])
    x1_ref[...] = x1
    xn = _rmsnorm_rows(x1, g2_ref[...])
    xp_ref[...] = pltpu.pack_elementwise([xn[:, :d // 2], xn[:, d // 2:]], packed_dtype=BF16)
    xh = xn.astype(BF16)
    xl = (xn - xh.astype(F32)).astype(BF16)
    wrh = wrh_ref[...]
    logits = _nt_dot(wrh, xh) + _nt_dot(wrl_ref[...], xh) + _nt_dot(wrh, xl) + br_ref[...]
    ids, wts = _route(logits)
    ids_ref[...] = ids
    wts_ref[...] = wts


def _mix(x2, ya, ys, u, g1, wgate, dsk, wglu, wpa, wps, wout, g2, wrh, wrl, br):
    t, d = x2.shape
    tm = TM_MIX
    full = lambda a: pl.BlockSpec(a.shape, lambda i: (0,) * a.ndim)
    row = lambda width: pl.BlockSpec((tm, width), lambda i: (i, 0))
    col = pl.BlockSpec((SUBLANES, tm), lambda i: (0, i))
    return pl.pallas_call(
        _mix_kernel,
        grid=(t // tm,),
        in_specs=[row(d), row(ya.shape[1]), row(ys.shape[1]), row(u.shape[1]), full(g1), full(wgate), full(dsk),
                  full(wglu), full(wpa),
                  full(wps), full(wout), full(g2), full(wrh), full(wrl), full(br)],
        out_specs=[row(d), row(d // 2), col, col],
        out_shape=[jax.ShapeDtypeStruct((t, d), F32), jax.ShapeDtypeStruct((t, d // 2), jnp.uint32),
                   jax.ShapeDtypeStruct((SUBLANES, t), I32), jax.ShapeDtypeStruct((SUBLANES, t), F32)],
        compiler_params=pltpu.CompilerParams(dimension_semantics=("arbitrary",), vmem_limit_bytes=VMEM_LIMIT),
        name="mix",
    )(x2, ya, ys, u, g1, wgate, dsk, wglu, wpa, wps, wout, g2, wrh, wrl, br)


def _meta_kernel(ids_ref, tri_ref, dest_ref, tile_ref, *, tile_rows):
    nk, nc, c = ids_ref.shape
    ne = N_EXPERTS
    erow = lax.broadcasted_iota(I32, (ne, c), 0)
    ones = jnp.ones((c, LANES), BF16)

    def onehot(k, ci):
        mask = erow == ids_ref[k, pl.ds(ci, 1), :]
        return mask, jnp.where(mask, 1.0, 0.0).astype(BF16)

    def count_body(n, acc):
        return acc + _dot(onehot(n // nc, n % nc)[1], ones)

    cnt = lax.fori_loop(0, nk * nc, count_body, jnp.zeros((ne, LANES), F32))
    ntiles = jnp.floor((cnt + (tile_rows - 1)) * (1.0 / tile_rows))
    lower = jnp.where(lax.broadcasted_iota(I32, (ne, ne), 1) < lax.broadcasted_iota(I32, (ne, ne), 0), 1.0, 0.0)
    start_tiles = _dot(lower.astype(BF16), ntiles.astype(BF16))
    base = start_tiles * tile_rows

    tri = tri_ref[...]

    def dest_body(n, carry):
        k, ci = n // nc, n % nc
        mask, oh = onehot(k, ci)
        prefix = _dot(oh, tri)
        slot = _lane_tile(base + carry, c // LANES) + prefix - 1.0
        dest = jnp.sum(jnp.where(mask, slot, 0.0), axis=0, keepdims=True)
        dest_ref[k, pl.ds(ci, 1), :] = dest.astype(I32)
        return carry + _dot(oh, ones)

    lax.fori_loop(0, nk * nc, dest_body, jnp.zeros((ne, LANES), F32))

    nt_lanes = tile_ref.shape[1]
    end_tiles = _lane_tile(start_tiles + ntiles, nt_lanes // LANES)
    tidx = lax.broadcasted_iota(I32, (ne, nt_lanes), 1).astype(F32)
    texp = jnp.sum(jnp.where(tidx >= end_tiles, 1.0, 0.0), axis=0, keepdims=True)
    valid = jnp.where(texp < ne, 1, 0)
    texp = jnp.minimum(texp, ne - 1.0).astype(I32)
    ridx = lax.broadcasted_iota(I32, tile_ref.shape, 0)
    tile_ref[...] = jnp.where(ridx == 0, texp, jnp.where(ridx == 1, valid, 0))


def _meta(ids2, tri, n_tiles):
    nk, t = ids2.shape
    c = SORT_CHUNK
    nt_lanes = pl.cdiv(n_tiles, LANES) * LANES
    ids3 = ids2.reshape(nk, t // c, c)
    dest, tile = pl.pallas_call(
        functools.partial(_meta_kernel, tile_rows=TM_EXPERT),
        out_shape=[jax.ShapeDtypeStruct(ids3.shape, I32), jax.ShapeDtypeStruct((SUBLANES, nt_lanes), I32)],
        compiler_params=pltpu.CompilerParams(vmem_limit_bytes=VMEM_LIMIT),
        name="meta",
    )(ids3, tri)
    return dest.reshape(nk * t), tile[0, :n_tiles], tile[1, :n_tiles]


def _row_copy(src_ref, src_row, dst_ref, dst_row, sem):
    return pltpu.make_async_copy(src_ref.at[pl.ds(src_row, 1)], dst_ref.at[pl.ds(dst_row, 1)], sem)


def _dispatch_kernel(dest_ref, xp_ref, xs_in_ref, xs_ref, sem, *, n_tokens):
    del xs_in_ref
    tm = xp_ref.shape[0]
    base = pl.program_id(0) * tm

    def body(r, carry):
        for k in range(2):
            _row_copy(xp_ref, r, xs_ref, dest_ref[k * n_tokens + base + r], sem).start()
        return carry

    lax.fori_loop(0, tm, body, 0)
    for k in range(2):
        pltpu.make_async_copy(xp_ref, xs_ref.at[pl.ds(0, tm)], sem).wait()


def _zero_fill_kernel(o_ref):
    o_ref[...] = jnp.zeros(o_ref.shape, o_ref.dtype)


def _zeros(n_rows, w, dtype):
    rows = n_rows // pl.cdiv(n_rows, ZERO_FILL_ROWS)
    assert n_rows % rows == 0 and rows % SUBLANES == 0
    return pl.pallas_call(
        _zero_fill_kernel,
        grid=(n_rows // rows,),
        out_specs=pl.BlockSpec((rows, w), lambda i: (i, 0)),
        out_shape=jax.ShapeDtypeStruct((n_rows, w), dtype),
        compiler_params=pltpu.CompilerParams(dimension_semantics=("arbitrary",)),
        name="zero_fill",
    )()


def _dispatch(dest, xp, n_slots):
    t, w = xp.shape
    tm = TM_DISPATCH
    xs0 = _zeros(n_slots, w, xp.dtype)
    return pl.pallas_call(
        functools.partial(_dispatch_kernel, n_tokens=t),
        grid_spec=pltpu.PrefetchScalarGridSpec(
            num_scalar_prefetch=1,
            grid=(t // tm,),
            in_specs=[pl.BlockSpec((tm, w), lambda i, d: (i, 0)), pl.BlockSpec(memory_space=pl.ANY)],
            out_specs=pl.BlockSpec(memory_space=pl.ANY),
            scratch_shapes=[pltpu.SemaphoreType.DMA(())]),
        out_shape=jax.ShapeDtypeStruct((n_slots, w), xp.dtype),
        input_output_aliases={2: 0},
        compiler_params=pltpu.CompilerParams(dimension_semantics=("arbitrary",), has_side_effects=True),
        name="dispatch",
    )(dest, xp, xs0)


def _unpack_rows(packed):
    lo = pltpu.unpack_elementwise(packed, index=0, packed_dtype=BF16, unpacked_dtype=F32)
    hi = pltpu.unpack_elementwise(packed, index=1, packed_dtype=BF16, unpacked_dtype=F32)
    return jnp.concatenate([lo, hi], axis=1)


def _expert_kernel(te_ref, tv_ref, xs_ref, wg_ref, wu_ref, wd_ref, ys_ref, wg_bf, wu_bf, wd_bf):
    i = pl.program_id(0)
    half = xs_ref.shape[1]

    @pl.when((i == 0) | (te_ref[i] != te_ref[jnp.maximum(i - 1, 0)]))
    def _():
        wg_bf[...] = wg_ref[...].astype(BF16)
        wu_bf[...] = wu_ref[...].astype(BF16)
        wd_bf[...] = wd_ref[...].astype(BF16)

    @pl.when(tv_ref[i] > 0)
    def _():
        x = _unpack_rows(xs_ref[...]).astype(BF16)
        hg = _dot(x, wg_bf[...])
        hu = _dot(x, wu_bf[...])
        h = (hg * _sigmoid(hg) * hu).astype(BF16)
        y = _dot(h, wd_bf[...])
        ys_ref[...] = pltpu.pack_elementwise([y[:, :half], y[:, half:]], packed_dtype=BF16)

    @pl.when(tv_ref[i] == 0)
    def _():
        zero = jnp.zeros(ys_ref.shape, F32)
        ys_ref[...] = pltpu.pack_elementwise([zero, zero], packed_dtype=BF16)


def _experts(tile_expert, tile_valid, xs, wg, wu, wd):
    n_slots, w = xs.shape
    tm = TM_EXPERT
    wspec = lambda a: pl.BlockSpec((None,) + a.shape[1:], lambda i, te, tv: (te[i], 0, 0))
    return pl.pallas_call(
        _expert_kernel,
        grid_spec=pltpu.PrefetchScalarGridSpec(
            num_scalar_prefetch=2,
            grid=(n_slots // tm,),
            in_specs=[pl.BlockSpec((tm, w), lambda i, te, tv: (i, 0)), wspec(wg), wspec(wu), wspec(wd)],
            out_specs=pl.BlockSpec((tm, w), lambda i, te, tv: (i, 0)),
            scratch_shapes=[pltpu.VMEM(a.shape[1:], BF16) for a in (wg, wu, wd)]),
        out_shape=jax.ShapeDtypeStruct((n_slots, w), xs.dtype),
        compiler_params=pltpu.CompilerParams(dimension_semantics=("arbitrary",), vmem_limit_bytes=VMEM_LIMIT),
        name="experts",
    )(tile_expert, tile_valid, xs, wg, wu, wd)


def _combine_kernel(dest_ref, x1_ref, w0_ref, w1_ref, ys_ref, out_ref, buf, sem, *, n_tokens):
    tm, d = x1_ref.shape
    base = pl.program_id(0) * tm

    def body(r, carry):
        for k in range(2):
            _row_copy(ys_ref, dest_ref[k * n_tokens + base + r], buf.at[k], r, sem).start()
        return carry

    lax.fori_loop(0, tm, body, 0)
    for k in range(2):
        pltpu.make_async_copy(ys_ref.at[pl.ds(0, tm)], buf.at[k], sem).wait()
    reps = d // LANES
    out_ref[...] = (x1_ref[...]
                    + _lane_tile(w0_ref[...], reps) * _unpack_rows(buf[0])
                    + _lane_tile(w1_ref[...], reps) * _unpack_rows(buf[1]))


def _combine(dest, x1, w0, w1, ys):
    t, d = x1.shape
    tm = TM_COMBINE
    row = lambda width: pl.BlockSpec((tm, width), lambda i, dref: (i, 0))
    return pl.pallas_call(
        functools.partial(_combine_kernel, n_tokens=t),
        grid_spec=pltpu.PrefetchScalarGridSpec(
            num_scalar_prefetch=1,
            grid=(t // tm,),
            in_specs=[row(d), row(LANES), row(LANES), pl.BlockSpec(memory_space=pl.ANY)],
            out_specs=row(d),
            scratch_shapes=[pltpu.VMEM((2, tm, ys.shape[1]), ys.dtype), pltpu.SemaphoreType.DMA(())]),
        out_shape=jax.ShapeDtypeStruct((t, d), x1.dtype),
        compiler_params=pltpu.CompilerParams(dimension_semantics=("arbitrary",), vmem_limit_bytes=VMEM_LIMIT),
        name="combine",
    )(dest, x1, w0, w1, ys)


def _lower_tri(n):
    return jnp.where(jnp.arange(n)[:, None] >= jnp.arange(n)[None, :], 1.0, 0.0).astype(BF16)


def _upper_tri(n):
    return jnp.where(jnp.arange(n)[:, None] <= jnp.arange(n)[None, :], 1.0, 0.0).astype(BF16)


def _split_bf16(w):
    hi = w.astype(BF16)
    lo = (w - hi.astype(F32)).astype(BF16)
    return hi, lo


def _layer(x, norm_mix_g, w_in, b_forget, q_norm_g, k_norm_g, ssm_A_re, ssm_A_im, ssm_log_dt, ssm_B_re, ssm_B_im,
           ssm_C_re, ssm_C_im, ssm_D, w_glu, w_proj_attn, w_proj_ssm, w_out, norm_ffn_g, w_router_group,
           b_router_group, w_router_expert, b_router_expert, w_expert_gate, w_expert_up, w_expert_down):
    b, s, d = x.shape
    t = b * s
    da = N_HEADS * HEAD_DIM
    dssm = ssm_D.shape[0]
    nk = s // SUPER
    assert s % SUPER == 0 and nk % SUBLANES == 0 and nk & (nk - 1) == 0 and s % TQ_ATTN == 0 and t % TM_PROJ == 0
    x2 = x.reshape(t, d)

    o_f = 3 * da
    o_u = o_f + N_HEADS
    o_g = o_u + dssm
    w_main = jnp.concatenate([w_in[:, :o_f], w_in[:, o_u:o_g]], axis=1).astype(BF16)
    wf = jnp.repeat(w_in[:, o_f:o_u], FORGET_COPIES_STRIDE, axis=1).astype(BF16)
    bf = jnp.repeat(b_forget, FORGET_COPIES_STRIDE)[None, :]
    w_gates = w_in[:, o_g:].astype(BF16)
    qg = (jnp.tile(q_norm_g, N_HEADS) * (HEAD_DIM ** -0.5 * LOG2E))[None, :]
    kg = jnp.tile(k_norm_g, N_HEADS)[None, :]
    head_of = jnp.arange(da) // HEAD_DIM
    bd = jnp.where(head_of[:, None] == head_of[None, :], 1.0 / HEAD_DIM, 0.0).astype(BF16)
    logit_bound = 1.02 * HEAD_DIM ** 0.5 * jnp.max(jnp.abs(q_norm_g)) * jnp.max(jnp.abs(k_norm_g))
    plain_ok = LOG2E * logit_bound <= PLAIN_SOFTMAX_MAX_LOG2
    shift = jnp.full((1, LANES), logit_bound, F32)
    hw = N_HEADS * LANES
    lane_in_head = jnp.arange(hw) % LANES
    ones = jnp.stack([
        (lane_in_head >= BIAS_K_LANE) & (lane_in_head < BIAS_K_LANE + 3),
        (lane_in_head >= BIAS_Q_LANE) & (lane_in_head < BIAS_Q_LANE + 3),
        lane_in_head == HEAD_DIM]
        + [jnp.zeros((hw,), bool)] * (SUBLANES - 3)).astype(F32)
    prow = jnp.arange(LANES)
    head, copy = prow // FORGET_COPIES_STRIDE, prow % FORGET_COPIES_STRIDE
    target = jnp.where(copy < 3, head * LANES + BIAS_Q_LANE + copy, hw + head * LANES + BIAS_K_LANE + copy - 3)
    place = ((jnp.arange(2 * hw)[None, :] == target[:, None]) & (copy < 6)[:, None]).astype(BF16)

    q, k, v, u = _proj(x2, norm_mix_g[None, :], w_main, wf, bf, qg, kg, bd, _lower_tri(TM_PROJ), place, ones,
                       shift, seq=s)
    y_attn = _attention(q.reshape(b, s, hw), k.reshape(b, s, hw), v.reshape(b, s, hw), plain_ok)
    w_pa = jnp.pad(w_proj_attn.reshape(N_HEADS, HEAD_DIM, d), ((0, 0), (0, LANES - HEAD_DIM), (0, 0)))
    w_pa = w_pa.reshape(hw, d).astype(BF16)

    win, tw, coef = _s5_params(ssm_A_re, ssm_A_im, ssm_log_dt, ssm_B_re, ssm_B_im, ssm_C_re, ssm_C_im, s // SUPER)
    y_ssm = _s5(u.reshape(b, s, dssm), win, tw, coef)

    wr = jnp.concatenate([w_router_expert.T, w_router_group.T,
                          jnp.zeros((ROUTER_ROWS - N_EXPERTS - N_EXPERT_GROUPS, d), F32)], axis=0)
    br = jnp.concatenate([b_router_expert, b_router_group,
                          jnp.full((SUBLANES - N_EXPERT_GROUPS,), NEG_INF, F32),
                          jnp.zeros((ROUTER_ROWS - N_EXPERTS - SUBLANES,), F32)])[:, None]
    x1, xp, ids, wts = _mix(x2, y_attn.reshape(t, hw), y_ssm.reshape(t, dssm), u, norm_mix_g[None, :], w_gates,
                            ssm_D[None, :], w_glu.astype(BF16), w_pa, w_proj_ssm.astype(BF16),
                            w_out.astype(BF16), norm_ffn_g[None, :], *_split_bf16(wr), br)

    n_tiles = (2 * t) // TM_EXPERT + N_EXPERTS
    dest, tile_expert, tile_valid = _meta(ids[:2], _upper_tri(SORT_CHUNK), n_tiles)
    xs = _dispatch(dest, xp, n_tiles * TM_EXPERT)
    ys = _experts(tile_expert, tile_valid, xs, w_expert_gate, w_expert_up, w_expert_down)
    w0 = jnp.broadcast_to(wts[0][:, None], (t, LANES))
    w1 = jnp.broadcast_to(wts[1][:, None], (t, LANES))
    out = _combine(dest, x1, w0, w1, ys)
    return out.reshape(b, s, d)


def kernel(x, norm_mix_g, w_in, b_forget, q_norm_g, k_norm_g, ssm_A_re, ssm_A_im, ssm_log_dt, ssm_B_re, ssm_B_im,
           ssm_C_re, ssm_C_im, ssm_D, w_glu, w_proj_attn, w_proj_ssm, w_out, norm_ffn_g, w_router_group,
           b_router_group, w_router_expert, b_router_expert, w_expert_gate, w_expert_up, w_expert_down):
    layer_params = (norm_mix_g, w_in, b_forget, q_norm_g, k_norm_g, ssm_A_re, ssm_A_im, ssm_log_dt, ssm_B_re,
                    ssm_B_im, ssm_C_re, ssm_C_im, ssm_D, w_glu, w_proj_attn, w_proj_ssm, w_out, norm_ffn_g,
                    w_router_group, b_router_group, w_router_expert, b_router_expert, w_expert_gate, w_expert_up,
                    w_expert_down)
    for layer in range(norm_mix_g.shape[0]):
        x = _layer(x, *[p[layer] for p in layer_params])
    return x
```

```python
import functools
import math

import jax
import jax.numpy as jnp
from jax import lax
from jax.experimental import pallas as pl
from jax.experimental.pallas import tpu as pltpu

F32 = jnp.float32
BF16 = jnp.bfloat16
I32 = jnp.int32

LANES = 128
SUBLANES = 8
MXU_DIM = 256

N_HEADS = 8
HEAD_DIM = 64
SSM_GROUP = 16
SSM_STATE = 64
N_EXPERT_GROUPS = 4
EXPERTS_PER_GROUP = 8
N_EXPERTS = N_EXPERT_GROUPS * EXPERTS_PER_GROUP
EPS = 1e-6
NEG_INF = -1e30
LOG2E = math.log2(math.e)
PLAIN_SOFTMAX_MAX_LOG2 = 60.0
BIAS_K_LANE = HEAD_DIM
BIAS_Q_LANE = HEAD_DIM + 3
FORGET_COPIES_STRIDE = LANES // N_HEADS

CHUNK = MXU_DIM // SSM_GROUP
CHUNKS_PER_SUPER = 8
SUPER = CHUNK * CHUNKS_PER_SUPER
GROUPS_PER_BLOCK = LANES // SSM_GROUP
RELAYOUT_UNROLL = 4
MIN_FACTORED_DECAY = -4.0

TM_PROJ = 512
TQ_ATTN = 1024
TS_ATTN = 512
TM_MIX = 512
TM_EXPERT = 512
TM_DISPATCH = 512
TM_COMBINE = 512
SORT_CHUNK = 512
ZERO_FILL_ROWS = 2048
ROUTER_ROWS = 48

VMEM_LIMIT = 48 * 1024 * 1024


def _nt_dot(a, b):
    return lax.dot_general(a, b, (((1,), (1,)), ((), ())), preferred_element_type=F32)


def _dot(a, b):
    return jnp.dot(a, b, preferred_element_type=F32)


def _lane_tile(x, n):
    return x if n == 1 else jnp.concatenate([x] * n, axis=1)


def _rmsnorm_rows(x, g):
    ms = jnp.mean(x * x, axis=-1, keepdims=True)
    return x * lax.rsqrt(ms + EPS) * g


def _sigmoid(x):
    return 1.0 / (1.0 + jnp.exp(-x))


def _split3(x):
    hi = x.astype(BF16)
    r1 = x - hi.astype(F32)
    mid = r1.astype(BF16)
    lo = (r1 - mid.astype(F32)).astype(BF16)
    return [hi, mid, lo]


def _expand_heads(z):
    lane = lax.broadcasted_iota(I32, (z.shape[0], LANES), 1)
    blocks = []
    for p in range(z.shape[1] // LANES):
        blk = z[:, p * LANES:(p + 1) * LANES]
        blocks.append(jnp.where(lane < HEAD_DIM, blk, 0.0))
        blocks.append(jnp.where(lane < HEAD_DIM, pltpu.roll(blk, HEAD_DIM, axis=1), 0.0))
    return jnp.concatenate(blocks, axis=1)


def _proj_kernel(x_ref, g_ref, w_ref, wf_ref, bf_ref, qg_ref, kg_ref, bd_ref, ltri_ref, place_ref, ones_ref,
                 shift_ref, q_ref, k_ref, v_ref, u_ref, carry_ref, *, tiles_per_seq):
    i = pl.program_id(0)

    @pl.when(i % tiles_per_seq == 0)
    def _():
        carry_ref[...] = jnp.zeros_like(carry_ref)

    tm = x_ref.shape[0]
    da = N_HEADS * HEAD_DIM
    hw = N_HEADS * LANES
    xb = _rmsnorm_rows(x_ref[...], g_ref[...]).astype(BF16)
    z = _dot(xb, w_ref[...])
    zq, zk = z[:, :da], z[:, da:2 * da]
    bd = bd_ref[...]
    msq = _dot((zq * zq).astype(BF16), bd)
    msk = _dot((zk * zk).astype(BF16), bd)
    qn = zq * lax.rsqrt(msq + EPS) * qg_ref[...]
    kn = zk * lax.rsqrt(msk + EPS) * kg_ref[...]
    u_ref[...] = z[:, 3 * da:]

    a = _dot(xb, wf_ref[...]) + bf_ref[...]
    logf = jnp.minimum(a, 0.0) - jnp.log(1.0 + jnp.exp(-jnp.abs(a)))
    cs = _dot(ltri_ref[...], jnp.concatenate(_split3(logf), axis=1))
    cum = cs[:, :LANES] + cs[:, LANES:2 * LANES] + cs[:, 2 * LANES:] + carry_ref[0:1, :]
    carry_ref[...] = jnp.broadcast_to(cum[tm - 1:tm, :], carry_ref.shape)
    copy = lax.broadcasted_iota(I32, cum.shape, 1) % FORGET_COPIES_STRIDE
    bias = jnp.where(copy < 3, (cum - shift_ref[...]) * LOG2E, cum * (-LOG2E))
    hi, mid, lo = [p.astype(F32) for p in _split3(bias)]
    piece = jnp.where(copy % 3 == 0, hi, jnp.where(copy % 3 == 1, mid, lo)).astype(BF16)
    extras = _dot(piece, place_ref[...])
    q_ref[...] = (_expand_heads(qn) + extras[:, :hw] + ones_ref[0:1, :]).astype(BF16)
    k_ref[...] = (_expand_heads(kn) + extras[:, hw:] + ones_ref[1:2, :]).astype(BF16)
    v_ref[...] = (_expand_heads(z[:, 2 * da:3 * da]) + ones_ref[2:3, :]).astype(BF16)


def _proj(x2, norm_g, w_main, wf, b_f, qg, kg, bd, ltri, place, ones, shift, *, seq):
    t, d = x2.shape
    tm = TM_PROJ
    hw = N_HEADS * LANES
    du = w_main.shape[1] - 3 * N_HEADS * HEAD_DIM
    tiles_per_seq = seq // tm
    full = lambda a: pl.BlockSpec(a.shape, lambda i: (0,) * a.ndim)
    row = lambda width: pl.BlockSpec((tm, width), lambda i: (i, 0))
    consts = (norm_g, w_main, wf, b_f, qg, kg, bd, ltri, place, ones, shift)
    return pl.pallas_call(
        functools.partial(_proj_kernel, tiles_per_seq=tiles_per_seq),
        grid=(t // tm,),
        in_specs=[row(d)] + [full(a) for a in consts],
        out_specs=[row(hw), row(hw), row(hw), row(du)],
        out_shape=[jax.ShapeDtypeStruct((t, hw), BF16)] * 3 + [jax.ShapeDtypeStruct((t, du), F32)],
        scratch_shapes=[pltpu.VMEM((SUBLANES, LANES), F32)],
        compiler_params=pltpu.CompilerParams(dimension_semantics=("arbitrary",), vmem_limit_bytes=VMEM_LIMIT),
        name="proj",
    )(x2, *consts)


def _attn_kernel(q_ref, k_ref, v_ref, o_ref, acc_scr, m_scr, *, ts, safe):
    i = pl.program_id(2)
    nsub = q_ref.shape[0] // ts
    acc_scr[...] = jnp.zeros(acc_scr.shape, F32)
    if safe:
        m_scr[...] = jnp.full(m_scr.shape, NEG_INF, F32)

    def block(sub, j, masked):
        off = pl.multiple_of(j * ts, ts)
        s = _nt_dot(q_ref[sub * ts:(sub + 1) * ts, :], k_ref[pl.ds(off, ts), :])
        if masked:
            rows = lax.broadcasted_iota(I32, (ts, ts), 0)
            cols = lax.broadcasted_iota(I32, (ts, ts), 1)
            s = jnp.where(cols <= rows, s, NEG_INF)
        vblk = v_ref[pl.ds(off, ts), :]
        if safe:
            m_prev = m_scr[sub]
            m_new = jnp.maximum(m_prev, jnp.max(s, axis=1, keepdims=True))
            p = jnp.exp2(s - _lane_tile(m_new, ts // LANES)).astype(BF16)
            acc_scr[sub] = jnp.exp2(m_prev - m_new) * acc_scr[sub] + _dot(p, vblk)
            m_scr[sub] = m_new
        else:
            acc_scr[sub] += _dot(jnp.exp2(s).astype(BF16), vblk)

    def body(j, carry):
        for sub in range(nsub):
            block(sub, j, False)
        return carry

    lax.fori_loop(0, nsub * i, body, 0)
    for jj in range(nsub):
        for sub in range(jj, nsub):
            block(sub, nsub * i + jj, sub == jj)
    for sub in range(nsub):
        acc = acc_scr[sub]
        o_ref[sub * ts:(sub + 1) * ts, :] = (acc / acc[:, HEAD_DIM:HEAD_DIM + 1]).astype(o_ref.dtype)


def _attention(q, k, v, plain_ok):
    b, s, hw = q.shape
    tq, ts = TQ_ATTN, TS_ATTN

    def call(safe):
        return pl.pallas_call(
            functools.partial(_attn_kernel, ts=ts, safe=safe),
            grid=(b, hw // LANES, s // tq),
            in_specs=[pl.BlockSpec((None, tq, LANES), lambda bi, h, i: (bi, i, h)),
                      pl.BlockSpec((None, s, LANES), lambda bi, h, i: (bi, 0, h)),
                      pl.BlockSpec((None, s, LANES), lambda bi, h, i: (bi, 0, h))],
            out_specs=pl.BlockSpec((None, tq, LANES), lambda bi, h, i: (bi, i, h)),
            out_shape=jax.ShapeDtypeStruct((b, s, hw), BF16),
            scratch_shapes=[pltpu.VMEM((tq // ts, ts, LANES), F32)] * 2,
            compiler_params=pltpu.CompilerParams(
                dimension_semantics=("arbitrary", "arbitrary", "arbitrary"), vmem_limit_bytes=VMEM_LIMIT),
            name="attn_safe" if safe else "attn",
        )(q, k, v)

    return lax.cond(plain_ok, lambda: call(False), lambda: call(True))


def _window_select(pieces, first_group):
    window = lax.broadcasted_iota(I32, pieces[0].shape, 1) // SSM_GROUP
    out = pieces[0]
    for m in range(1, GROUPS_PER_BLOCK):
        out = jnp.where(window == (first_group + m) % GROUPS_PER_BLOCK, pieces[m], out)
    return out


def _s5_kernel(u_ref, win_ref, tw_ref, coef_ref, y_ref, ub_scr, yb_scr, *, nk):
    nj = CHUNKS_PER_SUPER
    gpb = GROUPS_PER_BLOCK
    w = CHUNK * SSM_GROUP
    half = w // 2
    nkk = nk // SUBLANES
    row_stride = SUPER

    def gather_body(it, carry):
        j, kk = it // nkk, it % nkk
        for sp in range(CHUNK // gpb):
            rot = []
            for m in range(gpb):
                row0 = j * CHUNK + sp * gpb + m + kk * (SUBLANES * row_stride)
                src = u_ref[pl.ds(row0, SUBLANES, stride=row_stride), :]
                rot.append(src if m == 0 else pltpu.roll(src, m * SSM_GROUP, axis=1))
            for gl in range(gpb):
                ub_scr[gl, j, pl.ds(kk * SUBLANES, SUBLANES), sp * LANES:(sp + 1) * LANES] = _window_select(rot, gl)
        return carry

    lax.fori_loop(0, nj * nkk, gather_body, 0, unroll=RELAYOUT_UNROLL)

    def swap(val):
        return jnp.concatenate([val[:, half:], val[:, :half]], axis=1)

    def group_body(gl, carry):
        def cmul(val, idx):
            return coef_ref[gl, idx:idx + 1, :] * val + coef_ref[gl, idx + 1:idx + 2, :] * swap(val)

        u = ub_scr[gl].reshape(nj * nk, w).astype(BF16)
        s1 = _dot(u, win_ref[gl]).reshape(nj, nk, w)
        e = jnp.zeros((nk, w), F32)
        local = []
        for j in range(nj):
            local.append(e)
            e = cmul(e, 0) + s1[j]
        kidx = lax.broadcasted_iota(I32, (nk, w), 0)
        x = e
        step, d = 0, 1
        while d < nk:
            shifted = jnp.where(kidx >= d, pltpu.roll(x, d, axis=0), 0.0)
            x = x + cmul(shifted, 2 + 2 * nj + 2 * step)
            step, d = step + 1, d * 2
        x_start = jnp.where(kidx >= 1, pltpu.roll(x, 1, axis=0), 0.0)
        starts = [local[j] + cmul(x_start, 2 + 2 * j) for j in range(nj)]
        p = jnp.concatenate([st[:, :half] for st in starts], axis=0)
        hi = p.astype(BF16)
        lo = (p - hi.astype(F32)).astype(BF16)
        y = _dot(jnp.concatenate([u, hi, lo], axis=1), tw_ref[gl])
        yb_scr[gl] = y.reshape(nj, nk, w)
        return carry

    lax.fori_loop(0, gpb, group_body, 0)

    def scatter_body(it, carry):
        j, kk = it // nkk, it % nkk
        for tp in range(CHUNK // gpb):
            src = [yb_scr[gl, j, pl.ds(kk * SUBLANES, SUBLANES), tp * LANES:(tp + 1) * LANES] for gl in range(gpb)]
            for m in range(gpb):
                by_window = [src[(wi - m) % gpb] for wi in range(gpb)]
                window = lax.broadcasted_iota(I32, by_window[0].shape, 1) // SSM_GROUP
                merged = by_window[0]
                for wi in range(1, gpb):
                    merged = jnp.where(window == wi, by_window[wi], merged)
                nat = merged if m == 0 else pltpu.roll(merged, LANES - m * SSM_GROUP, axis=1)
                row0 = j * CHUNK + tp * gpb + m + kk * (SUBLANES * row_stride)
                y_ref[pl.ds(row0, SUBLANES, stride=row_stride), :] = nat
        return carry

    lax.fori_loop(0, nj * nkk, scatter_body, 0, unroll=RELAYOUT_UNROLL)


def _s5_params(a_re, a_im, log_dt, b_re, b_im, c_re, c_im, nk):
    args = (a_re, a_im, log_dt, b_re, b_im, c_re, c_im)
    slowest = jnp.min(a_re * jnp.exp(log_dt)[:, None])
    return lax.cond(slowest > MIN_FACTORED_DECAY, functools.partial(_s5_params_factored, nk=nk),
                    functools.partial(_s5_params_direct, nk=nk), *args)


def _s5_params_factored(a_re, a_im, log_dt, b_re, b_im, c_re, c_im, *, nk):
    g, p = a_re.shape
    c = SSM_GROUP
    gpb = GROUPS_PER_BLOCK
    dt = jnp.exp(log_dt)[:, None]
    adt_r, adt_i = a_re * dt, a_im * dt

    def lam_pow(n):
        nf = jnp.asarray(n, F32)
        nf = (nf[None] if nf.ndim == 1 else nf)[:, :, None]
        mag = jnp.exp(adt_r[:, None, :] * nf)
        ang = adt_i[:, None, :] * nf
        return mag * jnp.cos(ang), mag * jnp.sin(ang)

    gi = jnp.arange(g)[:, None, None] % gpb
    step = ((jnp.arange(CHUNK // gpb)[None, :, None]) * gpb
            + (jnp.arange(gpb)[None, None, :] - gi) % gpb).reshape(g, CHUNK)

    l1r, l1i = lam_pow(jnp.ones((1,)))
    den = a_re * a_re + a_im * a_im
    nr, ni = l1r[:, 0] - 1.0, l1i[:, 0]
    qr = (nr * a_re + ni * a_im) / den
    qi = (ni * a_re - nr * a_im) / den
    bcr = (qr[..., None] * b_re - qi[..., None] * b_im).transpose(0, 2, 1)
    bci = (qr[..., None] * b_im + qi[..., None] * b_re).transpose(0, 2, 1)

    def times_b(pr, pi):
        pr, pi = pr[:, :, None, :], pi[:, :, None, :]
        return pr * bcr[:, None] - pi * bci[:, None], pr * bci[:, None] + pi * bcr[:, None]

    def times_c(pr, pi):
        pr, pi = pr[:, :, None, :], pi[:, :, None, :]
        return c_re[:, None] * pr - c_im[:, None] * pi, c_re[:, None] * pi + c_im[:, None] * pr

    rows = lambda parts: jnp.concatenate(parts, axis=-1).reshape(g, CHUNK * c, -1)
    ir, ii = times_b(*lam_pow(CHUNK - 1 - step))
    win = rows([ir, ii, ii, ir])
    clr, cli = times_c(*lam_pow(step + 1))
    wout = rows([clr, -cli]).transpose(0, 2, 1)
    ar, ai = times_b(*lam_pow(-step))
    br, bi = times_c(*lam_pow(step))
    toep = jnp.einsum('gnp,gmp->gnm', rows([ar, ai]), rows([br, -bi]), precision=lax.Precision.HIGHEST)
    step_of_lane = jnp.repeat(step, c, axis=1)
    toep = jnp.where(step_of_lane[:, None, :] >= step_of_lane[:, :, None], toep, 0.0)
    tw = jnp.concatenate([toep, wout, wout], axis=1)
    return win.astype(BF16), tw.astype(BF16), _s5_scan_coefficients(lam_pow, g, p, nk)


def _s5_scan_coefficients(lam_pow, g, p, nk):
    n_steps = max(nk.bit_length() - 1, 0)
    powers = [CHUNK] + [CHUNK * j for j in range(CHUNKS_PER_SUPER)] + [SUPER * (1 << i) for i in range(n_steps)]
    ar, ai = lam_pow(jnp.array(powers, dtype=jnp.int32))
    c1 = jnp.concatenate([ar, ar, ar, ar], axis=2)
    c2 = jnp.concatenate([-ai, ai, ai, -ai], axis=2)
    coef = jnp.stack([c1, c2], axis=2).reshape(g, 2 * len(powers), 4 * p)
    return jnp.pad(coef, ((0, 0), (0, (-coef.shape[1]) % SUBLANES), (0, 0)))


def _s5_params_direct(a_re, a_im, log_dt, b_re, b_im, c_re, c_im, *, nk):
    g, p = a_re.shape
    c = SSM_GROUP
    hp = lax.Precision.HIGHEST
    dt = jnp.exp(log_dt)[:, None]
    adt_r, adt_i = a_re * dt, a_im * dt

    def lam_pow(n):
        nf = jnp.asarray(n, F32)[None, :, None]
        mag = jnp.exp(adt_r[:, None, :] * nf)
        ang = adt_i[:, None, :] * nf
        return mag * jnp.cos(ang), mag * jnp.sin(ang)

    lr, li = lam_pow(jnp.arange(CHUNK + 1))
    den = a_re * a_re + a_im * a_im
    nr, ni = lr[:, 1] - 1.0, li[:, 1]
    qr = (nr * a_re + ni * a_im) / den
    qi = (ni * a_re - nr * a_im) / den
    bbr = qr[..., None] * b_re - qi[..., None] * b_im
    bbi = qr[..., None] * b_im + qi[..., None] * b_re

    mr = lr[:, :CHUNK, :, None] * bbr[:, None] - li[:, :CHUNK, :, None] * bbi[:, None]
    mi = lr[:, :CHUNK, :, None] * bbi[:, None] + li[:, :CHUNK, :, None] * bbr[:, None]
    kern = (jnp.einsum('gcp,gtpd->gtcd', c_re, mr, precision=hp)
            - jnp.einsum('gcp,gtpd->gtcd', c_im, mi, precision=hp))
    toep = jnp.stack([jnp.pad(kern[:, :CHUNK - s], ((0, 0), (s, 0), (0, 0), (0, 0))) for s in range(CHUNK)],
                     axis=1)
    toep = toep.transpose(0, 1, 4, 2, 3).reshape(g, CHUNK * c, CHUNK * c)
    clr = c_re[:, None] * lr[:, 1:, None, :] - c_im[:, None] * li[:, 1:, None, :]
    cli = c_re[:, None] * li[:, 1:, None, :] + c_im[:, None] * lr[:, 1:, None, :]
    w_re = clr.transpose(0, 3, 1, 2).reshape(g, p, CHUNK * c)
    w_im = (-cli).transpose(0, 3, 1, 2).reshape(g, p, CHUNK * c)
    wout = jnp.concatenate([w_re, w_im], axis=1)
    pr, pi = lr[:, :CHUNK][:, ::-1], li[:, :CHUNK][:, ::-1]
    ir = pr[..., None] * bbr[:, None] - pi[..., None] * bbi[:, None]
    ii = pr[..., None] * bbi[:, None] + pi[..., None] * bbr[:, None]
    ir = ir.transpose(0, 1, 3, 2).reshape(g, CHUNK * c, p)
    ii = ii.transpose(0, 1, 3, 2).reshape(g, CHUNK * c, p)
    win = jnp.concatenate([ir, ii, ii, ir], axis=2)

    gi = jnp.arange(g)[:, None, None] % GROUPS_PER_BLOCK
    si = jnp.arange(CHUNK)[None, :, None]
    ci = jnp.arange(c)[None, None, :]
    lane_of = ((si // GROUPS_PER_BLOCK) * LANES + ((gi + si) % GROUPS_PER_BLOCK) * c + ci).reshape(g, CHUNK * c)
    perm = (lane_of[:, None, :] == jnp.arange(CHUNK * c)[None, :, None]).astype(BF16)
    win = jnp.einsum('gln,gnk->glk', perm, win.astype(BF16))
    toep = jnp.einsum('gln,gnk->glk', perm, toep.astype(BF16))
    tw = jnp.concatenate([toep, wout.astype(BF16), wout.astype(BF16)], axis=1)
    tw = jnp.einsum('grn,gln->grl', tw, perm)
    return win.astype(BF16), tw.astype(BF16), _s5_scan_coefficients(lam_pow, g, p, nk)


def _s5(u, win, tw, coef):
    b, s, dssm = u.shape
    nk = s // SUPER
    nj = CHUNKS_PER_SUPER
    w = CHUNK * SSM_GROUP
    gpb = GROUPS_PER_BLOCK
    nblk = dssm // LANES
    wspec = lambda a: pl.BlockSpec((gpb,) + a.shape[1:], lambda bi, li: (li, 0, 0))
    return pl.pallas_call(
        functools.partial(_s5_kernel, nk=nk),
        grid=(b, nblk),
        in_specs=[pl.BlockSpec((None, s, LANES), lambda bi, li: (bi, 0, li)), wspec(win), wspec(tw), wspec(coef)],
        out_specs=pl.BlockSpec((None, s, LANES), lambda bi, li: (bi, 0, li)),
        out_shape=jax.ShapeDtypeStruct((b, s, dssm), F32),
        scratch_shapes=[pltpu.VMEM((gpb, nj, nk, w), F32)] * 2,
        compiler_params=pltpu.CompilerParams(dimension_semantics=("arbitrary", "arbitrary"),
                                             vmem_limit_bytes=VMEM_LIMIT),
        name="s5",
    )(u, win, tw, coef)


def _route(logits):
    e_all = logits[0:N_EXPERTS]
    gl = logits[N_EXPERTS:N_EXPERTS + SUBLANES]
    tm = logits.shape[1]
    ridx = lax.broadcasted_iota(I32, (SUBLANES, tm), 0)
    ge = jnp.exp(gl - jnp.max(gl, axis=0, keepdims=True))
    gp = ge / jnp.sum(ge, axis=0, keepdims=True)
    g_top = jnp.max(gp, axis=0, keepdims=True)
    g_sel = jnp.min(jnp.where(gp == g_top, ridx, SUBLANES), axis=0, keepdims=True)
    e_in = e_all[(N_EXPERT_GROUPS - 1) * SUBLANES:]
    for gi in range(N_EXPERT_GROUPS - 2, -1, -1):
        e_in = jnp.where(g_sel == gi, e_all[gi * SUBLANES:(gi + 1) * SUBLANES], e_in)
    ee = jnp.exp(e_in - jnp.max(e_in, axis=0, keepdims=True))
    ep = ee / jnp.sum(ee, axis=0, keepdims=True)
    v1 = jnp.max(ep, axis=0, keepdims=True)
    i1 = jnp.min(jnp.where(ep == v1, ridx, SUBLANES), axis=0, keepdims=True)
    ep2 = jnp.where(ridx == i1, -1.0, ep)
    v2 = jnp.max(ep2, axis=0, keepdims=True)
    i2 = jnp.min(jnp.where(ep2 == v2, ridx, SUBLANES), axis=0, keepdims=True)
    den = v1 + v2
    w1 = g_top * v1 / den
    w2 = g_top * v2 / den
    e1 = g_sel * EXPERTS_PER_GROUP + i1
    e2 = g_sel * EXPERTS_PER_GROUP + i2
    ids = jnp.where(ridx == 0, e1, jnp.where(ridx == 1, e2, 0))
    wts = jnp.where(ridx == 0, w1, jnp.where(ridx == 1, w2, 0.0))
    return ids, wts


def _mix_kernel(x_ref, ya_ref, ys_ref, u_ref, g1_ref, wgate_ref, dsk_ref, wglu_ref, wpa_ref, wps_ref,
                wout_ref, g2_ref, wrh_ref, wrl_ref, br_ref, x1_ref, xp_ref, ids_ref, wts_ref):
    d = x_ref.shape[1]
    x = x_ref[...]
    xb = _rmsnorm_rows(x, g1_ref[...]).astype(BF16)
    gates = _dot(xb, wgate_ref[...])
    y = ys_ref[...].astype(F32) + dsk_ref[...] * u_ref[...].astype(F32)
    y = y * (0.5 * (1.0 + jnp.tanh(math.sqrt(2.0 / math.pi) * (y + 0.044715 * (y * y * y)))))
    y = y * _sigmoid(_dot(y.astype(BF16), wglu_ref[...]))
    mixed = (_sigmoid(gates[:, :d]) * _dot(ya_ref[...], wpa_ref[...])
             + _sigmoid(gates[:, d:]) * _dot(y.astype(BF16), wps_ref[...]))
    x1 = x + _dot(mixed.astype(BF16), wout_ref[...])
    x1_ref[...] = x1
    xn = _rmsnorm_rows(x1, g2_ref[...])
    xp_ref[...] = pltpu.pack_elementwise([xn[:, :d // 2], xn[:, d // 2:]], packed_dtype=BF16)
    xh = xn.astype(BF16)
    xl = (xn - xh.astype(F32)).astype(BF16)
    wrh = wrh_ref[...]
    logits = _nt_dot(wrh, xh) + _nt_dot(wrl_ref[...], xh) + _nt_dot(wrh, xl) + br_ref[...]
    ids, wts = _route(logits)
    ids_ref[...] = ids
    wts_ref[...] = wts


def _mix(x2, ya, ys, u, g1, wgate, dsk, wglu, wpa, wps, wout, g2, wrh, wrl, br):
    t, d = x2.shape
    tm = TM_MIX
    full = lambda a: pl.BlockSpec(a.shape, lambda i: (0,) * a.ndim)
    row = lambda width: pl.BlockSpec((tm, width), lambda i: (i, 0))
    col = pl.BlockSpec((SUBLANES, tm), lambda i: (0, i))
    return pl.pallas_call(
        _mix_kernel,
        grid=(t // tm,),
        in_specs=[row(d), row(ya.shape[1]), row(ys.shape[1]), row(u.shape[1]), full(g1), full(wgate), full(dsk),
                  full(wglu), full(wpa),
                  full(wps), full(wout), full(g2), full(wrh), full(wrl), full(br)],
        out_specs=[row(d), row(d // 2), col, col],
        out_shape=[jax.ShapeDtypeStruct((t, d), F32), jax.ShapeDtypeStruct((t, d // 2), jnp.uint32),
                   jax.ShapeDtypeStruct((SUBLANES, t), I32), jax.ShapeDtypeStruct((SUBLANES, t), F32)],
        compiler_params=pltpu.CompilerParams(dimension_semantics=("arbitrary",), vmem_limit_bytes=VMEM_LIMIT),
        name="mix",
    )(x2, ya, ys, u, g1, wgate, dsk, wglu, wpa, wps, wout, g2, wrh, wrl, br)


def _meta_kernel(ids_ref, tri_ref, dest_ref, tile_ref, *, tile_rows):
    nk, nc, c = ids_ref.shape
    ne = N_EXPERTS
    erow = lax.broadcasted_iota(I32, (ne, c), 0)
    ones = jnp.ones((c, LANES), BF16)

    def onehot(k, ci):
        mask = erow == ids_ref[k, pl.ds(ci, 1), :]
        return mask, jnp.where(mask, 1.0, 0.0).astype(BF16)

    def count_body(n, acc):
        return acc + _dot(onehot(n // nc, n % nc)[1], ones)

    cnt = lax.fori_loop(0, nk * nc, count_body, jnp.zeros((ne, LANES), F32))
    ntiles = jnp.floor((cnt + (tile_rows - 1)) * (1.0 / tile_rows))
    lower = jnp.where(lax.broadcasted_iota(I32, (ne, ne), 1) < lax.broadcasted_iota(I32, (ne, ne), 0), 1.0, 0.0)
    start_tiles = _dot(lower.astype(BF16), ntiles.astype(BF16))
    base = start_tiles * tile_rows

    tri = tri_ref[...]

    def dest_body(n, carry):
        k, ci = n // nc, n % nc
        mask, oh = onehot(k, ci)
        prefix = _dot(oh, tri)
        slot = _lane_tile(base + carry, c // LANES) + prefix - 1.0
        dest = jnp.sum(jnp.where(mask, slot, 0.0), axis=0, keepdims=True)
        dest_ref[k, pl.ds(ci, 1), :] = dest.astype(I32)
        return carry + _dot(oh, ones)

    lax.fori_loop(0, nk * nc, dest_body, jnp.zeros((ne, LANES), F32))

    nt_lanes = tile_ref.shape[1]
    end_tiles = _lane_tile(start_tiles + ntiles, nt_lanes // LANES)
    tidx = lax.broadcasted_iota(I32, (ne, nt_lanes), 1).astype(F32)
    texp = jnp.sum(jnp.where(tidx >= end_tiles, 1.0, 0.0), axis=0, keepdims=True)
    valid = jnp.where(texp < ne, 1, 0)
    texp = jnp.minimum(texp, ne - 1.0).astype(I32)
    ridx = lax.broadcasted_iota(I32, tile_ref.shape, 0)
    tile_ref[...] = jnp.where(ridx == 0, texp, jnp.where(ridx == 1, valid, 0))


def _meta(ids2, tri, n_tiles):
    nk, t = ids2.shape
    c = SORT_CHUNK
    nt_lanes = pl.cdiv(n_tiles, LANES) * LANES
    ids3 = ids2.reshape(nk, t // c, c)
    dest, tile = pl.pallas_call(
        functools.partial(_meta_kernel, tile_rows=TM_EXPERT),
        out_shape=[jax.ShapeDtypeStruct(ids3.shape, I32), jax.ShapeDtypeStruct((SUBLANES, nt_lanes), I32)],
        compiler_params=pltpu.CompilerParams(vmem_limit_bytes=VMEM_LIMIT),
        name="meta",
    )(ids3, tri)
    return dest.reshape(nk * t), tile[0, :n_tiles], tile[1, :n_tiles]


def _row_copy(src_ref, src_row, dst_ref, dst_row, sem):
    return pltpu.make_async_copy(src_ref.at[pl.ds(src_row, 1)], dst_ref.at[pl.ds(dst_row, 1)], sem)


def _dispatch_kernel(dest_ref, xp_ref, xs_in_ref, xs_ref, sem, *, n_tokens):
    del xs_in_ref
    tm = xp_ref.shape[0]
    base = pl.program_id(0) * tm

    def body(r, carry):
        for k in range(2):
            _row_copy(xp_ref, r, xs_ref, dest_ref[k * n_tokens + base + r], sem).start()
        return carry

    lax.fori_loop(0, tm, body, 0)
    for k in range(2):
        pltpu.make_async_copy(xp_ref, xs_ref.at[pl.ds(0, tm)], sem).wait()


def _zero_fill_kernel(o_ref):
    o_ref[...] = jnp.zeros(o_ref.shape, o_ref.dtype)


def _zeros(n_rows, w, dtype):
    rows = n_rows // pl.cdiv(n_rows, ZERO_FILL_ROWS)
    assert n_rows % rows == 0 and rows % SUBLANES == 0
    return pl.pallas_call(
        _zero_fill_kernel,
        grid=(n_rows // rows,),
        out_specs=pl.BlockSpec((rows, w), lambda i: (i, 0)),
        out_shape=jax.ShapeDtypeStruct((n_rows, w), dtype),
        compiler_params=pltpu.CompilerParams(dimension_semantics=("arbitrary",)),
        name="zero_fill",
    )()


def _dispatch(dest, xp, n_slots):
    t, w = xp.shape
    tm = TM_DISPATCH
    xs0 = _zeros(n_slots, w, xp.dtype)
    return pl.pallas_call(
        functools.partial(_dispatch_kernel, n_tokens=t),
        grid_spec=pltpu.PrefetchScalarGridSpec(
            num_scalar_prefetch=1,
            grid=(t // tm,),
            in_specs=[pl.BlockSpec((tm, w), lambda i, d: (i, 0)), pl.BlockSpec(memory_space=pl.ANY)],
            out_specs=pl.BlockSpec(memory_space=pl.ANY),
            scratch_shapes=[pltpu.SemaphoreType.DMA(())]),
        out_shape=jax.ShapeDtypeStruct((n_slots, w), xp.dtype),
        input_output_aliases={2: 0},
        compiler_params=pltpu.CompilerParams(dimension_semantics=("arbitrary",), has_side_effects=True),
        name="dispatch",
    )(dest, xp, xs0)


def _unpack_rows(packed):
    lo = pltpu.unpack_elementwise(packed, index=0, packed_dtype=BF16, unpacked_dtype=F32)
    hi = pltpu.unpack_elementwise(packed, index=1, packed_dtype=BF16, unpacked_dtype=F32)
    return jnp.concatenate([lo, hi], axis=1)


def _expert_kernel(te_ref, tv_ref, xs_ref, wg_ref, wu_ref, wd_ref, ys_ref, wg_bf, wu_bf, wd_bf):
    i = pl.program_id(0)
    half = xs_ref.shape[1]

    @pl.when((i == 0) | (te_ref[i] != te_ref[jnp.maximum(i - 1, 0)]))
    def _():
        wg_bf[...] = wg_ref[...].astype(BF16)
        wu_bf[...] = wu_ref[...].astype(BF16)
        wd_bf[...] = wd_ref[...].astype(BF16)

    @pl.when(tv_ref[i] > 0)
    def _():
        x = _unpack_rows(xs_ref[...]).astype(BF16)
        hg = _dot(x, wg_bf[...])
        hu = _dot(x, wu_bf[...])
        h = (hg * _sigmoid(hg) * hu).astype(BF16)
        y = _dot(h, wd_bf[...])
        ys_ref[...] = pltpu.pack_elementwise([y[:, :half], y[:, half:]], packed_dtype=BF16)

    @pl.when(tv_ref[i] == 0)
    def _():
        zero = jnp.zeros(ys_ref.shape, F32)
        ys_ref[...] = pltpu.pack_elementwise([zero, zero], packed_dtype=BF16)


def _experts(tile_expert, tile_valid, xs, wg, wu, wd):
    n_slots, w = xs.shape
    tm = TM_EXPERT
    wspec = lambda a: pl.BlockSpec((None,) + a.shape[1:], lambda i, te, tv: (te[i], 0, 0))
    return pl.pallas_call(
        _expert_kernel,
        grid_spec=pltpu.PrefetchScalarGridSpec(
            num_scalar_prefetch=2,
            grid=(n_slots // tm,),
            in_specs=[pl.BlockSpec((tm, w), lambda i, te, tv: (i, 0)), wspec(wg), wspec(wu), wspec(wd)],
            out_specs=pl.BlockSpec((tm, w), lambda i, te, tv: (i, 0)),
            scratch_shapes=[pltpu.VMEM(a.shape[1:], BF16) for a in (wg, wu, wd)]),
        out_shape=jax.ShapeDtypeStruct((n_slots, w), xs.dtype),
        compiler_params=pltpu.CompilerParams(dimension_semantics=("arbitrary",), vmem_limit_bytes=VMEM_LIMIT),
        name="experts",
    )(tile_expert, tile_valid, xs, wg, wu, wd)


def _combine_kernel(dest_ref, x1_ref, w0_ref, w1_ref, ys_ref, out_ref, buf, sem, *, n_tokens):
    tm, d = x1_ref.shape
    base = pl.program_id(0) * tm

    def body(r, carry):
        for k in range(2):
            _row_copy(ys_ref, dest_ref[k * n_tokens + base + r], buf.at[k], r, sem).start()
        return carry

    lax.fori_loop(0, tm, body, 0)
    for k in range(2):
        pltpu.make_async_copy(ys_ref.at[pl.ds(0, tm)], buf.at[k], sem).wait()
    reps = d // LANES
    out_ref[...] = (x1_ref[...]
                    + _lane_tile(w0_ref[...], reps) * _unpack_rows(buf[0])
                    + _lane_tile(w1_ref[...], reps) * _unpack_rows(buf[1]))


def _combine(dest, x1, w0, w1, ys):
    t, d = x1.shape
    tm = TM_COMBINE
    row = lambda width: pl.BlockSpec((tm, width), lambda i, dref: (i, 0))
    return pl.pallas_call(
        functools.partial(_combine_kernel, n_tokens=t),
        grid_spec=pltpu.PrefetchScalarGridSpec(
            num_scalar_prefetch=1,
            grid=(t // tm,),
            in_specs=[row(d), row(LANES), row(LANES), pl.BlockSpec(memory_space=pl.ANY)],
            out_specs=row(d),
            scratch_shapes=[pltpu.VMEM((2, tm, ys.shape[1]), ys.dtype), pltpu.SemaphoreType.DMA(())]),
        out_shape=jax.ShapeDtypeStruct((t, d), x1.dtype),
        compiler_params=pltpu.CompilerParams(dimension_semantics=("arbitrary",), vmem_limit_bytes=VMEM_LIMIT),
        name="combine",
    )(dest, x1, w0, w1, ys)


def _lower_tri(n):
    return jnp.where(jnp.arange(n)[:, None] >= jnp.arange(n)[None, :], 1.0, 0.0).astype(BF16)


def _upper_tri(n):
    return jnp.where(jnp.arange(n)[:, None] <= jnp.arange(n)[None, :], 1.0, 0.0).astype(BF16)


def _split_bf16(w):
    hi = w.astype(BF16)
    lo = (w - hi.astype(F32)).astype(BF16)
    return hi, lo


def _layer(x, norm_mix_g, w_in, b_forget, q_norm_g, k_norm_g, ssm_A_re, ssm_A_im, ssm_log_dt, ssm_B_re, ssm_B_im,
           ssm_C_re, ssm_C_im, ssm_D, w_glu, w_proj_attn, w_proj_ssm, w_out, norm_ffn_g, w_router_group,
           b_router_group, w_router_expert, b_router_expert, w_expert_gate, w_expert_up, w_expert_down):
    b, s, d = x.shape
    t = b * s
    da = N_HEADS * HEAD_DIM
    dssm = ssm_D.shape[0]
    nk = s // SUPER
    assert s % SUPER == 0 and nk % SUBLANES == 0 and nk & (nk - 1) == 0 and s % TQ_ATTN == 0 and t % TM_PROJ == 0
    x2 = x.reshape(t, d)

    o_f = 3 * da
    o_u = o_f + N_HEADS
    o_g = o_u + dssm
    w_main = jnp.concatenate([w_in[:, :o_f], w_in[:, o_u:o_g]], axis=1).astype(BF16)
    wf = jnp.repeat(w_in[:, o_f:o_u], FORGET_COPIES_STRIDE, axis=1).astype(BF16)
    bf = jnp.repeat(b_forget, FORGET_COPIES_STRIDE)[None, :]
    w_gates = w_in[:, o_g:].astype(BF16)
    qg = (jnp.tile(q_norm_g, N_HEADS) * (HEAD_DIM ** -0.5 * LOG2E))[None, :]
    kg = jnp.tile(k_norm_g, N_HEADS)[None, :]
    head_of = jnp.arange(da) // HEAD_DIM
    bd = jnp.where(head_of[:, None] == head_of[None, :], 1.0 / HEAD_DIM, 0.0).astype(BF16)
    logit_bound = 1.02 * HEAD_DIM ** 0.5 * jnp.max(jnp.abs(q_norm_g)) * jnp.max(jnp.abs(k_norm_g))
    plain_ok = LOG2E * logit_bound <= PLAIN_SOFTMAX_MAX_LOG2
    shift = jnp.full((1, LANES), logit_bound, F32)
    hw = N_HEADS * LANES
    lane_in_head = jnp.arange(hw) % LANES
    ones = jnp.stack([
        (lane_in_head >= BIAS_K_LANE) & (lane_in_head < BIAS_K_LANE + 3),
        (lane_in_head >= BIAS_Q_LANE) & (lane_in_head < BIAS_Q_LANE + 3),
        lane_in_head == HEAD_DIM]
        + [jnp.zeros((hw,), bool)] * (SUBLANES - 3)).astype(F32)
    prow = jnp.arange(LANES)
    head, copy = prow // FORGET_COPIES_STRIDE, prow % FORGET_COPIES_STRIDE
    target = jnp.where(copy < 3, head * LANES + BIAS_Q_LANE + copy, hw + head * LANES + BIAS_K_LANE + copy - 3)
    place = ((jnp.arange(2 * hw)[None, :] == target[:, None]) & (copy < 6)[:, None]).astype(BF16)

    q, k, v, u = _proj(x2, norm_mix_g[None, :], w_main, wf, bf, qg, kg, bd, _lower_tri(TM_PROJ), place, ones,
                       shift, seq=s)
    y_attn = _attention(q.reshape(b, s, hw), k.reshape(b, s, hw), v.reshape(b, s, hw), plain_ok)
    w_pa = jnp.pad(w_proj_attn.reshape(N_HEADS, HEAD_DIM, d), ((0, 0), (0, LANES - HEAD_DIM), (0, 0)))
    w_pa = w_pa.reshape(hw, d).astype(BF16)

    win, tw, coef = _s5_params(ssm_A_re, ssm_A_im, ssm_log_dt, ssm_B_re, ssm_B_im, ssm_C_re, ssm_C_im, s // SUPER)
    y_ssm = _s5(u.reshape(b, s, dssm), win, tw, coef)

    wr = jnp.concatenate([w_router_expert.T, w_router_group.T,
                          jnp.zeros((ROUTER_ROWS - N_EXPERTS - N_EXPERT_GROUPS, d), F32)], axis=0)
    br = jnp.concatenate([b_router_expert, b_router_group,
                          jnp.full((SUBLANES - N_EXPERT_GROUPS,), NEG_INF, F32),
                          jnp.zeros((ROUTER_ROWS - N_EXPERTS - SUBLANES,), F32)])[:, None]
    x1, xp, ids, wts = _mix(x2, y_attn.reshape(t, hw), y_ssm.reshape(t, dssm), u, norm_mix_g[None, :], w_gates,
                            ssm_D[None, :], w_glu.astype(BF16), w_pa, w_proj_ssm.astype(BF16),
                            w_out.astype(BF16), norm_ffn_g[None, :], *_split_bf16(wr), br)

    n_tiles = (2 * t) // TM_EXPERT + N_EXPERTS
    dest, tile_expert, tile_valid = _meta(ids[:2], _upper_tri(SORT_CHUNK), n_tiles)
    xs = _dispatch(dest, xp, n_tiles * TM_EXPERT)
    ys = _experts(tile_expert, tile_valid, xs, w_expert_gate, w_expert_up, w_expert_down)
    w0 = jnp.broadcast_to(wts[0][:, None], (t, LANES))
    w1 = jnp.broadcast_to(wts[1][:, None], (t, LANES))
    out = _combine(dest, x1, w0, w1, ys)
    return out.reshape(b, s, d)


def kernel(x, norm_mix_g, w_in, b_forget, q_norm_g, k_norm_g, ssm_A_re, ssm_A_im, ssm_log_dt, ssm_B_re, ssm_B_im,
           ssm_C_re, ssm_C_im, ssm_D, w_glu, w_proj_attn, w_proj_ssm, w_out, norm_ffn_g, w_router_group,
           b_router_group, w_router_expert, b_router_expert, w_expert_gate, w_expert_up, w_expert_down):
    layer_params = (norm_mix_g, w_in, b_forget, q_norm_g, k_norm_g, ssm_A_re, ssm_A_im, ssm_log_dt, ssm_B_re,
                    ssm_B_im, ssm_C_re, ssm_C_im, ssm_D, w_glu, w_proj_attn, w_proj_ssm, w_out, norm_ffn_g,
                    w_router_group, b_router_group, w_router_expert, b_router_expert, w_expert_gate, w_expert_up,
                    w_expert_down)
    for layer in range(norm_mix_g.shape[0]):
        x = _layer(x, *[p[layer] for p in layer_params])
    return x
```

```python
import functools
import math

import jax
import jax.numpy as jnp
from jax import lax
from jax.experimental import pallas as pl
from jax.experimental.pallas import tpu as pltpu

F32 = jnp.float32
BF16 = jnp.bfloat16
I32 = jnp.int32

LANES = 128
SUBLANES = 8
MXU_DIM = 256

N_HEADS = 8
HEAD_DIM = 64
SSM_GROUP = 16
SSM_STATE = 64
N_EXPERT_GROUPS = 4
EXPERTS_PER_GROUP = 8
N_EXPERTS = N_EXPERT_GROUPS * EXPERTS_PER_GROUP
EPS = 1e-6
NEG_INF = -1e30
LOG2E = math.log2(math.e)
PLAIN_SOFTMAX_MAX_LOG2 = 60.0
BIAS_K_LANE = HEAD_DIM
BIAS_Q_LANE = HEAD_DIM + 3
FORGET_COPIES_STRIDE = LANES // N_HEADS

CHUNK = MXU_DIM // SSM_GROUP
CHUNKS_PER_SUPER = 8
SUPER = CHUNK * CHUNKS_PER_SUPER
GROUPS_PER_BLOCK = LANES // SSM_GROUP
RELAYOUT_UNROLL = 4
MIN_FACTORED_DECAY = -4.0

TM_PROJ = 512
TQ_ATTN = 2048
TS_ATTN = 512
TM_MIX = 512
TM_EXPERT = 512
TM_DISPATCH = 512
TM_COMBINE = 512
SORT_CHUNK = 512
ZERO_FILL_ROWS = 2048
ROW_DMA_UNROLL = 8
ROUTER_ROWS = 48

VMEM_LIMIT = 48 * 1024 * 1024


def _nt_dot(a, b):
    return lax.dot_general(a, b, (((1,), (1,)), ((), ())), preferred_element_type=F32)


def _dot(a, b):
    return jnp.dot(a, b, preferred_element_type=F32)


def _lane_tile(x, n):
    return x if n == 1 else jnp.concatenate([x] * n, axis=1)


def _rmsnorm_rows(x, g):
    ms = jnp.mean(x * x, axis=-1, keepdims=True)
    return x * lax.rsqrt(ms + EPS) * g


def _sigmoid(x):
    return 1.0 / (1.0 + jnp.exp(-x))


def _split3(x):
    hi = x.astype(BF16)
    r1 = x - hi.astype(F32)
    mid = r1.astype(BF16)
    lo = (r1 - mid.astype(F32)).astype(BF16)
    return [hi, mid, lo]


def _expand_heads(z):
    lane = lax.broadcasted_iota(I32, (z.shape[0], LANES), 1)
    blocks = []
    for p in range(z.shape[1] // LANES):
        blk = z[:, p * LANES:(p + 1) * LANES]
        blocks.append(jnp.where(lane < HEAD_DIM, blk, 0.0))
        blocks.append(jnp.where(lane < HEAD_DIM, pltpu.roll(blk, HEAD_DIM, axis=1), 0.0))
    return jnp.concatenate(blocks, axis=1)


def _proj_kernel(x_ref, g_ref, w_ref, wvt_ref, vones_ref, wf_ref, bf_ref, qg_ref, kg_ref, bd_ref, ltri_ref, place_ref,
                 ones_ref, shift_ref, q_ref, k_ref, vt_ref, u_ref, carry_ref, *, tiles_per_seq):
    i = pl.program_id(0)

    @pl.when(i % tiles_per_seq == 0)
    def _():
        carry_ref[...] = jnp.zeros_like(carry_ref)

    tm = x_ref.shape[0]
    da = N_HEADS * HEAD_DIM
    hw = N_HEADS * LANES
    xb = _rmsnorm_rows(x_ref[...], g_ref[...]).astype(BF16)
    z = _dot(xb, w_ref[...])
    zq, zk = z[:, :da], z[:, da:2 * da]
    vt_ref[...] = (_nt_dot(wvt_ref[...], xb) + vones_ref[...]).astype(BF16)
    bd = bd_ref[...]
    msq = _dot((zq * zq).astype(BF16), bd)
    msk = _dot((zk * zk).astype(BF16), bd)
    qn = zq * lax.rsqrt(msq + EPS) * qg_ref[...]
    kn = zk * lax.rsqrt(msk + EPS) * kg_ref[...]
    u_ref[...] = z[:, 2 * da:]

    a = _dot(xb, wf_ref[...]) + bf_ref[...]
    logf = jnp.minimum(a, 0.0) - jnp.log(1.0 + jnp.exp(-jnp.abs(a)))
    cs = _dot(ltri_ref[...], jnp.concatenate(_split3(logf), axis=1))
    cum = cs[:, :LANES] + cs[:, LANES:2 * LANES] + cs[:, 2 * LANES:] + carry_ref[0:1, :]
    carry_ref[...] = jnp.broadcast_to(cum[tm - 1:tm, :], carry_ref.shape)
    copy = lax.broadcasted_iota(I32, cum.shape, 1) % FORGET_COPIES_STRIDE
    bias = jnp.where(copy < 3, (cum - shift_ref[...]) * LOG2E, cum * (-LOG2E))
    hi, mid, lo = [p.astype(F32) for p in _split3(bias)]
    piece = jnp.where(copy % 3 == 0, hi, jnp.where(copy % 3 == 1, mid, lo)).astype(BF16)
    extras = _dot(piece, place_ref[...])
    q_ref[...] = (_expand_heads(qn) + extras[:, :hw] + ones_ref[0:1, :]).astype(BF16)
    k_ref[...] = (_expand_heads(kn) + extras[:, hw:] + ones_ref[1:2, :]).astype(BF16)


def _proj(x2, norm_g, w_main, wvt, vones, wf, b_f, qg, kg, bd, ltri, place, ones, shift, *, seq):
    t, d = x2.shape
    tm = TM_PROJ
    hw = N_HEADS * LANES
    du = w_main.shape[1] - 2 * N_HEADS * HEAD_DIM
    tiles_per_seq = seq // tm
    full = lambda a: pl.BlockSpec(a.shape, lambda i: (0,) * a.ndim)
    row = lambda width: pl.BlockSpec((tm, width), lambda i: (i, 0))
    consts = (norm_g, w_main, wvt, vones, wf, b_f, qg, kg, bd, ltri, place, ones, shift)
    return pl.pallas_call(
        functools.partial(_proj_kernel, tiles_per_seq=tiles_per_seq),
        grid=(t // tm,),
        in_specs=[row(d)] + [full(a) for a in consts],
        out_specs=[row(hw), row(hw),
                   pl.BlockSpec((None, hw, tm), lambda i: (i // tiles_per_seq, 0, i % tiles_per_seq)), row(du)],
        out_shape=[jax.ShapeDtypeStruct((t, hw), BF16)] * 2
        + [jax.ShapeDtypeStruct((t // seq, hw, seq), BF16), jax.ShapeDtypeStruct((t, du), F32)],
        scratch_shapes=[pltpu.VMEM((SUBLANES, LANES), F32)],
        compiler_params=pltpu.CompilerParams(dimension_semantics=("arbitrary",), vmem_limit_bytes=VMEM_LIMIT),
        name="proj",
    )(x2, *consts)


def _attn_kernel(q_ref, k_ref, vt_ref, o_ref, acc_scr, m_scr, *, ts, safe):
    i = pl.program_id(2)
    nsub = q_ref.shape[0] // ts
    acc_scr[...] = jnp.zeros(acc_scr.shape, F32)
    if safe:
        m_scr[...] = jnp.full(m_scr.shape, NEG_INF, F32)

    def block(sub, j, masked):
        off = pl.multiple_of(j * ts, ts)
        st = _nt_dot(k_ref[pl.ds(off, ts), :], q_ref[sub * ts:(sub + 1) * ts, :])
        if masked:
            keys = lax.broadcasted_iota(I32, (ts, ts), 0)
            queries = lax.broadcasted_iota(I32, (ts, ts), 1)
            st = jnp.where(keys <= queries, st, NEG_INF)
        vt = vt_ref[:, pl.ds(off, ts)]
        if safe:
            m_prev = m_scr[sub]
            m_new = jnp.maximum(m_prev, jnp.max(st, axis=0, keepdims=True))
            p = jnp.exp2(st - m_new[0:1, :]).astype(BF16)
            acc_scr[sub] = jnp.exp2(m_prev - m_new)[0:1, :] * acc_scr[sub] + _dot(vt, p)
            m_scr[sub] = m_new
        else:
            acc_scr[sub] += _dot(vt, jnp.exp2(st).astype(BF16))

    def body(j, carry):
        for sub in range(nsub):
            block(sub, j, False)
        return carry

    lax.fori_loop(0, nsub * i, body, 0)
    for jj in range(nsub):
        for sub in range(jj, nsub):
            block(sub, nsub * i + jj, sub == jj)
    for sub in range(nsub):
        acc = acc_scr[sub]
        out_t = acc / acc[HEAD_DIM:HEAD_DIM + 1, :]
        o_ref[sub * ts:(sub + 1) * ts, :] = out_t.T.astype(o_ref.dtype)


def _attention(q, k, vt, plain_ok):
    b, s, hw = q.shape
    tq, ts = TQ_ATTN, TS_ATTN

    def call(safe):
        return pl.pallas_call(
            functools.partial(_attn_kernel, ts=ts, safe=safe),
            grid=(b, hw // LANES, s // tq),
            in_specs=[pl.BlockSpec((None, tq, LANES), lambda bi, h, i: (bi, i, h)),
                      pl.BlockSpec((None, s, LANES), lambda bi, h, i: (bi, 0, h)),
                      pl.BlockSpec((None, LANES, s), lambda bi, h, i: (bi, h, 0))],
            out_specs=pl.BlockSpec((None, tq, LANES), lambda bi, h, i: (bi, i, h)),
            out_shape=jax.ShapeDtypeStruct((b, s, hw), BF16),
            scratch_shapes=[pltpu.VMEM((tq // ts, LANES, ts), F32), pltpu.VMEM((tq // ts, SUBLANES, ts), F32)],
            compiler_params=pltpu.CompilerParams(
                dimension_semantics=("arbitrary", "arbitrary", "arbitrary"), vmem_limit_bytes=VMEM_LIMIT),
            name="attn_safe" if safe else "attn",
        )(q, k, vt)

    return lax.cond(plain_ok, lambda: call(False), lambda: call(True))


def _window_select(pieces, first_group):
    window = lax.broadcasted_iota(I32, pieces[0].shape, 1) // SSM_GROUP
    out = pieces[0]
    for m in range(1, GROUPS_PER_BLOCK):
        out = jnp.where(window == (first_group + m) % GROUPS_PER_BLOCK, pieces[m], out)
    return out


def _s5_kernel(u_ref, win_ref, tw_ref, coef_ref, y_ref, ub_scr, yb_scr, *, nk):
    nj = CHUNKS_PER_SUPER
    gpb = GROUPS_PER_BLOCK
    w = CHUNK * SSM_GROUP
    half = w // 2
    nkk = nk // SUBLANES
    row_stride = SUPER

    def gather_body(it, carry):
        j, kk = it // nkk, it % nkk
        for sp in range(CHUNK // gpb):
            rot = []
            for m in range(gpb):
                row0 = j * CHUNK + sp * gpb + m + kk * (SUBLANES * row_stride)
                src = u_ref[pl.ds(row0, SUBLANES, stride=row_stride), :]
                rot.append(src if m == 0 else pltpu.roll(src, m * SSM_GROUP, axis=1))
            for gl in range(gpb):
                ub_scr[gl, j, pl.ds(kk * SUBLANES, SUBLANES), sp * LANES:(sp + 1) * LANES] = _window_select(rot, gl)
        return carry

    lax.fori_loop(0, nj * nkk, gather_body, 0, unroll=RELAYOUT_UNROLL)

    def swap(val):
        return jnp.concatenate([val[:, half:], val[:, :half]], axis=1)

    def group_body(gl, carry):
        def cmul(val, idx):
            return coef_ref[gl, idx:idx + 1, :] * val + coef_ref[gl, idx + 1:idx + 2, :] * swap(val)

        u = ub_scr[gl].reshape(nj * nk, w).astype(BF16)
        s1 = _dot(u, win_ref[gl]).reshape(nj, nk, w)
        e = jnp.zeros((nk, w), F32)
        local = []
        for j in range(nj):
            local.append(e)
            e = cmul(e, 0) + s1[j]
        kidx = lax.broadcasted_iota(I32, (nk, w), 0)
        x = e
        step, d = 0, 1
        while d < nk:
            shifted = jnp.where(kidx >= d, pltpu.roll(x, d, axis=0), 0.0)
            x = x + cmul(shifted, 2 + 2 * nj + 2 * step)
            step, d = step + 1, d * 2
        x_start = jnp.where(kidx >= 1, pltpu.roll(x, 1, axis=0), 0.0)
        starts = [local[j] + cmul(x_start, 2 + 2 * j) for j in range(nj)]
        p = jnp.concatenate([st[:, :half] for st in starts], axis=0)
        hi = p.astype(BF16)
        lo = (p - hi.astype(F32)).astype(BF16)
        y = _dot(jnp.concatenate([u, hi, lo], axis=1), tw_ref[gl])
        yb_scr[gl] = y.reshape(nj, nk, w)
        return carry

    lax.fori_loop(0, gpb, group_body, 0)

    def scatter_body(it, carry):
        j, kk = it // nkk, it % nkk
        for tp in range(CHUNK // gpb):
            src = [yb_scr[gl, j, pl.ds(kk * SUBLANES, SUBLANES), tp * LANES:(tp + 1) * LANES] for gl in range(gpb)]
            for m in range(gpb):
                by_window = [src[(wi - m) % gpb] for wi in range(gpb)]
                window = lax.broadcasted_iota(I32, by_window[0].shape, 1) // SSM_GROUP
                merged = by_window[0]
                for wi in range(1, gpb):
                    merged = jnp.where(window == wi, by_window[wi], merged)
                nat = merged if m == 0 else pltpu.roll(merged, LANES - m * SSM_GROUP, axis=1)
                row0 = j * CHUNK + tp * gpb + m + kk * (SUBLANES * row_stride)
                y_ref[pl.ds(row0, SUBLANES, stride=row_stride), :] = nat
        return carry

    lax.fori_loop(0, nj * nkk, scatter_body, 0, unroll=RELAYOUT_UNROLL)


def _s5_params(a_re, a_im, log_dt, b_re, b_im, c_re, c_im, nk):
    args = (a_re, a_im, log_dt, b_re, b_im, c_re, c_im)
    slowest = jnp.min(a_re * jnp.exp(log_dt)[:, None])
    return lax.cond(slowest > MIN_FACTORED_DECAY, functools.partial(_s5_params_factored, nk=nk),
                    functools.partial(_s5_params_direct, nk=nk), *args)


def _s5_params_factored(a_re, a_im, log_dt, b_re, b_im, c_re, c_im, *, nk):
    g, p = a_re.shape
    c = SSM_GROUP
    gpb = GROUPS_PER_BLOCK
    dt = jnp.exp(log_dt)[:, None]
    adt_r, adt_i = a_re * dt, a_im * dt

    def lam_pow(n):
        nf = jnp.asarray(n, F32)
        nf = (nf[None] if nf.ndim == 1 else nf)[:, :, None]
        mag = jnp.exp(adt_r[:, None, :] * nf)
        ang = adt_i[:, None, :] * nf
        return mag * jnp.cos(ang), mag * jnp.sin(ang)

    gi = jnp.arange(g)[:, None, None] % gpb
    step = ((jnp.arange(CHUNK // gpb)[None, :, None]) * gpb
            + (jnp.arange(gpb)[None, None, :] - gi) % gpb).reshape(g, CHUNK)

    l1r, l1i = lam_pow(jnp.ones((1,)))
    den = a_re * a_re + a_im * a_im
    nr, ni = l1r[:, 0] - 1.0, l1i[:, 0]
    qr = (nr * a_re + ni * a_im) / den
    qi = (ni * a_re - nr * a_im) / den
    bcr = (qr[..., None] * b_re - qi[..., None] * b_im).transpose(0, 2, 1)
    bci = (qr[..., None] * b_im + qi[..., None] * b_re).transpose(0, 2, 1)

    def times_b(pr, pi):
        pr, pi = pr[:, :, None, :], pi[:, :, None, :]
        return pr * bcr[:, None] - pi * bci[:, None], pr * bci[:, None] + pi * bcr[:, None]

    def times_c(pr, pi):
        pr, pi = pr[:, :, None, :], pi[:, :, None, :]
        return c_re[:, None] * pr - c_im[:, None] * pi, c_re[:, None] * pi + c_im[:, None] * pr

    rows = lambda parts: jnp.concatenate(parts, axis=-1).reshape(g, CHUNK * c, -1)
    ir, ii = times_b(*lam_pow(CHUNK - 1 - step))
    win = rows([ir, ii, ii, ir])
    clr, cli = times_c(*lam_pow(step + 1))
    wout = rows([clr, -cli]).transpose(0, 2, 1)
    ar, ai = times_b(*lam_pow(-step))
    br, bi = times_c(*lam_pow(step))
    toep = jnp.einsum('gnp,gmp->gnm', rows([ar, ai]), rows([br, -bi]), precision=lax.Precision.HIGHEST)
    step_of_lane = jnp.repeat(step, c, axis=1)
    toep = jnp.where(step_of_lane[:, None, :] >= step_of_lane[:, :, None], toep, 0.0)
    tw = jnp.concatenate([toep, wout, wout], axis=1)
    return win.astype(BF16), tw.astype(BF16), _s5_scan_coefficients(lam_pow, g, p, nk)


def _s5_scan_coefficients(lam_pow, g, p, nk):
    n_steps = max(nk.bit_length() - 1, 0)
    powers = [CHUNK] + [CHUNK * j for j in range(CHUNKS_PER_SUPER)] + [SUPER * (1 << i) for i in range(n_steps)]
    ar, ai = lam_pow(jnp.array(powers, dtype=jnp.int32))
    c1 = jnp.concatenate([ar, ar, ar, ar], axis=2)
    c2 = jnp.concatenate([-ai, ai, ai, -ai], axis=2)
    coef = jnp.stack([c1, c2], axis=2).reshape(g, 2 * len(powers), 4 * p)
    return jnp.pad(coef, ((0, 0), (0, (-coef.shape[1]) % SUBLANES), (0, 0)))


def _s5_params_direct(a_re, a_im, log_dt, b_re, b_im, c_re, c_im, *, nk):
    g, p = a_re.shape
    c = SSM_GROUP
    hp = lax.Precision.HIGHEST
    dt = jnp.exp(log_dt)[:, None]
    adt_r, adt_i = a_re * dt, a_im * dt

    def lam_pow(n):
        nf = jnp.asarray(n, F32)[None, :, None]
        mag = jnp.exp(adt_r[:, None, :] * nf)
        ang = adt_i[:, None, :] * nf
        return mag * jnp.cos(ang), mag * jnp.sin(ang)

    lr, li = lam_pow(jnp.arange(CHUNK + 1))
    den = a_re * a_re + a_im * a_im
    nr, ni = lr[:, 1] - 1.0, li[:, 1]
    qr = (nr * a_re + ni * a_im) / den
    qi = (ni * a_re - nr * a_im) / den
    bbr = qr[..., None] * b_re - qi[..., None] * b_im
    bbi = qr[..., None] * b_im + qi[..., None] * b_re

    mr = lr[:, :CHUNK, :, None] * bbr[:, None] - li[:, :CHUNK, :, None] * bbi[:, None]
    mi = lr[:, :CHUNK, :, None] * bbi[:, None] + li[:, :CHUNK, :, None] * bbr[:, None]
    kern = (jnp.einsum('gcp,gtpd->gtcd', c_re, mr, precision=hp)
            - jnp.einsum('gcp,gtpd->gtcd', c_im, mi, precision=hp))
    toep = jnp.stack([jnp.pad(kern[:, :CHUNK - s], ((0, 0), (s, 0), (0, 0), (0, 0))) for s in range(CHUNK)],
                     axis=1)
    toep = toep.transpose(0, 1, 4, 2, 3).reshape(g, CHUNK * c, CHUNK * c)
    clr = c_re[:, None] * lr[:, 1:, None, :] - c_im[:, None] * li[:, 1:, None, :]
    cli = c_re[:, None] * li[:, 1:, None, :] + c_im[:, None] * lr[:, 1:, None, :]
    w_re = clr.transpose(0, 3, 1, 2).reshape(g, p, CHUNK * c)
    w_im = (-cli).transpose(0, 3, 1, 2).reshape(g, p, CHUNK * c)
    wout = jnp.concatenate([w_re, w_im], axis=1)
    pr, pi = lr[:, :CHUNK][:, ::-1], li[:, :CHUNK][:, ::-1]
    ir = pr[..., None] * bbr[:, None] - pi[..., None] * bbi[:, None]
    ii = pr[..., None] * bbi[:, None] + pi[..., None] * bbr[:, None]
    ir = ir.transpose(0, 1, 3, 2).reshape(g, CHUNK * c, p)
    ii = ii.transpose(0, 1, 3, 2).reshape(g, CHUNK * c, p)
    win = jnp.concatenate([ir, ii, ii, ir], axis=2)

    gi = jnp.arange(g)[:, None, None] % GROUPS_PER_BLOCK
    si = jnp.arange(CHUNK)[None, :, None]
    ci = jnp.arange(c)[None, None, :]
    lane_of = ((si // GROUPS_PER_BLOCK) * LANES + ((gi + si) % GROUPS_PER_BLOCK) * c + ci).reshape(g, CHUNK * c)
    perm = (lane_of[:, None, :] == jnp.arange(CHUNK * c)[None, :, None]).astype(BF16)
    win = jnp.einsum('gln,gnk->glk', perm, win.astype(BF16))
    toep = jnp.einsum('gln,gnk->glk', perm, toep.astype(BF16))
    tw = jnp.concatenate([toep, wout.astype(BF16), wout.astype(BF16)], axis=1)
    tw = jnp.einsum('grn,gln->grl', tw, perm)
    return win.astype(BF16), tw.astype(BF16), _s5_scan_coefficients(lam_pow, g, p, nk)


def _s5(u, win, tw, coef):
    b, s, dssm = u.shape
    nk = s // SUPER
    nj = CHUNKS_PER_SUPER
    w = CHUNK * SSM_GROUP
    gpb = GROUPS_PER_BLOCK
    nblk = dssm // LANES
    wspec = lambda a: pl.BlockSpec((gpb,) + a.shape[1:], lambda bi, li: (li, 0, 0))
    return pl.pallas_call(
        functools.partial(_s5_kernel, nk=nk),
        grid=(b, nblk),
        in_specs=[pl.BlockSpec((None, s, LANES), lambda bi, li: (bi, 0, li)), wspec(win), wspec(tw), wspec(coef)],
        out_specs=pl.BlockSpec((None, s, LANES), lambda bi, li: (bi, 0, li)),
        out_shape=jax.ShapeDtypeStruct((b, s, dssm), F32),
        scratch_shapes=[pltpu.VMEM((gpb, nj, nk, w), F32)] * 2,
        compiler_params=pltpu.CompilerParams(dimension_semantics=("arbitrary", "arbitrary"),
                                             vmem_limit_bytes=VMEM_LIMIT),
        name="s5",
    )(u, win, tw, coef)


def _route(logits):
    e_all = logits[0:N_EXPERTS]
    gl = logits[N_EXPERTS:N_EXPERTS + SUBLANES]
    tm = logits.shape[1]
    ridx = lax.broadcasted_iota(I32, (SUBLANES, tm), 0)
    ge = jnp.exp(gl - jnp.max(gl, axis=0, keepdims=True))
    gp = ge / jnp.sum(ge, axis=0, keepdims=True)
    g_top = jnp.max(gp, axis=0, keepdims=True)
    g_sel = jnp.min(jnp.where(gp == g_top, ridx, SUBLANES), axis=0, keepdims=True)
    e_in = e_all[(N_EXPERT_GROUPS - 1) * SUBLANES:]
    for gi in range(N_EXPERT_GROUPS - 2, -1, -1):
        e_in = jnp.where(g_sel == gi, e_all[gi * SUBLANES:(gi + 1) * SUBLANES], e_in)
    ee = jnp.exp(e_in - jnp.max(e_in, axis=0, keepdims=True))
    ep = ee / jnp.sum(ee, axis=0, keepdims=True)
    v1 = jnp.max(ep, axis=0, keepdims=True)
    i1 = jnp.min(jnp.where(ep == v1, ridx, SUBLANES), axis=0, keepdims=True)
    ep2 = jnp.where(ridx == i1, -1.0, ep)
    v2 = jnp.max(ep2, axis=0, keepdims=True)
    i2 = jnp.min(jnp.where(ep2 == v2, ridx, SUBLANES), axis=0, keepdims=True)
    den = v1 + v2
    w1 = g_top * v1 / den
    w2 = g_top * v2 / den
    e1 = g_sel * EXPERTS_PER_GROUP + i1
    e2 = g_sel * EXPERTS_PER_GROUP + i2
    ids = jnp.where(ridx == 0, e1, jnp.where(ridx == 1, e2, 0))
    wts = jnp.where(ridx == 0, w1, jnp.where(ridx == 1, w2, 0.0))
    return ids, wts


def _mix_kernel(x_ref, ya_ref, ys_ref, u_ref, g1_ref, wgate_ref, dsk_ref, wglu_ref, wpa_ref, wps_ref,
                wout_ref, g2_ref, wrh_ref, wrl_ref, br_ref, x1_ref, xp_ref, ids_ref, wts_ref):
    d = x_ref.shape[1]
    x = x_ref[...]
    xb = _rmsnorm_rows(x, g1_ref[...]).astype(BF16)
    gates = _dot(xb, wgate_ref[...])
    y = ys_ref[...].astype(F32) + dsk_ref[...] * u_ref[...].astype(F32)
    y = y * (0.5 * (1.0 + jnp.tanh(math.sqrt(2.0 / math.pi) * (y + 0.044715 * (y * y * y)))))
    y = y * _sigmoid(_dot(y.astype(BF16), wglu_ref[...]))
    mixed = (_sigmoid(gates[:, :d]) * _dot(ya_ref[...], wpa_ref[...])
             + _sigmoid(gates[:, d:]) * _dot(y.astype(BF16), wps_ref[...]))
    x1 = x + _dot(mixed.astype(BF16), wout_ref[...])
    x1_ref[...] = x1
    xn = _rmsnorm_rows(x1, g2_ref[...])
    xp_ref[...] = pltpu.pack_elementwise([xn[:, :d // 2], xn[:, d // 2:]], packed_dtype=BF16)
    xh = xn.astype(BF16)
    xl = (xn - xh.astype(F32)).astype(BF16)
    wrh = wrh_ref[...]
    logits = _nt_dot(wrh, xh) + _nt_dot(wrl_ref[...], xh) + _nt_dot(wrh, xl) + br_ref[...]
    ids, wts = _route(logits)
    ids_ref[...] = ids
    wts_ref[...] = wts


def _mix(x2, ya, ys, u, g1, wgate, dsk, wglu, wpa, wps, wout, g2, wrh, wrl, br):
    t, d = x2.shape
    tm = TM_MIX
    full = lambda a: pl.BlockSpec(a.shape, lambda i: (0,) * a.ndim)
    row = lambda width: pl.BlockSpec((tm, width), lambda i: (i, 0))
    col = pl.BlockSpec((SUBLANES, tm), lambda i: (0, i))
    return pl.pallas_call(
        _mix_kernel,
        grid=(t // tm,),
        in_specs=[row(d), row(ya.shape[1]), row(ys.shape[1]), row(u.shape[1]), full(g1), full(wgate), full(dsk),
                  full(wglu), full(wpa),
                  full(wps), full(wout), full(g2), full(wrh), full(wrl), full(br)],
        out_specs=[row(d), row(d // 2), col, col],
        out_shape=[jax.ShapeDtypeStruct((t, d), F32), jax.ShapeDtypeStruct((t, d // 2), jnp.uint32),
                   jax.ShapeDtypeStruct((SUBLANES, t), I32), jax.ShapeDtypeStruct((SUBLANES, t), F32)],
        compiler_params=pltpu.CompilerParams(dimension_semantics=("arbitrary",), vmem_limit_bytes=VMEM_LIMIT),
        name="mix",
    )(x2, ya, ys, u, g1, wgate, dsk, wglu, wpa, wps, wout, g2, wrh, wrl, br)


def _meta_kernel(ids_ref, tri_ref, dest_ref, tile_ref, *, tile_rows):
    nk, nc, c = ids_ref.shape
    ne = N_EXPERTS
    erow = lax.broadcasted_iota(I32, (ne, c), 0)
    ones = jnp.ones((c, LANES), BF16)

    def onehot(k, ci):
        mask = erow == ids_ref[k, pl.ds(ci, 1), :]
        return mask, jnp.where(mask, 1.0, 0.0).astype(BF16)

    def count_body(n, acc):
        return acc + _dot(onehot(n // nc, n % nc)[1], ones)

    cnt = lax.fori_loop(0, nk * nc, count_body, jnp.zeros((ne, LANES), F32))
    ntiles = jnp.floor((cnt + (tile_rows - 1)) * (1.0 / tile_rows))
    lower = jnp.where(lax.broadcasted_iota(I32, (ne, ne), 1) < lax.broadcasted_iota(I32, (ne, ne), 0), 1.0, 0.0)
    start_tiles = _dot(lower.astype(BF16), ntiles.astype(BF16))
    base = start_tiles * tile_rows

    tri = tri_ref[...]

    def dest_body(n, carry):
        k, ci = n // nc, n % nc
        mask, oh = onehot(k, ci)
        prefix = _dot(oh, tri)
        slot = _lane_tile(base + carry, c // LANES) + prefix - 1.0
        dest = jnp.sum(jnp.where(mask, slot, 0.0), axis=0, keepdims=True)
        dest_ref[k, pl.ds(ci, 1), :] = dest.astype(I32)
        return carry + _dot(oh, ones)

    lax.fori_loop(0, nk * nc, dest_body, jnp.zeros((ne, LANES), F32))

    nt_lanes = tile_ref.shape[1]
    end_tiles = _lane_tile(start_tiles + ntiles, nt_lanes // LANES)
    tidx = lax.broadcasted_iota(I32, (ne, nt_lanes), 1).astype(F32)
    texp = jnp.sum(jnp.where(tidx >= end_tiles, 1.0, 0.0), axis=0, keepdims=True)
    valid = jnp.where(texp < ne, 1, 0)
    texp = jnp.minimum(texp, ne - 1.0).astype(I32)
    ridx = lax.broadcasted_iota(I32, tile_ref.shape, 0)
    tile_ref[...] = jnp.where(ridx == 0, texp, jnp.where(ridx == 1, valid, 0))


def _meta(ids2, tri, n_tiles):
    nk, t = ids2.shape
    c = SORT_CHUNK
    nt_lanes = pl.cdiv(n_tiles, LANES) * LANES
    ids3 = ids2.reshape(nk, t // c, c)
    dest, tile = pl.pallas_call(
        functools.partial(_meta_kernel, tile_rows=TM_EXPERT),
        out_shape=[jax.ShapeDtypeStruct(ids3.shape, I32), jax.ShapeDtypeStruct((SUBLANES, nt_lanes), I32)],
        compiler_params=pltpu.CompilerParams(vmem_limit_bytes=VMEM_LIMIT),
        name="meta",
    )(ids3, tri)
    return dest.reshape(nk * t), tile[0, :n_tiles], tile[1, :n_tiles]


def _row_copy(src_ref, src_row, dst_ref, dst_row, sem):
    return pltpu.make_async_copy(src_ref.at[pl.ds(src_row, 1)], dst_ref.at[pl.ds(dst_row, 1)], sem)


def _dispatch_kernel(dest_ref, xp_ref, xs_in_ref, xs_ref, sem, *, n_tokens):
    del xs_in_ref
    tm = xp_ref.shape[0]
    base = pl.program_id(0) * tm

    def body(r, carry):
        for k in range(2):
            _row_copy(xp_ref, r, xs_ref, dest_ref[k * n_tokens + base + r], sem).start()
        return carry

    lax.fori_loop(0, tm, body, 0, unroll=ROW_DMA_UNROLL)
    for k in range(2):
        pltpu.make_async_copy(xp_ref, xs_ref.at[pl.ds(0, tm)], sem).wait()


def _zero_fill_kernel(o_ref):
    o_ref[...] = jnp.zeros(o_ref.shape, o_ref.dtype)


def _zeros(n_rows, width, dtype):
    rows = n_rows // pl.cdiv(n_rows, ZERO_FILL_ROWS)
    assert n_rows % rows == 0 and rows % SUBLANES == 0
    return pl.pallas_call(
        _zero_fill_kernel,
        grid=(n_rows // rows,),
        out_specs=pl.BlockSpec((rows, width), lambda i: (i, 0)),
        out_shape=jax.ShapeDtypeStruct((n_rows, width), dtype),
        compiler_params=pltpu.CompilerParams(dimension_semantics=("arbitrary",)),
        name="zero_fill",
    )()


def _dispatch(dest, xp, n_slots):
    t, w = xp.shape
    tm = TM_DISPATCH
    xs0 = _zeros(n_slots, w, xp.dtype)
    return pl.pallas_call(
        functools.partial(_dispatch_kernel, n_tokens=t),
        grid_spec=pltpu.PrefetchScalarGridSpec(
            num_scalar_prefetch=1,
            grid=(t // tm,),
            in_specs=[pl.BlockSpec((tm, w), lambda i, d: (i, 0)), pl.BlockSpec(memory_space=pl.ANY)],
            out_specs=pl.BlockSpec(memory_space=pl.ANY),
            scratch_shapes=[pltpu.SemaphoreType.DMA(())]),
        out_shape=jax.ShapeDtypeStruct(xs0.shape, xp.dtype),
        input_output_aliases={2: 0},
        compiler_params=pltpu.CompilerParams(dimension_semantics=("arbitrary",), has_side_effects=True),
        name="dispatch",
    )(dest, xp, xs0)


def _unpack_rows(packed):
    lo = pltpu.unpack_elementwise(packed, index=0, packed_dtype=BF16, unpacked_dtype=F32)
    hi = pltpu.unpack_elementwise(packed, index=1, packed_dtype=BF16, unpacked_dtype=F32)
    return jnp.concatenate([lo, hi], axis=1)


def _expert_kernel(te_ref, tv_ref, xs_ref, wg_ref, wu_ref, wd_ref, ys_ref, wg_bf, wu_bf, wd_bf):
    i = pl.program_id(0)
    half = xs_ref.shape[1]

    @pl.when((i == 0) | (te_ref[i] != te_ref[jnp.maximum(i - 1, 0)]))
    def _():
        wg_bf[...] = wg_ref[...].astype(BF16)
        wu_bf[...] = wu_ref[...].astype(BF16)
        wd_bf[...] = wd_ref[...].astype(BF16)

    @pl.when(tv_ref[i] > 0)
    def _():
        x = _unpack_rows(xs_ref[...]).astype(BF16)
        hg = _dot(x, wg_bf[...])
        hu = _dot(x, wu_bf[...])
        h = (hg * _sigmoid(hg) * hu).astype(BF16)
        y = _dot(h, wd_bf[...])
        ys_ref[...] = pltpu.pack_elementwise([y[:, :half], y[:, half:]], packed_dtype=BF16)

    @pl.when(tv_ref[i] == 0)
    def _():
        zero = jnp.zeros(ys_ref.shape, F32)
        ys_ref[...] = pltpu.pack_elementwise([zero, zero], packed_dtype=BF16)


def _experts(tile_expert, tile_valid, xs, wg, wu, wd):
    n_slots, w = xs.shape
    tm = TM_EXPERT
    wspec = lambda a: pl.BlockSpec((None,) + a.shape[1:], lambda i, te, tv: (te[i], 0, 0))
    rows = pl.BlockSpec((tm, w), lambda i, te, tv: (i, 0))
    return pl.pallas_call(
        _expert_kernel,
        grid_spec=pltpu.PrefetchScalarGridSpec(
            num_scalar_prefetch=2,
            grid=(n_slots // tm,),
            in_specs=[rows, wspec(wg), wspec(wu), wspec(wd)],
            out_specs=rows,
            scratch_shapes=[pltpu.VMEM(a.shape[1:], BF16) for a in (wg, wu, wd)]),
        out_shape=jax.ShapeDtypeStruct(xs.shape, xs.dtype),
        compiler_params=pltpu.CompilerParams(dimension_semantics=("arbitrary",), vmem_limit_bytes=VMEM_LIMIT),
        name="experts",
    )(tile_expert, tile_valid, xs, wg, wu, wd)


def _combine_kernel(dest_ref, x1_ref, w0_ref, w1_ref, ys_ref, out_ref, buf, sem, *, n_tokens):
    tm, d = x1_ref.shape
    base = pl.program_id(0) * tm

    def body(r, carry):
        for k in range(2):
            _row_copy(ys_ref, dest_ref[k * n_tokens + base + r], buf.at[k], r, sem).start()
        return carry

    lax.fori_loop(0, tm, body, 0, unroll=ROW_DMA_UNROLL)
    for k in range(2):
        pltpu.make_async_copy(ys_ref.at[pl.ds(0, tm)], buf.at[k], sem).wait()
    reps = d // LANES
    out_ref[...] = (x1_ref[...]
                    + _lane_tile(w0_ref[...], reps) * _unpack_rows(buf[0])
                    + _lane_tile(w1_ref[...], reps) * _unpack_rows(buf[1]))


def _combine(dest, x1, w0, w1, ys):
    t, d = x1.shape
    tm = TM_COMBINE
    row = lambda width: pl.BlockSpec((tm, width), lambda i, dref: (i, 0))
    return pl.pallas_call(
        functools.partial(_combine_kernel, n_tokens=t),
        grid_spec=pltpu.PrefetchScalarGridSpec(
            num_scalar_prefetch=1,
            grid=(t // tm,),
            in_specs=[row(d), row(LANES), row(LANES), pl.BlockSpec(memory_space=pl.ANY)],
            out_specs=row(d),
            scratch_shapes=[pltpu.VMEM((2, tm, ys.shape[1]), ys.dtype), pltpu.SemaphoreType.DMA(())]),
        out_shape=jax.ShapeDtypeStruct((t, d), x1.dtype),
        compiler_params=pltpu.CompilerParams(dimension_semantics=("arbitrary",), vmem_limit_bytes=VMEM_LIMIT),
        name="combine",
    )(dest, x1, w0, w1, ys)


def _lower_tri(n):
    return jnp.where(jnp.arange(n)[:, None] >= jnp.arange(n)[None, :], 1.0, 0.0).astype(BF16)


def _upper_tri(n):
    return jnp.where(jnp.arange(n)[:, None] <= jnp.arange(n)[None, :], 1.0, 0.0).astype(BF16)


def _split_bf16(w):
    hi = w.astype(BF16)
    lo = (w - hi.astype(F32)).astype(BF16)
    return hi, lo


def _layer(x, norm_mix_g, w_in, b_forget, q_norm_g, k_norm_g, ssm_A_re, ssm_A_im, ssm_log_dt, ssm_B_re, ssm_B_im,
           ssm_C_re, ssm_C_im, ssm_D, w_glu, w_proj_attn, w_proj_ssm, w_out, norm_ffn_g, w_router_group,
           b_router_group, w_router_expert, b_router_expert, w_expert_gate, w_expert_up, w_expert_down):
    b, s, d = x.shape
    t = b * s
    da = N_HEADS * HEAD_DIM
    dssm = ssm_D.shape[0]
    nk = s // SUPER
    assert s % SUPER == 0 and nk % SUBLANES == 0 and nk & (nk - 1) == 0 and s % TQ_ATTN == 0 and t % TM_PROJ == 0
    x2 = x.reshape(t, d)

    o_f = 3 * da
    o_u = o_f + N_HEADS
    o_g = o_u + dssm
    w_main = jnp.concatenate([w_in[:, :2 * da], w_in[:, o_u:o_g]], axis=1).astype(BF16)
    wvt = jnp.pad(w_in[:, 2 * da:o_f].T.reshape(N_HEADS, HEAD_DIM, d), ((0, 0), (0, LANES - HEAD_DIM), (0, 0)))
    wvt = wvt.reshape(N_HEADS * LANES, d).astype(BF16)
    vones = (jnp.arange(N_HEADS * LANES) % LANES == HEAD_DIM).astype(F32)[:, None]
    wf =jnp.repeat(w_in[:, o_f:o_u], FORGET_COPIES_STRIDE, axis=1).astype(BF16)
    bf = jnp.repeat(b_forget, FORGET_COPIES_STRIDE)[None, :]
    w_gates = w_in[:, o_g:].astype(BF16)
    qg = (jnp.tile(q_norm_g, N_HEADS) * (HEAD_DIM ** -0.5 * LOG2E))[None, :]
    kg = jnp.tile(k_norm_g, N_HEADS)[None, :]
    head_of = jnp.arange(da) // HEAD_DIM
    bd = jnp.where(head_of[:, None] == head_of[None, :], 1.0 / HEAD_DIM, 0.0).astype(BF16)
    logit_bound = 1.02 * HEAD_DIM ** 0.5 * jnp.max(jnp.abs(q_norm_g)) * jnp.max(jnp.abs(k_norm_g))
    plain_ok = LOG2E * logit_bound <= PLAIN_SOFTMAX_MAX_LOG2
    shift = jnp.full((1, LANES), logit_bound, F32)
    hw = N_HEADS * LANES
    lane_in_head = jnp.arange(hw) % LANES
    ones = jnp.stack([
        (lane_in_head >= BIAS_K_LANE) & (lane_in_head < BIAS_K_LANE + 3),
        (lane_in_head >= BIAS_Q_LANE) & (lane_in_head < BIAS_Q_LANE + 3)]
        + [jnp.zeros((hw,), bool)] * (SUBLANES - 2)).astype(F32)
    prow = jnp.arange(LANES)
    head, copy = prow // FORGET_COPIES_STRIDE, prow % FORGET_COPIES_STRIDE
    target = jnp.where(copy < 3, head * LANES + BIAS_Q_LANE + copy, hw + head * LANES + BIAS_K_LANE + copy - 3)
    place = ((jnp.arange(2 * hw)[None, :] == target[:, None]) & (copy < 6)[:, None]).astype(BF16)

    q, k, vt, u = _proj(x2, norm_mix_g[None, :], w_main, wvt, vones, wf, bf, qg, kg, bd, _lower_tri(TM_PROJ), place,
                        ones, shift, seq=s)
    y_attn = _attention(q.reshape(b, s, hw), k.reshape(b, s, hw), vt, plain_ok)
    w_pa = jnp.pad(w_proj_attn.reshape(N_HEADS, HEAD_DIM, d), ((0, 0), (0, LANES - HEAD_DIM), (0, 0)))
    w_pa = w_pa.reshape(hw, d).astype(BF16)

    win, tw, coef = _s5_params(ssm_A_re, ssm_A_im, ssm_log_dt, ssm_B_re, ssm_B_im, ssm_C_re, ssm_C_im, s // SUPER)
    y_ssm = _s5(u.reshape(b, s, dssm), win, tw, coef)

    wr = jnp.concatenate([w_router_expert.T, w_router_group.T,
                          jnp.zeros((ROUTER_ROWS - N_EXPERTS - N_EXPERT_GROUPS, d), F32)], axis=0)
    br = jnp.concatenate([b_router_expert, b_router_group,
                          jnp.full((SUBLANES - N_EXPERT_GROUPS,), NEG_INF, F32),
                          jnp.zeros((ROUTER_ROWS - N_EXPERTS - SUBLANES,), F32)])[:, None]
    x1, xp, ids, wts = _mix(x2, y_attn.reshape(t, hw), y_ssm.reshape(t, dssm), u, norm_mix_g[None, :], w_gates,
                            ssm_D[None, :], w_glu.astype(BF16), w_pa, w_proj_ssm.astype(BF16),
                            w_out.astype(BF16), norm_ffn_g[None, :], *_split_bf16(wr), br)

    n_tiles = (2 * t) // TM_EXPERT + N_EXPERTS
    dest, tile_expert, tile_valid = _meta(ids[:2], _upper_tri(SORT_CHUNK), n_tiles)
    xs = _dispatch(dest, xp, n_tiles * TM_EXPERT)
    ys = _experts(tile_expert, tile_valid, xs, w_expert_gate, w_expert_up, w_expert_down)
    w0 = jnp.broadcast_to(wts[0][:, None], (t, LANES))
    w1 = jnp.broadcast_to(wts[1][:, None], (t, LANES))
    out = _combine(dest, x1, w0, w1, ys)
    return out.reshape(b, s, d)


def kernel(x, norm_mix_g, w_in, b_forget, q_norm_g, k_norm_g, ssm_A_re, ssm_A_im, ssm_log_dt, ssm_B_re, ssm_B_im,
           ssm_C_re, ssm_C_im, ssm_D, w_glu, w_proj_attn, w_proj_ssm, w_out, norm_ffn_g, w_router_group,
           b_router_group, w_router_expert, b_router_expert, w_expert_gate, w_expert_up, w_expert_down):
    layer_params = (norm_mix_g, w_in, b_forget, q_norm_g, k_norm_g, ssm_A_re, ssm_A_im, ssm_log_dt, ssm_B_re,
                    ssm_B_im, ssm_C_re, ssm_C_im, ssm_D, w_glu, w_proj_attn, w_proj_ssm, w_out, norm_ffn_g,
                    w_router_group, b_router_group, w_router_expert, b_router_expert, w_expert_gate, w_expert_up,
                    w_expert_down)
    for layer in range(norm_mix_g.shape[0]):
        x = _layer(x, *[p[layer] for p in layer_params])
    return x
```

```python
import functools
import math

import jax
import jax.numpy as jnp
from jax import lax
from jax.experimental import pallas as pl
from jax.experimental.pallas import tpu as pltpu

F32 = jnp.float32
BF16 = jnp.bfloat16
I32 = jnp.int32

LANES = 128
SUBLANES = 8
MXU_DIM = 256

N_HEADS = 8
HEAD_DIM = 64
SSM_GROUP = 16
SSM_STATE = 64
N_EXPERT_GROUPS = 4
EXPERTS_PER_GROUP = 8
N_EXPERTS = N_EXPERT_GROUPS * EXPERTS_PER_GROUP
EPS = 1e-6
NEG_INF = -1e30
LOG2E = math.log2(math.e)
PLAIN_SOFTMAX_MAX_LOG2 = 60.0
UNDERFLOW_LOG2 = 160.0
BIAS_K_LANE = HEAD_DIM
BIAS_Q_LANE = HEAD_DIM + 3
FORGET_COPIES_STRIDE = LANES // N_HEADS

CHUNK = MXU_DIM // SSM_GROUP
CHUNKS_PER_SUPER = 8
SUPER = CHUNK * CHUNKS_PER_SUPER
GROUPS_PER_BLOCK = LANES // SSM_GROUP
RELAYOUT_UNROLL = 4
MIN_FACTORED_DECAY = -4.0

TM_PROJ = 512
TQ_ATTN = 1024
TS_ATTN = 512
TM_MIX = 512
TM_EXPERT = 512
TM_DISPATCH = 512
TM_COMBINE = 512
SORT_CHUNK = 512
ZERO_FILL_ROWS = 2048
ROW_DMA_UNROLL = 8
ROUTER_ROWS = 48

VMEM_LIMIT = 48 * 1024 * 1024


def _nt_dot(a, b):
    return lax.dot_general(a, b, (((1,), (1,)), ((), ())), preferred_element_type=F32)


def _dot(a, b):
    return jnp.dot(a, b, preferred_element_type=F32)


def _lane_tile(x, n):
    return x if n == 1 else jnp.concatenate([x] * n, axis=1)


def _rmsnorm_rows(x, g):
    ms = jnp.mean(x * x, axis=-1, keepdims=True)
    return x * lax.rsqrt(ms + EPS) * g


def _sigmoid(x):
    return 1.0 / (1.0 + jnp.exp(-x))


def _split3(x):
    hi = x.astype(BF16)
    r1 = x - hi.astype(F32)
    mid = r1.astype(BF16)
    lo = (r1 - mid.astype(F32)).astype(BF16)
    return [hi, mid, lo]


def _expand_heads(z):
    lane = lax.broadcasted_iota(I32, (z.shape[0], LANES), 1)
    blocks = []
    for p in range(z.shape[1] // LANES):
        blk = z[:, p * LANES:(p + 1) * LANES]
        blocks.append(jnp.where(lane < HEAD_DIM, blk, 0.0))
        blocks.append(jnp.where(lane < HEAD_DIM, pltpu.roll(blk, HEAD_DIM, axis=1), 0.0))
    return jnp.concatenate(blocks, axis=1)


def _proj_kernel(x_ref, g_ref, w_ref, wvt_ref, vones_ref, wf_ref, bf_ref, qg_ref, kg_ref, bd_ref, ltri_ref, place_ref,
                 ones_ref, shift_ref, q_ref, k_ref, vt_ref, u_ref, fedge_ref, carry_ref, *, tiles_per_seq):
    i = pl.program_id(0)

    @pl.when(i % tiles_per_seq == 0)
    def _():
        carry_ref[...] = jnp.zeros_like(carry_ref)

    tm = x_ref.shape[0]
    da = N_HEADS * HEAD_DIM
    hw = N_HEADS * LANES
    xb = _rmsnorm_rows(x_ref[...], g_ref[...]).astype(BF16)
    z = _dot(xb, w_ref[...])
    zq, zk = z[:, :da], z[:, da:2 * da]
    vt_ref[...] = (_nt_dot(wvt_ref[...], xb) + vones_ref[...]).astype(BF16)
    bd = bd_ref[...]
    msq = _dot((zq * zq).astype(BF16), bd)
    msk = _dot((zk * zk).astype(BF16), bd)
    qn = zq * lax.rsqrt(msq + EPS) * qg_ref[...]
    kn = zk * lax.rsqrt(msk + EPS) * kg_ref[...]
    u_ref[...] = z[:, 2 * da:]

    a = _dot(xb, wf_ref[...]) + bf_ref[...]
    logf = jnp.minimum(a, 0.0) - jnp.log(1.0 + jnp.exp(-jnp.abs(a)))
    cs = _dot(ltri_ref[...], jnp.concatenate(_split3(logf), axis=1))
    cum = cs[:, :LANES] + cs[:, LANES:2 * LANES] + cs[:, 2 * LANES:] + carry_ref[0:1, :]
    carry_ref[...] = jnp.broadcast_to(cum[tm - 1:tm, :], carry_ref.shape)
    edge_row = lax.broadcasted_iota(I32, fedge_ref.shape, 0)
    fedge_ref[...] = jnp.where(edge_row == 0, cum[0:1, :], jnp.where(edge_row == 1, cum[tm - 1:tm, :], 0.0))
    copy = lax.broadcasted_iota(I32, cum.shape, 1) % FORGET_COPIES_STRIDE
    bias = jnp.where(copy < 3, (cum - shift_ref[...]) * LOG2E, cum * (-LOG2E))
    hi, mid, lo = [p.astype(F32) for p in _split3(bias)]
    piece = jnp.where(copy % 3 == 0, hi, jnp.where(copy % 3 == 1, mid, lo)).astype(BF16)
    extras = _dot(piece, place_ref[...])
    q_ref[...] = (_expand_heads(qn) + extras[:, :hw] + ones_ref[0:1, :]).astype(BF16)
    k_ref[...] = (_expand_heads(kn) + extras[:, hw:] + ones_ref[1:2, :]).astype(BF16)


def _proj(x2, norm_g, w_main, wvt, vones, wf, b_f, qg, kg, bd, ltri, place, ones, shift, *, seq):
    t, d = x2.shape
    tm = TM_PROJ
    hw = N_HEADS * LANES
    du = w_main.shape[1] - 2 * N_HEADS * HEAD_DIM
    tiles_per_seq = seq // tm
    full = lambda a: pl.BlockSpec(a.shape, lambda i: (0,) * a.ndim)
    row = lambda width: pl.BlockSpec((tm, width), lambda i: (i, 0))
    consts = (norm_g, w_main, wvt, vones, wf, b_f, qg, kg, bd, ltri, place, ones, shift)
    return pl.pallas_call(
        functools.partial(_proj_kernel, tiles_per_seq=tiles_per_seq),
        grid=(t // tm,),
        in_specs=[row(d)] + [full(a) for a in consts],
        out_specs=[row(hw), row(hw),
                   pl.BlockSpec((None, hw, tm), lambda i: (i // tiles_per_seq, 0, i % tiles_per_seq)), row(du),
                   pl.BlockSpec((None, SUBLANES, LANES), lambda i: (i, 0, 0))],
        out_shape=[jax.ShapeDtypeStruct((t, hw), BF16)] * 2
        + [jax.ShapeDtypeStruct((t // seq, hw, seq), BF16), jax.ShapeDtypeStruct((t, du), F32),
           jax.ShapeDtypeStruct((t // tm, SUBLANES, LANES), F32)],
        scratch_shapes=[pltpu.VMEM((SUBLANES, LANES), F32)],
        compiler_params=pltpu.CompilerParams(dimension_semantics=("arbitrary",), vmem_limit_bytes=VMEM_LIMIT),
        name="proj",
    )(x2, *consts)


def _attn_kernel(first_ref, q_ref, k_ref, vt_ref, o_ref, acc_scr, m_scr, *, ts, safe):
    i = pl.program_id(2)
    nsub = q_ref.shape[0] // ts
    acc_scr[...] = jnp.zeros(acc_scr.shape, F32)
    if safe:
        m_scr[...] = jnp.full(m_scr.shape, NEG_INF, F32)

    def block(sub, j, masked):
        off = pl.multiple_of(j * ts, ts)
        st = _nt_dot(k_ref[pl.ds(off, ts), :], q_ref[sub * ts:(sub + 1) * ts, :])
        if masked:
            keys = lax.broadcasted_iota(I32, (ts, ts), 0)
            queries = lax.broadcasted_iota(I32, (ts, ts), 1)
            st = jnp.where(keys <= queries, st, NEG_INF)
        vt = vt_ref[:, pl.ds(off, ts)]
        if safe:
            m_prev = m_scr[sub]
            m_new = jnp.maximum(m_prev, jnp.max(st, axis=0, keepdims=True))
            p = jnp.exp2(st - m_new[0:1, :]).astype(BF16)
            acc_scr[sub] = jnp.exp2(m_prev - m_new)[0:1, :] * acc_scr[sub] + _dot(vt, p)
            m_scr[sub] = m_new
        else:
            acc_scr[sub] += _dot(vt, jnp.exp2(st).astype(BF16))

    def body(j, carry):
        for sub in range(nsub):
            block(sub, j, False)
        return carry

    step = (pl.program_id(0) * pl.num_programs(1) + pl.program_id(1)) * pl.num_programs(2) + i
    lax.fori_loop(first_ref[step], nsub * i, body, 0)
    for jj in range(nsub):
        for sub in range(jj, nsub):
            block(sub, nsub * i + jj, sub == jj)
    for sub in range(nsub):
        acc = acc_scr[sub]
        out_t = acc / acc[HEAD_DIM:HEAD_DIM + 1, :]
        o_ref[sub * ts:(sub + 1) * ts, :] = out_t.T.astype(o_ref.dtype)


def _first_kv_block(fedge, b, s, tq, ts):
    nblk = s // ts
    f_first = fedge[:, 0, ::FORGET_COPIES_STRIDE].reshape(b, nblk, N_HEADS).transpose(0, 2, 1)
    f_last = fedge[:, 1, ::FORGET_COPIES_STRIDE].reshape(b, nblk, N_HEADS).transpose(0, 2, 1)
    tile_start = f_first[:, :, ::tq // ts]
    bound = LOG2E * (tile_start[:, :, :, None] - f_last[:, :, None, :])
    return jnp.sum(bound < -UNDERFLOW_LOG2, axis=-1).astype(I32).reshape(-1)


def _attention(q, k, vt, fedge, plain_ok):
    b, s, hw = q.shape
    tq, ts = TQ_ATTN, TS_ATTN
    nq = s // tq

    def call(safe):
        first = jnp.zeros((b * N_HEADS * nq,), I32) if safe else _first_kv_block(fedge, b, s, tq, ts)
        return pl.pallas_call(
            functools.partial(_attn_kernel, ts=ts, safe=safe),
            grid_spec=pltpu.PrefetchScalarGridSpec(
                num_scalar_prefetch=1,
                grid=(b, hw // LANES, nq),
                in_specs=[pl.BlockSpec((None, tq, LANES), lambda bi, h, i, f: (bi, i, h)),
                          pl.BlockSpec((None, s, LANES), lambda bi, h, i, f: (bi, 0, h)),
                          pl.BlockSpec((None, LANES, s), lambda bi, h, i, f: (bi, h, 0))],
                out_specs=pl.BlockSpec((None, tq, LANES), lambda bi, h, i, f: (bi, i, h)),
                scratch_shapes=[pltpu.VMEM((tq // ts, LANES, ts), F32), pltpu.VMEM((tq // ts, SUBLANES, ts), F32)]),
            out_shape=jax.ShapeDtypeStruct((b, s, hw), BF16),
            compiler_params=pltpu.CompilerParams(
                dimension_semantics=("arbitrary", "arbitrary", "arbitrary"), vmem_limit_bytes=VMEM_LIMIT),
            name="attn_safe" if safe else "attn",
        )(first, q, k, vt)

    return lax.cond(plain_ok, lambda: call(False), lambda: call(True))


def _window_select(pieces, first_group):
    window = lax.broadcasted_iota(I32, pieces[0].shape, 1) // SSM_GROUP
    out = pieces[0]
    for m in range(1, GROUPS_PER_BLOCK):
        out = jnp.where(window == (first_group + m) % GROUPS_PER_BLOCK, pieces[m], out)
    return out


def _s5_kernel(u_ref, win_ref, tw_ref, coef_ref, y_ref, ub_scr, yb_scr, *, nk):
    nj = CHUNKS_PER_SUPER
    gpb = GROUPS_PER_BLOCK
    w = CHUNK * SSM_GROUP
    half = w // 2
    nkk = nk // SUBLANES
    row_stride = SUPER

    def gather_body(it, carry):
        j, kk = it // nkk, it % nkk
        for sp in range(CHUNK // gpb):
            rot = []
            for m in range(gpb):
                row0 = j * CHUNK + sp * gpb + m + kk * (SUBLANES * row_stride)
                src = u_ref[pl.ds(row0, SUBLANES, stride=row_stride), :]
                rot.append(src if m == 0 else pltpu.roll(src, m * SSM_GROUP, axis=1))
            for gl in range(gpb):
                ub_scr[gl, j, pl.ds(kk * SUBLANES, SUBLANES), sp * LANES:(sp + 1) * LANES] = _window_select(rot, gl)
        return carry

    lax.fori_loop(0, nj * nkk, gather_body, 0, unroll=RELAYOUT_UNROLL)

    def swap(val):
        return jnp.concatenate([val[:, half:], val[:, :half]], axis=1)

    def group_body(gl, carry):
        def cmul(val, idx):
            return coef_ref[gl, idx:idx + 1, :] * val + coef_ref[gl, idx + 1:idx + 2, :] * swap(val)

        u = ub_scr[gl].reshape(nj * nk, w).astype(BF16)
        s1 = _dot(u, win_ref[gl]).reshape(nj, nk, w)
        e = jnp.zeros((nk, w), F32)
        local = []
        for j in range(nj):
            local.append(e)
            e = cmul(e, 0) + s1[j]
        kidx = lax.broadcasted_iota(I32, (nk, w), 0)
        x = e
        step, d = 0, 1
        while d < nk:
            shifted = jnp.where(kidx >= d, pltpu.roll(x, d, axis=0), 0.0)
            x = x + cmul(shifted, 2 + 2 * nj + 2 * step)
            step, d = step + 1, d * 2
        x_start = jnp.where(kidx >= 1, pltpu.roll(x, 1, axis=0), 0.0)
        starts = [local[j] + cmul(x_start, 2 + 2 * j) for j in range(nj)]
        p = jnp.concatenate([st[:, :half] for st in starts], axis=0)
        hi = p.astype(BF16)
        lo = (p - hi.astype(F32)).astype(BF16)
        y = _dot(jnp.concatenate([u, hi, lo], axis=1), tw_ref[gl])
        yb_scr[gl] = y.reshape(nj, nk, w)
        return carry

    lax.fori_loop(0, gpb, group_body, 0)

    def scatter_body(it, carry):
        j, kk = it // nkk, it % nkk
        for tp in range(CHUNK // gpb):
            src = [yb_scr[gl, j, pl.ds(kk * SUBLANES, SUBLANES), tp * LANES:(tp + 1) * LANES] for gl in range(gpb)]
            for m in range(gpb):
                by_window = [src[(wi - m) % gpb] for wi in range(gpb)]
                window = lax.broadcasted_iota(I32, by_window[0].shape, 1) // SSM_GROUP
                merged = by_window[0]
                for wi in range(1, gpb):
                    merged = jnp.where(window == wi, by_window[wi], merged)
                nat = merged if m == 0 else pltpu.roll(merged, LANES - m * SSM_GROUP, axis=1)
                row0 = j * CHUNK + tp * gpb + m + kk * (SUBLANES * row_stride)
                y_ref[pl.ds(row0, SUBLANES, stride=row_stride), :] = nat
        return carry

    lax.fori_loop(0, nj * nkk, scatter_body, 0, unroll=RELAYOUT_UNROLL)


def _s5_params(a_re, a_im, log_dt, b_re, b_im, c_re, c_im, nk):
    args = (a_re, a_im, log_dt, b_re, b_im, c_re, c_im)
    slowest = jnp.min(a_re * jnp.exp(log_dt)[:, None])
    return lax.cond(slowest > MIN_FACTORED_DECAY, functools.partial(_s5_params_factored, nk=nk),
                    functools.partial(_s5_params_direct, nk=nk), *args)


def _s5_params_factored(a_re, a_im, log_dt, b_re, b_im, c_re, c_im, *, nk):
    g, p = a_re.shape
    c = SSM_GROUP
    gpb = GROUPS_PER_BLOCK
    dt = jnp.exp(log_dt)[:, None]
    adt_r, adt_i = a_re * dt, a_im * dt

    def lam_pow(n):
        nf = jnp.asarray(n, F32)
        nf = (nf[None] if nf.ndim == 1 else nf)[:, :, None]
        mag = jnp.exp(adt_r[:, None, :] * nf)
        ang = adt_i[:, None, :] * nf
        return mag * jnp.cos(ang), mag * jnp.sin(ang)

    gi = jnp.arange(g)[:, None, None] % gpb
    step = ((jnp.arange(CHUNK // gpb)[None, :, None]) * gpb
            + (jnp.arange(gpb)[None, None, :] - gi) % gpb).reshape(g, CHUNK)

    l1r, l1i = lam_pow(jnp.ones((1,)))
    den = a_re * a_re + a_im * a_im
    nr, ni = l1r[:, 0] - 1.0, l1i[:, 0]
    qr = (nr * a_re + ni * a_im) / den
    qi = (ni * a_re - nr * a_im) / den
    bcr = (qr[..., None] * b_re - qi[..., None] * b_im).transpose(0, 2, 1)
    bci = (qr[..., None] * b_im + qi[..., None] * b_re).transpose(0, 2, 1)

    def times_b(pr, pi):
        pr, pi = pr[:, :, None, :], pi[:, :, None, :]
        return pr * bcr[:, None] - pi * bci[:, None], pr * bci[:, None] + pi * bcr[:, None]

    def times_c(pr, pi):
        pr, pi = pr[:, :, None, :], pi[:, :, None, :]
        return c_re[:, None] * pr - c_im[:, None] * pi, c_re[:, None] * pi + c_im[:, None] * pr

    rows = lambda parts: jnp.concatenate(parts, axis=-1).reshape(g, CHUNK * c, -1)
    ir, ii = times_b(*lam_pow(CHUNK - 1 - step))
    win = rows([ir, ii, ii, ir])
    clr, cli = times_c(*lam_pow(step + 1))
    wout = rows([clr, -cli]).transpose(0, 2, 1)
    ar, ai = times_b(*lam_pow(-step))
    br, bi = times_c(*lam_pow(step))
    toep = jnp.einsum('gnp,gmp->gnm', rows([ar, ai]), rows([br, -bi]), precision=lax.Precision.HIGHEST)
    step_of_lane = jnp.repeat(step, c, axis=1)
    toep = jnp.where(step_of_lane[:, None, :] >= step_of_lane[:, :, None], toep, 0.0)
    tw = jnp.concatenate([toep, wout, wout], axis=1)
    return win.astype(BF16), tw.astype(BF16), _s5_scan_coefficients(lam_pow, g, p, nk)


def _s5_scan_coefficients(lam_pow, g, p, nk):
    n_steps = max(nk.bit_length() - 1, 0)
    powers = [CHUNK] + [CHUNK * j for j in range(CHUNKS_PER_SUPER)] + [SUPER * (1 << i) for i in range(n_steps)]
    ar, ai = lam_pow(jnp.array(powers, dtype=jnp.int32))
    c1 = jnp.concatenate([ar, ar, ar, ar], axis=2)
    c2 = jnp.concatenate([-ai, ai, ai, -ai], axis=2)
    coef = jnp.stack([c1, c2], axis=2).reshape(g, 2 * len(powers), 4 * p)
    return jnp.pad(coef, ((0, 0), (0, (-coef.shape[1]) % SUBLANES), (0, 0)))


def _s5_params_direct(a_re, a_im, log_dt, b_re, b_im, c_re, c_im, *, nk):
    g, p = a_re.shape
    c = SSM_GROUP
    hp = lax.Precision.HIGHEST
    dt = jnp.exp(log_dt)[:, None]
    adt_r, adt_i = a_re * dt, a_im * dt

    def lam_pow(n):
        nf = jnp.asarray(n, F32)[None, :, None]
        mag = jnp.exp(adt_r[:, None, :] * nf)
        ang = adt_i[:, None, :] * nf
        return mag * jnp.cos(ang), mag * jnp.sin(ang)

    lr, li = lam_pow(jnp.arange(CHUNK + 1))
    den = a_re * a_re + a_im * a_im
    nr, ni = lr[:, 1] - 1.0, li[:, 1]
    qr = (nr * a_re + ni * a_im) / den
    qi = (ni * a_re - nr * a_im) / den
    bbr = qr[..., None] * b_re - qi[..., None] * b_im
    bbi = qr[..., None] * b_im + qi[..., None] * b_re

    mr = lr[:, :CHUNK, :, None] * bbr[:, None] - li[:, :CHUNK, :, None] * bbi[:, None]
    mi = lr[:, :CHUNK, :, None] * bbi[:, None] + li[:, :CHUNK, :, None] * bbr[:, None]
    kern = (jnp.einsum('gcp,gtpd->gtcd', c_re, mr, precision=hp)
            - jnp.einsum('gcp,gtpd->gtcd', c_im, mi, precision=hp))
    toep = jnp.stack([jnp.pad(kern[:, :CHUNK - s], ((0, 0), (s, 0), (0, 0), (0, 0))) for s in range(CHUNK)],
                     axis=1)
    toep = toep.transpose(0, 1, 4, 2, 3).reshape(g, CHUNK * c, CHUNK * c)
    clr = c_re[:, None] * lr[:, 1:, None, :] - c_im[:, None] * li[:, 1:, None, :]
    cli = c_re[:, None] * li[:, 1:, None, :] + c_im[:, None] * lr[:, 1:, None, :]
    w_re = clr.transpose(0, 3, 1, 2).reshape(g, p, CHUNK * c)
    w_im = (-cli).transpose(0, 3, 1, 2).reshape(g, p, CHUNK * c)
    wout = jnp.concatenate([w_re, w_im], axis=1)
    pr, pi = lr[:, :CHUNK][:, ::-1], li[:, :CHUNK][:, ::-1]
    ir = pr[..., None] * bbr[:, None] - pi[..., None] * bbi[:, None]
    ii = pr[..., None] * bbi[:, None] + pi[..., None] * bbr[:, None]
    ir = ir.transpose(0, 1, 3, 2).reshape(g, CHUNK * c, p)
    ii = ii.transpose(0, 1, 3, 2).reshape(g, CHUNK * c, p)
    win = jnp.concatenate([ir, ii, ii, ir], axis=2)

    gi = jnp.arange(g)[:, None, None] % GROUPS_PER_BLOCK
    si = jnp.arange(CHUNK)[None, :, None]
    ci = jnp.arange(c)[None, None, :]
    lane_of = ((si // GROUPS_PER_BLOCK) * LANES + ((gi + si) % GROUPS_PER_BLOCK) * c + ci).reshape(g, CHUNK * c)
    perm = (lane_of[:, None, :] == jnp.arange(CHUNK * c)[None, :, None]).astype(BF16)
    win = jnp.einsum('gln,gnk->glk', perm, win.astype(BF16))
    toep = jnp.einsum('gln,gnk->glk', perm, toep.astype(BF16))
    tw = jnp.concatenate([toep, wout.astype(BF16), wout.astype(BF16)], axis=1)
    tw = jnp.einsum('grn,gln->grl', tw, perm)
    return win.astype(BF16), tw.astype(BF16), _s5_scan_coefficients(lam_pow, g, p, nk)


def _s5(u, win, tw, coef):
    b, s, dssm = u.shape
    nk = s // SUPER
    nj = CHUNKS_PER_SUPER
    w = CHUNK * SSM_GROUP
    gpb = GROUPS_PER_BLOCK
    nblk = dssm // LANES
    wspec = lambda a: pl.BlockSpec((gpb,) + a.shape[1:], lambda bi, li: (li, 0, 0))
    return pl.pallas_call(
        functools.partial(_s5_kernel, nk=nk),
        grid=(b, nblk),
        in_specs=[pl.BlockSpec((None, s, LANES), lambda bi, li: (bi, 0, li)), wspec(win), wspec(tw), wspec(coef)],
        out_specs=pl.BlockSpec((None, s, LANES), lambda bi, li: (bi, 0, li)),
        out_shape=jax.ShapeDtypeStruct((b, s, dssm), F32),
        scratch_shapes=[pltpu.VMEM((gpb, nj, nk, w), F32)] * 2,
        compiler_params=pltpu.CompilerParams(dimension_semantics=("arbitrary", "arbitrary"),
                                             vmem_limit_bytes=VMEM_LIMIT),
        name="s5",
    )(u, win, tw, coef)


def _route(logits):
    e_all = logits[0:N_EXPERTS]
    gl = logits[N_EXPERTS:N_EXPERTS + SUBLANES]
    tm = logits.shape[1]
    ridx = lax.broadcasted_iota(I32, (SUBLANES, tm), 0)
    ge = jnp.exp(gl - jnp.max(gl, axis=0, keepdims=True))
    gp = ge / jnp.sum(ge, axis=0, keepdims=True)
    g_top = jnp.max(gp, axis=0, keepdims=True)
    g_sel = jnp.min(jnp.where(gp == g_top, ridx, SUBLANES), axis=0, keepdims=True)
    e_in = e_all[(N_EXPERT_GROUPS - 1) * SUBLANES:]
    for gi in range(N_EXPERT_GROUPS - 2, -1, -1):
        e_in = jnp.where(g_sel == gi, e_all[gi * SUBLANES:(gi + 1) * SUBLANES], e_in)
    ee = jnp.exp(e_in - jnp.max(e_in, axis=0, keepdims=True))
    ep = ee / jnp.sum(ee, axis=0, keepdims=True)
    v1 = jnp.max(ep, axis=0, keepdims=True)
    i1 = jnp.min(jnp.where(ep == v1, ridx, SUBLANES), axis=0, keepdims=True)
    ep2 = jnp.where(ridx == i1, -1.0, ep)
    v2 = jnp.max(ep2, axis=0, keepdims=True)
    i2 = jnp.min(jnp.where(ep2 == v2, ridx, SUBLANES), axis=0, keepdims=True)
    den = v1 + v2
    w1 = g_top * v1 / den
    w2 = g_top * v2 / den
    e1 = g_sel * EXPERTS_PER_GROUP + i1
    e2 = g_sel * EXPERTS_PER_GROUP + i2
    ids = jnp.where(ridx == 0, e1, jnp.where(ridx == 1, e2, 0))
    wts = jnp.where(ridx == 0, w1, jnp.where(ridx == 1, w2, 0.0))
    return ids, wts


def _mix_kernel(x_ref, ya_ref, ys_ref, u_ref, g1_ref, wgate_ref, dsk_ref, wglu_ref, wpa_ref, wps_ref,
                wout_ref, g2_ref, wrh_ref, wrl_ref, br_ref, x1_ref, xp_ref, ids_ref, wts_ref):
    d = x_ref.shape[1]
    x = x_ref[...]
    xb = _rmsnorm_rows(x, g1_ref[...]).astype(BF16)
    gates = _dot(xb, wgate_ref[...])
    y = ys_ref[...].astype(F32) + dsk_ref[...] * u_ref[...].astype(F32)
    y = y * (0.5 * (1.0 + jnp.tanh(math.sqrt(2.0 / math.pi) * (y + 0.044715 * (y * y * y)))))
    y = y * _sigmoid(_dot(y.astype(BF16), wglu_ref[...]))
    mixed = (_sigmoid(gates[:, :d]) * _dot(ya_ref[...], wpa_ref[...])
             + _sigmoid(gates[:, d:]) * _dot(y.astype(BF16), wps_ref[...]))
    x1 = x + _dot(mixed.astype(BF16), wout_ref[...])
    x1_ref[...] = x1
    xn = _rmsnorm_rows(x1, g2_ref[...])
    xp_ref[...] = pltpu.pack_elementwise([xn[:, :d // 2], xn[:, d // 2:]], packed_dtype=BF16)
    xh = xn.astype(BF16)
    xl = (xn - xh.astype(F32)).astype(BF16)
    wrh = wrh_ref[...]
    logits = _nt_dot(wrh, xh) + _nt_dot(wrl_ref[...], xh) + _nt_dot(wrh, xl) + br_ref[...]
    ids, wts = _route(logits)
    ids_ref[...] = ids
    wts_ref[...] = wts


def _mix(x2, ya, ys, u, g1, wgate, dsk, wglu, wpa, wps, wout, g2, wrh, wrl, br):
    t, d = x2.shape
    tm = TM_MIX
    full = lambda a: pl.BlockSpec(a.shape, lambda i: (0,) * a.ndim)
    row = lambda width: pl.BlockSpec((tm, width), lambda i: (i, 0))
    col = pl.BlockSpec((SUBLANES, tm), lambda i: (0, i))
    return pl.pallas_call(
        _mix_kernel,
        grid=(t // tm,),
        in_specs=[row(d), row(ya.shape[1]), row(ys.shape[1]), row(u.shape[1]), full(g1), full(wgate), full(dsk),
                  full(wglu), full(wpa),
                  full(wps), full(wout), full(g2), full(wrh), full(wrl), full(br)],
        out_specs=[row(d), row(d // 2), col, col],
        out_shape=[jax.ShapeDtypeStruct((t, d), F32), jax.ShapeDtypeStruct((t, d // 2), jnp.uint32),
                   jax.ShapeDtypeStruct((SUBLANES, t), I32), jax.ShapeDtypeStruct((SUBLANES, t), F32)],
        compiler_params=pltpu.CompilerParams(dimension_semantics=("arbitrary",), vmem_limit_bytes=VMEM_LIMIT),
        name="mix",
    )(x2, ya, ys, u, g1, wgate, dsk, wglu, wpa, wps, wout, g2, wrh, wrl, br)


def _meta_kernel(ids_ref, tri_ref, dest_ref, tile_ref, *, tile_rows):
    nk, nc, c = ids_ref.shape
    ne = N_EXPERTS
    erow = lax.broadcasted_iota(I32, (ne, c), 0)
    ones = jnp.ones((c, LANES), BF16)

    def onehot(k, ci):
        mask = erow == ids_ref[k, pl.ds(ci, 1), :]
        return mask, jnp.where(mask, 1.0, 0.0).astype(BF16)

    def count_body(n, acc):
        return acc + _dot(onehot(n // nc, n % nc)[1], ones)

    cnt = lax.fori_loop(0, nk * nc, count_body, jnp.zeros((ne, LANES), F32))
    ntiles = jnp.floor((cnt + (tile_rows - 1)) * (1.0 / tile_rows))
    lower = jnp.where(lax.broadcasted_iota(I32, (ne, ne), 1) < lax.broadcasted_iota(I32, (ne, ne), 0), 1.0, 0.0)
    start_tiles = _dot(lower.astype(BF16), ntiles.astype(BF16))
    base = start_tiles * tile_rows

    tri = tri_ref[...]

    def dest_body(n, carry):
        k, ci = n // nc, n % nc
        mask, oh = onehot(k, ci)
        prefix = _dot(oh, tri)
        slot = _lane_tile(base + carry, c // LANES) + prefix - 1.0
        dest = jnp.sum(jnp.where(mask, slot, 0.0), axis=0, keepdims=True)
        dest_ref[k, pl.ds(ci, 1), :] = dest.astype(I32)
        return carry + _dot(oh, ones)

    lax.fori_loop(0, nk * nc, dest_body, jnp.zeros((ne, LANES), F32))

    nt_lanes = tile_ref.shape[1]
    end_tiles = _lane_tile(start_tiles + ntiles, nt_lanes // LANES)
    tidx = lax.broadcasted_iota(I32, (ne, nt_lanes), 1).astype(F32)
    texp = jnp.sum(jnp.where(tidx >= end_tiles, 1.0, 0.0), axis=0, keepdims=True)
    valid = jnp.where(texp < ne, 1, 0)
    texp = jnp.minimum(texp, ne - 1.0).astype(I32)
    ridx = lax.broadcasted_iota(I32, tile_ref.shape, 0)
    tile_ref[...] = jnp.where(ridx == 0, texp, jnp.where(ridx == 1, valid, 0))


def _meta(ids2, tri, n_tiles):
    nk, t = ids2.shape
    c = SORT_CHUNK
    nt_lanes = pl.cdiv(n_tiles, LANES) * LANES
    ids3 = ids2.reshape(nk, t // c, c)
    dest, tile = pl.pallas_call(
        functools.partial(_meta_kernel, tile_rows=TM_EXPERT),
        out_shape=[jax.ShapeDtypeStruct(ids3.shape, I32), jax.ShapeDtypeStruct((SUBLANES, nt_lanes), I32)],
        compiler_params=pltpu.CompilerParams(vmem_limit_bytes=VMEM_LIMIT),
        name="meta",
    )(ids3, tri)
    return dest.reshape(nk * t), tile[0, :n_tiles], tile[1, :n_tiles]


def _row_copy(src_ref, src_row, dst_ref, dst_row, sem):
    return pltpu.make_async_copy(src_ref.at[pl.ds(src_row, 1)], dst_ref.at[pl.ds(dst_row, 1)], sem)


def _dispatch_kernel(dest_ref, xp_ref, xs_in_ref, xs_ref, sem, *, n_tokens):
    del xs_in_ref
    tm = xp_ref.shape[0]
    base = pl.program_id(0) * tm

    def body(r, carry):
        for k in range(2):
            _row_copy(xp_ref, r, xs_ref, dest_ref[k * n_tokens + base + r], sem).start()
        return carry

    lax.fori_loop(0, tm, body, 0, unroll=ROW_DMA_UNROLL)
    for k in range(2):
        pltpu.make_async_copy(xp_ref, xs_ref.at[pl.ds(0, tm)], sem).wait()


def _zero_fill_kernel(o_ref):
    o_ref[...] = jnp.zeros(o_ref.shape, o_ref.dtype)


def _zeros(n_rows, width, dtype):
    rows = n_rows // pl.cdiv(n_rows, ZERO_FILL_ROWS)
    assert n_rows % rows == 0 and rows % SUBLANES == 0
    return pl.pallas_call(
        _zero_fill_kernel,
        grid=(n_rows // rows,),
        out_specs=pl.BlockSpec((rows, width), lambda i: (i, 0)),
        out_shape=jax.ShapeDtypeStruct((n_rows, width), dtype),
        compiler_params=pltpu.CompilerParams(dimension_semantics=("arbitrary",)),
        name="zero_fill",
    )()


def _dispatch(dest, xp, n_slots):
    t, w = xp.shape
    tm = TM_DISPATCH
    xs0 = _zeros(n_slots, w, xp.dtype)
    return pl.pallas_call(
        functools.partial(_dispatch_kernel, n_tokens=t),
        grid_spec=pltpu.PrefetchScalarGridSpec(
            num_scalar_prefetch=1,
            grid=(t // tm,),
            in_specs=[pl.BlockSpec((tm, w), lambda i, d: (i, 0)), pl.BlockSpec(memory_space=pl.ANY)],
            out_specs=pl.BlockSpec(memory_space=pl.ANY),
            scratch_shapes=[pltpu.SemaphoreType.DMA(())]),
        out_shape=jax.ShapeDtypeStruct(xs0.shape, xp.dtype),
        input_output_aliases={2: 0},
        compiler_params=pltpu.CompilerParams(dimension_semantics=("arbitrary",), has_side_effects=True),
        name="dispatch",
    )(dest, xp, xs0)


def _unpack_rows(packed):
    lo = pltpu.unpack_elementwise(packed, index=0, packed_dtype=BF16, unpacked_dtype=F32)
    hi = pltpu.unpack_elementwise(packed, index=1, packed_dtype=BF16, unpacked_dtype=F32)
    return jnp.concatenate([lo, hi], axis=1)


def _expert_kernel(te_ref, tv_ref, xs_ref, wg_ref, wu_ref, wd_ref, ys_ref, wg_bf, wu_bf, wd_bf):
    i = pl.program_id(0)
    half = xs_ref.shape[1]

    @pl.when((i == 0) | (te_ref[i] != te_ref[jnp.maximum(i - 1, 0)]))
    def _():
        wg_bf[...] = wg_ref[...].astype(BF16)
        wu_bf[...] = wu_ref[...].astype(BF16)
        wd_bf[...] = wd_ref[...].astype(BF16)

    @pl.when(tv_ref[i] > 0)
    def _():
        x = _unpack_rows(xs_ref[...]).astype(BF16)
        hg = _dot(x, wg_bf[...])
        hu = _dot(x, wu_bf[...])
        h = (hg * _sigmoid(hg) * hu).astype(BF16)
        y = _dot(h, wd_bf[...])
        ys_ref[...] = pltpu.pack_elementwise([y[:, :half], y[:, half:]], packed_dtype=BF16)

    @pl.when(tv_ref[i] == 0)
    def _():
        zero = jnp.zeros(ys_ref.shape, F32)
        ys_ref[...] = pltpu.pack_elementwise([zero, zero], packed_dtype=BF16)


def _experts(tile_expert, tile_valid, xs, wg, wu, wd):
    n_slots, w = xs.shape
    tm = TM_EXPERT
    wspec = lambda a: pl.BlockSpec((None,) + a.shape[1:], lambda i, te, tv: (te[i], 0, 0))
    rows = pl.BlockSpec((tm, w), lambda i, te, tv: (i, 0))
    return pl.pallas_call(
        _expert_kernel,
        grid_spec=pltpu.PrefetchScalarGridSpec(
            num_scalar_prefetch=2,
            grid=(n_slots // tm,),
            in_specs=[rows, wspec(wg), wspec(wu), wspec(wd)],
            out_specs=rows,
            scratch_shapes=[pltpu.VMEM(a.shape[1:], BF16) for a in (wg, wu, wd)]),
        out_shape=jax.ShapeDtypeStruct(xs.shape, xs.dtype),
        compiler_params=pltpu.CompilerParams(dimension_semantics=("arbitrary",), vmem_limit_bytes=VMEM_LIMIT),
        name="experts",
    )(tile_expert, tile_valid, xs, wg, wu, wd)


def _combine_kernel(dest_ref, x1_ref, w0_ref, w1_ref, ys_ref, out_ref, buf, sem, *, n_tokens):
    tm, d = x1_ref.shape
    base = pl.program_id(0) * tm

    def body(r, carry):
        for k in range(2):
            _row_copy(ys_ref, dest_ref[k * n_tokens + base + r], buf.at[k], r, sem).start()
        return carry

    lax.fori_loop(0, tm, body, 0, unroll=ROW_DMA_UNROLL)
    for k in range(2):
        pltpu.make_async_copy(ys_ref.at[pl.ds(0, tm)], buf.at[k], sem).wait()
    reps = d // LANES
    out_ref[...] = (x1_ref[...]
                    + _lane_tile(w0_ref[...], reps) * _unpack_rows(buf[0])
                    + _lane_tile(w1_ref[...], reps) * _unpack_rows(buf[1]))


def _combine(dest, x1, w0, w1, ys):
    t, d = x1.shape
    tm = TM_COMBINE
    row = lambda width: pl.BlockSpec((tm, width), lambda i, dref: (i, 0))
    return pl.pallas_call(
        functools.partial(_combine_kernel, n_tokens=t),
        grid_spec=pltpu.PrefetchScalarGridSpec(
            num_scalar_prefetch=1,
            grid=(t // tm,),
            in_specs=[row(d), row(LANES), row(LANES), pl.BlockSpec(memory_space=pl.ANY)],
            out_specs=row(d),
            scratch_shapes=[pltpu.VMEM((2, tm, ys.shape[1]), ys.dtype), pltpu.SemaphoreType.DMA(())]),
        out_shape=jax.ShapeDtypeStruct((t, d), x1.dtype),
        compiler_params=pltpu.CompilerParams(dimension_semantics=("arbitrary",), vmem_limit_bytes=VMEM_LIMIT),
        name="combine",
    )(dest, x1, w0, w1, ys)


def _lower_tri(n):
    return jnp.where(jnp.arange(n)[:, None] >= jnp.arange(n)[None, :], 1.0, 0.0).astype(BF16)


def _upper_tri(n):
    return jnp.where(jnp.arange(n)[:, None] <= jnp.arange(n)[None, :], 1.0, 0.0).astype(BF16)


def _split_bf16(w):
    hi = w.astype(BF16)
    lo = (w - hi.astype(F32)).astype(BF16)
    return hi, lo


def _layer(x, norm_mix_g, w_in, b_forget, q_norm_g, k_norm_g, ssm_A_re, ssm_A_im, ssm_log_dt, ssm_B_re, ssm_B_im,
           ssm_C_re, ssm_C_im, ssm_D, w_glu, w_proj_attn, w_proj_ssm, w_out, norm_ffn_g, w_router_group,
           b_router_group, w_router_expert, b_router_expert, w_expert_gate, w_expert_up, w_expert_down):
    b, s, d = x.shape
    t = b * s
    da = N_HEADS * HEAD_DIM
    dssm = ssm_D.shape[0]
    nk = s // SUPER
    assert s % SUPER == 0 and nk % SUBLANES == 0 and nk & (nk - 1) == 0 and s % TQ_ATTN == 0 and t % TM_PROJ == 0
    assert TM_PROJ == TS_ATTN
    x2 = x.reshape(t, d)

    o_f = 3 * da
    o_u = o_f + N_HEADS
    o_g = o_u + dssm
    w_main = jnp.concatenate([w_in[:, :2 * da], w_in[:, o_u:o_g]], axis=1).astype(BF16)
    wvt = jnp.pad(w_in[:, 2 * da:o_f].T.reshape(N_HEADS, HEAD_DIM, d), ((0, 0), (0, LANES - HEAD_DIM), (0, 0)))
    wvt = wvt.reshape(N_HEADS * LANES, d).astype(BF16)
    vones = (jnp.arange(N_HEADS * LANES) % LANES == HEAD_DIM).astype(F32)[:, None]
    wf =jnp.repeat(w_in[:, o_f:o_u], FORGET_COPIES_STRIDE, axis=1).astype(BF16)
    bf = jnp.repeat(b_forget, FORGET_COPIES_STRIDE)[None, :]
    w_gates = w_in[:, o_g:].astype(BF16)
    qg = (jnp.tile(q_norm_g, N_HEADS) * (HEAD_DIM ** -0.5 * LOG2E))[None, :]
    kg = jnp.tile(k_norm_g, N_HEADS)[None, :]
    head_of = jnp.arange(da) // HEAD_DIM
    bd = jnp.where(head_of[:, None] == head_of[None, :], 1.0 / HEAD_DIM, 0.0).astype(BF16)
    logit_bound = 1.02 * HEAD_DIM ** 0.5 * jnp.max(jnp.abs(q_norm_g)) * jnp.max(jnp.abs(k_norm_g))
    plain_ok = LOG2E * logit_bound <= PLAIN_SOFTMAX_MAX_LOG2
    shift = jnp.full((1, LANES), logit_bound, F32)
    hw = N_HEADS * LANES
    lane_in_head = jnp.arange(hw) % LANES
    ones = jnp.stack([
        (lane_in_head >= BIAS_K_LANE) & (lane_in_head < BIAS_K_LANE + 3),
        (lane_in_head >= BIAS_Q_LANE) & (lane_in_head < BIAS_Q_LANE + 3)]
        + [jnp.zeros((hw,), bool)] * (SUBLANES - 2)).astype(F32)
    prow = jnp.arange(LANES)
    head, copy = prow // FORGET_COPIES_STRIDE, prow % FORGET_COPIES_STRIDE
    target = jnp.where(copy < 3, head * LANES + BIAS_Q_LANE + copy, hw + head * LANES + BIAS_K_LANE + copy - 3)
    place = ((jnp.arange(2 * hw)[None, :] == target[:, None]) & (copy < 6)[:, None]).astype(BF16)

    q, k, vt, u, fedge = _proj(x2, norm_mix_g[None, :], w_main, wvt, vones, wf, bf, qg, kg, bd, _lower_tri(TM_PROJ),
                               place, ones, shift, seq=s)
    y_attn = _attention(q.reshape(b, s, hw), k.reshape(b, s, hw), vt, fedge, plain_ok)
    w_pa = jnp.pad(w_proj_attn.reshape(N_HEADS, HEAD_DIM, d), ((0, 0), (0, LANES - HEAD_DIM), (0, 0)))
    w_pa = w_pa.reshape(hw, d).astype(BF16)

    win, tw, coef = _s5_params(ssm_A_re, ssm_A_im, ssm_log_dt, ssm_B_re, ssm_B_im, ssm_C_re, ssm_C_im, s // SUPER)
    y_ssm = _s5(u.reshape(b, s, dssm), win, tw, coef)

    wr = jnp.concatenate([w_router_expert.T, w_router_group.T,
                          jnp.zeros((ROUTER_ROWS - N_EXPERTS - N_EXPERT_GROUPS, d), F32)], axis=0)
    br = jnp.concatenate([b_router_expert, b_router_group,
                          jnp.full((SUBLANES - N_EXPERT_GROUPS,), NEG_INF, F32),
                          jnp.zeros((ROUTER_ROWS - N_EXPERTS - SUBLANES,), F32)])[:, None]
    x1, xp, ids, wts = _mix(x2, y_attn.reshape(t, hw), y_ssm.reshape(t, dssm), u, norm_mix_g[None, :], w_gates,
                            ssm_D[None, :], w_glu.astype(BF16), w_pa, w_proj_ssm.astype(BF16),
                            w_out.astype(BF16), norm_ffn_g[None, :], *_split_bf16(wr), br)

    n_tiles = (2 * t) // TM_EXPERT + N_EXPERTS
    dest, tile_expert, tile_valid = _meta(ids[:2], _upper_tri(SORT_CHUNK), n_tiles)
    xs = _dispatch(dest, xp, n_tiles * TM_EXPERT)
    ys = _experts(tile_expert, tile_valid, xs, w_expert_gate, w_expert_up, w_expert_down)
    w0 = jnp.broadcast_to(wts[0][:, None], (t, LANES))
    w1 = jnp.broadcast_to(wts[1][:, None], (t, LANES))
    out = _combine(dest, x1, w0, w1, ys)
    return out.reshape(b, s, d)


def kernel(x, norm_mix_g, w_in, b_forget, q_norm_g, k_norm_g, ssm_A_re, ssm_A_im, ssm_log_dt, ssm_B_re, ssm_B_im,
           ssm_C_re, ssm_C_im, ssm_D, w_glu, w_proj_attn, w_proj_ssm, w_out, norm_ffn_g, w_router_group,
           b_router_group, w_router_expert, b_router_expert, w_expert_gate, w_expert_up, w_expert_down):
    layer_params = (norm_mix_g, w_in, b_forget, q_norm_g, k_norm_g, ssm_A_re, ssm_A_im, ssm_log_dt, ssm_B_re,
                    ssm_B_im, ssm_C_re, ssm_C_im, ssm_D, w_glu, w_proj_attn, w_proj_ssm, w_out, norm_ffn_g,
                    w_router_group, b_router_group, w_router_expert, b_router_expert, w_expert_gate, w_expert_up,
                    w_expert_down)
    for layer in range(norm_mix_g.shape[0]):
        x = _layer(x, *[p[layer] for p in layer_params])
    return x
```

```python
import functools
import math

import jax
import jax.numpy as jnp
from jax import lax
from jax.experimental import pallas as pl
from jax.experimental.pallas import tpu as pltpu

F32 = jnp.float32
BF16 = jnp.bfloat16
I32 = jnp.int32

LANES = 128
SUBLANES = 8
MXU_DIM = 256

N_HEADS = 8
HEAD_DIM = 64
SSM_GROUP = 16
SSM_STATE = 64
N_EXPERT_GROUPS = 4
EXPERTS_PER_GROUP = 8
N_EXPERTS = N_EXPERT_GROUPS * EXPERTS_PER_GROUP
EPS = 1e-6
NEG_INF = -1e30
LOG2E = math.log2(math.e)
PLAIN_SOFTMAX_MAX_LOG2 = 60.0
UNDERFLOW_LOG2 = 160.0
BIAS_K_LANE = HEAD_DIM
BIAS_Q_LANE = HEAD_DIM + 3
FORGET_COPIES_STRIDE = LANES // N_HEADS

CHUNK = MXU_DIM // SSM_GROUP
CHUNKS_PER_SUPER = 8
SUPER = CHUNK * CHUNKS_PER_SUPER
GROUPS_PER_BLOCK = LANES // SSM_GROUP
RELAYOUT_UNROLL = 8
MIN_FACTORED_DECAY = -4.0

TM_PROJ = 512
TQ_ATTN = 2048
TS_ATTN = 512
KV_UNROLL = 1
TM_MIX = 512
TM_EXPERT = 512
TM_DISPATCH = 512
TM_COMBINE = 512
SORT_CHUNK = 512
ZERO_FILL_ROWS = 2048
ROW_DMA_UNROLL = 8
ROUTE_LANES = 2048
ROUTER_ROWS = 48

VMEM_LIMIT = 48 * 1024 * 1024


def _nt_dot(a, b):
    return lax.dot_general(a, b, (((1,), (1,)), ((), ())), preferred_element_type=F32)


def _dot(a, b):
    return jnp.dot(a, b, preferred_element_type=F32)


def _lane_tile(x, n):
    return x if n == 1 else jnp.concatenate([x] * n, axis=1)


def _rmsnorm_rows(x, g):
    ms = jnp.mean(x * x, axis=-1, keepdims=True)
    return x * lax.rsqrt(ms + EPS) * g


def _sigmoid(x):
    return 1.0 / (1.0 + jnp.exp(-x))


def _split3(x):
    hi = x.astype(BF16)
    r1 = x - hi.astype(F32)
    mid = r1.astype(BF16)
    lo = (r1 - mid.astype(F32)).astype(BF16)
    return [hi, mid, lo]


def _expand_heads(z):
    lane = lax.broadcasted_iota(I32, (z.shape[0], LANES), 1)
    blocks = []
    for p in range(z.shape[1] // LANES):
        blk = z[:, p * LANES:(p + 1) * LANES]
        blocks.append(jnp.where(lane < HEAD_DIM, blk, 0.0))
        blocks.append(jnp.where(lane < HEAD_DIM, pltpu.roll(blk, HEAD_DIM, axis=1), 0.0))
    return jnp.concatenate(blocks, axis=1)


def _proj_kernel(x_ref, g_ref, w_ref, wvt_ref, vones_ref, wf_ref, bf_ref, qg_ref, kg_ref, bd_ref, ltri_ref, place_ref,
                 ones_ref, shift_ref, q_ref, k_ref, vt_ref, u_ref, fedge_ref, carry_ref, *, tiles_per_seq):
    i = pl.program_id(0)

    @pl.when(i % tiles_per_seq == 0)
    def _():
        carry_ref[...] = jnp.zeros_like(carry_ref)

    tm = x_ref.shape[0]
    da = N_HEADS * HEAD_DIM
    hw = N_HEADS * LANES
    xb = _rmsnorm_rows(x_ref[...], g_ref[...]).astype(BF16)
    z = _dot(xb, w_ref[...])
    zq, zk = z[:, :da], z[:, da:2 * da]
    vt_ref[...] = (_nt_dot(wvt_ref[...], xb) + vones_ref[...]).astype(BF16)
    bd = bd_ref[...]
    msq = _dot((zq * zq).astype(BF16), bd)
    msk = _dot((zk * zk).astype(BF16), bd)
    qn = zq * lax.rsqrt(msq + EPS) * qg_ref[...]
    kn = zk * lax.rsqrt(msk + EPS) * kg_ref[...]
    u_ref[...] = z[:, 2 * da:]

    a = _dot(xb, wf_ref[...]) + bf_ref[...]
    logf = jnp.minimum(a, 0.0) - jnp.log(1.0 + jnp.exp(-jnp.abs(a)))
    cs = _dot(ltri_ref[...], jnp.concatenate(_split3(logf), axis=1))
    cum = cs[:, :LANES] + cs[:, LANES:2 * LANES] + cs[:, 2 * LANES:] + carry_ref[0:1, :]
    carry_ref[...] = jnp.broadcast_to(cum[tm - 1:tm, :], carry_ref.shape)
    edge_row = lax.broadcasted_iota(I32, fedge_ref.shape, 0)
    fedge_ref[...] = jnp.where(edge_row == 0, cum[0:1, :], jnp.where(edge_row == 1, cum[tm - 1:tm, :], 0.0))
    copy = lax.broadcasted_iota(I32, cum.shape, 1) % FORGET_COPIES_STRIDE
    bias = jnp.where(copy < 3, (cum - shift_ref[...]) * LOG2E, cum * (-LOG2E))
    hi, mid, lo = [p.astype(F32) for p in _split3(bias)]
    piece = jnp.where(copy % 3 == 0, hi, jnp.where(copy % 3 == 1, mid, lo)).astype(BF16)
    extras = _dot(piece, place_ref[...])
    q_ref[...] = (_expand_heads(qn) + extras[:, :hw] + ones_ref[0:1, :]).astype(BF16)
    k_ref[...] = (_expand_heads(kn) + extras[:, hw:] + ones_ref[1:2, :]).astype(BF16)


def _proj(x2, norm_g, w_main, wvt, vones, wf, b_f, qg, kg, bd, ltri, place, ones, shift, *, seq):
    t, d = x2.shape
    tm = TM_PROJ
    hw = N_HEADS * LANES
    du = w_main.shape[1] - 2 * N_HEADS * HEAD_DIM
    tiles_per_seq = seq // tm
    full = lambda a: pl.BlockSpec(a.shape, lambda i: (0,) * a.ndim)
    row = lambda width: pl.BlockSpec((tm, width), lambda i: (i, 0))
    consts = (norm_g, w_main, wvt, vones, wf, b_f, qg, kg, bd, ltri, place, ones, shift)
    return pl.pallas_call(
        functools.partial(_proj_kernel, tiles_per_seq=tiles_per_seq),
        grid=(t // tm,),
        in_specs=[row(d)] + [full(a) for a in consts],
        out_specs=[row(hw), row(hw),
                   pl.BlockSpec((None, hw, tm), lambda i: (i // tiles_per_seq, 0, i % tiles_per_seq)), row(du),
                   pl.BlockSpec((None, SUBLANES, LANES), lambda i: (i, 0, 0))],
        out_shape=[jax.ShapeDtypeStruct((t, hw), BF16)] * 2
        + [jax.ShapeDtypeStruct((t // seq, hw, seq), BF16), jax.ShapeDtypeStruct((t, du), F32),
           jax.ShapeDtypeStruct((t // tm, SUBLANES, LANES), F32)],
        scratch_shapes=[pltpu.VMEM((SUBLANES, LANES), F32)],
        compiler_params=pltpu.CompilerParams(dimension_semantics=("arbitrary",), vmem_limit_bytes=VMEM_LIMIT),
        name="proj",
    )(x2, *consts)


def _attn_kernel(first_ref, q_ref, k_ref, vt_ref, o_ref, acc_scr, m_scr, *, ts, safe):
    i = pl.program_id(2)
    nsub = q_ref.shape[0] // ts
    acc_scr[...] = jnp.zeros(acc_scr.shape, F32)
    if safe:
        m_scr[...] = jnp.full(m_scr.shape, NEG_INF, F32)

    def block(sub, j, masked):
        off = pl.multiple_of(j * ts, ts)
        st = _nt_dot(k_ref[pl.ds(off, ts), :], q_ref[sub * ts:(sub + 1) * ts, :])
        if masked:
            keys = lax.broadcasted_iota(I32, (ts, ts), 0)
            queries = lax.broadcasted_iota(I32, (ts, ts), 1)
            st = jnp.where(keys <= queries, st, NEG_INF)
        vt = vt_ref[:, pl.ds(off, ts)]
        if safe:
            m_prev = m_scr[sub]
            m_new = jnp.maximum(m_prev, jnp.max(st, axis=0, keepdims=True))
            p = jnp.exp2(st - m_new[0:1, :]).astype(BF16)
            acc_scr[sub] = jnp.exp2(m_prev - m_new)[0:1, :] * acc_scr[sub] + _dot(vt, p)
            m_scr[sub] = m_new
        else:
            acc_scr[sub] += _dot(vt, jnp.exp2(st).astype(BF16))

    def body(jp, carry):
        for dj in range(KV_UNROLL):
            for sub in range(nsub):
                block(sub, jp * KV_UNROLL + dj, False)
        return carry

    step = (pl.program_id(0) * pl.num_programs(1) + pl.program_id(1)) * pl.num_programs(2) + i
    lax.fori_loop(first_ref[step] // KV_UNROLL, (nsub * i) // KV_UNROLL, body, 0)
    for jj in range(nsub):
        for sub in range(jj, nsub):
            block(sub, nsub * i + jj, sub == jj)
    for sub in range(nsub):
        acc = acc_scr[sub]
        out_t = acc / acc[HEAD_DIM:HEAD_DIM + 1, :]
        o_ref[sub * ts:(sub + 1) * ts, :] = out_t.T.astype(o_ref.dtype)


def _first_kv_block(fedge, b, s, tq, ts):
    nblk = s // ts
    f_first = fedge[:, 0, ::FORGET_COPIES_STRIDE].reshape(b, nblk, N_HEADS).transpose(0, 2, 1)
    f_last = fedge[:, 1, ::FORGET_COPIES_STRIDE].reshape(b, nblk, N_HEADS).transpose(0, 2, 1)
    tile_start = f_first[:, :, ::tq // ts]
    bound = LOG2E * (tile_start[:, :, :, None] - f_last[:, :, None, :])
    return jnp.sum(bound < -UNDERFLOW_LOG2, axis=-1).astype(I32).reshape(-1)


def _attention(q, k, vt, fedge, plain_ok):
    b, s, hw = q.shape
    tq, ts = TQ_ATTN, TS_ATTN
    nq = s // tq

    def call(safe):
        first = jnp.zeros((b * N_HEADS * nq,), I32) if safe else _first_kv_block(fedge, b, s, tq, ts)
        return pl.pallas_call(
            functools.partial(_attn_kernel, ts=ts, safe=safe),
            grid_spec=pltpu.PrefetchScalarGridSpec(
                num_scalar_prefetch=1,
                grid=(b, hw // LANES, nq),
                in_specs=[pl.BlockSpec((None, tq, LANES), lambda bi, h, i, f: (bi, i, h)),
                          pl.BlockSpec((None, s, LANES), lambda bi, h, i, f: (bi, 0, h)),
                          pl.BlockSpec((None, LANES, s), lambda bi, h, i, f: (bi, h, 0))],
                out_specs=pl.BlockSpec((None, tq, LANES), lambda bi, h, i, f: (bi, i, h)),
                scratch_shapes=[pltpu.VMEM((tq // ts, LANES, ts), F32), pltpu.VMEM((tq // ts, SUBLANES, ts), F32)]),
            out_shape=jax.ShapeDtypeStruct((b, s, hw), BF16),
            compiler_params=pltpu.CompilerParams(
                dimension_semantics=("arbitrary", "arbitrary", "arbitrary"), vmem_limit_bytes=VMEM_LIMIT),
            name="attn_safe" if safe else "attn",
        )(first, q, k, vt)

    return lax.cond(plain_ok, lambda: call(False), lambda: call(True))


def _window_select(pieces, first_group):
    window = lax.broadcasted_iota(I32, pieces[0].shape, 1) // SSM_GROUP
    out = pieces[0]
    for m in range(1, GROUPS_PER_BLOCK):
        out = jnp.where(window == (first_group + m) % GROUPS_PER_BLOCK, pieces[m], out)
    return out


def _s5_kernel(u_ref, win_ref, tw_ref, coef_ref, y_ref, ub_scr, yb_scr, *, nk):
    nj = CHUNKS_PER_SUPER
    gpb = GROUPS_PER_BLOCK
    w = CHUNK * SSM_GROUP
    half = w // 2
    nkk = nk // SUBLANES
    row_stride = SUPER

    def gather_body(it, carry):
        j, kk = it // nkk, it % nkk
        for sp in range(CHUNK // gpb):
            rot = []
            for m in range(gpb):
                row0 = j * CHUNK + sp * gpb + m + kk * (SUBLANES * row_stride)
                src = u_ref[pl.ds(row0, SUBLANES, stride=row_stride), :]
                rot.append(src if m == 0 else pltpu.roll(src, m * SSM_GROUP, axis=1))
            for gl in range(gpb):
                ub_scr[gl, j, pl.ds(kk * SUBLANES, SUBLANES), sp * LANES:(sp + 1) * LANES] = _window_select(rot, gl)
        return carry

    lax.fori_loop(0, nj * nkk, gather_body, 0, unroll=RELAYOUT_UNROLL)

    def swap(val):
        return jnp.concatenate([val[:, half:], val[:, :half]], axis=1)

    def group_body(gl, carry):
        def cmul(val, idx):
            return coef_ref[gl, idx:idx + 1, :] * val + coef_ref[gl, idx + 1:idx + 2, :] * swap(val)

        u = ub_scr[gl].reshape(nj * nk, w).astype(BF16)
        s1 = _dot(u, win_ref[gl]).reshape(nj, nk, w)
        e = jnp.zeros((nk, w), F32)
        local = []
        for j in range(nj):
            local.append(e)
            e = cmul(e, 0) + s1[j]
        kidx = lax.broadcasted_iota(I32, (nk, w), 0)
        x = e
        step, d = 0, 1
        while d < nk:
            shifted = jnp.where(kidx >= d, pltpu.roll(x, d, axis=0), 0.0)
            x = x + cmul(shifted, 2 + 2 * nj + 2 * step)
            step, d = step + 1, d * 2
        x_start = jnp.where(kidx >= 1, pltpu.roll(x, 1, axis=0), 0.0)
        starts = [local[j] + cmul(x_start, 2 + 2 * j) for j in range(nj)]
        p = jnp.concatenate([st[:, :half] for st in starts], axis=0)
        hi = p.astype(BF16)
        lo = (p - hi.astype(F32)).astype(BF16)
        y = _dot(jnp.concatenate([u, hi, lo], axis=1), tw_ref[gl])
        yb_scr[gl] = y.reshape(nj, nk, w)
        return carry

    lax.fori_loop(0, gpb, group_body, 0)

    def scatter_body(it, carry):
        j, kk = it // nkk, it % nkk
        for tp in range(CHUNK // gpb):
            src = [yb_scr[gl, j, pl.ds(kk * SUBLANES, SUBLANES), tp * LANES:(tp + 1) * LANES] for gl in range(gpb)]
            for m in range(gpb):
                by_window = [src[(wi - m) % gpb] for wi in range(gpb)]
                window = lax.broadcasted_iota(I32, by_window[0].shape, 1) // SSM_GROUP
                merged = by_window[0]
                for wi in range(1, gpb):
                    merged = jnp.where(window == wi, by_window[wi], merged)
                nat = merged if m == 0 else pltpu.roll(merged, LANES - m * SSM_GROUP, axis=1)
                row0 = j * CHUNK + tp * gpb + m + kk * (SUBLANES * row_stride)
                y_ref[pl.ds(row0, SUBLANES, stride=row_stride), :] = nat
        return carry

    lax.fori_loop(0, nj * nkk, scatter_body, 0, unroll=RELAYOUT_UNROLL)


def _s5_params(a_re, a_im, log_dt, b_re, b_im, c_re, c_im, nk):
    args = (a_re, a_im, log_dt, b_re, b_im, c_re, c_im)
    slowest = jnp.min(a_re * jnp.exp(log_dt)[:, None])
    return lax.cond(slowest > MIN_FACTORED_DECAY, functools.partial(_s5_params_factored, nk=nk),
                    functools.partial(_s5_params_direct, nk=nk), *args)


def _s5_params_factored(a_re, a_im, log_dt, b_re, b_im, c_re, c_im, *, nk):
    g, p = a_re.shape
    c = SSM_GROUP
    gpb = GROUPS_PER_BLOCK
    dt = jnp.exp(log_dt)[:, None]
    adt_r, adt_i = a_re * dt, a_im * dt

    def lam_pow(n):
        nf = jnp.asarray(n, F32)
        nf = (nf[None] if nf.ndim == 1 else nf)[:, :, None]
        mag = jnp.exp(adt_r[:, None, :] * nf)
        ang = adt_i[:, None, :] * nf
        return mag * jnp.cos(ang), mag * jnp.sin(ang)

    gi = jnp.arange(g)[:, None, None] % gpb
    step = ((jnp.arange(CHUNK // gpb)[None, :, None]) * gpb
            + (jnp.arange(gpb)[None, None, :] - gi) % gpb).reshape(g, CHUNK)

    l1r, l1i = lam_pow(jnp.ones((1,)))
    den = a_re * a_re + a_im * a_im
    nr, ni = l1r[:, 0] - 1.0, l1i[:, 0]
    qr = (nr * a_re + ni * a_im) / den
    qi = (ni * a_re - nr * a_im) / den
    bcr = (qr[..., None] * b_re - qi[..., None] * b_im).transpose(0, 2, 1)
    bci = (qr[..., None] * b_im + qi[..., None] * b_re).transpose(0, 2, 1)

    def times_b(pr, pi):
        pr, pi = pr[:, :, None, :], pi[:, :, None, :]
        return pr * bcr[:, None] - pi * bci[:, None], pr * bci[:, None] + pi * bcr[:, None]

    def times_c(pr, pi):
        pr, pi = pr[:, :, None, :], pi[:, :, None, :]
        return c_re[:, None] * pr - c_im[:, None] * pi, c_re[:, None] * pi + c_im[:, None] * pr

    rows = lambda parts: jnp.concatenate(parts, axis=-1).reshape(g, CHUNK * c, -1)
    ir, ii = times_b(*lam_pow(CHUNK - 1 - step))
    win = rows([ir, ii, ii, ir])
    clr, cli = times_c(*lam_pow(step + 1))
    wout = rows([clr, -cli]).transpose(0, 2, 1)
    ar, ai = times_b(*lam_pow(-step))
    br, bi = times_c(*lam_pow(step))
    toep = jnp.einsum('gnp,gmp->gnm', rows([ar, ai]), rows([br, -bi]), precision=lax.Precision.HIGHEST)
    step_of_lane = jnp.repeat(step, c, axis=1)
    toep = jnp.where(step_of_lane[:, None, :] >= step_of_lane[:, :, None], toep, 0.0)
    tw = jnp.concatenate([toep, wout, wout], axis=1)
    return win.astype(BF16), tw.astype(BF16), _s5_scan_coefficients(lam_pow, g, p, nk)


def _s5_scan_coefficients(lam_pow, g, p, nk):
    n_steps = max(nk.bit_length() - 1, 0)
    powers = [CHUNK] + [CHUNK * j for j in range(CHUNKS_PER_SUPER)] + [SUPER * (1 << i) for i in range(n_steps)]
    ar, ai = lam_pow(jnp.array(powers, dtype=jnp.int32))
    c1 = jnp.concatenate([ar, ar, ar, ar], axis=2)
    c2 = jnp.concatenate([-ai, ai, ai, -ai], axis=2)
    coef = jnp.stack([c1, c2], axis=2).reshape(g, 2 * len(powers), 4 * p)
    return jnp.pad(coef, ((0, 0), (0, (-coef.shape[1]) % SUBLANES), (0, 0)))


def _s5_params_direct(a_re, a_im, log_dt, b_re, b_im, c_re, c_im, *, nk):
    g, p = a_re.shape
    c = SSM_GROUP
    hp = lax.Precision.HIGHEST
    dt = jnp.exp(log_dt)[:, None]
    adt_r, adt_i = a_re * dt, a_im * dt

    def lam_pow(n):
        nf = jnp.asarray(n, F32)[None, :, None]
        mag = jnp.exp(adt_r[:, None, :] * nf)
        ang = adt_i[:, None, :] * nf
        return mag * jnp.cos(ang), mag * jnp.sin(ang)

    lr, li = lam_pow(jnp.arange(CHUNK + 1))
    den = a_re * a_re + a_im * a_im
    nr, ni = lr[:, 1] - 1.0, li[:, 1]
    qr = (nr * a_re + ni * a_im) / den
    qi = (ni * a_re - nr * a_im) / den
    bbr = qr[..., None] * b_re - qi[..., None] * b_im
    bbi = qr[..., None] * b_im + qi[..., None] * b_re

    mr = lr[:, :CHUNK, :, None] * bbr[:, None] - li[:, :CHUNK, :, None] * bbi[:, None]
    mi = lr[:, :CHUNK, :, None] * bbi[:, None] + li[:, :CHUNK, :, None] * bbr[:, None]
    kern = (jnp.einsum('gcp,gtpd->gtcd', c_re, mr, precision=hp)
            - jnp.einsum('gcp,gtpd->gtcd', c_im, mi, precision=hp))
    toep = jnp.stack([jnp.pad(kern[:, :CHUNK - s], ((0, 0), (s, 0), (0, 0), (0, 0))) for s in range(CHUNK)],
                     axis=1)
    toep = toep.transpose(0, 1, 4, 2, 3).reshape(g, CHUNK * c, CHUNK * c)
    clr = c_re[:, None] * lr[:, 1:, None, :] - c_im[:, None] * li[:, 1:, None, :]
    cli = c_re[:, None] * li[:, 1:, None, :] + c_im[:, None] * lr[:, 1:, None, :]
    w_re = clr.transpose(0, 3, 1, 2).reshape(g, p, CHUNK * c)
    w_im = (-cli).transpose(0, 3, 1, 2).reshape(g, p, CHUNK * c)
    wout = jnp.concatenate([w_re, w_im], axis=1)
    pr, pi = lr[:, :CHUNK][:, ::-1], li[:, :CHUNK][:, ::-1]
    ir = pr[..., None] * bbr[:, None] - pi[..., None] * bbi[:, None]
    ii = pr[..., None] * bbi[:, None] + pi[..., None] * bbr[:, None]
    ir = ir.transpose(0, 1, 3, 2).reshape(g, CHUNK * c, p)
    ii = ii.transpose(0, 1, 3, 2).reshape(g, CHUNK * c, p)
    win = jnp.concatenate([ir, ii, ii, ir], axis=2)

    gi = jnp.arange(g)[:, None, None] % GROUPS_PER_BLOCK
    si = jnp.arange(CHUNK)[None, :, None]
    ci = jnp.arange(c)[None, None, :]
    lane_of = ((si // GROUPS_PER_BLOCK) * LANES + ((gi + si) % GROUPS_PER_BLOCK) * c + ci).reshape(g, CHUNK * c)
    perm = (lane_of[:, None, :] == jnp.arange(CHUNK * c)[None, :, None]).astype(BF16)
    win = jnp.einsum('gln,gnk->glk', perm, win.astype(BF16))
    toep = jnp.einsum('gln,gnk->glk', perm, toep.astype(BF16))
    tw = jnp.concatenate([toep, wout.astype(BF16), wout.astype(BF16)], axis=1)
    tw = jnp.einsum('grn,gln->grl', tw, perm)
    return win.astype(BF16), tw.astype(BF16), _s5_scan_coefficients(lam_pow, g, p, nk)


def _s5(u, win, tw, coef):
    b, s, dssm = u.shape
    nk = s // SUPER
    nj = CHUNKS_PER_SUPER
    w = CHUNK * SSM_GROUP
    gpb = GROUPS_PER_BLOCK
    nblk = dssm // LANES
    wspec = lambda a: pl.BlockSpec((gpb,) + a.shape[1:], lambda bi, li: (li, 0, 0))
    return pl.pallas_call(
        functools.partial(_s5_kernel, nk=nk),
        grid=(b, nblk),
        in_specs=[pl.BlockSpec((None, s, LANES), lambda bi, li: (bi, 0, li)), wspec(win), wspec(tw), wspec(coef)],
        out_specs=pl.BlockSpec((None, s, LANES), lambda bi, li: (bi, 0, li)),
        out_shape=jax.ShapeDtypeStruct((b, s, dssm), F32),
        scratch_shapes=[pltpu.VMEM((gpb, nj, nk, w), F32)] * 2,
        compiler_params=pltpu.CompilerParams(dimension_semantics=("arbitrary", "arbitrary"),
                                             vmem_limit_bytes=VMEM_LIMIT),
        name="s5",
    )(u, win, tw, coef)


def _route(logits):
    e_all = logits[0:N_EXPERTS]
    gl = logits[N_EXPERTS:N_EXPERTS + SUBLANES]
    tm = logits.shape[1]
    ridx = lax.broadcasted_iota(I32, (SUBLANES, tm), 0)
    ge = jnp.exp(gl - jnp.max(gl, axis=0, keepdims=True))
    gp = ge / jnp.sum(ge, axis=0, keepdims=True)
    g_top = jnp.max(gp, axis=0, keepdims=True)
    g_sel = jnp.min(jnp.where(gp == g_top, ridx, SUBLANES), axis=0, keepdims=True)
    e_in = e_all[(N_EXPERT_GROUPS - 1) * SUBLANES:]
    for gi in range(N_EXPERT_GROUPS - 2, -1, -1):
        e_in = jnp.where(g_sel == gi, e_all[gi * SUBLANES:(gi + 1) * SUBLANES], e_in)
    ee = jnp.exp(e_in - jnp.max(e_in, axis=0, keepdims=True))
    ep = ee / jnp.sum(ee, axis=0, keepdims=True)
    v1 = jnp.max(ep, axis=0, keepdims=True)
    i1 = jnp.min(jnp.where(ep == v1, ridx, SUBLANES), axis=0, keepdims=True)
    ep2 = jnp.where(ridx == i1, -1.0, ep)
    v2 = jnp.max(ep2, axis=0, keepdims=True)
    i2 = jnp.min(jnp.where(ep2 == v2, ridx, SUBLANES), axis=0, keepdims=True)
    den = v1 + v2
    w1 = g_top * v1 / den
    w2 = g_top * v2 / den
    e1 = g_sel * EXPERTS_PER_GROUP + i1
    e2 = g_sel * EXPERTS_PER_GROUP + i2
    ids = jnp.where(ridx == 0, e1, jnp.where(ridx == 1, e2, 0))
    wts = jnp.where(ridx == 0, w1, jnp.where(ridx == 1, w2, 0.0))
    return ids, wts


def _mix_kernel(x_ref, ya_ref, ys_ref, u_ref, g1_ref, wgate_ref, dsk_ref, wglu_ref, wpa_ref, wps_ref,
                wout_ref, g2_ref, wrh_ref, wrl_ref, br_ref, x1_ref, xp_ref, logits_ref):
    d = x_ref.shape[1]
    x = x_ref[...]
    xb = _rmsnorm_rows(x, g1_ref[...]).astype(BF16)
    gates = _dot(xb, wgate_ref[...])
    y = ys_ref[...].astype(F32) + dsk_ref[...] * u_ref[...].astype(F32)
    y = y * (0.5 * (1.0 + jnp.tanh(math.sqrt(2.0 / math.pi) * (y + 0.044715 * (y * y * y)))))
    y = y * _sigmoid(_dot(y.astype(BF16), wglu_ref[...]))
    mixed = (_sigmoid(gates[:, :d]) * _dot(ya_ref[...], wpa_ref[...])
             + _sigmoid(gates[:, d:]) * _dot(y.astype(BF16), wps_ref[...]))
    x1 = x + _dot(mixed.astype(BF16), wout_ref[...])
    x1_ref[...] = x1
    xn = _rmsnorm_rows(x1, g2_ref[...])
    xp_ref[...] = pltpu.pack_elementwise([xn[:, :d // 2], xn[:, d // 2:]], packed_dtype=BF16)
    xh = xn.astype(BF16)
    xl = (xn - xh.astype(F32)).astype(BF16)
    wrh = wrh_ref[...]
    logits_ref[...] = _nt_dot(wrh, xh) + _nt_dot(wrl_ref[...], xh) + _nt_dot(wrh, xl) + br_ref[...]


def _mix(x2, ya, ys, u, g1, wgate, dsk, wglu, wpa, wps, wout, g2, wrh, wrl, br):
    t, d = x2.shape
    tm = TM_MIX
    full = lambda a: pl.BlockSpec(a.shape, lambda i: (0,) * a.ndim)
    row = lambda width: pl.BlockSpec((tm, width), lambda i: (i, 0))
    return pl.pallas_call(
        _mix_kernel,
        grid=(t // tm,),
        in_specs=[row(d), row(ya.shape[1]), row(ys.shape[1]), row(u.shape[1]), full(g1), full(wgate), full(dsk),
                  full(wglu), full(wpa),
                  full(wps), full(wout), full(g2), full(wrh), full(wrl), full(br)],
        out_specs=[row(d), row(d // 2), pl.BlockSpec((ROUTER_ROWS, tm), lambda i: (0, i))],
        out_shape=[jax.ShapeDtypeStruct((t, d), F32), jax.ShapeDtypeStruct((t, d // 2), jnp.uint32),
                   jax.ShapeDtypeStruct((ROUTER_ROWS, t), F32)],
        compiler_params=pltpu.CompilerParams(dimension_semantics=("arbitrary",), vmem_limit_bytes=VMEM_LIMIT),
        name="mix",
    )(x2, ya, ys, u, g1, wgate, dsk, wglu, wpa, wps, wout, g2, wrh, wrl, br)


def _route_kernel(logits_ref, ids_ref, wts_ref):
    ids, wts = _route(logits_ref[...])
    ids_ref[...] = ids
    wts_ref[...] = wts


def _route_call(logits):
    t = logits.shape[1]
    tl = ROUTE_LANES
    col = pl.BlockSpec((SUBLANES, tl), lambda i: (0, i))
    return pl.pallas_call(
        _route_kernel,
        grid=(t // tl,),
        in_specs=[pl.BlockSpec((ROUTER_ROWS, tl), lambda i: (0, i))],
        out_specs=[col, col],
        out_shape=[jax.ShapeDtypeStruct((SUBLANES, t), I32), jax.ShapeDtypeStruct((SUBLANES, t), F32)],
        compiler_params=pltpu.CompilerParams(dimension_semantics=("arbitrary",)),
        name="route",
    )(logits)


def _meta_kernel(ids_ref, tri_ref, dest_ref, tile_ref, *, tile_rows):
    nk, nc, c = ids_ref.shape
    ne = N_EXPERTS
    erow = lax.broadcasted_iota(I32, (ne, c), 0)
    ones = jnp.ones((c, LANES), BF16)

    def onehot(k, ci):
        mask = erow == ids_ref[k, pl.ds(ci, 1), :]
        return mask, jnp.where(mask, 1.0, 0.0).astype(BF16)

    def count_body(n, acc):
        return acc + _dot(onehot(n // nc, n % nc)[1], ones)

    cnt = lax.fori_loop(0, nk * nc, count_body, jnp.zeros((ne, LANES), F32))
    ntiles = jnp.floor((cnt + (tile_rows - 1)) * (1.0 / tile_rows))
    lower = jnp.where(lax.broadcasted_iota(I32, (ne, ne), 1) < lax.broadcasted_iota(I32, (ne, ne), 0), 1.0, 0.0)
    start_tiles = _dot(lower.astype(BF16), ntiles.astype(BF16))
    base = start_tiles * tile_rows

    tri = tri_ref[...]

    def dest_body(n, carry):
        k, ci = n // nc, n % nc
        mask, oh = onehot(k, ci)
        prefix = _dot(oh, tri)
        slot = _lane_tile(base + carry, c // LANES) + prefix - 1.0
        dest = jnp.sum(jnp.where(mask, slot, 0.0), axis=0, keepdims=True)
        dest_ref[k, pl.ds(ci, 1), :] = dest.astype(I32)
        return carry + _dot(oh, ones)

    lax.fori_loop(0, nk * nc, dest_body, jnp.zeros((ne, LANES), F32))

    nt_lanes = tile_ref.shape[1]
    end_tiles = _lane_tile(start_tiles + ntiles, nt_lanes // LANES)
    tidx = lax.broadcasted_iota(I32, (ne, nt_lanes), 1).astype(F32)
    texp = jnp.sum(jnp.where(tidx >= end_tiles, 1.0, 0.0), axis=0, keepdims=True)
    valid = jnp.where(texp < ne, 1, 0)
    texp = jnp.minimum(texp, ne - 1.0).astype(I32)
    ridx = lax.broadcasted_iota(I32, tile_ref.shape, 0)
    tile_ref[...] = jnp.where(ridx == 0, texp, jnp.where(ridx == 1, valid, 0))


def _meta(ids2, tri, n_tiles):
    nk, t = ids2.shape
    c = SORT_CHUNK
    nt_lanes = pl.cdiv(n_tiles, LANES) * LANES
    ids3 = ids2.reshape(nk, t // c, c)
    dest, tile = pl.pallas_call(
        functools.partial(_meta_kernel, tile_rows=TM_EXPERT),
        out_shape=[jax.ShapeDtypeStruct(ids3.shape, I32), jax.ShapeDtypeStruct((SUBLANES, nt_lanes), I32)],
        compiler_params=pltpu.CompilerParams(vmem_limit_bytes=VMEM_LIMIT),
        name="meta",
    )(ids3, tri)
    return dest.reshape(nk * t), tile[0, :n_tiles], tile[1, :n_tiles]


def _row_copy(src_ref, src_row, dst_ref, dst_row, sem):
    return pltpu.make_async_copy(src_ref.at[pl.ds(src_row, 1)], dst_ref.at[pl.ds(dst_row, 1)], sem)


def _dispatch_kernel(dest_ref, xp_ref, xs_in_ref, xs_ref, sem, *, n_tokens):
    del xs_in_ref
    tm = xp_ref.shape[0]
    base = pl.program_id(0) * tm

    def body(r, carry):
        for k in range(2):
            _row_copy(xp_ref, r, xs_ref, dest_ref[k * n_tokens + base + r], sem).start()
        return carry

    lax.fori_loop(0, tm, body, 0, unroll=ROW_DMA_UNROLL)
    for k in range(2):
        pltpu.make_async_copy(xp_ref, xs_ref.at[pl.ds(0, tm)], sem).wait()


def _zero_fill_kernel(o_ref):
    o_ref[...] = jnp.zeros(o_ref.shape, o_ref.dtype)


def _zeros(n_rows, width, dtype):
    rows = n_rows // pl.cdiv(n_rows, ZERO_FILL_ROWS)
    assert n_rows % rows == 0 and rows % SUBLANES == 0
    return pl.pallas_call(
        _zero_fill_kernel,
        grid=(n_rows // rows,),
        out_specs=pl.BlockSpec((rows, width), lambda i: (i, 0)),
        out_shape=jax.ShapeDtypeStruct((n_rows, width), dtype),
        compiler_params=pltpu.CompilerParams(dimension_semantics=("arbitrary",)),
        name="zero_fill",
    )()


def _dispatch(dest, xp, n_slots):
    t, w = xp.shape
    tm = TM_DISPATCH
    xs0 = _zeros(n_slots, w, xp.dtype)
    return pl.pallas_call(
        functools.partial(_dispatch_kernel, n_tokens=t),
        grid_spec=pltpu.PrefetchScalarGridSpec(
            num_scalar_prefetch=1,
            grid=(t // tm,),
            in_specs=[pl.BlockSpec((tm, w), lambda i, d: (i, 0)), pl.BlockSpec(memory_space=pl.ANY)],
            out_specs=pl.BlockSpec(memory_space=pl.ANY),
            scratch_shapes=[pltpu.SemaphoreType.DMA(())]),
        out_shape=jax.ShapeDtypeStruct(xs0.shape, xp.dtype),
        input_output_aliases={2: 0},
        compiler_params=pltpu.CompilerParams(dimension_semantics=("arbitrary",), has_side_effects=True),
        name="dispatch",
    )(dest, xp, xs0)


def _unpack_rows(packed):
    lo = pltpu.unpack_elementwise(packed, index=0, packed_dtype=BF16, unpacked_dtype=F32)
    hi = pltpu.unpack_elementwise(packed, index=1, packed_dtype=BF16, unpacked_dtype=F32)
    return jnp.concatenate([lo, hi], axis=1)


def _expert_kernel(te_ref, tv_ref, xs_ref, wg_ref, wu_ref, wd_ref, ys_ref, wg_bf, wu_bf, wd_bf):
    i = pl.program_id(0)
    half = xs_ref.shape[1]

    @pl.when((i == 0) | (te_ref[i] != te_ref[jnp.maximum(i - 1, 0)]))
    def _():
        wg_bf[...] = wg_ref[...].astype(BF16)
        wu_bf[...] = wu_ref[...].astype(BF16)
        wd_bf[...] = wd_ref[...].astype(BF16)

    @pl.when(tv_ref[i] > 0)
    def _():
        x = _unpack_rows(xs_ref[...]).astype(BF16)
        hg = _dot(x, wg_bf[...])
        hu = _dot(x, wu_bf[...])
        h = (hg * _sigmoid(hg) * hu).astype(BF16)
        y = _dot(h, wd_bf[...])
        ys_ref[...] = pltpu.pack_elementwise([y[:, :half], y[:, half:]], packed_dtype=BF16)

    @pl.when(tv_ref[i] == 0)
    def _():
        zero = jnp.zeros(ys_ref.shape, F32)
        ys_ref[...] = pltpu.pack_elementwise([zero, zero], packed_dtype=BF16)


def _experts(tile_expert, tile_valid, xs, wg, wu, wd):
    n_slots, w = xs.shape
    tm = TM_EXPERT
    wspec = lambda a: pl.BlockSpec((None,) + a.shape[1:], lambda i, te, tv: (te[i], 0, 0))
    rows = pl.BlockSpec((tm, w), lambda i, te, tv: (i, 0))
    return pl.pallas_call(
        _expert_kernel,
        grid_spec=pltpu.PrefetchScalarGridSpec(
            num_scalar_prefetch=2,
            grid=(n_slots // tm,),
            in_specs=[rows, wspec(wg), wspec(wu), wspec(wd)],
            out_specs=rows,
            scratch_shapes=[pltpu.VMEM(a.shape[1:], BF16) for a in (wg, wu, wd)]),
        out_shape=jax.ShapeDtypeStruct(xs.shape, xs.dtype),
        compiler_params=pltpu.CompilerParams(dimension_semantics=("arbitrary",), vmem_limit_bytes=VMEM_LIMIT),
        name="experts",
    )(tile_expert, tile_valid, xs, wg, wu, wd)


def _combine_kernel(dest_ref, x1_ref, w0_ref, w1_ref, ys_ref, out_ref, buf, sem, *, n_tokens):
    tm, d = x1_ref.shape
    base = pl.program_id(0) * tm

    def body(r, carry):
        for k in range(2):
            _row_copy(ys_ref, dest_ref[k * n_tokens + base + r], buf.at[k], r, sem).start()
        return carry

    lax.fori_loop(0, tm, body, 0, unroll=ROW_DMA_UNROLL)
    for k in range(2):
        pltpu.make_async_copy(ys_ref.at[pl.ds(0, tm)], buf.at[k], sem).wait()
    reps = d // LANES
    out_ref[...] = (x1_ref[...]
                    + _lane_tile(w0_ref[...], reps) * _unpack_rows(buf[0])
                    + _lane_tile(w1_ref[...], reps) * _unpack_rows(buf[1]))


def _combine(dest, x1, w0, w1, ys):
    t, d = x1.shape
    tm = TM_COMBINE
    row = lambda width: pl.BlockSpec((tm, width), lambda i, dref: (i, 0))
    return pl.pallas_call(
        functools.partial(_combine_kernel, n_tokens=t),
        grid_spec=pltpu.PrefetchScalarGridSpec(
            num_scalar_prefetch=1,
            grid=(t // tm,),
            in_specs=[row(d), row(LANES), row(LANES), pl.BlockSpec(memory_space=pl.ANY)],
            out_specs=row(d),
            scratch_shapes=[pltpu.VMEM((2, tm, ys.shape[1]), ys.dtype), pltpu.SemaphoreType.DMA(())]),
        out_shape=jax.ShapeDtypeStruct((t, d), x1.dtype),
        compiler_params=pltpu.CompilerParams(dimension_semantics=("arbitrary",), vmem_limit_bytes=VMEM_LIMIT),
        name="combine",
    )(dest, x1, w0, w1, ys)


def _lower_tri(n):
    return jnp.where(jnp.arange(n)[:, None] >= jnp.arange(n)[None, :], 1.0, 0.0).astype(BF16)


def _upper_tri(n):
    return jnp.where(jnp.arange(n)[:, None] <= jnp.arange(n)[None, :], 1.0, 0.0).astype(BF16)


def _split_bf16(w):
    hi = w.astype(BF16)
    lo = (w - hi.astype(F32)).astype(BF16)
    return hi, lo


def _layer(x, norm_mix_g, w_in, b_forget, q_norm_g, k_norm_g, ssm_A_re, ssm_A_im, ssm_log_dt, ssm_B_re, ssm_B_im,
           ssm_C_re, ssm_C_im, ssm_D, w_glu, w_proj_attn, w_proj_ssm, w_out, norm_ffn_g, w_router_group,
           b_router_group, w_router_expert, b_router_expert, w_expert_gate, w_expert_up, w_expert_down):
    b, s, d = x.shape
    t = b * s
    da = N_HEADS * HEAD_DIM
    dssm = ssm_D.shape[0]
    nk = s // SUPER
    assert s % SUPER == 0 and nk % SUBLANES == 0 and nk & (nk - 1) == 0 and s % TQ_ATTN == 0 and t % TM_PROJ == 0
    assert TM_PROJ == TS_ATTN
    x2 = x.reshape(t, d)

    o_f = 3 * da
    o_u = o_f + N_HEADS
    o_g = o_u + dssm
    w_main = jnp.concatenate([w_in[:, :2 * da], w_in[:, o_u:o_g]], axis=1).astype(BF16)
    wvt = jnp.pad(w_in[:, 2 * da:o_f].T.reshape(N_HEADS, HEAD_DIM, d), ((0, 0), (0, LANES - HEAD_DIM), (0, 0)))
    wvt = wvt.reshape(N_HEADS * LANES, d).astype(BF16)
    vones = (jnp.arange(N_HEADS * LANES) % LANES == HEAD_DIM).astype(F32)[:, None]
    wf =jnp.repeat(w_in[:, o_f:o_u], FORGET_COPIES_STRIDE, axis=1).astype(BF16)
    bf = jnp.repeat(b_forget, FORGET_COPIES_STRIDE)[None, :]
    w_gates = w_in[:, o_g:].astype(BF16)
    qg = (jnp.tile(q_norm_g, N_HEADS) * (HEAD_DIM ** -0.5 * LOG2E))[None, :]
    kg = jnp.tile(k_norm_g, N_HEADS)[None, :]
    head_of = jnp.arange(da) // HEAD_DIM
    bd = jnp.where(head_of[:, None] == head_of[None, :], 1.0 / HEAD_DIM, 0.0).astype(BF16)
    logit_bound = 1.02 * HEAD_DIM ** 0.5 * jnp.max(jnp.abs(q_norm_g)) * jnp.max(jnp.abs(k_norm_g))
    plain_ok = LOG2E * logit_bound <= PLAIN_SOFTMAX_MAX_LOG2
    shift = jnp.full((1, LANES), logit_bound, F32)
    hw = N_HEADS * LANES
    lane_in_head = jnp.arange(hw) % LANES
    ones = jnp.stack([
        (lane_in_head >= BIAS_K_LANE) & (lane_in_head < BIAS_K_LANE + 3),
        (lane_in_head >= BIAS_Q_LANE) & (lane_in_head < BIAS_Q_LANE + 3)]
        + [jnp.zeros((hw,), bool)] * (SUBLANES - 2)).astype(F32)
    prow = jnp.arange(LANES)
    head, copy = prow // FORGET_COPIES_STRIDE, prow % FORGET_COPIES_STRIDE
    target = jnp.where(copy < 3, head * LANES + BIAS_Q_LANE + copy, hw + head * LANES + BIAS_K_LANE + copy - 3)
    place = ((jnp.arange(2 * hw)[None, :] == target[:, None]) & (copy < 6)[:, None]).astype(BF16)

    q, k, vt, u, fedge = _proj(x2, norm_mix_g[None, :], w_main, wvt, vones, wf, bf, qg, kg, bd, _lower_tri(TM_PROJ),
                               place, ones, shift, seq=s)
    y_attn = _attention(q.reshape(b, s, hw), k.reshape(b, s, hw), vt, fedge, plain_ok)
    w_pa = jnp.pad(w_proj_attn.reshape(N_HEADS, HEAD_DIM, d), ((0, 0), (0, LANES - HEAD_DIM), (0, 0)))
    w_pa = w_pa.reshape(hw, d).astype(BF16)

    win, tw, coef = _s5_params(ssm_A_re, ssm_A_im, ssm_log_dt, ssm_B_re, ssm_B_im, ssm_C_re, ssm_C_im, s // SUPER)
    y_ssm = _s5(u.reshape(b, s, dssm), win, tw, coef)

    wr = jnp.concatenate([w_router_expert.T, w_router_group.T,
                          jnp.zeros((ROUTER_ROWS - N_EXPERTS - N_EXPERT_GROUPS, d), F32)], axis=0)
    br = jnp.concatenate([b_router_expert, b_router_group,
                          jnp.full((SUBLANES - N_EXPERT_GROUPS,), NEG_INF, F32),
                          jnp.zeros((ROUTER_ROWS - N_EXPERTS - SUBLANES,), F32)])[:, None]
    x1, xp, logits = _mix(x2, y_attn.reshape(t, hw), y_ssm.reshape(t, dssm), u, norm_mix_g[None, :], w_gates,
                          ssm_D[None, :], w_glu.astype(BF16), w_pa, w_proj_ssm.astype(BF16),
                          w_out.astype(BF16), norm_ffn_g[None, :], *_split_bf16(wr), br)
    ids, wts = _route_call(logits)

    n_tiles = (2 * t) // TM_EXPERT + N_EXPERTS
    dest, tile_expert, tile_valid = _meta(ids[:2], _upper_tri(SORT_CHUNK), n_tiles)
    xs = _dispatch(dest, xp, n_tiles * TM_EXPERT)
    ys = _experts(tile_expert, tile_valid, xs, w_expert_gate, w_expert_up, w_expert_down)
    w0 = jnp.broadcast_to(wts[0][:, None], (t, LANES))
    w1 = jnp.broadcast_to(wts[1][:, None], (t, LANES))
    out = _combine(dest, x1, w0, w1, ys)
    return out.reshape(b, s, d)


def kernel(x, norm_mix_g, w_in, b_forget, q_norm_g, k_norm_g, ssm_A_re, ssm_A_im, ssm_log_dt, ssm_B_re, ssm_B_im,
           ssm_C_re, ssm_C_im, ssm_D, w_glu, w_proj_attn, w_proj_ssm, w_out, norm_ffn_g, w_router_group,
           b_router_group, w_router_expert, b_router_expert, w_expert_gate, w_expert_up, w_expert_down):
    layer_params = (norm_mix_g, w_in, b_forget, q_norm_g, k_norm_g, ssm_A_re, ssm_A_im, ssm_log_dt, ssm_B_re,
                    ssm_B_im, ssm_C_re, ssm_C_im, ssm_D, w_glu, w_proj_attn, w_proj_ssm, w_out, norm_ffn_g,
                    w_router_group, b_router_group, w_router_expert, b_router_expert, w_expert_gate, w_expert_up,
                    w_expert_down)
    for layer in range(norm_mix_g.shape[0]):
        x = _layer(x, *[p[layer] for p in layer_params])
    return x
```

```python
import functools
import math

import jax
import jax.numpy as jnp
from jax import lax
from jax.experimental import pallas as pl
from jax.experimental.pallas import tpu as pltpu

F32 = jnp.float32
BF16 = jnp.bfloat16
I32 = jnp.int32

LANES = 128
SUBLANES = 8
MXU_DIM = 256

N_HEADS = 8
HEAD_DIM = 64
SSM_GROUP = 16
SSM_STATE = 64
N_EXPERT_GROUPS = 4
EXPERTS_PER_GROUP = 8
N_EXPERTS = N_EXPERT_GROUPS * EXPERTS_PER_GROUP
EPS = 1e-6
NEG_INF = -1e30
LOG2E = math.log2(math.e)
PLAIN_SOFTMAX_MAX_LOG2 = 60.0
UNDERFLOW_LOG2 = 160.0
BIAS_K_LANE = HEAD_DIM
BIAS_Q_LANE = HEAD_DIM + 3
FORGET_COPIES_STRIDE = LANES // N_HEADS

CHUNK = MXU_DIM // SSM_GROUP
CHUNKS_PER_SUPER = 8
SUPER = CHUNK * CHUNKS_PER_SUPER
GROUPS_PER_BLOCK = LANES // SSM_GROUP
RELAYOUT_UNROLL = 8
MIN_FACTORED_DECAY = -4.0

TM_PROJ = 512
TQ_ATTN = 2048
TS_ATTN = 512
KV_UNROLL = 1
TM_MIX = 512
MIX_CHAINS = 1
TM_EXPERT = 512
TM_DISPATCH = 512
TM_COMBINE = 512
SORT_CHUNK = 512
ZERO_FILL_ROWS = 2048
ROW_DMA_UNROLL = 8
ROUTE_LANES = 2048
ROUTER_ROWS = 48

VMEM_LIMIT = 48 * 1024 * 1024


def _nt_dot(a, b):
    return lax.dot_general(a, b, (((1,), (1,)), ((), ())), preferred_element_type=F32)


def _dot(a, b):
    return jnp.dot(a, b, preferred_element_type=F32)


def _lane_tile(x, n):
    return x if n == 1 else jnp.concatenate([x] * n, axis=1)


def _rmsnorm_rows(x, g):
    ms = jnp.mean(x * x, axis=-1, keepdims=True)
    return x * lax.rsqrt(ms + EPS) * g


def _sigmoid(x):
    return 1.0 / (1.0 + jnp.exp(-x))


def _split3(x):
    hi = x.astype(BF16)
    r1 = x - hi.astype(F32)
    mid = r1.astype(BF16)
    lo = (r1 - mid.astype(F32)).astype(BF16)
    return [hi, mid, lo]


def _expand_heads(z):
    lane = lax.broadcasted_iota(I32, (z.shape[0], LANES), 1)
    blocks = []
    for p in range(z.shape[1] // LANES):
        blk = z[:, p * LANES:(p + 1) * LANES]
        blocks.append(jnp.where(lane < HEAD_DIM, blk, 0.0))
        blocks.append(jnp.where(lane < HEAD_DIM, pltpu.roll(blk, HEAD_DIM, axis=1), 0.0))
    return jnp.concatenate(blocks, axis=1)


def _proj_kernel(x_ref, g_ref, w_ref, wf_ref, bf_ref, qg_ref, kg_ref, bd_ref, ltri_ref, place_ref,
                 ones_ref, shift_ref, q_ref, k_ref, v_ref, u_ref, fedge_ref, carry_ref, *, tiles_per_seq):
    i = pl.program_id(0)

    @pl.when(i % tiles_per_seq == 0)
    def _():
        carry_ref[...] = jnp.zeros_like(carry_ref)

    tm = x_ref.shape[0]
    da = N_HEADS * HEAD_DIM
    hw = N_HEADS * LANES
    xb = _rmsnorm_rows(x_ref[...], g_ref[...]).astype(BF16)
    z = _dot(xb, w_ref[...])
    zq, zk = z[:, :da], z[:, da:2 * da]
    bd = bd_ref[...]
    msq = _dot((zq * zq).astype(BF16), bd)
    msk = _dot((zk * zk).astype(BF16), bd)
    qn = zq * lax.rsqrt(msq + EPS) * qg_ref[...]
    kn = zk * lax.rsqrt(msk + EPS) * kg_ref[...]
    u_ref[...] = z[:, 3 * da:]

    a = _dot(xb, wf_ref[...]) + bf_ref[...]
    logf = jnp.minimum(a, 0.0) - jnp.log(1.0 + jnp.exp(-jnp.abs(a)))
    cs = _dot(ltri_ref[...], jnp.concatenate(_split3(logf), axis=1))
    cum = cs[:, :LANES] + cs[:, LANES:2 * LANES] + cs[:, 2 * LANES:] + carry_ref[0:1, :]
    carry_ref[...] = jnp.broadcast_to(cum[tm - 1:tm, :], carry_ref.shape)
    edge_row = lax.broadcasted_iota(I32, fedge_ref.shape, 0)
    fedge_ref[...] = jnp.where(edge_row == 0, cum[0:1, :], jnp.where(edge_row == 1, cum[tm - 1:tm, :], 0.0))
    copy = lax.broadcasted_iota(I32, cum.shape, 1) % FORGET_COPIES_STRIDE
    bias = jnp.where(copy < 3, (cum - shift_ref[...]) * LOG2E, cum * (-LOG2E))
    hi, mid, lo = [p.astype(F32) for p in _split3(bias)]
    piece = jnp.where(copy % 3 == 0, hi, jnp.where(copy % 3 == 1, mid, lo)).astype(BF16)
    extras = _dot(piece, place_ref[...])
    q_ref[...] = (_expand_heads(qn) + extras[:, :hw] + ones_ref[0:1, :]).astype(BF16)
    k_ref[...] = (_expand_heads(kn) + extras[:, hw:] + ones_ref[1:2, :]).astype(BF16)
    v_ref[...] = (_expand_heads(z[:, 2 * da:3 * da]) + ones_ref[2:3, :]).astype(BF16)


def _proj(x2, norm_g, w_main, wf, b_f, qg, kg, bd, ltri, place, ones, shift, *, seq):
    t, d = x2.shape
    tm = TM_PROJ
    hw = N_HEADS * LANES
    du = w_main.shape[1] - 3 * N_HEADS * HEAD_DIM
    tiles_per_seq = seq // tm
    full = lambda a: pl.BlockSpec(a.shape, lambda i: (0,) * a.ndim)
    row = lambda width: pl.BlockSpec((tm, width), lambda i: (i, 0))
    consts = (norm_g, w_main, wf, b_f, qg, kg, bd, ltri, place, ones, shift)
    return pl.pallas_call(
        functools.partial(_proj_kernel, tiles_per_seq=tiles_per_seq),
        grid=(t // tm,),
        in_specs=[row(d)] + [full(a) for a in consts],
        out_specs=[row(hw), row(hw), row(hw), row(du), pl.BlockSpec((None, SUBLANES, LANES), lambda i: (i, 0, 0))],
        out_shape=[jax.ShapeDtypeStruct((t, hw), BF16)] * 3
        + [jax.ShapeDtypeStruct((t, du), F32), jax.ShapeDtypeStruct((t // tm, SUBLANES, LANES), F32)],
        scratch_shapes=[pltpu.VMEM((SUBLANES, LANES), F32)],
        compiler_params=pltpu.CompilerParams(dimension_semantics=("arbitrary",), vmem_limit_bytes=VMEM_LIMIT),
        name="proj",
    )(x2, *consts)


def _attn_kernel(first_ref, q_ref, k_ref, v_ref, o_ref, acc_scr, m_scr, *, ts, safe):
    i = pl.program_id(2)
    nsub = q_ref.shape[0] // ts
    acc_scr[...] = jnp.zeros(acc_scr.shape, F32)
    if safe:
        m_scr[...] = jnp.full(m_scr.shape, NEG_INF, F32)

    def block(sub, j, masked):
        off = pl.multiple_of(j * ts, ts)
        s = _nt_dot(q_ref[sub * ts:(sub + 1) * ts, :], k_ref[pl.ds(off, ts), :])
        if masked:
            rows = lax.broadcasted_iota(I32, (ts, ts), 0)
            cols = lax.broadcasted_iota(I32, (ts, ts), 1)
            s = jnp.where(cols <= rows, s, NEG_INF)
        vblk = v_ref[pl.ds(off, ts), :]
        if safe:
            m_prev = m_scr[sub]
            m_new = jnp.maximum(m_prev, jnp.max(s, axis=1, keepdims=True))
            p = jnp.exp2(s - _lane_tile(m_new, ts // LANES)).astype(BF16)
            acc_scr[sub] = jnp.exp2(m_prev - m_new) * acc_scr[sub] + _dot(p, vblk)
            m_scr[sub] = m_new
        else:
            acc_scr[sub] += _dot(jnp.exp2(s).astype(BF16), vblk)

    def body(jp, carry):
        for dj in range(KV_UNROLL):
            for sub in range(nsub):
                block(sub, jp * KV_UNROLL + dj, False)
        return carry

    step = (pl.program_id(0) * pl.num_programs(1) + pl.program_id(1)) * pl.num_programs(2) + i
    lax.fori_loop(first_ref[step] // KV_UNROLL, (nsub * i) // KV_UNROLL, body, 0)
    for jj in range(nsub):
        for sub in range(jj, nsub):
            block(sub, nsub * i + jj, sub == jj)
    for sub in range(nsub):
        acc = acc_scr[sub]
        o_ref[sub * ts:(sub + 1) * ts, :] = (acc / acc[:, HEAD_DIM:HEAD_DIM + 1]).astype(o_ref.dtype)


def _first_kv_block(fedge, b, s, tq, ts):
    nblk = s // ts
    f_first = fedge[:, 0, ::FORGET_COPIES_STRIDE].reshape(b, nblk, N_HEADS).transpose(0, 2, 1)
    f_last = fedge[:, 1, ::FORGET_COPIES_STRIDE].reshape(b, nblk, N_HEADS).transpose(0, 2, 1)
    tile_start = f_first[:, :, ::tq // ts]
    bound = LOG2E * (tile_start[:, :, :, None] - f_last[:, :, None, :])
    return jnp.sum(bound < -UNDERFLOW_LOG2, axis=-1).astype(I32).reshape(-1)


def _attention(q, k, v, fedge, plain_ok):
    b, s, hw = q.shape
    tq, ts = TQ_ATTN, TS_ATTN
    nq = s // tq

    def call(safe):
        first = jnp.zeros((b * N_HEADS * nq,), I32) if safe else _first_kv_block(fedge, b, s, tq, ts)
        return pl.pallas_call(
            functools.partial(_attn_kernel, ts=ts, safe=safe),
            grid_spec=pltpu.PrefetchScalarGridSpec(
                num_scalar_prefetch=1,
                grid=(b, hw // LANES, nq),
                in_specs=[pl.BlockSpec((None, tq, LANES), lambda bi, h, i, f: (bi, i, h)),
                          pl.BlockSpec((None, s, LANES), lambda bi, h, i, f: (bi, 0, h)),
                          pl.BlockSpec((None, s, LANES), lambda bi, h, i, f: (bi, 0, h))],
                out_specs=pl.BlockSpec((None, tq, LANES), lambda bi, h, i, f: (bi, i, h)),
                scratch_shapes=[pltpu.VMEM((tq // ts, ts, LANES), F32)] * 2),
            out_shape=jax.ShapeDtypeStruct((b, s, hw), BF16),
            compiler_params=pltpu.CompilerParams(
                dimension_semantics=("arbitrary", "arbitrary", "arbitrary"), vmem_limit_bytes=VMEM_LIMIT),
            name="attn_safe" if safe else "attn",
        )(first, q, k, v)

    return lax.cond(plain_ok, lambda: call(False), lambda: call(True))


def _window_select(pieces, first_group):
    window = lax.broadcasted_iota(I32, pieces[0].shape, 1) // SSM_GROUP
    out = pieces[0]
    for m in range(1, GROUPS_PER_BLOCK):
        out = jnp.where(window == (first_group + m) % GROUPS_PER_BLOCK, pieces[m], out)
    return out


def _s5_kernel(u_ref, win_ref, tw_ref, coef_ref, y_ref, ub_scr, yb_scr, *, nk):
    nj = CHUNKS_PER_SUPER
    gpb = GROUPS_PER_BLOCK
    w = CHUNK * SSM_GROUP
    half = w // 2
    nkk = nk // SUBLANES
    row_stride = SUPER

    def gather_body(it, carry):
        j, kk = it // nkk, it % nkk
        for sp in range(CHUNK // gpb):
            rot = []
            for m in range(gpb):
                row0 = j * CHUNK + sp * gpb + m + kk * (SUBLANES * row_stride)
                src = u_ref[pl.ds(row0, SUBLANES, stride=row_stride), :]
                rot.append(src if m == 0 else pltpu.roll(src, m * SSM_GROUP, axis=1))
            for gl in range(gpb):
                ub_scr[gl, j, pl.ds(kk * SUBLANES, SUBLANES), sp * LANES:(sp + 1) * LANES] = _window_select(rot, gl)
        return carry

    lax.fori_loop(0, nj * nkk, gather_body, 0, unroll=RELAYOUT_UNROLL)

    def swap(val):
        return jnp.concatenate([val[:, half:], val[:, :half]], axis=1)

    def group_body(gl, carry):
        def cmul(val, idx):
            return coef_ref[gl, idx:idx + 1, :] * val + coef_ref[gl, idx + 1:idx + 2, :] * swap(val)

        u = ub_scr[gl].reshape(nj * nk, w).astype(BF16)
        s1 = _dot(u, win_ref[gl]).reshape(nj, nk, w)
        e = jnp.zeros((nk, w), F32)
        local = []
        for j in range(nj):
            local.append(e)
            e = cmul(e, 0) + s1[j]
        kidx = lax.broadcasted_iota(I32, (nk, w), 0)
        x = e
        step, d = 0, 1
        while d < nk:
            shifted = jnp.where(kidx >= d, pltpu.roll(x, d, axis=0), 0.0)
            x = x + cmul(shifted, 2 + 2 * nj + 2 * step)
            step, d = step + 1, d * 2
        x_start = jnp.where(kidx >= 1, pltpu.roll(x, 1, axis=0), 0.0)
        starts = [local[j] + cmul(x_start, 2 + 2 * j) for j in range(nj)]
        p = jnp.concatenate([st[:, :half] for st in starts], axis=0)
        hi = p.astype(BF16)
        lo = (p - hi.astype(F32)).astype(BF16)
        y = _dot(jnp.concatenate([u, hi, lo], axis=1), tw_ref[gl])
        yb_scr[gl] = y.reshape(nj, nk, w)
        return carry

    lax.fori_loop(0, gpb, group_body, 0)

    def scatter_body(it, carry):
        j, kk = it // nkk, it % nkk
        for tp in range(CHUNK // gpb):
            src = [yb_scr[gl, j, pl.ds(kk * SUBLANES, SUBLANES), tp * LANES:(tp + 1) * LANES] for gl in range(gpb)]
            for m in range(gpb):
                by_window = [src[(wi - m) % gpb] for wi in range(gpb)]
                window = lax.broadcasted_iota(I32, by_window[0].shape, 1) // SSM_GROUP
                merged = by_window[0]
                for wi in range(1, gpb):
                    merged = jnp.where(window == wi, by_window[wi], merged)
                nat = merged if m == 0 else pltpu.roll(merged, LANES - m * SSM_GROUP, axis=1)
                row0 = j * CHUNK + tp * gpb + m + kk * (SUBLANES * row_stride)
                y_ref[pl.ds(row0, SUBLANES, stride=row_stride), :] = nat
        return carry

    lax.fori_loop(0, nj * nkk, scatter_body, 0, unroll=RELAYOUT_UNROLL)


def _s5_params(a_re, a_im, log_dt, b_re, b_im, c_re, c_im, nk):
    args = (a_re, a_im, log_dt, b_re, b_im, c_re, c_im)
    slowest = jnp.min(a_re * jnp.exp(log_dt)[:, None])
    return lax.cond(slowest > MIN_FACTORED_DECAY, functools.partial(_s5_params_factored, nk=nk),
                    functools.partial(_s5_params_direct, nk=nk), *args)


def _s5_params_factored(a_re, a_im, log_dt, b_re, b_im, c_re, c_im, *, nk):
    g, p = a_re.shape
    c = SSM_GROUP
    gpb = GROUPS_PER_BLOCK
    dt = jnp.exp(log_dt)[:, None]
    adt_r, adt_i = a_re * dt, a_im * dt

    def lam_pow(n):
        nf = jnp.asarray(n, F32)
        nf = (nf[None] if nf.ndim == 1 else nf)[:, :, None]
        mag = jnp.exp(adt_r[:, None, :] * nf)
        ang = adt_i[:, None, :] * nf
        return mag * jnp.cos(ang), mag * jnp.sin(ang)

    gi = jnp.arange(g)[:, None, None] % gpb
    step = ((jnp.arange(CHUNK // gpb)[None, :, None]) * gpb
            + (jnp.arange(gpb)[None, None, :] - gi) % gpb).reshape(g, CHUNK)

    l1r, l1i = lam_pow(jnp.ones((1,)))
    den = a_re * a_re + a_im * a_im
    nr, ni = l1r[:, 0] - 1.0, l1i[:, 0]
    qr = (nr * a_re + ni * a_im) / den
    qi = (ni * a_re - nr * a_im) / den
    bcr = (qr[..., None] * b_re - qi[..., None] * b_im).transpose(0, 2, 1)
    bci = (qr[..., None] * b_im + qi[..., None] * b_re).transpose(0, 2, 1)

    def times_b(pr, pi):
        pr, pi = pr[:, :, None, :], pi[:, :, None, :]
        return pr * bcr[:, None] - pi * bci[:, None], pr * bci[:, None] + pi * bcr[:, None]

    def times_c(pr, pi):
        pr, pi = pr[:, :, None, :], pi[:, :, None, :]
        return c_re[:, None] * pr - c_im[:, None] * pi, c_re[:, None] * pi + c_im[:, None] * pr

    rows = lambda parts: jnp.concatenate(parts, axis=-1).reshape(g, CHUNK * c, -1)
    ir, ii = times_b(*lam_pow(CHUNK - 1 - step))
    win = rows([ir, ii, ii, ir])
    clr, cli = times_c(*lam_pow(step + 1))
    wout = rows([clr, -cli]).transpose(0, 2, 1)
    ar, ai = times_b(*lam_pow(-step))
    br, bi = times_c(*lam_pow(step))
    toep = jnp.einsum('gnp,gmp->gnm', rows([ar, ai]), rows([br, -bi]), precision=lax.Precision.HIGHEST)
    step_of_lane = jnp.repeat(step, c, axis=1)
    toep = jnp.where(step_of_lane[:, None, :] >= step_of_lane[:, :, None], toep, 0.0)
    tw = jnp.concatenate([toep, wout, wout], axis=1)
    return win.astype(BF16), tw.astype(BF16), _s5_scan_coefficients(lam_pow, g, p, nk)


def _s5_scan_coefficients(lam_pow, g, p, nk):
    n_steps = max(nk.bit_length() - 1, 0)
    powers = [CHUNK] + [CHUNK * j for j in range(CHUNKS_PER_SUPER)] + [SUPER * (1 << i) for i in range(n_steps)]
    ar, ai = lam_pow(jnp.array(powers, dtype=jnp.int32))
    c1 = jnp.concatenate([ar, ar, ar, ar], axis=2)
    c2 = jnp.concatenate([-ai, ai, ai, -ai], axis=2)
    coef = jnp.stack([c1, c2], axis=2).reshape(g, 2 * len(powers), 4 * p)
    return jnp.pad(coef, ((0, 0), (0, (-coef.shape[1]) % SUBLANES), (0, 0)))


def _s5_params_direct(a_re, a_im, log_dt, b_re, b_im, c_re, c_im, *, nk):
    g, p = a_re.shape
    c = SSM_GROUP
    hp = lax.Precision.HIGHEST
    dt = jnp.exp(log_dt)[:, None]
    adt_r, adt_i = a_re * dt, a_im * dt

    def lam_pow(n):
        nf = jnp.asarray(n, F32)[None, :, None]
        mag = jnp.exp(adt_r[:, None, :] * nf)
        ang = adt_i[:, None, :] * nf
        return mag * jnp.cos(ang), mag * jnp.sin(ang)

    lr, li = lam_pow(jnp.arange(CHUNK + 1))
    den = a_re * a_re + a_im * a_im
    nr, ni = lr[:, 1] - 1.0, li[:, 1]
    qr = (nr * a_re + ni * a_im) / den
    qi = (ni * a_re - nr * a_im) / den
    bbr = qr[..., None] * b_re - qi[..., None] * b_im
    bbi = qr[..., None] * b_im + qi[..., None] * b_re

    mr = lr[:, :CHUNK, :, None] * bbr[:, None] - li[:, :CHUNK, :, None] * bbi[:, None]
    mi = lr[:, :CHUNK, :, None] * bbi[:, None] + li[:, :CHUNK, :, None] * bbr[:, None]
    kern = (jnp.einsum('gcp,gtpd->gtcd', c_re, mr, precision=hp)
            - jnp.einsum('gcp,gtpd->gtcd', c_im, mi, precision=hp))
    toep = jnp.stack([jnp.pad(kern[:, :CHUNK - s], ((0, 0), (s, 0), (0, 0), (0, 0))) for s in range(CHUNK)],
                     axis=1)
    toep = toep.transpose(0, 1, 4, 2, 3).reshape(g, CHUNK * c, CHUNK * c)
    clr = c_re[:, None] * lr[:, 1:, None, :] - c_im[:, None] * li[:, 1:, None, :]
    cli = c_re[:, None] * li[:, 1:, None, :] + c_im[:, None] * lr[:, 1:, None, :]
    w_re = clr.transpose(0, 3, 1, 2).reshape(g, p, CHUNK * c)
    w_im = (-cli).transpose(0, 3, 1, 2).reshape(g, p, CHUNK * c)
    wout = jnp.concatenate([w_re, w_im], axis=1)
    pr, pi = lr[:, :CHUNK][:, ::-1], li[:, :CHUNK][:, ::-1]
    ir = pr[..., None] * bbr[:, None] - pi[..., None] * bbi[:, None]
    ii = pr[..., None] * bbi[:, None] + pi[..., None] * bbr[:, None]
    ir = ir.transpose(0, 1, 3, 2).reshape(g, CHUNK * c, p)
    ii = ii.transpose(0, 1, 3, 2).reshape(g, CHUNK * c, p)
    win = jnp.concatenate([ir, ii, ii, ir], axis=2)

    gi = jnp.arange(g)[:, None, None] % GROUPS_PER_BLOCK
    si = jnp.arange(CHUNK)[None, :, None]
    ci = jnp.arange(c)[None, None, :]
    lane_of = ((si // GROUPS_PER_BLOCK) * LANES + ((gi + si) % GROUPS_PER_BLOCK) * c + ci).reshape(g, CHUNK * c)
    perm = (lane_of[:, None, :] == jnp.arange(CHUNK * c)[None, :, None]).astype(BF16)
    win = jnp.einsum('gln,gnk->glk', perm, win.astype(BF16))
    toep = jnp.einsum('gln,gnk->glk', perm, toep.astype(BF16))
    tw = jnp.concatenate([toep, wout.astype(BF16), wout.astype(BF16)], axis=1)
    tw = jnp.einsum('grn,gln->grl', tw, perm)
    return win.astype(BF16), tw.astype(BF16), _s5_scan_coefficients(lam_pow, g, p, nk)


def _s5(u, win, tw, coef):
    b, s, dssm = u.shape
    nk = s // SUPER
    nj = CHUNKS_PER_SUPER
    w = CHUNK * SSM_GROUP
    gpb = GROUPS_PER_BLOCK
    nblk = dssm // LANES
    wspec = lambda a: pl.BlockSpec((gpb,) + a.shape[1:], lambda bi, li: (li, 0, 0))
    return pl.pallas_call(
        functools.partial(_s5_kernel, nk=nk),
        grid=(b, nblk),
        in_specs=[pl.BlockSpec((None, s, LANES), lambda bi, li: (bi, 0, li)), wspec(win), wspec(tw), wspec(coef)],
        out_specs=pl.BlockSpec((None, s, LANES), lambda bi, li: (bi, 0, li)),
        out_shape=jax.ShapeDtypeStruct((b, s, dssm), F32),
        scratch_shapes=[pltpu.VMEM((gpb, nj, nk, w), F32)] * 2,
        compiler_params=pltpu.CompilerParams(dimension_semantics=("arbitrary", "arbitrary"),
                                             vmem_limit_bytes=VMEM_LIMIT),
        name="s5",
    )(u, win, tw, coef)


def _route(logits):
    e_all = logits[0:N_EXPERTS]
    gl = logits[N_EXPERTS:N_EXPERTS + SUBLANES]
    tm = logits.shape[1]
    ridx = lax.broadcasted_iota(I32, (SUBLANES, tm), 0)
    ge = jnp.exp(gl - jnp.max(gl, axis=0, keepdims=True))
    gp = ge / jnp.sum(ge, axis=0, keepdims=True)
    g_top = jnp.max(gp, axis=0, keepdims=True)
    g_sel = jnp.min(jnp.where(gp == g_top, ridx, SUBLANES), axis=0, keepdims=True)
    e_in = e_all[(N_EXPERT_GROUPS - 1) * SUBLANES:]
    for gi in range(N_EXPERT_GROUPS - 2, -1, -1):
        e_in = jnp.where(g_sel == gi, e_all[gi * SUBLANES:(gi + 1) * SUBLANES], e_in)
    ee = jnp.exp(e_in - jnp.max(e_in, axis=0, keepdims=True))
    ep = ee / jnp.sum(ee, axis=0, keepdims=True)
    v1 = jnp.max(ep, axis=0, keepdims=True)
    i1 = jnp.min(jnp.where(ep == v1, ridx, SUBLANES), axis=0, keepdims=True)
    ep2 = jnp.where(ridx == i1, -1.0, ep)
    v2 = jnp.max(ep2, axis=0, keepdims=True)
    i2 = jnp.min(jnp.where(ep2 == v2, ridx, SUBLANES), axis=0, keepdims=True)
    den = v1 + v2
    w1 = g_top * v1 / den
    w2 = g_top * v2 / den
    e1 = g_sel * EXPERTS_PER_GROUP + i1
    e2 = g_sel * EXPERTS_PER_GROUP + i2
    ids = jnp.where(ridx == 0, e1, jnp.where(ridx == 1, e2, 0))
    wts = jnp.where(ridx == 0, w1, jnp.where(ridx == 1, w2, 0.0))
    return ids, wts


def _mix_kernel(x_ref, ya_ref, ys_ref, u_ref, g1_ref, wgate_ref, dsk_ref, wglu_ref, wpa_ref, wps_ref,
                wout_ref, g2_ref, wrh_ref, wrl_ref, br_ref, x1_ref, xp_ref, logits_ref):
    d = x_ref.shape[1]
    rows = x_ref.shape[0] // MIX_CHAINS
    for chain in range(MIX_CHAINS):
        r = slice(chain * rows, (chain + 1) * rows)
        x = x_ref[r, :]
        xb = _rmsnorm_rows(x, g1_ref[...]).astype(BF16)
        gates = _dot(xb, wgate_ref[...])
        y = ys_ref[r, :].astype(F32) + dsk_ref[...] * u_ref[r, :].astype(F32)
        y = y * (0.5 * (1.0 + jnp.tanh(math.sqrt(2.0 / math.pi) * (y + 0.044715 * (y * y * y)))))
        y = y * _sigmoid(_dot(y.astype(BF16), wglu_ref[...]))
        mixed = (_sigmoid(gates[:, :d]) * _dot(ya_ref[r, :], wpa_ref[...])
                 + _sigmoid(gates[:, d:]) * _dot(y.astype(BF16), wps_ref[...]))
        x1 = x + _dot(mixed.astype(BF16), wout_ref[...])
        x1_ref[r, :] = x1
        xn = _rmsnorm_rows(x1, g2_ref[...])
        xp_ref[r, :] = pltpu.pack_elementwise([xn[:, :d // 2], xn[:, d // 2:]], packed_dtype=BF16)
        xh = xn.astype(BF16)
        xl = (xn - xh.astype(F32)).astype(BF16)
        wrh = wrh_ref[...]
        logits_ref[:, r] = _nt_dot(wrh, xh) + _nt_dot(wrl_ref[...], xh) + _nt_dot(wrh, xl) + br_ref[...]


def _mix(x2, ya, ys, u, g1, wgate, dsk, wglu, wpa, wps, wout, g2, wrh, wrl, br):
    t, d = x2.shape
    tm = TM_MIX
    full = lambda a: pl.BlockSpec(a.shape, lambda i: (0,) * a.ndim)
    row = lambda width: pl.BlockSpec((tm, width), lambda i: (i, 0))
    return pl.pallas_call(
        _mix_kernel,
        grid=(t // tm,),
        in_specs=[row(d), row(ya.shape[1]), row(ys.shape[1]), row(u.shape[1]), full(g1), full(wgate), full(dsk),
                  full(wglu), full(wpa),
                  full(wps), full(wout), full(g2), full(wrh), full(wrl), full(br)],
        out_specs=[row(d), row(d // 2), pl.BlockSpec((ROUTER_ROWS, tm), lambda i: (0, i))],
        out_shape=[jax.ShapeDtypeStruct((t, d), F32), jax.ShapeDtypeStruct((t, d // 2), jnp.uint32),
                   jax.ShapeDtypeStruct((ROUTER_ROWS, t), F32)],
        compiler_params=pltpu.CompilerParams(dimension_semantics=("arbitrary",), vmem_limit_bytes=VMEM_LIMIT),
        name="mix",
    )(x2, ya, ys, u, g1, wgate, dsk, wglu, wpa, wps, wout, g2, wrh, wrl, br)


def _route_kernel(logits_ref, ids_ref, wts_ref):
    ids, wts = _route(logits_ref[...])
    ids_ref[...] = ids
    wts_ref[...] = wts


def _route_call(logits):
    t = logits.shape[1]
    tl = ROUTE_LANES
    col = pl.BlockSpec((SUBLANES, tl), lambda i: (0, i))
    return pl.pallas_call(
        _route_kernel,
        grid=(t // tl,),
        in_specs=[pl.BlockSpec((ROUTER_ROWS, tl), lambda i: (0, i))],
        out_specs=[col, col],
        out_shape=[jax.ShapeDtypeStruct((SUBLANES, t), I32), jax.ShapeDtypeStruct((SUBLANES, t), F32)],
        compiler_params=pltpu.CompilerParams(dimension_semantics=("arbitrary",)),
        name="route",
    )(logits)


def _meta_kernel(ids_ref, tri_ref, dest_ref, tile_ref, *, tile_rows):
    nk, nc, c = ids_ref.shape
    ne = N_EXPERTS
    erow = lax.broadcasted_iota(I32, (ne, c), 0)
    ones = jnp.ones((c, LANES), BF16)

    def onehot(k, ci):
        mask = erow == ids_ref[k, pl.ds(ci, 1), :]
        return mask, jnp.where(mask, 1.0, 0.0).astype(BF16)

    def count_body(n, acc):
        return acc + _dot(onehot(n // nc, n % nc)[1], ones)

    cnt = lax.fori_loop(0, nk * nc, count_body, jnp.zeros((ne, LANES), F32))
    ntiles = jnp.floor((cnt + (tile_rows - 1)) * (1.0 / tile_rows))
    lower = jnp.where(lax.broadcasted_iota(I32, (ne, ne), 1) < lax.broadcasted_iota(I32, (ne, ne), 0), 1.0, 0.0)
    start_tiles = _dot(lower.astype(BF16), ntiles.astype(BF16))
    base = start_tiles * tile_rows

    tri = tri_ref[...]

    def dest_body(n, carry):
        k, ci = n // nc, n % nc
        mask, oh = onehot(k, ci)
        prefix = _dot(oh, tri)
        slot = _lane_tile(base + carry, c // LANES) + prefix - 1.0
        dest = jnp.sum(jnp.where(mask, slot, 0.0), axis=0, keepdims=True)
        dest_ref[k, pl.ds(ci, 1), :] = dest.astype(I32)
        return carry + _dot(oh, ones)

    lax.fori_loop(0, nk * nc, dest_body, jnp.zeros((ne, LANES), F32))

    nt_lanes = tile_ref.shape[1]
    end_tiles = _lane_tile(start_tiles + ntiles, nt_lanes // LANES)
    tidx = lax.broadcasted_iota(I32, (ne, nt_lanes), 1).astype(F32)
    texp = jnp.sum(jnp.where(tidx >= end_tiles, 1.0, 0.0), axis=0, keepdims=True)
    valid = jnp.where(texp < ne, 1, 0)
    texp = jnp.minimum(texp, ne - 1.0).astype(I32)
    ridx = lax.broadcasted_iota(I32, tile_ref.shape, 0)
    tile_ref[...] = jnp.where(ridx == 0, texp, jnp.where(ridx == 1, valid, 0))


def _meta(ids2, tri, n_tiles):
    nk, t = ids2.shape
    c = SORT_CHUNK
    nt_lanes = pl.cdiv(n_tiles, LANES) * LANES
    ids3 = ids2.reshape(nk, t // c, c)
    dest, tile = pl.pallas_call(
        functools.partial(_meta_kernel, tile_rows=TM_EXPERT),
        out_shape=[jax.ShapeDtypeStruct(ids3.shape, I32), jax.ShapeDtypeStruct((SUBLANES, nt_lanes), I32)],
        compiler_params=pltpu.CompilerParams(vmem_limit_bytes=VMEM_LIMIT),
        name="meta",
    )(ids3, tri)
    return dest.reshape(nk * t), tile[0, :n_tiles], tile[1, :n_tiles]


def _row_copy(src_ref, src_row, dst_ref, dst_row, sem):
    return pltpu.make_async_copy(src_ref.at[pl.ds(src_row, 1)], dst_ref.at[pl.ds(dst_row, 1)], sem)


def _dispatch_kernel(dest_ref, xp_ref, xs_in_ref, xs_ref, sem, *, n_tokens):
    del xs_in_ref
    tm = xp_ref.shape[0]
    base = pl.program_id(0) * tm

    def body(r, carry):
        for k in range(2):
            _row_copy(xp_ref, r, xs_ref, dest_ref[k * n_tokens + base + r], sem).start()
        return carry

    lax.fori_loop(0, tm, body, 0, unroll=ROW_DMA_UNROLL)
    for k in range(2):
        pltpu.make_async_copy(xp_ref, xs_ref.at[pl.ds(0, tm)], sem).wait()


def _zero_fill_kernel(o_ref):
    o_ref[...] = jnp.zeros(o_ref.shape, o_ref.dtype)


def _zeros(n_rows, width, dtype):
    rows = n_rows // pl.cdiv(n_rows, ZERO_FILL_ROWS)
    assert n_rows % rows == 0 and rows % SUBLANES == 0
    return pl.pallas_call(
        _zero_fill_kernel,
        grid=(n_rows // rows,),
        out_specs=pl.BlockSpec((rows, width), lambda i: (i, 0)),
        out_shape=jax.ShapeDtypeStruct((n_rows, width), dtype),
        compiler_params=pltpu.CompilerParams(dimension_semantics=("arbitrary",)),
        name="zero_fill",
    )()


def _dispatch(dest, xp, n_slots):
    t, w = xp.shape
    tm = TM_DISPATCH
    xs0 = _zeros(n_slots, w, xp.dtype)
    return pl.pallas_call(
        functools.partial(_dispatch_kernel, n_tokens=t),
        grid_spec=pltpu.PrefetchScalarGridSpec(
            num_scalar_prefetch=1,
            grid=(t // tm,),
            in_specs=[pl.BlockSpec((tm, w), lambda i, d: (i, 0)), pl.BlockSpec(memory_space=pl.ANY)],
            out_specs=pl.BlockSpec(memory_space=pl.ANY),
            scratch_shapes=[pltpu.SemaphoreType.DMA(())]),
        out_shape=jax.ShapeDtypeStruct(xs0.shape, xp.dtype),
        input_output_aliases={2: 0},
        compiler_params=pltpu.CompilerParams(dimension_semantics=("arbitrary",), has_side_effects=True),
        name="dispatch",
    )(dest, xp, xs0)


def _unpack_rows(packed):
    lo = pltpu.unpack_elementwise(packed, index=0, packed_dtype=BF16, unpacked_dtype=F32)
    hi = pltpu.unpack_elementwise(packed, index=1, packed_dtype=BF16, unpacked_dtype=F32)
    return jnp.concatenate([lo, hi], axis=1)


def _expert_kernel(te_ref, tv_ref, nv_ref, xs_ref, wg_ref, wu_ref, wd_ref, ys_ref, wg_bf, wu_bf, wd_bf):
    del nv_ref
    i = pl.program_id(0)
    half = xs_ref.shape[1]

    @pl.when((i == 0) | (te_ref[i] != te_ref[jnp.maximum(i - 1, 0)]))
    def _():
        wg_bf[...] = wg_ref[...].astype(BF16)
        wu_bf[...] = wu_ref[...].astype(BF16)
        wd_bf[...] = wd_ref[...].astype(BF16)

    @pl.when(tv_ref[i] > 0)
    def _():
        x = _unpack_rows(xs_ref[...]).astype(BF16)
        hg = _dot(x, wg_bf[...])
        hu = _dot(x, wu_bf[...])
        h = (hg * _sigmoid(hg) * hu).astype(BF16)
        y = _dot(h, wd_bf[...])
        ys_ref[...] = pltpu.pack_elementwise([y[:, :half], y[:, half:]], packed_dtype=BF16)

    @pl.when(tv_ref[i] == 0)
    def _():
        zero = jnp.zeros(ys_ref.shape, F32)
        ys_ref[...] = pltpu.pack_elementwise([zero, zero], packed_dtype=BF16)


def _experts(tile_expert, tile_valid, xs, wg, wu, wd):
    n_slots, w = xs.shape
    tm = TM_EXPERT
    n_tiles = n_slots // tm
    n_valid = jnp.sum(tile_valid).astype(I32).reshape(1)
    wspec = lambda a: pl.BlockSpec((None,) + a.shape[1:], lambda i, te, tv, nv: (te[i], 0, 0))
    rows_in = pl.BlockSpec((tm, w), lambda i, te, tv, nv: (jnp.minimum(i, nv[0] - 1), 0))
    rows_out = pl.BlockSpec((tm, w), lambda i, te, tv, nv: (i, 0))
    return pl.pallas_call(
        _expert_kernel,
        grid_spec=pltpu.PrefetchScalarGridSpec(
            num_scalar_prefetch=3,
            grid=(n_tiles,),
            in_specs=[rows_in, wspec(wg), wspec(wu), wspec(wd)],
            out_specs=rows_out,
            scratch_shapes=[pltpu.VMEM(a.shape[1:], BF16) for a in (wg, wu, wd)]),
        out_shape=jax.ShapeDtypeStruct(xs.shape, xs.dtype),
        compiler_params=pltpu.CompilerParams(dimension_semantics=("arbitrary",), vmem_limit_bytes=VMEM_LIMIT),
        name="experts",
    )(tile_expert, tile_valid, n_valid, xs, wg, wu, wd)


def _combine_kernel(dest_ref, x1_ref, w0_ref, w1_ref, ys_ref, out_ref, buf, sem, *, n_tokens):
    tm, d = x1_ref.shape
    base = pl.program_id(0) * tm

    def body(r, carry):
        for k in range(2):
            _row_copy(ys_ref, dest_ref[k * n_tokens + base + r], buf.at[k], r, sem).start()
        return carry

    lax.fori_loop(0, tm, body, 0, unroll=ROW_DMA_UNROLL)
    for k in range(2):
        pltpu.make_async_copy(ys_ref.at[pl.ds(0, tm)], buf.at[k], sem).wait()
    reps = d // LANES
    out_ref[...] = (x1_ref[...]
                    + _lane_tile(w0_ref[...], reps) * _unpack_rows(buf[0])
                    + _lane_tile(w1_ref[...], reps) * _unpack_rows(buf[1]))


def _combine(dest, x1, w0, w1, ys):
    t, d = x1.shape
    tm = TM_COMBINE
    row = lambda width: pl.BlockSpec((tm, width), lambda i, dref: (i, 0))
    return pl.pallas_call(
        functools.partial(_combine_kernel, n_tokens=t),
        grid_spec=pltpu.PrefetchScalarGridSpec(
            num_scalar_prefetch=1,
            grid=(t // tm,),
            in_specs=[row(d), row(LANES), row(LANES), pl.BlockSpec(memory_space=pl.ANY)],
            out_specs=row(d),
            scratch_shapes=[pltpu.VMEM((2, tm, ys.shape[1]), ys.dtype), pltpu.SemaphoreType.DMA(())]),
        out_shape=jax.ShapeDtypeStruct((t, d), x1.dtype),
        compiler_params=pltpu.CompilerParams(dimension_semantics=("arbitrary",), vmem_limit_bytes=VMEM_LIMIT),
        name="combine",
    )(dest, x1, w0, w1, ys)


def _lower_tri(n):
    return jnp.where(jnp.arange(n)[:, None] >= jnp.arange(n)[None, :], 1.0, 0.0).astype(BF16)


def _upper_tri(n):
    return jnp.where(jnp.arange(n)[:, None] <= jnp.arange(n)[None, :], 1.0, 0.0).astype(BF16)


def _split_bf16(w):
    hi = w.astype(BF16)
    lo = (w - hi.astype(F32)).astype(BF16)
    return hi, lo


def _layer(x, norm_mix_g, w_in, b_forget, q_norm_g, k_norm_g, ssm_A_re, ssm_A_im, ssm_log_dt, ssm_B_re, ssm_B_im,
           ssm_C_re, ssm_C_im, ssm_D, w_glu, w_proj_attn, w_proj_ssm, w_out, norm_ffn_g, w_router_group,
           b_router_group, w_router_expert, b_router_expert, w_expert_gate, w_expert_up, w_expert_down):
    b, s, d = x.shape
    t = b * s
    da = N_HEADS * HEAD_DIM
    dssm = ssm_D.shape[0]
    nk = s // SUPER
    assert s % SUPER == 0 and nk % SUBLANES == 0 and nk & (nk - 1) == 0 and s % TQ_ATTN == 0 and t % TM_PROJ == 0
    assert TM_PROJ == TS_ATTN
    x2 = x.reshape(t, d)

    o_f = 3 * da
    o_u = o_f + N_HEADS
    o_g = o_u + dssm
    w_main = jnp.concatenate([w_in[:, :o_f], w_in[:, o_u:o_g]], axis=1).astype(BF16)
    wf = jnp.repeat(w_in[:, o_f:o_u], FORGET_COPIES_STRIDE, axis=1).astype(BF16)
    bf = jnp.repeat(b_forget, FORGET_COPIES_STRIDE)[None, :]
    w_gates = w_in[:, o_g:].astype(BF16)
    qg = (jnp.tile(q_norm_g, N_HEADS) * (HEAD_DIM ** -0.5 * LOG2E))[None, :]
    kg = jnp.tile(k_norm_g, N_HEADS)[None, :]
    head_of = jnp.arange(da) // HEAD_DIM
    bd = jnp.where(head_of[:, None] == head_of[None, :], 1.0 / HEAD_DIM, 0.0).astype(BF16)
    logit_bound = 1.02 * HEAD_DIM ** 0.5 * jnp.max(jnp.abs(q_norm_g)) * jnp.max(jnp.abs(k_norm_g))
    plain_ok = LOG2E * logit_bound <= PLAIN_SOFTMAX_MAX_LOG2
    shift = jnp.full((1, LANES), logit_bound, F32)
    hw = N_HEADS * LANES
    lane_in_head = jnp.arange(hw) % LANES
    ones = jnp.stack([
        (lane_in_head >= BIAS_K_LANE) & (lane_in_head < BIAS_K_LANE + 3),
        (lane_in_head >= BIAS_Q_LANE) & (lane_in_head < BIAS_Q_LANE + 3),
        lane_in_head == HEAD_DIM]
        + [jnp.zeros((hw,), bool)] * (SUBLANES - 3)).astype(F32)
    prow = jnp.arange(LANES)
    head, copy = prow // FORGET_COPIES_STRIDE, prow % FORGET_COPIES_STRIDE
    target = jnp.where(copy < 3, head * LANES + BIAS_Q_LANE + copy, hw + head * LANES + BIAS_K_LANE + copy - 3)
    place = ((jnp.arange(2 * hw)[None, :] == target[:, None]) & (copy < 6)[:, None]).astype(BF16)

    q, k, v, u, fedge = _proj(x2, norm_mix_g[None, :], w_main, wf, bf, qg, kg, bd, _lower_tri(TM_PROJ), place, ones,
                              shift, seq=s)
    y_attn = _attention(q.reshape(b, s, hw), k.reshape(b, s, hw), v.reshape(b, s, hw), fedge, plain_ok)
    w_pa = jnp.pad(w_proj_attn.reshape(N_HEADS, HEAD_DIM, d), ((0, 0), (0, LANES - HEAD_DIM), (0, 0)))
    w_pa = w_pa.reshape(hw, d).astype(BF16)

    win, tw, coef = _s5_params(ssm_A_re, ssm_A_im, ssm_log_dt, ssm_B_re, ssm_B_im, ssm_C_re, ssm_C_im, s // SUPER)
    y_ssm = _s5(u.reshape(b, s, dssm), win, tw, coef)

    wr = jnp.concatenate([w_router_expert.T, w_router_group.T,
                          jnp.zeros((ROUTER_ROWS - N_EXPERTS - N_EXPERT_GROUPS, d), F32)], axis=0)
    br = jnp.concatenate([b_router_expert, b_router_group,
                          jnp.full((SUBLANES - N_EXPERT_GROUPS,), NEG_INF, F32),
                          jnp.zeros((ROUTER_ROWS - N_EXPERTS - SUBLANES,), F32)])[:, None]
    x1, xp, logits = _mix(x2, y_attn.reshape(t, hw), y_ssm.reshape(t, dssm), u, norm_mix_g[None, :], w_gates,
                          ssm_D[None, :], w_glu.astype(BF16), w_pa, w_proj_ssm.astype(BF16),
                          w_out.astype(BF16), norm_ffn_g[None, :], *_split_bf16(wr), br)
    ids, wts = _route_call(logits)

    n_tiles = (2 * t) // TM_EXPERT + N_EXPERTS
    dest, tile_expert, tile_valid = _meta(ids[:2], _upper_tri(SORT_CHUNK), n_tiles)
    xs = _dispatch(dest, xp, n_tiles * TM_EXPERT)
    ys = _experts(tile_expert, tile_valid, xs, w_expert_gate, w_expert_up, w_expert_down)
    w0 = jnp.broadcast_to(wts[0][:, None], (t, LANES))
    w1 = jnp.broadcast_to(wts[1][:, None], (t, LANES))
    out = _combine(dest, x1, w0, w1, ys)
    return out.reshape(b, s, d)


def kernel(x, norm_mix_g, w_in, b_forget, q_norm_g, k_norm_g, ssm_A_re, ssm_A_im, ssm_log_dt, ssm_B_re, ssm_B_im,
           ssm_C_re, ssm_C_im, ssm_D, w_glu, w_proj_attn, w_proj_ssm, w_out, norm_ffn_g, w_router_group,
           b_router_group, w_router_expert, b_router_expert, w_expert_gate, w_expert_up, w_expert_down):
    layer_params = (norm_mix_g, w_in, b_forget, q_norm_g, k_norm_g, ssm_A_re, ssm_A_im, ssm_log_dt, ssm_B_re,
                    ssm_B_im, ssm_C_re, ssm_C_im, ssm_D, w_glu, w_proj_attn, w_proj_ssm, w_out, norm_ffn_g,
                    w_router_group, b_router_group, w_router_expert, b_router_expert, w_expert_gate, w_expert_up,
                    w_expert_down)
    for layer in range(norm_mix_g.shape[0]):
        x = _layer(x, *[p[layer] for p in layer_params])
    return x
```

```python
import functools
import math

import jax
import jax.numpy as jnp
from jax import lax
from jax.experimental import pallas as pl
from jax.experimental.pallas import tpu as pltpu

F32 = jnp.float32
BF16 = jnp.bfloat16
I32 = jnp.int32

LANES = 128
SUBLANES = 8
MXU_DIM = 256

N_HEADS = 8
HEAD_DIM = 64
SSM_GROUP = 16
SSM_STATE = 64
N_EXPERT_GROUPS = 4
EXPERTS_PER_GROUP = 8
N_EXPERTS = N_EXPERT_GROUPS * EXPERTS_PER_GROUP
EPS = 1e-6
NEG_INF = -1e30
LOG2E = math.log2(math.e)
PLAIN_SOFTMAX_MAX_LOG2 = 60.0
UNDERFLOW_LOG2 = 160.0
BIAS_K_LANE = HEAD_DIM
BIAS_Q_LANE = HEAD_DIM + 3
FORGET_COPIES_STRIDE = LANES // N_HEADS

CHUNK = MXU_DIM // SSM_GROUP
CHUNKS_PER_SUPER = 8
SUPER = CHUNK * CHUNKS_PER_SUPER
GROUPS_PER_BLOCK = LANES // SSM_GROUP
RELAYOUT_UNROLL = 8
MIN_FACTORED_DECAY = -4.0

TM_PROJ = 512
TQ_ATTN = 2048
TS_ATTN = 512
HEADS_PER_STEP = 2
TM_MIX = 512
MIX_CHAINS = 1
TM_EXPERT = 512
TM_DISPATCH = 512
TM_COMBINE = 512
SORT_CHUNK = 512
ZERO_FILL_ROWS = 2048
ROW_DMA_UNROLL = 8
ROUTE_LANES = 2048
ROUTER_ROWS = 48

VMEM_LIMIT = 48 * 1024 * 1024


def _nt_dot(a, b):
    return lax.dot_general(a, b, (((1,), (1,)), ((), ())), preferred_element_type=F32)


def _dot(a, b):
    return jnp.dot(a, b, preferred_element_type=F32)


def _lane_tile(x, n):
    return x if n == 1 else jnp.concatenate([x] * n, axis=1)


def _rmsnorm_rows(x, g):
    ms = jnp.mean(x * x, axis=-1, keepdims=True)
    return x * lax.rsqrt(ms + EPS) * g


def _sigmoid(x):
    return 1.0 / (1.0 + jnp.exp(-x))


def _split3(x):
    hi = x.astype(BF16)
    r1 = x - hi.astype(F32)
    mid = r1.astype(BF16)
    lo = (r1 - mid.astype(F32)).astype(BF16)
    return [hi, mid, lo]


def _expand_heads(z):
    lane = lax.broadcasted_iota(I32, (z.shape[0], LANES), 1)
    blocks = []
    for p in range(z.shape[1] // LANES):
        blk = z[:, p * LANES:(p + 1) * LANES]
        blocks.append(jnp.where(lane < HEAD_DIM, blk, 0.0))
        blocks.append(jnp.where(lane < HEAD_DIM, pltpu.roll(blk, HEAD_DIM, axis=1), 0.0))
    return jnp.concatenate(blocks, axis=1)


def _proj_kernel(x_ref, g_ref, w_ref, wf_ref, bf_ref, qg_ref, kg_ref, bd_ref, ltri_ref, place_ref,
                 ones_ref, shift_ref, q_ref, k_ref, v_ref, u_ref, fedge_ref, carry_ref, *, tiles_per_seq):
    i = pl.program_id(0)

    @pl.when(i % tiles_per_seq == 0)
    def _():
        carry_ref[...] = jnp.zeros_like(carry_ref)

    tm = x_ref.shape[0]
    da = N_HEADS * HEAD_DIM
    hw = N_HEADS * LANES
    xb = _rmsnorm_rows(x_ref[...], g_ref[...]).astype(BF16)
    z = _dot(xb, w_ref[...])
    zq, zk = z[:, :da], z[:, da:2 * da]
    bd = bd_ref[...]
    msq = _dot((zq * zq).astype(BF16), bd)
    msk = _dot((zk * zk).astype(BF16), bd)
    qn = zq * lax.rsqrt(msq + EPS) * qg_ref[...]
    kn = zk * lax.rsqrt(msk + EPS) * kg_ref[...]
    u_ref[...] = z[:, 3 * da:]

    a = _dot(xb, wf_ref[...]) + bf_ref[...]
    logf = jnp.minimum(a, 0.0) - jnp.log(1.0 + jnp.exp(-jnp.abs(a)))
    cs = _dot(ltri_ref[...], jnp.concatenate(_split3(logf), axis=1))
    cum = cs[:, :LANES] + cs[:, LANES:2 * LANES] + cs[:, 2 * LANES:] + carry_ref[0:1, :]
    carry_ref[...] = jnp.broadcast_to(cum[tm - 1:tm, :], carry_ref.shape)
    edge_row = lax.broadcasted_iota(I32, fedge_ref.shape, 0)
    fedge_ref[...] = jnp.where(edge_row == 0, cum[0:1, :], jnp.where(edge_row == 1, cum[tm - 1:tm, :], 0.0))
    copy = lax.broadcasted_iota(I32, cum.shape, 1) % FORGET_COPIES_STRIDE
    bias = jnp.where(copy < 3, (cum - shift_ref[...]) * LOG2E, cum * (-LOG2E))
    hi, mid, lo = [p.astype(F32) for p in _split3(bias)]
    piece = jnp.where(copy % 3 == 0, hi, jnp.where(copy % 3 == 1, mid, lo)).astype(BF16)
    extras = _dot(piece, place_ref[...])
    q_ref[...] = (_expand_heads(qn) + extras[:, :hw] + ones_ref[0:1, :]).astype(BF16)
    k_ref[...] = (_expand_heads(kn) + extras[:, hw:] + ones_ref[1:2, :]).astype(BF16)
    v_ref[...] = (_expand_heads(z[:, 2 * da:3 * da]) + ones_ref[2:3, :]).astype(BF16)


def _proj(x2, norm_g, w_main, wf, b_f, qg, kg, bd, ltri, place, ones, shift, *, seq):
    t, d = x2.shape
    tm = TM_PROJ
    hw = N_HEADS * LANES
    du = w_main.shape[1] - 3 * N_HEADS * HEAD_DIM
    tiles_per_seq = seq // tm
    full = lambda a: pl.BlockSpec(a.shape, lambda i: (0,) * a.ndim)
    row = lambda width: pl.BlockSpec((tm, width), lambda i: (i, 0))
    consts = (norm_g, w_main, wf, b_f, qg, kg, bd, ltri, place, ones, shift)
    return pl.pallas_call(
        functools.partial(_proj_kernel, tiles_per_seq=tiles_per_seq),
        grid=(t // tm,),
        in_specs=[row(d)] + [full(a) for a in consts],
        out_specs=[row(hw), row(hw), row(hw), row(du), pl.BlockSpec((None, SUBLANES, LANES), lambda i: (i, 0, 0))],
        out_shape=[jax.ShapeDtypeStruct((t, hw), BF16)] * 3
        + [jax.ShapeDtypeStruct((t, du), F32), jax.ShapeDtypeStruct((t // tm, SUBLANES, LANES), F32)],
        scratch_shapes=[pltpu.VMEM((SUBLANES, LANES), F32)],
        compiler_params=pltpu.CompilerParams(dimension_semantics=("arbitrary",), vmem_limit_bytes=VMEM_LIMIT),
        name="proj",
    )(x2, *consts)


def _attn_kernel(first_ref, q_ref, k_ref, v_ref, o_ref, acc_scr, m_scr, *, ts, safe):
    i = pl.program_id(2)
    nsub = q_ref.shape[0] // ts
    acc_scr[...] = jnp.zeros(acc_scr.shape, F32)
    if safe:
        m_scr[...] = jnp.full(m_scr.shape, NEG_INF, F32)

    def block(hd, sub, j, masked):
        off = pl.multiple_of(j * ts, ts)
        lanes = slice(hd * LANES, (hd + 1) * LANES)
        chain = hd * nsub + sub
        s = _nt_dot(q_ref[sub * ts:(sub + 1) * ts, lanes], k_ref[pl.ds(off, ts), lanes])
        if masked:
            rows = lax.broadcasted_iota(I32, (ts, ts), 0)
            cols = lax.broadcasted_iota(I32, (ts, ts), 1)
            s = jnp.where(cols <= rows, s, NEG_INF)
        vblk = v_ref[pl.ds(off, ts), lanes]
        if safe:
            m_prev = m_scr[chain]
            m_new = jnp.maximum(m_prev, jnp.max(s, axis=1, keepdims=True))
            p = jnp.exp2(s - _lane_tile(m_new, ts // LANES)).astype(BF16)
            acc_scr[chain] = jnp.exp2(m_prev - m_new) * acc_scr[chain] + _dot(p, vblk)
            m_scr[chain] = m_new
        else:
            acc_scr[chain] += _dot(jnp.exp2(s).astype(BF16), vblk)

    def body(j, carry):
        for hd in range(HEADS_PER_STEP):
            for sub in range(nsub):
                block(hd, sub, j, False)
        return carry

    step = (pl.program_id(0) * pl.num_programs(1) + pl.program_id(1)) * pl.num_programs(2) + i
    lax.fori_loop(first_ref[step], nsub * i, body, 0)
    for jj in range(nsub):
        for hd in range(HEADS_PER_STEP):
            for sub in range(jj, nsub):
                block(hd, sub, nsub * i + jj, sub == jj)
    lane = lax.broadcasted_iota(I32, (ts, LANES), 1)
    for sub in range(nsub):
        outs = []
        for hd in range(HEADS_PER_STEP):
            acc = acc_scr[hd * nsub + sub]
            outs.append(acc / acc[:, HEAD_DIM:HEAD_DIM + 1])
        both = jnp.where(lane < HEAD_DIM, outs[0], pltpu.roll(outs[1], HEAD_DIM, axis=1))
        o_ref[sub * ts:(sub + 1) * ts, :] = both.astype(o_ref.dtype)


def _first_kv_block(fedge, b, s, tq, ts):
    nblk = s // ts
    f_first = fedge[:, 0, ::FORGET_COPIES_STRIDE].reshape(b, nblk, N_HEADS).transpose(0, 2, 1)
    f_last = fedge[:, 1, ::FORGET_COPIES_STRIDE].reshape(b, nblk, N_HEADS).transpose(0, 2, 1)
    tile_start = f_first[:, :, ::tq // ts]
    bound = LOG2E * (tile_start[:, :, :, None] - f_last[:, :, None, :])
    first = jnp.sum(bound < -UNDERFLOW_LOG2, axis=-1).astype(I32)
    first = jnp.min(first.reshape(b, N_HEADS // HEADS_PER_STEP, HEADS_PER_STEP, -1), axis=2)
    return first.reshape(-1)


def _attention(q, k, v, fedge, plain_ok):
    b, s, hw = q.shape
    tq, ts = TQ_ATTN, TS_ATTN
    nq = s // tq
    steps = N_HEADS // HEADS_PER_STEP
    width = HEADS_PER_STEP * LANES
    chains = HEADS_PER_STEP * tq // ts

    def call(safe):
        first = jnp.zeros((b * steps * nq,), I32) if safe else _first_kv_block(fedge, b, s, tq, ts)
        return pl.pallas_call(
            functools.partial(_attn_kernel, ts=ts, safe=safe),
            grid_spec=pltpu.PrefetchScalarGridSpec(
                num_scalar_prefetch=1,
                grid=(b, steps, nq),
                in_specs=[pl.BlockSpec((None, tq, width), lambda bi, h, i, f: (bi, i, h)),
                          pl.BlockSpec((None, s, width), lambda bi, h, i, f: (bi, 0, h)),
                          pl.BlockSpec((None, s, width), lambda bi, h, i, f: (bi, 0, h))],
                out_specs=pl.BlockSpec((None, tq, LANES), lambda bi, h, i, f: (bi, i, h)),
                scratch_shapes=[pltpu.VMEM((chains, ts, LANES), F32)] * 2),
            out_shape=jax.ShapeDtypeStruct((b, s, steps * LANES), BF16),
            compiler_params=pltpu.CompilerParams(
                dimension_semantics=("arbitrary", "arbitrary", "arbitrary"), vmem_limit_bytes=VMEM_LIMIT),
            name="attn_safe" if safe else "attn",
        )(first, q, k, v)

    return lax.cond(plain_ok, lambda: call(False), lambda: call(True))


def _window_select(pieces, first_group):
    window = lax.broadcasted_iota(I32, pieces[0].shape, 1) // SSM_GROUP
    out = pieces[0]
    for m in range(1, GROUPS_PER_BLOCK):
        out = jnp.where(window == (first_group + m) % GROUPS_PER_BLOCK, pieces[m], out)
    return out


def _s5_kernel(u_ref, win_ref, tw_ref, coef_ref, y_ref, ub_scr, yb_scr, *, nk):
    nj = CHUNKS_PER_SUPER
    gpb = GROUPS_PER_BLOCK
    w = CHUNK * SSM_GROUP
    half = w // 2
    nkk = nk // SUBLANES
    row_stride = SUPER

    def gather_body(it, carry):
        j, kk = it // nkk, it % nkk
        for sp in range(CHUNK // gpb):
            rot = []
            for m in range(gpb):
                row0 = j * CHUNK + sp * gpb + m + kk * (SUBLANES * row_stride)
                src = u_ref[pl.ds(row0, SUBLANES, stride=row_stride), :]
                rot.append(src if m == 0 else pltpu.roll(src, m * SSM_GROUP, axis=1))
            for gl in range(gpb):
                ub_scr[gl, j, pl.ds(kk * SUBLANES, SUBLANES), sp * LANES:(sp + 1) * LANES] = _window_select(rot, gl)
        return carry

    lax.fori_loop(0, nj * nkk, gather_body, 0, unroll=RELAYOUT_UNROLL)

    def swap(val):
        return jnp.concatenate([val[:, half:], val[:, :half]], axis=1)

    def group_body(gl, carry):
        def cmul(val, idx):
            return coef_ref[gl, idx:idx + 1, :] * val + coef_ref[gl, idx + 1:idx + 2, :] * swap(val)

        u = ub_scr[gl].reshape(nj * nk, w).astype(BF16)
        s1 = _dot(u, win_ref[gl]).reshape(nj, nk, w)
        e = jnp.zeros((nk, w), F32)
        local = []
        for j in range(nj):
            local.append(e)
            e = cmul(e, 0) + s1[j]
        kidx = lax.broadcasted_iota(I32, (nk, w), 0)
        x = e
        step, d = 0, 1
        while d < nk:
            shifted = jnp.where(kidx >= d, pltpu.roll(x, d, axis=0), 0.0)
            x = x + cmul(shifted, 2 + 2 * nj + 2 * step)
            step, d = step + 1, d * 2
        x_start = jnp.where(kidx >= 1, pltpu.roll(x, 1, axis=0), 0.0)
        starts = [local[j] + cmul(x_start, 2 + 2 * j) for j in range(nj)]
        p = jnp.concatenate([st[:, :half] for st in starts], axis=0)
        hi = p.astype(BF16)
        lo = (p - hi.astype(F32)).astype(BF16)
        y = _dot(jnp.concatenate([u, hi, lo], axis=1), tw_ref[gl])
        yb_scr[gl] = y.reshape(nj, nk, w)
        return carry

    lax.fori_loop(0, gpb, group_body, 0)

    def scatter_body(it, carry):
        j, kk = it // nkk, it % nkk
        for tp in range(CHUNK // gpb):
            src = [yb_scr[gl, j, pl.ds(kk * SUBLANES, SUBLANES), tp * LANES:(tp + 1) * LANES] for gl in range(gpb)]
            for m in range(gpb):
                by_window = [src[(wi - m) % gpb] for wi in range(gpb)]
                window = lax.broadcasted_iota(I32, by_window[0].shape, 1) // SSM_GROUP
                merged = by_window[0]
                for wi in range(1, gpb):
                    merged = jnp.where(window == wi, by_window[wi], merged)
                nat = merged if m == 0 else pltpu.roll(merged, LANES - m * SSM_GROUP, axis=1)
                row0 = j * CHUNK + tp * gpb + m + kk * (SUBLANES * row_stride)
                y_ref[pl.ds(row0, SUBLANES, stride=row_stride), :] = nat
        return carry

    lax.fori_loop(0, nj * nkk, scatter_body, 0, unroll=RELAYOUT_UNROLL)


def _s5_params(a_re, a_im, log_dt, b_re, b_im, c_re, c_im, nk):
    args = (a_re, a_im, log_dt, b_re, b_im, c_re, c_im)
    slowest = jnp.min(a_re * jnp.exp(log_dt)[:, None])
    return lax.cond(slowest > MIN_FACTORED_DECAY, functools.partial(_s5_params_factored, nk=nk),
                    functools.partial(_s5_params_direct, nk=nk), *args)


def _s5_params_factored(a_re, a_im, log_dt, b_re, b_im, c_re, c_im, *, nk):
    g, p = a_re.shape
    c = SSM_GROUP
    gpb = GROUPS_PER_BLOCK
    dt = jnp.exp(log_dt)[:, None]
    adt_r, adt_i = a_re * dt, a_im * dt

    def lam_pow(n):
        nf = jnp.asarray(n, F32)
        nf = (nf[None] if nf.ndim == 1 else nf)[:, :, None]
        mag = jnp.exp(adt_r[:, None, :] * nf)
        ang = adt_i[:, None, :] * nf
        return mag * jnp.cos(ang), mag * jnp.sin(ang)

    gi = jnp.arange(g)[:, None, None] % gpb
    step = ((jnp.arange(CHUNK // gpb)[None, :, None]) * gpb
            + (jnp.arange(gpb)[None, None, :] - gi) % gpb).reshape(g, CHUNK)

    l1r, l1i = lam_pow(jnp.ones((1,)))
    den = a_re * a_re + a_im * a_im
    nr, ni = l1r[:, 0] - 1.0, l1i[:, 0]
    qr = (nr * a_re + ni * a_im) / den
    qi = (ni * a_re - nr * a_im) / den
    bcr = (qr[..., None] * b_re - qi[..., None] * b_im).transpose(0, 2, 1)
    bci = (qr[..., None] * b_im + qi[..., None] * b_re).transpose(0, 2, 1)

    def times_b(pr, pi):
        pr, pi = pr[:, :, None, :], pi[:, :, None, :]
        return pr * bcr[:, None] - pi * bci[:, None], pr * bci[:, None] + pi * bcr[:, None]

    def times_c(pr, pi):
        pr, pi = pr[:, :, None, :], pi[:, :, None, :]
        return c_re[:, None] * pr - c_im[:, None] * pi, c_re[:, None] * pi + c_im[:, None] * pr

    rows = lambda parts: jnp.concatenate(parts, axis=-1).reshape(g, CHUNK * c, -1)
    ir, ii = times_b(*lam_pow(CHUNK - 1 - step))
    win = rows([ir, ii, ii, ir])
    clr, cli = times_c(*lam_pow(step + 1))
    wout = rows([clr, -cli]).transpose(0, 2, 1)
    ar, ai = times_b(*lam_pow(-step))
    br, bi = times_c(*lam_pow(step))
    toep = jnp.einsum('gnp,gmp->gnm', rows([ar, ai]), rows([br, -bi]), precision=lax.Precision.HIGHEST)
    step_of_lane = jnp.repeat(step, c, axis=1)
    toep = jnp.where(step_of_lane[:, None, :] >= step_of_lane[:, :, None], toep, 0.0)
    tw = jnp.concatenate([toep, wout, wout], axis=1)
    return win.astype(BF16), tw.astype(BF16), _s5_scan_coefficients(lam_pow, g, p, nk)


def _s5_scan_coefficients(lam_pow, g, p, nk):
    n_steps = max(nk.bit_length() - 1, 0)
    powers = [CHUNK] + [CHUNK * j for j in range(CHUNKS_PER_SUPER)] + [SUPER * (1 << i) for i in range(n_steps)]
    ar, ai = lam_pow(jnp.array(powers, dtype=jnp.int32))
    c1 = jnp.concatenate([ar, ar, ar, ar], axis=2)
    c2 = jnp.concatenate([-ai, ai, ai, -ai], axis=2)
    coef = jnp.stack([c1, c2], axis=2).reshape(g, 2 * len(powers), 4 * p)
    return jnp.pad(coef, ((0, 0), (0, (-coef.shape[1]) % SUBLANES), (0, 0)))


def _s5_params_direct(a_re, a_im, log_dt, b_re, b_im, c_re, c_im, *, nk):
    g, p = a_re.shape
    c = SSM_GROUP
    hp = lax.Precision.HIGHEST
    dt = jnp.exp(log_dt)[:, None]
    adt_r, adt_i = a_re * dt, a_im * dt

    def lam_pow(n):
        nf = jnp.asarray(n, F32)[None, :, None]
        mag = jnp.exp(adt_r[:, None, :] * nf)
        ang = adt_i[:, None, :] * nf
        return mag * jnp.cos(ang), mag * jnp.sin(ang)

    lr, li = lam_pow(jnp.arange(CHUNK + 1))
    den = a_re * a_re + a_im * a_im
    nr, ni = lr[:, 1] - 1.0, li[:, 1]
    qr = (nr * a_re + ni * a_im) / den
    qi = (ni * a_re - nr * a_im) / den
    bbr = qr[..., None] * b_re - qi[..., None] * b_im
    bbi = qr[..., None] * b_im + qi[..., None] * b_re

    mr = lr[:, :CHUNK, :, None] * bbr[:, None] - li[:, :CHUNK, :, None] * bbi[:, None]
    mi = lr[:, :CHUNK, :, None] * bbi[:, None] + li[:, :CHUNK, :, None] * bbr[:, None]
    kern = (jnp.einsum('gcp,gtpd->gtcd', c_re, mr, precision=hp)
            - jnp.einsum('gcp,gtpd->gtcd', c_im, mi, precision=hp))
    toep = jnp.stack([jnp.pad(kern[:, :CHUNK - s], ((0, 0), (s, 0), (0, 0), (0, 0))) for s in range(CHUNK)],
                     axis=1)
    toep = toep.transpose(0, 1, 4, 2, 3).reshape(g, CHUNK * c, CHUNK * c)
    clr = c_re[:, None] * lr[:, 1:, None, :] - c_im[:, None] * li[:, 1:, None, :]
    cli = c_re[:, None] * li[:, 1:, None, :] + c_im[:, None] * lr[:, 1:, None, :]
    w_re = clr.transpose(0, 3, 1, 2).reshape(g, p, CHUNK * c)
    w_im = (-cli).transpose(0, 3, 1, 2).reshape(g, p, CHUNK * c)
    wout = jnp.concatenate([w_re, w_im], axis=1)
    pr, pi = lr[:, :CHUNK][:, ::-1], li[:, :CHUNK][:, ::-1]
    ir = pr[..., None] * bbr[:, None] - pi[..., None] * bbi[:, None]
    ii = pr[..., None] * bbi[:, None] + pi[..., None] * bbr[:, None]
    ir = ir.transpose(0, 1, 3, 2).reshape(g, CHUNK * c, p)
    ii = ii.transpose(0, 1, 3, 2).reshape(g, CHUNK * c, p)
    win = jnp.concatenate([ir, ii, ii, ir], axis=2)

    gi = jnp.arange(g)[:, None, None] % GROUPS_PER_BLOCK
    si = jnp.arange(CHUNK)[None, :, None]
    ci = jnp.arange(c)[None, None, :]
    lane_of = ((si // GROUPS_PER_BLOCK) * LANES + ((gi + si) % GROUPS_PER_BLOCK) * c + ci).reshape(g, CHUNK * c)
    perm = (lane_of[:, None, :] == jnp.arange(CHUNK * c)[None, :, None]).astype(BF16)
    win = jnp.einsum('gln,gnk->glk', perm, win.astype(BF16))
    toep = jnp.einsum('gln,gnk->glk', perm, toep.astype(BF16))
    tw = jnp.concatenate([toep, wout.astype(BF16), wout.astype(BF16)], axis=1)
    tw = jnp.einsum('grn,gln->grl', tw, perm)
    return win.astype(BF16), tw.astype(BF16), _s5_scan_coefficients(lam_pow, g, p, nk)


def _s5(u, win, tw, coef):
    b, s, dssm = u.shape
    nk = s // SUPER
    nj = CHUNKS_PER_SUPER
    w = CHUNK * SSM_GROUP
    gpb = GROUPS_PER_BLOCK
    nblk = dssm // LANES
    wspec = lambda a: pl.BlockSpec((gpb,) + a.shape[1:], lambda bi, li: (li, 0, 0))
    return pl.pallas_call(
        functools.partial(_s5_kernel, nk=nk),
        grid=(b, nblk),
        in_specs=[pl.BlockSpec((None, s, LANES), lambda bi, li: (bi, 0, li)), wspec(win), wspec(tw), wspec(coef)],
        out_specs=pl.BlockSpec((None, s, LANES), lambda bi, li: (bi, 0, li)),
        out_shape=jax.ShapeDtypeStruct((b, s, dssm), F32),
        scratch_shapes=[pltpu.VMEM((gpb, nj, nk, w), F32)] * 2,
        compiler_params=pltpu.CompilerParams(dimension_semantics=("arbitrary", "arbitrary"),
                                             vmem_limit_bytes=VMEM_LIMIT),
        name="s5",
    )(u, win, tw, coef)


def _route(logits):
    e_all = logits[0:N_EXPERTS]
    gl = logits[N_EXPERTS:N_EXPERTS + SUBLANES]
    tm = logits.shape[1]
    ridx = lax.broadcasted_iota(I32, (SUBLANES, tm), 0)
    ge = jnp.exp(gl - jnp.max(gl, axis=0, keepdims=True))
    gp = ge / jnp.sum(ge, axis=0, keepdims=True)
    g_top = jnp.max(gp, axis=0, keepdims=True)
    g_sel = jnp.min(jnp.where(gp == g_top, ridx, SUBLANES), axis=0, keepdims=True)
    e_in = e_all[(N_EXPERT_GROUPS - 1) * SUBLANES:]
    for gi in range(N_EXPERT_GROUPS - 2, -1, -1):
        e_in = jnp.where(g_sel == gi, e_all[gi * SUBLANES:(gi + 1) * SUBLANES], e_in)
    ee = jnp.exp(e_in - jnp.max(e_in, axis=0, keepdims=True))
    ep = ee / jnp.sum(ee, axis=0, keepdims=True)
    v1 = jnp.max(ep, axis=0, keepdims=True)
    i1 = jnp.min(jnp.where(ep == v1, ridx, SUBLANES), axis=0, keepdims=True)
    ep2 = jnp.where(ridx == i1, -1.0, ep)
    v2 = jnp.max(ep2, axis=0, keepdims=True)
    i2 = jnp.min(jnp.where(ep2 == v2, ridx, SUBLANES), axis=0, keepdims=True)
    den = v1 + v2
    w1 = g_top * v1 / den
    w2 = g_top * v2 / den
    e1 = g_sel * EXPERTS_PER_GROUP + i1
    e2 = g_sel * EXPERTS_PER_GROUP + i2
    ids = jnp.where(ridx == 0, e1, jnp.where(ridx == 1, e2, 0))
    wts = jnp.where(ridx == 0, w1, jnp.where(ridx == 1, w2, 0.0))
    return ids, wts


def _mix_kernel(x_ref, ya_ref, ys_ref, u_ref, g1_ref, wgate_ref, dsk_ref, wglu_ref, wpa_ref, wps_ref,
                wout_ref, g2_ref, wrh_ref, wrl_ref, br_ref, x1_ref, xp_ref, logits_ref):
    d = x_ref.shape[1]
    rows = x_ref.shape[0] // MIX_CHAINS
    for chain in range(MIX_CHAINS):
        r = slice(chain * rows, (chain + 1) * rows)
        x = x_ref[r, :]
        xb = _rmsnorm_rows(x, g1_ref[...]).astype(BF16)
        gates = _dot(xb, wgate_ref[...])
        y = ys_ref[r, :].astype(F32) + dsk_ref[...] * u_ref[r, :].astype(F32)
        y = y * (0.5 * (1.0 + jnp.tanh(math.sqrt(2.0 / math.pi) * (y + 0.044715 * (y * y * y)))))
        y = y * _sigmoid(_dot(y.astype(BF16), wglu_ref[...]))
        mixed = (_sigmoid(gates[:, :d]) * _dot(ya_ref[r, :], wpa_ref[...])
                 + _sigmoid(gates[:, d:]) * _dot(y.astype(BF16), wps_ref[...]))
        x1 = x + _dot(mixed.astype(BF16), wout_ref[...])
        x1_ref[r, :] = x1
        xn = _rmsnorm_rows(x1, g2_ref[...])
        xp_ref[r, :] = pltpu.pack_elementwise([xn[:, :d // 2], xn[:, d // 2:]], packed_dtype=BF16)
        xh = xn.astype(BF16)
        xl = (xn - xh.astype(F32)).astype(BF16)
        wrh = wrh_ref[...]
        logits_ref[:, r] = _nt_dot(wrh, xh) + _nt_dot(wrl_ref[...], xh) + _nt_dot(wrh, xl) + br_ref[...]


def _mix(x2, ya, ys, u, g1, wgate, dsk, wglu, wpa, wps, wout, g2, wrh, wrl, br):
    t, d = x2.shape
    tm = TM_MIX
    full = lambda a: pl.BlockSpec(a.shape, lambda i: (0,) * a.ndim)
    row = lambda width: pl.BlockSpec((tm, width), lambda i: (i, 0))
    return pl.pallas_call(
        _mix_kernel,
        grid=(t // tm,),
        in_specs=[row(d), row(ya.shape[1]), row(ys.shape[1]), row(u.shape[1]), full(g1), full(wgate), full(dsk),
                  full(wglu), full(wpa),
                  full(wps), full(wout), full(g2), full(wrh), full(wrl), full(br)],
        out_specs=[row(d), row(d // 2), pl.BlockSpec((ROUTER_ROWS, tm), lambda i: (0, i))],
        out_shape=[jax.ShapeDtypeStruct((t, d), F32), jax.ShapeDtypeStruct((t, d // 2), jnp.uint32),
                   jax.ShapeDtypeStruct((ROUTER_ROWS, t), F32)],
        compiler_params=pltpu.CompilerParams(dimension_semantics=("arbitrary",), vmem_limit_bytes=VMEM_LIMIT),
        name="mix",
    )(x2, ya, ys, u, g1, wgate, dsk, wglu, wpa, wps, wout, g2, wrh, wrl, br)


def _route_kernel(logits_ref, ids_ref, wts_ref):
    ids, wts = _route(logits_ref[...])
    ids_ref[...] = ids
    wts_ref[...] = wts


def _route_call(logits):
    t = logits.shape[1]
    tl = ROUTE_LANES
    col = pl.BlockSpec((SUBLANES, tl), lambda i: (0, i))
    return pl.pallas_call(
        _route_kernel,
        grid=(t // tl,),
        in_specs=[pl.BlockSpec((ROUTER_ROWS, tl), lambda i: (0, i))],
        out_specs=[col, col],
        out_shape=[jax.ShapeDtypeStruct((SUBLANES, t), I32), jax.ShapeDtypeStruct((SUBLANES, t), F32)],
        compiler_params=pltpu.CompilerParams(dimension_semantics=("arbitrary",)),
        name="route",
    )(logits)


def _meta_kernel(ids_ref, tri_ref, dest_ref, tile_ref, *, tile_rows):
    nk, nc, c = ids_ref.shape
    ne = N_EXPERTS
    erow = lax.broadcasted_iota(I32, (ne, c), 0)
    ones = jnp.ones((c, LANES), BF16)

    def onehot(k, ci):
        mask = erow == ids_ref[k, pl.ds(ci, 1), :]
        return mask, jnp.where(mask, 1.0, 0.0).astype(BF16)

    def count_body(n, acc):
        return acc + _dot(onehot(n // nc, n % nc)[1], ones)

    cnt = lax.fori_loop(0, nk * nc, count_body, jnp.zeros((ne, LANES), F32))
    ntiles = jnp.floor((cnt + (tile_rows - 1)) * (1.0 / tile_rows))
    lower = jnp.where(lax.broadcasted_iota(I32, (ne, ne), 1) < lax.broadcasted_iota(I32, (ne, ne), 0), 1.0, 0.0)
    start_tiles = _dot(lower.astype(BF16), ntiles.astype(BF16))
    base = start_tiles * tile_rows

    tri = tri_ref[...]

    def dest_body(n, carry):
        k, ci = n // nc, n % nc
        mask, oh = onehot(k, ci)
        prefix = _dot(oh, tri)
        slot = _lane_tile(base + carry, c // LANES) + prefix - 1.0
        dest = jnp.sum(jnp.where(mask, slot, 0.0), axis=0, keepdims=True)
        dest_ref[k, pl.ds(ci, 1), :] = dest.astype(I32)
        return carry + _dot(oh, ones)

    lax.fori_loop(0, nk * nc, dest_body, jnp.zeros((ne, LANES), F32))

    nt_lanes = tile_ref.shape[1]
    end_tiles = _lane_tile(start_tiles + ntiles, nt_lanes // LANES)
    tidx = lax.broadcasted_iota(I32, (ne, nt_lanes), 1).astype(F32)
    texp = jnp.sum(jnp.where(tidx >= end_tiles, 1.0, 0.0), axis=0, keepdims=True)
    valid = jnp.where(texp < ne, 1, 0)
    texp = jnp.minimum(texp, ne - 1.0).astype(I32)
    ridx = lax.broadcasted_iota(I32, tile_ref.shape, 0)
    tile_ref[...] = jnp.where(ridx == 0, texp, jnp.where(ridx == 1, valid, 0))


def _meta(ids2, tri, n_tiles):
    nk, t = ids2.shape
    c = SORT_CHUNK
    nt_lanes = pl.cdiv(n_tiles, LANES) * LANES
    ids3 = ids2.reshape(nk, t // c, c)
    dest, tile = pl.pallas_call(
        functools.partial(_meta_kernel, tile_rows=TM_EXPERT),
        out_shape=[jax.ShapeDtypeStruct(ids3.shape, I32), jax.ShapeDtypeStruct((SUBLANES, nt_lanes), I32)],
        compiler_params=pltpu.CompilerParams(vmem_limit_bytes=VMEM_LIMIT),
        name="meta",
    )(ids3, tri)
    return dest.reshape(nk * t), tile[0, :n_tiles], tile[1, :n_tiles]


def _row_copy(src_ref, src_row, dst_ref, dst_row, sem):
    return pltpu.make_async_copy(src_ref.at[pl.ds(src_row, 1)], dst_ref.at[pl.ds(dst_row, 1)], sem)


def _dispatch_kernel(dest_ref, xp_ref, xs_in_ref, xs_ref, sem, *, n_tokens):
    del xs_in_ref
    tm = xp_ref.shape[0]
    base = pl.program_id(0) * tm

    def body(r, carry):
        for k in range(2):
            _row_copy(xp_ref, r, xs_ref, dest_ref[k * n_tokens + base + r], sem).start()
        return carry

    lax.fori_loop(0, tm, body, 0, unroll=ROW_DMA_UNROLL)
    for k in range(2):
        pltpu.make_async_copy(xp_ref, xs_ref.at[pl.ds(0, tm)], sem).wait()


def _zero_fill_kernel(o_ref):
    o_ref[...] = jnp.zeros(o_ref.shape, o_ref.dtype)


def _zeros(n_rows, width, dtype):
    rows = n_rows // pl.cdiv(n_rows, ZERO_FILL_ROWS)
    assert n_rows % rows == 0 and rows % SUBLANES == 0
    return pl.pallas_call(
        _zero_fill_kernel,
        grid=(n_rows // rows,),
        out_specs=pl.BlockSpec((rows, width), lambda i: (i, 0)),
        out_shape=jax.ShapeDtypeStruct((n_rows, width), dtype),
        compiler_params=pltpu.CompilerParams(dimension_semantics=("arbitrary",)),
        name="zero_fill",
    )()


def _dispatch(dest, xp, n_slots):
    t, w = xp.shape
    tm = TM_DISPATCH
    xs0 = _zeros(n_slots, w, xp.dtype)
    return pl.pallas_call(
        functools.partial(_dispatch_kernel, n_tokens=t),
        grid_spec=pltpu.PrefetchScalarGridSpec(
            num_scalar_prefetch=1,
            grid=(t // tm,),
            in_specs=[pl.BlockSpec((tm, w), lambda i, d: (i, 0)), pl.BlockSpec(memory_space=pl.ANY)],
            out_specs=pl.BlockSpec(memory_space=pl.ANY),
            scratch_shapes=[pltpu.SemaphoreType.DMA(())]),
        out_shape=jax.ShapeDtypeStruct(xs0.shape, xp.dtype),
        input_output_aliases={2: 0},
        compiler_params=pltpu.CompilerParams(dimension_semantics=("arbitrary",), has_side_effects=True),
        name="dispatch",
    )(dest, xp, xs0)


def _unpack_rows(packed):
    lo = pltpu.unpack_elementwise(packed, index=0, packed_dtype=BF16, unpacked_dtype=F32)
    hi = pltpu.unpack_elementwise(packed, index=1, packed_dtype=BF16, unpacked_dtype=F32)
    return jnp.concatenate([lo, hi], axis=1)


def _expert_kernel(te_ref, tv_ref, nv_ref, xs_ref, wg_ref, wu_ref, wd_ref, ys_ref, wg_bf, wu_bf, wd_bf):
    del nv_ref
    i = pl.program_id(0)
    half = xs_ref.shape[1]

    @pl.when((i == 0) | (te_ref[i] != te_ref[jnp.maximum(i - 1, 0)]))
    def _():
        wg_bf[...] = wg_ref[...].astype(BF16)
        wu_bf[...] = wu_ref[...].astype(BF16)
        wd_bf[...] = wd_ref[...].astype(BF16)

    @pl.when(tv_ref[i] > 0)
    def _():
        x = _unpack_rows(xs_ref[...]).astype(BF16)
        hg = _dot(x, wg_bf[...])
        hu = _dot(x, wu_bf[...])
        h = (hg * _sigmoid(hg) * hu).astype(BF16)
        y = _dot(h, wd_bf[...])
        ys_ref[...] = pltpu.pack_elementwise([y[:, :half], y[:, half:]], packed_dtype=BF16)

    @pl.when(tv_ref[i] == 0)
    def _():
        zero = jnp.zeros(ys_ref.shape, F32)
        ys_ref[...] = pltpu.pack_elementwise([zero, zero], packed_dtype=BF16)


def _experts(tile_expert, tile_valid, xs, wg, wu, wd):
    n_slots, w = xs.shape
    tm = TM_EXPERT
    n_tiles = n_slots // tm
    n_valid = jnp.sum(tile_valid).astype(I32).reshape(1)
    wspec = lambda a: pl.BlockSpec((None,) + a.shape[1:], lambda i, te, tv, nv: (te[i], 0, 0))
    rows_in = pl.BlockSpec((tm, w), lambda i, te, tv, nv: (jnp.minimum(i, nv[0] - 1), 0))
    rows_out = pl.BlockSpec((tm, w), lambda i, te, tv, nv: (i, 0))
    return pl.pallas_call(
        _expert_kernel,
        grid_spec=pltpu.PrefetchScalarGridSpec(
            num_scalar_prefetch=3,
            grid=(n_tiles,),
            in_specs=[rows_in, wspec(wg), wspec(wu), wspec(wd)],
            out_specs=rows_out,
            scratch_shapes=[pltpu.VMEM(a.shape[1:], BF16) for a in (wg, wu, wd)]),
        out_shape=jax.ShapeDtypeStruct(xs.shape, xs.dtype),
        compiler_params=pltpu.CompilerParams(dimension_semantics=("arbitrary",), vmem_limit_bytes=VMEM_LIMIT),
        name="experts",
    )(tile_expert, tile_valid, n_valid, xs, wg, wu, wd)


def _combine_kernel(dest_ref, x1_ref, w0_ref, w1_ref, ys_ref, out_ref, buf, sem, *, n_tokens):
    tm, d = x1_ref.shape
    base = pl.program_id(0) * tm

    def body(r, carry):
        for k in range(2):
            _row_copy(ys_ref, dest_ref[k * n_tokens + base + r], buf.at[k], r, sem).start()
        return carry

    lax.fori_loop(0, tm, body, 0, unroll=ROW_DMA_UNROLL)
    for k in range(2):
        pltpu.make_async_copy(ys_ref.at[pl.ds(0, tm)], buf.at[k], sem).wait()
    reps = d // LANES
    out_ref[...] = (x1_ref[...]
                    + _lane_tile(w0_ref[...], reps) * _unpack_rows(buf[0])
                    + _lane_tile(w1_ref[...], reps) * _unpack_rows(buf[1]))


def _combine(dest, x1, w0, w1, ys):
    t, d = x1.shape
    tm = TM_COMBINE
    row = lambda width: pl.BlockSpec((tm, width), lambda i, dref: (i, 0))
    return pl.pallas_call(
        functools.partial(_combine_kernel, n_tokens=t),
        grid_spec=pltpu.PrefetchScalarGridSpec(
            num_scalar_prefetch=1,
            grid=(t // tm,),
            in_specs=[row(d), row(LANES), row(LANES), pl.BlockSpec(memory_space=pl.ANY)],
            out_specs=row(d),
            scratch_shapes=[pltpu.VMEM((2, tm, ys.shape[1]), ys.dtype), pltpu.SemaphoreType.DMA(())]),
        out_shape=jax.ShapeDtypeStruct((t, d), x1.dtype),
        compiler_params=pltpu.CompilerParams(dimension_semantics=("arbitrary",), vmem_limit_bytes=VMEM_LIMIT),
        name="combine",
    )(dest, x1, w0, w1, ys)


def _lower_tri(n):
    return jnp.where(jnp.arange(n)[:, None] >= jnp.arange(n)[None, :], 1.0, 0.0).astype(BF16)


def _upper_tri(n):
    return jnp.where(jnp.arange(n)[:, None] <= jnp.arange(n)[None, :], 1.0, 0.0).astype(BF16)


def _split_bf16(w):
    hi = w.astype(BF16)
    lo = (w - hi.astype(F32)).astype(BF16)
    return hi, lo


def _layer(x, norm_mix_g, w_in, b_forget, q_norm_g, k_norm_g, ssm_A_re, ssm_A_im, ssm_log_dt, ssm_B_re, ssm_B_im,
           ssm_C_re, ssm_C_im, ssm_D, w_glu, w_proj_attn, w_proj_ssm, w_out, norm_ffn_g, w_router_group,
           b_router_group, w_router_expert, b_router_expert, w_expert_gate, w_expert_up, w_expert_down):
    b, s, d = x.shape
    t = b * s
    da = N_HEADS * HEAD_DIM
    dssm = ssm_D.shape[0]
    nk = s // SUPER
    assert s % SUPER == 0 and nk % SUBLANES == 0 and nk & (nk - 1) == 0 and s % TQ_ATTN == 0 and t % TM_PROJ == 0
    assert TM_PROJ == TS_ATTN
    x2 = x.reshape(t, d)

    o_f = 3 * da
    o_u = o_f + N_HEADS
    o_g = o_u + dssm
    w_main = jnp.concatenate([w_in[:, :o_f], w_in[:, o_u:o_g]], axis=1).astype(BF16)
    wf = jnp.repeat(w_in[:, o_f:o_u], FORGET_COPIES_STRIDE, axis=1).astype(BF16)
    bf = jnp.repeat(b_forget, FORGET_COPIES_STRIDE)[None, :]
    w_gates = w_in[:, o_g:].astype(BF16)
    qg = (jnp.tile(q_norm_g, N_HEADS) * (HEAD_DIM ** -0.5 * LOG2E))[None, :]
    kg = jnp.tile(k_norm_g, N_HEADS)[None, :]
    head_of = jnp.arange(da) // HEAD_DIM
    bd = jnp.where(head_of[:, None] == head_of[None, :], 1.0 / HEAD_DIM, 0.0).astype(BF16)
    logit_bound = 1.02 * HEAD_DIM ** 0.5 * jnp.max(jnp.abs(q_norm_g)) * jnp.max(jnp.abs(k_norm_g))
    plain_ok = LOG2E * logit_bound <= PLAIN_SOFTMAX_MAX_LOG2
    shift = jnp.full((1, LANES), logit_bound, F32)
    hw = N_HEADS * LANES
    lane_in_head = jnp.arange(hw) % LANES
    ones = jnp.stack([
        (lane_in_head >= BIAS_K_LANE) & (lane_in_head < BIAS_K_LANE + 3),
        (lane_in_head >= BIAS_Q_LANE) & (lane_in_head < BIAS_Q_LANE + 3),
        lane_in_head == HEAD_DIM]
        + [jnp.zeros((hw,), bool)] * (SUBLANES - 3)).astype(F32)
    prow = jnp.arange(LANES)
    head, copy = prow // FORGET_COPIES_STRIDE, prow % FORGET_COPIES_STRIDE
    target = jnp.where(copy < 3, head * LANES + BIAS_Q_LANE + copy, hw + head * LANES + BIAS_K_LANE + copy - 3)
    place = ((jnp.arange(2 * hw)[None, :] == target[:, None]) & (copy < 6)[:, None]).astype(BF16)

    q, k, v, u, fedge = _proj(x2, norm_mix_g[None, :], w_main, wf, bf, qg, kg, bd, _lower_tri(TM_PROJ), place, ones,
                              shift, seq=s)
    y_attn = _attention(q.reshape(b, s, hw), k.reshape(b, s, hw), v.reshape(b, s, hw), fedge, plain_ok)

    win, tw, coef = _s5_params(ssm_A_re, ssm_A_im, ssm_log_dt, ssm_B_re, ssm_B_im, ssm_C_re, ssm_C_im, s // SUPER)
    y_ssm = _s5(u.reshape(b, s, dssm), win, tw, coef)

    wr = jnp.concatenate([w_router_expert.T, w_router_group.T,
                          jnp.zeros((ROUTER_ROWS - N_EXPERTS - N_EXPERT_GROUPS, d), F32)], axis=0)
    br = jnp.concatenate([b_router_expert, b_router_group,
                          jnp.full((SUBLANES - N_EXPERT_GROUPS,), NEG_INF, F32),
                          jnp.zeros((ROUTER_ROWS - N_EXPERTS - SUBLANES,), F32)])[:, None]
    x1, xp, logits = _mix(x2, y_attn.reshape(t, da), y_ssm.reshape(t, dssm), u, norm_mix_g[None, :], w_gates,
                          ssm_D[None, :], w_glu.astype(BF16), w_proj_attn.astype(BF16), w_proj_ssm.astype(BF16),
                          w_out.astype(BF16), norm_ffn_g[None, :], *_split_bf16(wr), br)
    ids, wts = _route_call(logits)

    n_tiles = (2 * t) // TM_EXPERT + N_EXPERTS
    dest, tile_expert, tile_valid = _meta(ids[:2], _upper_tri(SORT_CHUNK), n_tiles)
    xs = _dispatch(dest, xp, n_tiles * TM_EXPERT)
    ys = _experts(tile_expert, tile_valid, xs, w_expert_gate, w_expert_up, w_expert_down)
    w0 = jnp.broadcast_to(wts[0][:, None], (t, LANES))
    w1 = jnp.broadcast_to(wts[1][:, None], (t, LANES))
    out = _combine(dest, x1, w0, w1, ys)
    return out.reshape(b, s, d)


def kernel(x, norm_mix_g, w_in, b_forget, q_norm_g, k_norm_g, ssm_A_re, ssm_A_im, ssm_log_dt, ssm_B_re, ssm_B_im,
           ssm_C_re, ssm_C_im, ssm_D, w_glu, w_proj_attn, w_proj_ssm, w_out, norm_ffn_g, w_router_group,
           b_router_group, w_router_expert, b_router_expert, w_expert_gate, w_expert_up, w_expert_down):
    layer_params = (norm_mix_g, w_in, b_forget, q_norm_g, k_norm_g, ssm_A_re, ssm_A_im, ssm_log_dt, ssm_B_re,
                    ssm_B_im, ssm_C_re, ssm_C_im, ssm_D, w_glu, w_proj_attn, w_proj_ssm, w_out, norm_ffn_g,
                    w_router_group, b_router_group, w_router_expert, b_router_expert, w_expert_gate, w_expert_up,
                    w_expert_down)
    for layer in range(norm_mix_g.shape[0]):
        x = _layer(x, *[p[layer] for p in layer_params])
    return x
```

```python
import functools
import math

import jax
import jax.numpy as jnp
from jax import lax
from jax.experimental import pallas as pl
from jax.experimental.pallas import tpu as pltpu

F32 = jnp.float32
BF16 = jnp.bfloat16
I32 = jnp.int32

LANES = 128
SUBLANES = 8
MXU_DIM = 256

N_HEADS = 8
HEAD_DIM = 64
SSM_GROUP = 16
SSM_STATE = 64
N_EXPERT_GROUPS = 4
EXPERTS_PER_GROUP = 8
N_EXPERTS = N_EXPERT_GROUPS * EXPERTS_PER_GROUP
EPS = 1e-6
NEG_INF = -1e30
LOG2E = math.log2(math.e)
PLAIN_SOFTMAX_MAX_LOG2 = 60.0
UNDERFLOW_LOG2 = 160.0
BIAS_K_LANE = HEAD_DIM
BIAS_Q_LANE = HEAD_DIM + 3
FORGET_COPIES_STRIDE = LANES // N_HEADS

CHUNK = MXU_DIM // SSM_GROUP
CHUNKS_PER_SUPER = 8
SUPER = CHUNK * CHUNKS_PER_SUPER
GROUPS_PER_BLOCK = LANES // SSM_GROUP
RELAYOUT_UNROLL = 8
MIN_FACTORED_DECAY = -4.0

TM_PROJ = 512
TQ_ATTN = 2048
TS_ATTN = 512
HEADS_PER_STEP = 2
TM_MIX = 512
MIX_CHAINS = 1
TM_EXPERT = 512
TM_DISPATCH = 512
TM_COMBINE = 512
SORT_CHUNK = 512
ZERO_FILL_ROWS = 2048
ROW_DMA_UNROLL = 16
ROUTE_LANES = 2048
ROUTER_ROWS = 48

VMEM_LIMIT = 48 * 1024 * 1024


def _nt_dot(a, b):
    return lax.dot_general(a, b, (((1,), (1,)), ((), ())), preferred_element_type=F32)


def _dot(a, b):
    return jnp.dot(a, b, preferred_element_type=F32)


def _lane_tile(x, n):
    return x if n == 1 else jnp.concatenate([x] * n, axis=1)


def _rmsnorm_rows(x, g):
    ms = jnp.mean(x * x, axis=-1, keepdims=True)
    return x * lax.rsqrt(ms + EPS) * g


def _sigmoid(x):
    return 1.0 / (1.0 + jnp.exp(-x))


def _split3(x):
    hi = x.astype(BF16)
    r1 = x - hi.astype(F32)
    mid = r1.astype(BF16)
    lo = (r1 - mid.astype(F32)).astype(BF16)
    return [hi, mid, lo]


def _expand_heads(z):
    lane = lax.broadcasted_iota(I32, (z.shape[0], LANES), 1)
    blocks = []
    for p in range(z.shape[1] // LANES):
        blk = z[:, p * LANES:(p + 1) * LANES]
        blocks.append(jnp.where(lane < HEAD_DIM, blk, 0.0))
        blocks.append(jnp.where(lane < HEAD_DIM, pltpu.roll(blk, HEAD_DIM, axis=1), 0.0))
    return jnp.concatenate(blocks, axis=1)


def _proj_kernel(x_ref, g_ref, w_ref, wf_ref, bf_ref, qg_ref, kg_ref, bd_ref, ltri_ref, place_ref,
                 ones_ref, shift_ref, q_ref, k_ref, v_ref, u_ref, fedge_ref, carry_ref, *, tiles_per_seq):
    i = pl.program_id(0)

    @pl.when(i % tiles_per_seq == 0)
    def _():
        carry_ref[...] = jnp.zeros_like(carry_ref)

    tm = x_ref.shape[0]
    da = N_HEADS * HEAD_DIM
    hw = N_HEADS * LANES
    xb = _rmsnorm_rows(x_ref[...], g_ref[...]).astype(BF16)
    z = _dot(xb, w_ref[...])
    zq, zk = z[:, :da], z[:, da:2 * da]
    bd = bd_ref[...]
    msq = _dot((zq * zq).astype(BF16), bd)
    msk = _dot((zk * zk).astype(BF16), bd)
    qn = zq * lax.rsqrt(msq + EPS) * qg_ref[...]
    kn = zk * lax.rsqrt(msk + EPS) * kg_ref[...]
    u_ref[...] = z[:, 3 * da:]

    a = _dot(xb, wf_ref[...]) + bf_ref[...]
    logf = jnp.minimum(a, 0.0) - jnp.log(1.0 + jnp.exp(-jnp.abs(a)))
    cs = _dot(ltri_ref[...], jnp.concatenate(_split3(logf), axis=1))
    cum = cs[:, :LANES] + cs[:, LANES:2 * LANES] + cs[:, 2 * LANES:] + carry_ref[0:1, :]
    carry_ref[...] = jnp.broadcast_to(cum[tm - 1:tm, :], carry_ref.shape)
    edge_row = lax.broadcasted_iota(I32, fedge_ref.shape, 0)
    fedge_ref[...] = jnp.where(edge_row == 0, cum[0:1, :], jnp.where(edge_row == 1, cum[tm - 1:tm, :], 0.0))
    copy = lax.broadcasted_iota(I32, cum.shape, 1) % FORGET_COPIES_STRIDE
    bias = jnp.where(copy < 3, (cum - shift_ref[...]) * LOG2E, cum * (-LOG2E))
    hi, mid, lo = [p.astype(F32) for p in _split3(bias)]
    piece = jnp.where(copy % 3 == 0, hi, jnp.where(copy % 3 == 1, mid, lo)).astype(BF16)
    extras = _dot(piece, place_ref[...])
    q_ref[...] = (_expand_heads(qn) + extras[:, :hw] + ones_ref[0:1, :]).astype(BF16)
    k_ref[...] = (_expand_heads(kn) + extras[:, hw:] + ones_ref[1:2, :]).astype(BF16)
    v_ref[...] = (_expand_heads(z[:, 2 * da:3 * da]) + ones_ref[2:3, :]).astype(BF16)


def _proj(x2, norm_g, w_main, wf, b_f, qg, kg, bd, ltri, place, ones, shift, *, seq):
    t, d = x2.shape
    tm = TM_PROJ
    hw = N_HEADS * LANES
    du = w_main.shape[1] - 3 * N_HEADS * HEAD_DIM
    tiles_per_seq = seq // tm
    full = lambda a: pl.BlockSpec(a.shape, lambda i: (0,) * a.ndim)
    row = lambda width: pl.BlockSpec((tm, width), lambda i: (i, 0))
    consts = (norm_g, w_main, wf, b_f, qg, kg, bd, ltri, place, ones, shift)
    return pl.pallas_call(
        functools.partial(_proj_kernel, tiles_per_seq=tiles_per_seq),
        grid=(t // tm,),
        in_specs=[row(d)] + [full(a) for a in consts],
        out_specs=[row(hw), row(hw), row(hw), row(du), pl.BlockSpec((None, SUBLANES, LANES), lambda i: (i, 0, 0))],
        out_shape=[jax.ShapeDtypeStruct((t, hw), BF16)] * 3
        + [jax.ShapeDtypeStruct((t, du), F32), jax.ShapeDtypeStruct((t // tm, SUBLANES, LANES), F32)],
        scratch_shapes=[pltpu.VMEM((SUBLANES, LANES), F32)],
        compiler_params=pltpu.CompilerParams(dimension_semantics=("arbitrary",), vmem_limit_bytes=VMEM_LIMIT),
        name="proj",
    )(x2, *consts)


def _attn_kernel(first_ref, q_ref, k_ref, v_ref, o_ref, acc_scr, m_scr, *, ts, safe):
    i = pl.program_id(2)
    nsub = q_ref.shape[0] // ts
    acc_scr[...] = jnp.zeros(acc_scr.shape, F32)
    if safe:
        m_scr[...] = jnp.full(m_scr.shape, NEG_INF, F32)

    def block(hd, sub, j, masked):
        off = pl.multiple_of(j * ts, ts)
        lanes = slice(hd * LANES, (hd + 1) * LANES)
        chain = hd * nsub + sub
        s = _nt_dot(q_ref[sub * ts:(sub + 1) * ts, lanes], k_ref[pl.ds(off, ts), lanes])
        if masked:
            rows = lax.broadcasted_iota(I32, (ts, ts), 0)
            cols = lax.broadcasted_iota(I32, (ts, ts), 1)
            s = jnp.where(cols <= rows, s, NEG_INF)
        vblk = v_ref[pl.ds(off, ts), lanes]
        if safe:
            m_prev = m_scr[chain]
            m_new = jnp.maximum(m_prev, jnp.max(s, axis=1, keepdims=True))
            p = jnp.exp2(s - _lane_tile(m_new, ts // LANES)).astype(BF16)
            acc_scr[chain] = jnp.exp2(m_prev - m_new) * acc_scr[chain] + _dot(p, vblk)
            m_scr[chain] = m_new
        else:
            acc_scr[chain] += _dot(jnp.exp2(s).astype(BF16), vblk)

    def body(heads, j, carry):
        for hd in heads:
            for sub in range(nsub):
                block(hd, sub, j, False)
        return carry

    step = (pl.program_id(0) * pl.num_programs(1) + pl.program_id(1)) * pl.num_programs(2) + i
    starts = [first_ref[step * HEADS_PER_STEP + hd] for hd in range(HEADS_PER_STEP)]
    shared = functools.reduce(jnp.maximum, starts)
    for hd in range(HEADS_PER_STEP):
        lax.fori_loop(starts[hd], shared, functools.partial(body, (hd,)), 0)
    lax.fori_loop(shared, nsub * i, functools.partial(body, tuple(range(HEADS_PER_STEP))), 0)
    for jj in range(nsub):
        for hd in range(HEADS_PER_STEP):
            for sub in range(jj, nsub):
                block(hd, sub, nsub * i + jj, sub == jj)
    lane = lax.broadcasted_iota(I32, (ts, LANES), 1)
    for sub in range(nsub):
        outs = []
        for hd in range(HEADS_PER_STEP):
            acc = acc_scr[hd * nsub + sub]
            outs.append(acc / acc[:, HEAD_DIM:HEAD_DIM + 1])
        both = jnp.where(lane < HEAD_DIM, outs[0], pltpu.roll(outs[1], HEAD_DIM, axis=1))
        o_ref[sub * ts:(sub + 1) * ts, :] = both.astype(o_ref.dtype)


def _first_kv_block(fedge, b, s, tq, ts):
    nblk = s // ts
    f_first = fedge[:, 0, ::FORGET_COPIES_STRIDE].reshape(b, nblk, N_HEADS).transpose(0, 2, 1)
    f_last = fedge[:, 1, ::FORGET_COPIES_STRIDE].reshape(b, nblk, N_HEADS).transpose(0, 2, 1)
    tile_start = f_first[:, :, ::tq // ts]
    bound = LOG2E * (tile_start[:, :, :, None] - f_last[:, :, None, :])
    first = jnp.sum(bound < -UNDERFLOW_LOG2, axis=-1).astype(I32)
    first = first.reshape(b, N_HEADS // HEADS_PER_STEP, HEADS_PER_STEP, -1).transpose(0, 1, 3, 2)
    return first.reshape(-1)


def _attention(q, k, v, fedge, plain_ok):
    b, s, hw = q.shape
    tq, ts = TQ_ATTN, TS_ATTN
    nq = s // tq
    steps = N_HEADS // HEADS_PER_STEP
    width = HEADS_PER_STEP * LANES
    chains = HEADS_PER_STEP * tq // ts

    def call(safe):
        first = jnp.zeros((b * N_HEADS * nq,), I32) if safe else _first_kv_block(fedge, b, s, tq, ts)
        return pl.pallas_call(
            functools.partial(_attn_kernel, ts=ts, safe=safe),
            grid_spec=pltpu.PrefetchScalarGridSpec(
                num_scalar_prefetch=1,
                grid=(b, steps, nq),
                in_specs=[pl.BlockSpec((None, tq, width), lambda bi, h, i, f: (bi, i, h)),
                          pl.BlockSpec((None, s, width), lambda bi, h, i, f: (bi, 0, h)),
                          pl.BlockSpec((None, s, width), lambda bi, h, i, f: (bi, 0, h))],
                out_specs=pl.BlockSpec((None, tq, LANES), lambda bi, h, i, f: (bi, i, h)),
                scratch_shapes=[pltpu.VMEM((chains, ts, LANES), F32)] * 2),
            out_shape=jax.ShapeDtypeStruct((b, s, steps * LANES), BF16),
            compiler_params=pltpu.CompilerParams(
                dimension_semantics=("arbitrary", "arbitrary", "arbitrary"), vmem_limit_bytes=VMEM_LIMIT),
            name="attn_safe" if safe else "attn",
        )(first, q, k, v)

    return lax.cond(plain_ok, lambda: call(False), lambda: call(True))


def _window_select(pieces, first_group):
    window = lax.broadcasted_iota(I32, pieces[0].shape, 1) // SSM_GROUP
    out = pieces[0]
    for m in range(1, GROUPS_PER_BLOCK):
        out = jnp.where(window == (first_group + m) % GROUPS_PER_BLOCK, pieces[m], out)
    return out


def _s5_kernel(u_ref, win_ref, tw_ref, coef_ref, y_ref, ub_scr, yb_scr, *, nk):
    nj = CHUNKS_PER_SUPER
    gpb = GROUPS_PER_BLOCK
    w = CHUNK * SSM_GROUP
    half = w // 2
    nkk = nk // SUBLANES
    row_stride = SUPER

    def gather_body(it, carry):
        j, kk = it // nkk, it % nkk
        for sp in range(CHUNK // gpb):
            rot = []
            for m in range(gpb):
                row0 = j * CHUNK + sp * gpb + m + kk * (SUBLANES * row_stride)
                src = u_ref[pl.ds(row0, SUBLANES, stride=row_stride), :]
                rot.append(src if m == 0 else pltpu.roll(src, m * SSM_GROUP, axis=1))
            for gl in range(gpb):
                ub_scr[gl, j, pl.ds(kk * SUBLANES, SUBLANES), sp * LANES:(sp + 1) * LANES] = _window_select(rot, gl)
        return carry

    lax.fori_loop(0, nj * nkk, gather_body, 0, unroll=RELAYOUT_UNROLL)

    def swap(val):
        return jnp.concatenate([val[:, half:], val[:, :half]], axis=1)

    def group_body(gl, carry):
        def cmul(val, idx):
            return coef_ref[gl, idx:idx + 1, :] * val + coef_ref[gl, idx + 1:idx + 2, :] * swap(val)

        u = ub_scr[gl].reshape(nj * nk, w).astype(BF16)
        s1 = _dot(u, win_ref[gl]).reshape(nj, nk, w)
        e = jnp.zeros((nk, w), F32)
        local = []
        for j in range(nj):
            local.append(e)
            e = cmul(e, 0) + s1[j]
        kidx = lax.broadcasted_iota(I32, (nk, w), 0)
        x = e
        step, d = 0, 1
        while d < nk:
            shifted = jnp.where(kidx >= d, pltpu.roll(x, d, axis=0), 0.0)
            x = x + cmul(shifted, 2 + 2 * nj + 2 * step)
            step, d = step + 1, d * 2
        x_start = jnp.where(kidx >= 1, pltpu.roll(x, 1, axis=0), 0.0)
        starts = [local[j] + cmul(x_start, 2 + 2 * j) for j in range(nj)]
        p = jnp.concatenate([st[:, :half] for st in starts], axis=0)
        hi = p.astype(BF16)
        lo = (p - hi.astype(F32)).astype(BF16)
        y = _dot(jnp.concatenate([u, hi, lo], axis=1), tw_ref[gl])
        yb_scr[gl] = y.reshape(nj, nk, w)
        return carry

    lax.fori_loop(0, gpb, group_body, 0)

    def scatter_body(it, carry):
        j, kk = it // nkk, it % nkk
        for tp in range(CHUNK // gpb):
            src = [yb_scr[gl, j, pl.ds(kk * SUBLANES, SUBLANES), tp * LANES:(tp + 1) * LANES] for gl in range(gpb)]
            for m in range(gpb):
                by_window = [src[(wi - m) % gpb] for wi in range(gpb)]
                window = lax.broadcasted_iota(I32, by_window[0].shape, 1) // SSM_GROUP
                merged = by_window[0]
                for wi in range(1, gpb):
                    merged = jnp.where(window == wi, by_window[wi], merged)
                nat = merged if m == 0 else pltpu.roll(merged, LANES - m * SSM_GROUP, axis=1)
                row0 = j * CHUNK + tp * gpb + m + kk * (SUBLANES * row_stride)
                y_ref[pl.ds(row0, SUBLANES, stride=row_stride), :] = nat
        return carry

    lax.fori_loop(0, nj * nkk, scatter_body, 0, unroll=RELAYOUT_UNROLL)


def _s5_params(a_re, a_im, log_dt, b_re, b_im, c_re, c_im, nk):
    args = (a_re, a_im, log_dt, b_re, b_im, c_re, c_im)
    slowest = jnp.min(a_re * jnp.exp(log_dt)[:, None])
    return lax.cond(slowest > MIN_FACTORED_DECAY, functools.partial(_s5_params_factored, nk=nk),
                    functools.partial(_s5_params_direct, nk=nk), *args)


def _s5_params_factored(a_re, a_im, log_dt, b_re, b_im, c_re, c_im, *, nk):
    g, p = a_re.shape
    c = SSM_GROUP
    gpb = GROUPS_PER_BLOCK
    dt = jnp.exp(log_dt)[:, None]
    adt_r, adt_i = a_re * dt, a_im * dt

    def lam_pow(n):
        nf = jnp.asarray(n, F32)
        nf = (nf[None] if nf.ndim == 1 else nf)[:, :, None]
        mag = jnp.exp(adt_r[:, None, :] * nf)
        ang = adt_i[:, None, :] * nf
        return mag * jnp.cos(ang), mag * jnp.sin(ang)

    gi = jnp.arange(g)[:, None, None] % gpb
    step = ((jnp.arange(CHUNK // gpb)[None, :, None]) * gpb
            + (jnp.arange(gpb)[None, None, :] - gi) % gpb).reshape(g, CHUNK)

    l1r, l1i = lam_pow(jnp.ones((1,)))
    den = a_re * a_re + a_im * a_im
    nr, ni = l1r[:, 0] - 1.0, l1i[:, 0]
    qr = (nr * a_re + ni * a_im) / den
    qi = (ni * a_re - nr * a_im) / den
    bcr = (qr[..., None] * b_re - qi[..., None] * b_im).transpose(0, 2, 1)
    bci = (qr[..., None] * b_im + qi[..., None] * b_re).transpose(0, 2, 1)

    def times_b(pr, pi):
        pr, pi = pr[:, :, None, :], pi[:, :, None, :]
        return pr * bcr[:, None] - pi * bci[:, None], pr * bci[:, None] + pi * bcr[:, None]

    def times_c(pr, pi):
        pr, pi = pr[:, :, None, :], pi[:, :, None, :]
        return c_re[:, None] * pr - c_im[:, None] * pi, c_re[:, None] * pi + c_im[:, None] * pr

    rows = lambda parts: jnp.concatenate(parts, axis=-1).reshape(g, CHUNK * c, -1)
    ir, ii = times_b(*lam_pow(CHUNK - 1 - step))
    win = rows([ir, ii, ii, ir])
    clr, cli = times_c(*lam_pow(step + 1))
    wout = rows([clr, -cli]).transpose(0, 2, 1)
    ar, ai = times_b(*lam_pow(-step))
    br, bi = times_c(*lam_pow(step))
    toep = jnp.einsum('gnp,gmp->gnm', rows([ar, ai]), rows([br, -bi]), precision=lax.Precision.HIGHEST)
    step_of_lane = jnp.repeat(step, c, axis=1)
    toep = jnp.where(step_of_lane[:, None, :] >= step_of_lane[:, :, None], toep, 0.0)
    tw = jnp.concatenate([toep, wout, wout], axis=1)
    return win.astype(BF16), tw.astype(BF16), _s5_scan_coefficients(lam_pow, g, p, nk)


def _s5_scan_coefficients(lam_pow, g, p, nk):
    n_steps = max(nk.bit_length() - 1, 0)
    powers = [CHUNK] + [CHUNK * j for j in range(CHUNKS_PER_SUPER)] + [SUPER * (1 << i) for i in range(n_steps)]
    ar, ai = lam_pow(jnp.array(powers, dtype=jnp.int32))
    c1 = jnp.concatenate([ar, ar, ar, ar], axis=2)
    c2 = jnp.concatenate([-ai, ai, ai, -ai], axis=2)
    coef = jnp.stack([c1, c2], axis=2).reshape(g, 2 * len(powers), 4 * p)
    return jnp.pad(coef, ((0, 0), (0, (-coef.shape[1]) % SUBLANES), (0, 0)))


def _s5_params_direct(a_re, a_im, log_dt, b_re, b_im, c_re, c_im, *, nk):
    g, p = a_re.shape
    c = SSM_GROUP
    hp = lax.Precision.HIGHEST
    dt = jnp.exp(log_dt)[:, None]
    adt_r, adt_i = a_re * dt, a_im * dt

    def lam_pow(n):
        nf = jnp.asarray(n, F32)[None, :, None]
        mag = jnp.exp(adt_r[:, None, :] * nf)
        ang = adt_i[:, None, :] * nf
        return mag * jnp.cos(ang), mag * jnp.sin(ang)

    lr, li = lam_pow(jnp.arange(CHUNK + 1))
    den = a_re * a_re + a_im * a_im
    nr, ni = lr[:, 1] - 1.0, li[:, 1]
    qr = (nr * a_re + ni * a_im) / den
    qi = (ni * a_re - nr * a_im) / den
    bbr = qr[..., None] * b_re - qi[..., None] * b_im
    bbi = qr[..., None] * b_im + qi[..., None] * b_re

    mr = lr[:, :CHUNK, :, None] * bbr[:, None] - li[:, :CHUNK, :, None] * bbi[:, None]
    mi = lr[:, :CHUNK, :, None] * bbi[:, None] + li[:, :CHUNK, :, None] * bbr[:, None]
    kern = (jnp.einsum('gcp,gtpd->gtcd', c_re, mr, precision=hp)
            - jnp.einsum('gcp,gtpd->gtcd', c_im, mi, precision=hp))
    toep = jnp.stack([jnp.pad(kern[:, :CHUNK - s], ((0, 0), (s, 0), (0, 0), (0, 0))) for s in range(CHUNK)],
                     axis=1)
    toep = toep.transpose(0, 1, 4, 2, 3).reshape(g, CHUNK * c, CHUNK * c)
    clr = c_re[:, None] * lr[:, 1:, None, :] - c_im[:, None] * li[:, 1:, None, :]
    cli = c_re[:, None] * li[:, 1:, None, :] + c_im[:, None] * lr[:, 1:, None, :]
    w_re = clr.transpose(0, 3, 1, 2).reshape(g, p, CHUNK * c)
    w_im = (-cli).transpose(0, 3, 1, 2).reshape(g, p, CHUNK * c)
    wout = jnp.concatenate([w_re, w_im], axis=1)
    pr, pi = lr[:, :CHUNK][:, ::-1], li[:, :CHUNK][:, ::-1]
    ir = pr[..., None] * bbr[:, None] - pi[..., None] * bbi[:, None]
    ii = pr[..., None] * bbi[:, None] + pi[..., None] * bbr[:, None]
    ir = ir.transpose(0, 1, 3, 2).reshape(g, CHUNK * c, p)
    ii = ii.transpose(0, 1, 3, 2).reshape(g, CHUNK * c, p)
    win = jnp.concatenate([ir, ii, ii, ir], axis=2)

    gi = jnp.arange(g)[:, None, None] % GROUPS_PER_BLOCK
    si = jnp.arange(CHUNK)[None, :, None]
    ci = jnp.arange(c)[None, None, :]
    lane_of = ((si // GROUPS_PER_BLOCK) * LANES + ((gi + si) % GROUPS_PER_BLOCK) * c + ci).reshape(g, CHUNK * c)
    perm = (lane_of[:, None, :] == jnp.arange(CHUNK * c)[None, :, None]).astype(BF16)
    win = jnp.einsum('gln,gnk->glk', perm, win.astype(BF16))
    toep = jnp.einsum('gln,gnk->glk', perm, toep.astype(BF16))
    tw = jnp.concatenate([toep, wout.astype(BF16), wout.astype(BF16)], axis=1)
    tw = jnp.einsum('grn,gln->grl', tw, perm)
    return win.astype(BF16), tw.astype(BF16), _s5_scan_coefficients(lam_pow, g, p, nk)


def _s5(u, win, tw, coef):
    b, s, dssm = u.shape
    nk = s // SUPER
    nj = CHUNKS_PER_SUPER
    w = CHUNK * SSM_GROUP
    gpb = GROUPS_PER_BLOCK
    nblk = dssm // LANES
    wspec = lambda a: pl.BlockSpec((gpb,) + a.shape[1:], lambda bi, li: (li, 0, 0))
    return pl.pallas_call(
        functools.partial(_s5_kernel, nk=nk),
        grid=(b, nblk),
        in_specs=[pl.BlockSpec((None, s, LANES), lambda bi, li: (bi, 0, li)), wspec(win), wspec(tw), wspec(coef)],
        out_specs=pl.BlockSpec((None, s, LANES), lambda bi, li: (bi, 0, li)),
        out_shape=jax.ShapeDtypeStruct((b, s, dssm), F32),
        scratch_shapes=[pltpu.VMEM((gpb, nj, nk, w), F32)] * 2,
        compiler_params=pltpu.CompilerParams(dimension_semantics=("arbitrary", "arbitrary"),
                                             vmem_limit_bytes=VMEM_LIMIT),
        name="s5",
    )(u, win, tw, coef)


def _route(logits):
    e_all = logits[0:N_EXPERTS]
    gl = logits[N_EXPERTS:N_EXPERTS + SUBLANES]
    tm = logits.shape[1]
    ridx = lax.broadcasted_iota(I32, (SUBLANES, tm), 0)
    ge = jnp.exp(gl - jnp.max(gl, axis=0, keepdims=True))
    gp = ge / jnp.sum(ge, axis=0, keepdims=True)
    g_top = jnp.max(gp, axis=0, keepdims=True)
    g_sel = jnp.min(jnp.where(gp == g_top, ridx, SUBLANES), axis=0, keepdims=True)
    e_in = e_all[(N_EXPERT_GROUPS - 1) * SUBLANES:]
    for gi in range(N_EXPERT_GROUPS - 2, -1, -1):
        e_in = jnp.where(g_sel == gi, e_all[gi * SUBLANES:(gi + 1) * SUBLANES], e_in)
    ee = jnp.exp(e_in - jnp.max(e_in, axis=0, keepdims=True))
    ep = ee / jnp.sum(ee, axis=0, keepdims=True)
    v1 = jnp.max(ep, axis=0, keepdims=True)
    i1 = jnp.min(jnp.where(ep == v1, ridx, SUBLANES), axis=0, keepdims=True)
    ep2 = jnp.where(ridx == i1, -1.0, ep)
    v2 = jnp.max(ep2, axis=0, keepdims=True)
    i2 = jnp.min(jnp.where(ep2 == v2, ridx, SUBLANES), axis=0, keepdims=True)
    den = v1 + v2
    w1 = g_top * v1 / den
    w2 = g_top * v2 / den
    e1 = g_sel * EXPERTS_PER_GROUP + i1
    e2 = g_sel * EXPERTS_PER_GROUP + i2
    ids = jnp.where(ridx == 0, e1, jnp.where(ridx == 1, e2, 0))
    wts = jnp.where(ridx == 0, w1, jnp.where(ridx == 1, w2, 0.0))
    return ids, wts


def _mix_kernel(x_ref, ya_ref, ys_ref, u_ref, g1_ref, wgate_ref, dsk_ref, wglu_ref, wpa_ref, wps_ref,
                wout_ref, g2_ref, wrh_ref, wrl_ref, br_ref, x1_ref, xp_ref, logits_ref):
    d = x_ref.shape[1]
    rows = x_ref.shape[0] // MIX_CHAINS
    for chain in range(MIX_CHAINS):
        r = slice(chain * rows, (chain + 1) * rows)
        x = x_ref[r, :]
        xb = _rmsnorm_rows(x, g1_ref[...]).astype(BF16)
        gates = _dot(xb, wgate_ref[...])
        y = ys_ref[r, :].astype(F32) + dsk_ref[...] * u_ref[r, :].astype(F32)
        y = y * (0.5 * (1.0 + jnp.tanh(math.sqrt(2.0 / math.pi) * (y + 0.044715 * (y * y * y)))))
        y = y * _sigmoid(_dot(y.astype(BF16), wglu_ref[...]))
        mixed = (_sigmoid(gates[:, :d]) * _dot(ya_ref[r, :], wpa_ref[...])
                 + _sigmoid(gates[:, d:]) * _dot(y.astype(BF16), wps_ref[...]))
        x1 = x + _dot(mixed.astype(BF16), wout_ref[...])
        x1_ref[r, :] = x1
        xn = _rmsnorm_rows(x1, g2_ref[...])
        xp_ref[r, :] = pltpu.pack_elementwise([xn[:, :d // 2], xn[:, d // 2:]], packed_dtype=BF16)
        xh = xn.astype(BF16)
        xl = (xn - xh.astype(F32)).astype(BF16)
        wrh = wrh_ref[...]
        logits_ref[:, r] = _nt_dot(wrh, xh) + _nt_dot(wrl_ref[...], xh) + _nt_dot(wrh, xl) + br_ref[...]


def _mix(x2, ya, ys, u, g1, wgate, dsk, wglu, wpa, wps, wout, g2, wrh, wrl, br):
    t, d = x2.shape
    tm = TM_MIX
    full = lambda a: pl.BlockSpec(a.shape, lambda i: (0,) * a.ndim)
    row = lambda width: pl.BlockSpec((tm, width), lambda i: (i, 0))
    return pl.pallas_call(
        _mix_kernel,
        grid=(t // tm,),
        in_specs=[row(d), row(ya.shape[1]), row(ys.shape[1]), row(u.shape[1]), full(g1), full(wgate), full(dsk),
                  full(wglu), full(wpa),
                  full(wps), full(wout), full(g2), full(wrh), full(wrl), full(br)],
        out_specs=[row(d), row(d // 2), pl.BlockSpec((ROUTER_ROWS, tm), lambda i: (0, i))],
        out_shape=[jax.ShapeDtypeStruct((t, d), F32), jax.ShapeDtypeStruct((t, d // 2), jnp.uint32),
                   jax.ShapeDtypeStruct((ROUTER_ROWS, t), F32)],
        compiler_params=pltpu.CompilerParams(dimension_semantics=("arbitrary",), vmem_limit_bytes=VMEM_LIMIT),
        name="mix",
    )(x2, ya, ys, u, g1, wgate, dsk, wglu, wpa, wps, wout, g2, wrh, wrl, br)


def _route_kernel(logits_ref, ids_ref, wts_ref):
    ids, wts = _route(logits_ref[...])
    ids_ref[...] = ids
    wts_ref[...] = wts


def _route_call(logits):
    t = logits.shape[1]
    tl = ROUTE_LANES
    col = pl.BlockSpec((SUBLANES, tl), lambda i: (0, i))
    return pl.pallas_call(
        _route_kernel,
        grid=(t // tl,),
        in_specs=[pl.BlockSpec((ROUTER_ROWS, tl), lambda i: (0, i))],
        out_specs=[col, col],
        out_shape=[jax.ShapeDtypeStruct((SUBLANES, t), I32), jax.ShapeDtypeStruct((SUBLANES, t), F32)],
        compiler_params=pltpu.CompilerParams(dimension_semantics=("arbitrary",)),
        name="route",
    )(logits)


def _meta_kernel(ids_ref, tri_ref, dest_ref, tile_ref, *, tile_rows):
    nk, nc, c = ids_ref.shape
    ne = N_EXPERTS
    erow = lax.broadcasted_iota(I32, (ne, c), 0)
    ones = jnp.ones((c, LANES), BF16)

    def onehot(k, ci):
        mask = erow == ids_ref[k, pl.ds(ci, 1), :]
        return mask, jnp.where(mask, 1.0, 0.0).astype(BF16)

    def count_body(n, acc):
        return acc + _dot(onehot(n // nc, n % nc)[1], ones)

    cnt = lax.fori_loop(0, nk * nc, count_body, jnp.zeros((ne, LANES), F32))
    ntiles = jnp.floor((cnt + (tile_rows - 1)) * (1.0 / tile_rows))
    lower = jnp.where(lax.broadcasted_iota(I32, (ne, ne), 1) < lax.broadcasted_iota(I32, (ne, ne), 0), 1.0, 0.0)
    start_tiles = _dot(lower.astype(BF16), ntiles.astype(BF16))
    base = start_tiles * tile_rows

    tri = tri_ref[...]

    def dest_body(n, carry):
        k, ci = n // nc, n % nc
        mask, oh = onehot(k, ci)
        prefix = _dot(oh, tri)
        slot = _lane_tile(base + carry, c // LANES) + prefix - 1.0
        dest = jnp.sum(jnp.where(mask, slot, 0.0), axis=0, keepdims=True)
        dest_ref[k, pl.ds(ci, 1), :] = dest.astype(I32)
        return carry + _dot(oh, ones)

    lax.fori_loop(0, nk * nc, dest_body, jnp.zeros((ne, LANES), F32))

    nt_lanes = tile_ref.shape[1]
    end_tiles = _lane_tile(start_tiles + ntiles, nt_lanes // LANES)
    tidx = lax.broadcasted_iota(I32, (ne, nt_lanes), 1).astype(F32)
    texp = jnp.sum(jnp.where(tidx >= end_tiles, 1.0, 0.0), axis=0, keepdims=True)
    valid = jnp.where(texp < ne, 1, 0)
    texp = jnp.minimum(texp, ne - 1.0).astype(I32)
    ridx = lax.broadcasted_iota(I32, tile_ref.shape, 0)
    tile_ref[...] = jnp.where(ridx == 0, texp, jnp.where(ridx == 1, valid, 0))


def _meta(ids2, tri, n_tiles):
    nk, t = ids2.shape
    c = SORT_CHUNK
    nt_lanes = pl.cdiv(n_tiles, LANES) * LANES
    ids3 = ids2.reshape(nk, t // c, c)
    dest, tile = pl.pallas_call(
        functools.partial(_meta_kernel, tile_rows=TM_EXPERT),
        out_shape=[jax.ShapeDtypeStruct(ids3.shape, I32), jax.ShapeDtypeStruct((SUBLANES, nt_lanes), I32)],
        compiler_params=pltpu.CompilerParams(vmem_limit_bytes=VMEM_LIMIT),
        name="meta",
    )(ids3, tri)
    return dest.reshape(nk * t), tile[0, :n_tiles], tile[1, :n_tiles]


def _row_copy(src_ref, src_row, dst_ref, dst_row, sem):
    return pltpu.make_async_copy(src_ref.at[pl.ds(src_row, 1)], dst_ref.at[pl.ds(dst_row, 1)], sem)


def _dispatch_kernel(dest_ref, xp_ref, xs_in_ref, xs_ref, sem, *, n_tokens):
    del xs_in_ref
    tm = xp_ref.shape[0]
    base = pl.program_id(0) * tm

    def body(r, carry):
        for k in range(2):
            _row_copy(xp_ref, r, xs_ref, dest_ref[k * n_tokens + base + r], sem).start()
        return carry

    lax.fori_loop(0, tm, body, 0, unroll=ROW_DMA_UNROLL)
    for k in range(2):
        pltpu.make_async_copy(xp_ref, xs_ref.at[pl.ds(0, tm)], sem).wait()


def _zero_fill_kernel(o_ref):
    o_ref[...] = jnp.zeros(o_ref.shape, o_ref.dtype)


def _zeros(n_rows, width, dtype):
    rows = n_rows // pl.cdiv(n_rows, ZERO_FILL_ROWS)
    assert n_rows % rows == 0 and rows % SUBLANES == 0
    return pl.pallas_call(
        _zero_fill_kernel,
        grid=(n_rows // rows,),
        out_specs=pl.BlockSpec((rows, width), lambda i: (i, 0)),
        out_shape=jax.ShapeDtypeStruct((n_rows, width), dtype),
        compiler_params=pltpu.CompilerParams(dimension_semantics=("arbitrary",)),
        name="zero_fill",
    )()


def _dispatch(dest, xp, n_slots):
    t, w = xp.shape
    tm = TM_DISPATCH
    xs0 = _zeros(n_slots, w, xp.dtype)
    return pl.pallas_call(
        functools.partial(_dispatch_kernel, n_tokens=t),
        grid_spec=pltpu.PrefetchScalarGridSpec(
            num_scalar_prefetch=1,
            grid=(t // tm,),
            in_specs=[pl.BlockSpec((tm, w), lambda i, d: (i, 0)), pl.BlockSpec(memory_space=pl.ANY)],
            out_specs=pl.BlockSpec(memory_space=pl.ANY),
            scratch_shapes=[pltpu.SemaphoreType.DMA(())]),
        out_shape=jax.ShapeDtypeStruct(xs0.shape, xp.dtype),
        input_output_aliases={2: 0},
        compiler_params=pltpu.CompilerParams(dimension_semantics=("arbitrary",), has_side_effects=True),
        name="dispatch",
    )(dest, xp, xs0)


def _unpack_rows(packed):
    lo = pltpu.unpack_elementwise(packed, index=0, packed_dtype=BF16, unpacked_dtype=F32)
    hi = pltpu.unpack_elementwise(packed, index=1, packed_dtype=BF16, unpacked_dtype=F32)
    return jnp.concatenate([lo, hi], axis=1)


def _expert_kernel(te_ref, tv_ref, nv_ref, xs_ref, wg_ref, wu_ref, wd_ref, ys_ref, wg_bf, wu_bf, wd_bf):
    del nv_ref
    i = pl.program_id(0)
    half = xs_ref.shape[1]

    @pl.when((i == 0) | (te_ref[i] != te_ref[jnp.maximum(i - 1, 0)]))
    def _():
        wg_bf[...] = wg_ref[...].astype(BF16)
        wu_bf[...] = wu_ref[...].astype(BF16)
        wd_bf[...] = wd_ref[...].astype(BF16)

    @pl.when(tv_ref[i] > 0)
    def _():
        x = _unpack_rows(xs_ref[...]).astype(BF16)
        hg = _dot(x, wg_bf[...])
        hu = _dot(x, wu_bf[...])
        h = (hg * _sigmoid(hg) * hu).astype(BF16)
        y = _dot(h, wd_bf[...])
        ys_ref[...] = pltpu.pack_elementwise([y[:, :half], y[:, half:]], packed_dtype=BF16)

    @pl.when(tv_ref[i] == 0)
    def _():
        zero = jnp.zeros(ys_ref.shape, F32)
        ys_ref[...] = pltpu.pack_elementwise([zero, zero], packed_dtype=BF16)


def _experts(tile_expert, tile_valid, xs, wg, wu, wd):
    n_slots, w = xs.shape
    tm = TM_EXPERT
    n_tiles = n_slots // tm
    n_valid = jnp.sum(tile_valid).astype(I32).reshape(1)
    wspec = lambda a: pl.BlockSpec((None,) + a.shape[1:], lambda i, te, tv, nv: (te[i], 0, 0))
    rows_in = pl.BlockSpec((tm, w), lambda i, te, tv, nv: (jnp.minimum(i, nv[0] - 1), 0))
    rows_out = pl.BlockSpec((tm, w), lambda i, te, tv, nv: (i, 0))
    return pl.pallas_call(
        _expert_kernel,
        grid_spec=pltpu.PrefetchScalarGridSpec(
            num_scalar_prefetch=3,
            grid=(n_tiles,),
            in_specs=[rows_in, wspec(wg), wspec(wu), wspec(wd)],
            out_specs=rows_out,
            scratch_shapes=[pltpu.VMEM(a.shape[1:], BF16) for a in (wg, wu, wd)]),
        out_shape=jax.ShapeDtypeStruct(xs.shape, xs.dtype),
        compiler_params=pltpu.CompilerParams(dimension_semantics=("arbitrary",), vmem_limit_bytes=VMEM_LIMIT),
        name="experts",
    )(tile_expert, tile_valid, n_valid, xs, wg, wu, wd)


def _combine_kernel(dest_ref, x1_ref, w0_ref, w1_ref, ys_ref, out_ref, buf, sem, *, n_tokens):
    tm, d = x1_ref.shape
    base = pl.program_id(0) * tm

    def body(r, carry):
        for k in range(2):
            _row_copy(ys_ref, dest_ref[k * n_tokens + base + r], buf.at[k], r, sem).start()
        return carry

    lax.fori_loop(0, tm, body, 0, unroll=ROW_DMA_UNROLL)
    for k in range(2):
        pltpu.make_async_copy(ys_ref.at[pl.ds(0, tm)], buf.at[k], sem).wait()
    reps = d // LANES
    out_ref[...] = (x1_ref[...]
                    + _lane_tile(w0_ref[...], reps) * _unpack_rows(buf[0])
                    + _lane_tile(w1_ref[...], reps) * _unpack_rows(buf[1]))


def _combine(dest, x1, w0, w1, ys):
    t, d = x1.shape
    tm = TM_COMBINE
    row = lambda width: pl.BlockSpec((tm, width), lambda i, dref: (i, 0))
    return pl.pallas_call(
        functools.partial(_combine_kernel, n_tokens=t),
        grid_spec=pltpu.PrefetchScalarGridSpec(
            num_scalar_prefetch=1,
            grid=(t // tm,),
            in_specs=[row(d), row(LANES), row(LANES), pl.BlockSpec(memory_space=pl.ANY)],
            out_specs=row(d),
            scratch_shapes=[pltpu.VMEM((2, tm, ys.shape[-1]), ys.dtype), pltpu.SemaphoreType.DMA(())]),
        out_shape=jax.ShapeDtypeStruct((t, d), x1.dtype),
        compiler_params=pltpu.CompilerParams(dimension_semantics=("arbitrary",), vmem_limit_bytes=VMEM_LIMIT),
        name="combine",
    )(dest, x1, w0, w1, ys)


def _lower_tri(n):
    return jnp.where(jnp.arange(n)[:, None] >= jnp.arange(n)[None, :], 1.0, 0.0).astype(BF16)


def _upper_tri(n):
    return jnp.where(jnp.arange(n)[:, None] <= jnp.arange(n)[None, :], 1.0, 0.0).astype(BF16)


def _split_bf16(w):
    hi = w.astype(BF16)
    lo = (w - hi.astype(F32)).astype(BF16)
    return hi, lo


def _layer(x, norm_mix_g, w_in, b_forget, q_norm_g, k_norm_g, ssm_A_re, ssm_A_im, ssm_log_dt, ssm_B_re, ssm_B_im,
           ssm_C_re, ssm_C_im, ssm_D, w_glu, w_proj_attn, w_proj_ssm, w_out, norm_ffn_g, w_router_group,
           b_router_group, w_router_expert, b_router_expert, w_expert_gate, w_expert_up, w_expert_down):
    b, s, d = x.shape
    t = b * s
    da = N_HEADS * HEAD_DIM
    dssm = ssm_D.shape[0]
    nk = s // SUPER
    assert s % SUPER == 0 and nk % SUBLANES == 0 and nk & (nk - 1) == 0 and s % TQ_ATTN == 0 and t % TM_PROJ == 0
    assert TM_PROJ == TS_ATTN
    x2 = x.reshape(t, d)

    o_f = 3 * da
    o_u = o_f + N_HEADS
    o_g = o_u + dssm
    w_main = jnp.concatenate([w_in[:, :o_f], w_in[:, o_u:o_g]], axis=1).astype(BF16)
    wf = jnp.repeat(w_in[:, o_f:o_u], FORGET_COPIES_STRIDE, axis=1).astype(BF16)
    bf = jnp.repeat(b_forget, FORGET_COPIES_STRIDE)[None, :]
    w_gates = w_in[:, o_g:].astype(BF16)
    qg = (jnp.tile(q_norm_g, N_HEADS) * (HEAD_DIM ** -0.5 * LOG2E))[None, :]
    kg = jnp.tile(k_norm_g, N_HEADS)[None, :]
    head_of = jnp.arange(da) // HEAD_DIM
    bd = jnp.where(head_of[:, None] == head_of[None, :], 1.0 / HEAD_DIM, 0.0).astype(BF16)
    logit_bound = 1.02 * HEAD_DIM ** 0.5 * jnp.max(jnp.abs(q_norm_g)) * jnp.max(jnp.abs(k_norm_g))
    plain_ok = LOG2E * logit_bound <= PLAIN_SOFTMAX_MAX_LOG2
    shift = jnp.full((1, LANES), logit_bound, F32)
    hw = N_HEADS * LANES
    lane_in_head = jnp.arange(hw) % LANES
    ones = jnp.stack([
        (lane_in_head >= BIAS_K_LANE) & (lane_in_head < BIAS_K_LANE + 3),
        (lane_in_head >= BIAS_Q_LANE) & (lane_in_head < BIAS_Q_LANE + 3),
        lane_in_head == HEAD_DIM]
        + [jnp.zeros((hw,), bool)] * (SUBLANES - 3)).astype(F32)
    prow = jnp.arange(LANES)
    head, copy = prow // FORGET_COPIES_STRIDE, prow % FORGET_COPIES_STRIDE
    target = jnp.where(copy < 3, head * LANES + BIAS_Q_LANE + copy, hw + head * LANES + BIAS_K_LANE + copy - 3)
    place = ((jnp.arange(2 * hw)[None, :] == target[:, None]) & (copy < 6)[:, None]).astype(BF16)

    q, k, v, u, fedge = _proj(x2, norm_mix_g[None, :], w_main, wf, bf, qg, kg, bd, _lower_tri(TM_PROJ), place, ones,
                              shift, seq=s)
    y_attn = _attention(q.reshape(b, s, hw), k.reshape(b, s, hw), v.reshape(b, s, hw), fedge, plain_ok)

    win, tw, coef = _s5_params(ssm_A_re, ssm_A_im, ssm_log_dt, ssm_B_re, ssm_B_im, ssm_C_re, ssm_C_im, s // SUPER)
    y_ssm = _s5(u.reshape(b, s, dssm), win, tw, coef)

    wr = jnp.concatenate([w_router_expert.T, w_router_group.T,
                          jnp.zeros((ROUTER_ROWS - N_EXPERTS - N_EXPERT_GROUPS, d), F32)], axis=0)
    br = jnp.concatenate([b_router_expert, b_router_group,
                          jnp.full((SUBLANES - N_EXPERT_GROUPS,), NEG_INF, F32),
                          jnp.zeros((ROUTER_ROWS - N_EXPERTS - SUBLANES,), F32)])[:, None]
    x1, xp, logits = _mix(x2, y_attn.reshape(t, da), y_ssm.reshape(t, dssm), u, norm_mix_g[None, :], w_gates,
                          ssm_D[None, :], w_glu.astype(BF16), w_proj_attn.astype(BF16), w_proj_ssm.astype(BF16),
                          w_out.astype(BF16), norm_ffn_g[None, :], *_split_bf16(wr), br)
    ids, wts = _route_call(logits)

    n_tiles = (2 * t) // TM_EXPERT + N_EXPERTS
    dest, tile_expert, tile_valid = _meta(ids[:2], _upper_tri(SORT_CHUNK), n_tiles)
    xs = _dispatch(dest, xp, n_tiles * TM_EXPERT)
    ys = _experts(tile_expert, tile_valid, xs, w_expert_gate, w_expert_up, w_expert_down)
    w0 = jnp.broadcast_to(wts[0][:, None], (t, LANES))
    w1 = jnp.broadcast_to(wts[1][:, None], (t, LANES))
    out = _combine(dest, x1, w0, w1, ys)
    return out.reshape(b, s, d)


def kernel(x, norm_mix_g, w_in, b_forget, q_norm_g, k_norm_g, ssm_A_re, ssm_A_im, ssm_log_dt, ssm_B_re, ssm_B_im,
           ssm_C_re, ssm_C_im, ssm_D, w_glu, w_proj_attn, w_proj_ssm, w_out, norm_ffn_g, w_router_group,
           b_router_group, w_router_expert, b_router_expert, w_expert_gate, w_expert_up, w_expert_down):
    layer_params = (norm_mix_g, w_in, b_forget, q_norm_g, k_norm_g, ssm_A_re, ssm_A_im, ssm_log_dt, ssm_B_re,
                    ssm_B_im, ssm_C_re, ssm_C_im, ssm_D, w_glu, w_proj_attn, w_proj_ssm, w_out, norm_ffn_g,
                    w_router_group, b_router_group, w_router_expert, b_router_expert, w_expert_gate, w_expert_up,
                    w_expert_down)
    for layer in range(norm_mix_g.shape[0]):
        x = _layer(x, *[p[layer] for p in layer_params])
    return x
```

```python
import functools
import math

import jax
import jax.numpy as jnp
from jax import lax
from jax.experimental import pallas as pl
from jax.experimental.pallas import tpu as pltpu

F32 = jnp.float32
BF16 = jnp.bfloat16
I32 = jnp.int32

LANES = 128
SUBLANES = 8
MXU_DIM = 256

N_HEADS = 8
HEAD_DIM = 64
SSM_GROUP = 16
SSM_STATE = 64
N_EXPERT_GROUPS = 4
EXPERTS_PER_GROUP = 8
N_EXPERTS = N_EXPERT_GROUPS * EXPERTS_PER_GROUP
EPS = 1e-6
NEG_INF = -1e30
LOG2E = math.log2(math.e)
PLAIN_SOFTMAX_MAX_LOG2 = 60.0
UNDERFLOW_LOG2 = 160.0
BIAS_K_LANE = HEAD_DIM
BIAS_Q_LANE = HEAD_DIM + 3
FORGET_COPIES_STRIDE = LANES // N_HEADS

CHUNK = MXU_DIM // SSM_GROUP
CHUNKS_PER_SUPER = 8
SUPER = CHUNK * CHUNKS_PER_SUPER
GROUPS_PER_BLOCK = LANES // SSM_GROUP
RELAYOUT_UNROLL = 8
MIN_FACTORED_DECAY = -4.0

TM_PROJ = 512
TQ_ATTN = 2048
TS_ATTN = 512
HEADS_PER_STEP = 2
TM_MIX = 512
MIX_CHAINS = 1
TM_EXPERT = 512
TM_DISPATCH = 1024
TM_COMBINE = 1024
SORT_CHUNK = 1024
ZERO_FILL_ROWS = 2048
ROW_DMA_UNROLL = 16
ROUTE_LANES = 2048
ROUTER_ROWS = 48

VMEM_LIMIT = 48 * 1024 * 1024


def _nt_dot(a, b):
    return lax.dot_general(a, b, (((1,), (1,)), ((), ())), preferred_element_type=F32)


def _dot(a, b):
    return jnp.dot(a, b, preferred_element_type=F32)


def _lane_tile(x, n):
    return x if n == 1 else jnp.concatenate([x] * n, axis=1)


def _rmsnorm_rows(x, g):
    ms = jnp.mean(x * x, axis=-1, keepdims=True)
    return x * lax.rsqrt(ms + EPS) * g


def _sigmoid(x):
    return 1.0 / (1.0 + jnp.exp(-x))


def _split3(x):
    hi = x.astype(BF16)
    r1 = x - hi.astype(F32)
    mid = r1.astype(BF16)
    lo = (r1 - mid.astype(F32)).astype(BF16)
    return [hi, mid, lo]


def _expand_heads(z):
    lane = lax.broadcasted_iota(I32, (z.shape[0], LANES), 1)
    blocks = []
    for p in range(z.shape[1] // LANES):
        blk = z[:, p * LANES:(p + 1) * LANES]
        blocks.append(jnp.where(lane < HEAD_DIM, blk, 0.0))
        blocks.append(jnp.where(lane < HEAD_DIM, pltpu.roll(blk, HEAD_DIM, axis=1), 0.0))
    return jnp.concatenate(blocks, axis=1)


def _proj_kernel(x_ref, g_ref, w_ref, wf_ref, bf_ref, qg_ref, kg_ref, bd_ref, ltri_ref, place_ref,
                 ones_ref, shift_ref, q_ref, k_ref, v_ref, u_ref, fedge_ref, carry_ref, *, tiles_per_seq):
    i = pl.program_id(0)

    @pl.when(i % tiles_per_seq == 0)
    def _():
        carry_ref[...] = jnp.zeros_like(carry_ref)

    tm = x_ref.shape[0]
    da = N_HEADS * HEAD_DIM
    hw = N_HEADS * LANES
    xb = _rmsnorm_rows(x_ref[...], g_ref[...]).astype(BF16)
    z = _dot(xb, w_ref[...])
    zq, zk = z[:, :da], z[:, da:2 * da]
    bd = bd_ref[...]
    msq = _dot((zq * zq).astype(BF16), bd)
    msk = _dot((zk * zk).astype(BF16), bd)
    qn = zq * lax.rsqrt(msq + EPS) * qg_ref[...]
    kn = zk * lax.rsqrt(msk + EPS) * kg_ref[...]
    u_ref[...] = z[:, 3 * da:]

    a = _dot(xb, wf_ref[...]) + bf_ref[...]
    logf = jnp.minimum(a, 0.0) - jnp.log(1.0 + jnp.exp(-jnp.abs(a)))
    cs = _dot(ltri_ref[...], jnp.concatenate(_split3(logf), axis=1))
    cum = cs[:, :LANES] + cs[:, LANES:2 * LANES] + cs[:, 2 * LANES:] + carry_ref[0:1, :]
    carry_ref[...] = jnp.broadcast_to(cum[tm - 1:tm, :], carry_ref.shape)
    edge_row = lax.broadcasted_iota(I32, fedge_ref.shape, 0)
    fedge_ref[...] = jnp.where(edge_row == 0, cum[0:1, :], jnp.where(edge_row == 1, cum[tm - 1:tm, :], 0.0))
    copy = lax.broadcasted_iota(I32, cum.shape, 1) % FORGET_COPIES_STRIDE
    bias = jnp.where(copy < 3, (cum - shift_ref[...]) * LOG2E, cum * (-LOG2E))
    hi, mid, lo = [p.astype(F32) for p in _split3(bias)]
    piece = jnp.where(copy % 3 == 0, hi, jnp.where(copy % 3 == 1, mid, lo)).astype(BF16)
    extras = _dot(piece, place_ref[...])
    q_ref[...] = (_expand_heads(qn) + extras[:, :hw] + ones_ref[0:1, :]).astype(BF16)
    k_ref[...] = (_expand_heads(kn) + extras[:, hw:] + ones_ref[1:2, :]).astype(BF16)
    v_ref[...] = (_expand_heads(z[:, 2 * da:3 * da]) + ones_ref[2:3, :]).astype(BF16)


def _proj(x2, norm_g, w_main, wf, b_f, qg, kg, bd, ltri, place, ones, shift, *, seq):
    t, d = x2.shape
    tm = TM_PROJ
    hw = N_HEADS * LANES
    du = w_main.shape[1] - 3 * N_HEADS * HEAD_DIM
    tiles_per_seq = seq // tm
    full = lambda a: pl.BlockSpec(a.shape, lambda i: (0,) * a.ndim)
    row = lambda width: pl.BlockSpec((tm, width), lambda i: (i, 0))
    consts = (norm_g, w_main, wf, b_f, qg, kg, bd, ltri, place, ones, shift)
    return pl.pallas_call(
        functools.partial(_proj_kernel, tiles_per_seq=tiles_per_seq),
        grid=(t // tm,),
        in_specs=[row(d)] + [full(a) for a in consts],
        out_specs=[row(hw), row(hw), row(hw), row(du), pl.BlockSpec((None, SUBLANES, LANES), lambda i: (i, 0, 0))],
        out_shape=[jax.ShapeDtypeStruct((t, hw), BF16)] * 3
        + [jax.ShapeDtypeStruct((t, du), F32), jax.ShapeDtypeStruct((t // tm, SUBLANES, LANES), F32)],
        scratch_shapes=[pltpu.VMEM((SUBLANES, LANES), F32)],
        compiler_params=pltpu.CompilerParams(dimension_semantics=("arbitrary",), vmem_limit_bytes=VMEM_LIMIT),
        name="proj",
    )(x2, *consts)


def _attn_kernel(first_ref, q_ref, k_ref, v_ref, o_ref, acc_scr, m_scr, *, ts, safe):
    i = pl.program_id(2)
    nsub = q_ref.shape[0] // ts
    acc_scr[...] = jnp.zeros(acc_scr.shape, F32)
    if safe:
        m_scr[...] = jnp.full(m_scr.shape, NEG_INF, F32)

    def block(hd, sub, j, masked):
        off = pl.multiple_of(j * ts, ts)
        lanes = slice(hd * LANES, (hd + 1) * LANES)
        chain = hd * nsub + sub
        s = _nt_dot(q_ref[sub * ts:(sub + 1) * ts, lanes], k_ref[pl.ds(off, ts), lanes])
        if masked:
            rows = lax.broadcasted_iota(I32, (ts, ts), 0)
            cols = lax.broadcasted_iota(I32, (ts, ts), 1)
            s = jnp.where(cols <= rows, s, NEG_INF)
        vblk = v_ref[pl.ds(off, ts), lanes]
        if safe:
            m_prev = m_scr[chain]
            m_new = jnp.maximum(m_prev, jnp.max(s, axis=1, keepdims=True))
            p = jnp.exp2(s - _lane_tile(m_new, ts // LANES)).astype(BF16)
            acc_scr[chain] = jnp.exp2(m_prev - m_new) * acc_scr[chain] + _dot(p, vblk)
            m_scr[chain] = m_new
        else:
            acc_scr[chain] += _dot(jnp.exp2(s).astype(BF16), vblk)

    def body(heads, j, carry):
        for hd in heads:
            for sub in range(nsub):
                block(hd, sub, j, False)
        return carry

    step = (pl.program_id(0) * pl.num_programs(1) + pl.program_id(1)) * pl.num_programs(2) + i
    starts = [first_ref[step * HEADS_PER_STEP + hd] for hd in range(HEADS_PER_STEP)]
    shared = functools.reduce(jnp.maximum, starts)
    for hd in range(HEADS_PER_STEP):
        lax.fori_loop(starts[hd], shared, functools.partial(body, (hd,)), 0)
    lax.fori_loop(shared, nsub * i, functools.partial(body, tuple(range(HEADS_PER_STEP))), 0)
    for jj in range(nsub):
        for hd in range(HEADS_PER_STEP):
            for sub in range(jj, nsub):
                block(hd, sub, nsub * i + jj, sub == jj)
    lane = lax.broadcasted_iota(I32, (ts, LANES), 1)
    for sub in range(nsub):
        outs = []
        for hd in range(HEADS_PER_STEP):
            acc = acc_scr[hd * nsub + sub]
            outs.append(acc / acc[:, HEAD_DIM:HEAD_DIM + 1])
        both = jnp.where(lane < HEAD_DIM, outs[0], pltpu.roll(outs[1], HEAD_DIM, axis=1))
        o_ref[sub * ts:(sub + 1) * ts, :] = both.astype(o_ref.dtype)


def _first_kv_block(fedge, b, s, tq, ts):
    nblk = s // ts
    f_first = fedge[:, 0, ::FORGET_COPIES_STRIDE].reshape(b, nblk, N_HEADS).transpose(0, 2, 1)
    f_last = fedge[:, 1, ::FORGET_COPIES_STRIDE].reshape(b, nblk, N_HEADS).transpose(0, 2, 1)
    tile_start = f_first[:, :, ::tq // ts]
    bound = LOG2E * (tile_start[:, :, :, None] - f_last[:, :, None, :])
    first = jnp.sum(bound < -UNDERFLOW_LOG2, axis=-1).astype(I32)
    first = first.reshape(b, N_HEADS // HEADS_PER_STEP, HEADS_PER_STEP, -1).transpose(0, 1, 3, 2)
    return first.reshape(-1)


def _attention(q, k, v, fedge, plain_ok):
    b, s, hw = q.shape
    tq, ts = TQ_ATTN, TS_ATTN
    nq = s // tq
    steps = N_HEADS // HEADS_PER_STEP
    width = HEADS_PER_STEP * LANES
    chains = HEADS_PER_STEP * tq // ts

    def call(safe):
        first = jnp.zeros((b * N_HEADS * nq,), I32) if safe else _first_kv_block(fedge, b, s, tq, ts)
        return pl.pallas_call(
            functools.partial(_attn_kernel, ts=ts, safe=safe),
            grid_spec=pltpu.PrefetchScalarGridSpec(
                num_scalar_prefetch=1,
                grid=(b, steps, nq),
                in_specs=[pl.BlockSpec((None, tq, width), lambda bi, h, i, f: (bi, i, h)),
                          pl.BlockSpec((None, s, width), lambda bi, h, i, f: (bi, 0, h)),
                          pl.BlockSpec((None, s, width), lambda bi, h, i, f: (bi, 0, h))],
                out_specs=pl.BlockSpec((None, tq, LANES), lambda bi, h, i, f: (bi, i, h)),
                scratch_shapes=[pltpu.VMEM((chains, ts, LANES), F32)] * 2),
            out_shape=jax.ShapeDtypeStruct((b, s, steps * LANES), BF16),
            compiler_params=pltpu.CompilerParams(
                dimension_semantics=("arbitrary", "arbitrary", "arbitrary"), vmem_limit_bytes=VMEM_LIMIT),
            name="attn_safe" if safe else "attn",
        )(first, q, k, v)

    return lax.cond(plain_ok, lambda: call(False), lambda: call(True))


def _window_select(pieces, first_group):
    window = lax.broadcasted_iota(I32, pieces[0].shape, 1) // SSM_GROUP
    out = pieces[0]
    for m in range(1, GROUPS_PER_BLOCK):
        out = jnp.where(window == (first_group + m) % GROUPS_PER_BLOCK, pieces[m], out)
    return out


def _s5_kernel(u_ref, win_ref, toep_ref, wout_ref, coef_ref, y_ref, ub_scr, yb_scr, *, nk):
    nj = CHUNKS_PER_SUPER
    gpb = GROUPS_PER_BLOCK
    w = CHUNK * SSM_GROUP
    half = w // 2
    nkk = nk // SUBLANES
    row_stride = SUPER

    def gather_body(it, carry):
        j, kk = it // nkk, it % nkk
        for sp in range(CHUNK // gpb):
            rot = []
            for m in range(gpb):
                row0 = j * CHUNK + sp * gpb + m + kk * (SUBLANES * row_stride)
                src = u_ref[pl.ds(row0, SUBLANES, stride=row_stride), :]
                rot.append(src if m == 0 else pltpu.roll(src, m * SSM_GROUP, axis=1))
            for gl in range(gpb):
                ub_scr[gl, j, pl.ds(kk * SUBLANES, SUBLANES), sp * LANES:(sp + 1) * LANES] = _window_select(rot, gl)
        return carry

    lax.fori_loop(0, nj * nkk, gather_body, 0, unroll=RELAYOUT_UNROLL)

    def swap(val):
        return jnp.concatenate([val[:, half:], val[:, :half]], axis=1)

    def group_body(gl, carry):
        def cmul(val, idx):
            return coef_ref[gl, idx:idx + 1, :] * val + coef_ref[gl, idx + 1:idx + 2, :] * swap(val)

        u = ub_scr[gl].reshape(nj * nk, w).astype(BF16)
        s1 = _dot(u, win_ref[gl]).reshape(nj, nk, w)
        e = jnp.zeros((nk, w), F32)
        local = []
        for j in range(nj):
            local.append(e)
            e = cmul(e, 0) + s1[j]
        kidx = lax.broadcasted_iota(I32, (nk, w), 0)
        x = e
        step, d = 0, 1
        while d < nk:
            shifted = jnp.where(kidx >= d, pltpu.roll(x, d, axis=0), 0.0)
            x = x + cmul(shifted, 2 + 2 * nj + 2 * step)
            step, d = step + 1, d * 2
        x_start = jnp.where(kidx >= 1, pltpu.roll(x, 1, axis=0), 0.0)
        starts = [local[j] + cmul(x_start, 2 + 2 * j) for j in range(nj)]
        p = jnp.concatenate([st[:, :half] for st in starts], axis=0)
        hi = p.astype(BF16)
        lo = (p - hi.astype(F32)).astype(BF16)
        wout_t = wout_ref[gl]
        y = _dot(u, toep_ref[gl]) + _nt_dot(hi, wout_t) + _nt_dot(lo, wout_t)
        yb_scr[gl] = y.reshape(nj, nk, w)
        return carry

    lax.fori_loop(0, gpb, group_body, 0)

    def scatter_body(it, carry):
        j, kk = it // nkk, it % nkk
        for tp in range(CHUNK // gpb):
            src = [yb_scr[gl, j, pl.ds(kk * SUBLANES, SUBLANES), tp * LANES:(tp + 1) * LANES] for gl in range(gpb)]
            for m in range(gpb):
                by_window = [src[(wi - m) % gpb] for wi in range(gpb)]
                window = lax.broadcasted_iota(I32, by_window[0].shape, 1) // SSM_GROUP
                merged = by_window[0]
                for wi in range(1, gpb):
                    merged = jnp.where(window == wi, by_window[wi], merged)
                nat = merged if m == 0 else pltpu.roll(merged, LANES - m * SSM_GROUP, axis=1)
                row0 = j * CHUNK + tp * gpb + m + kk * (SUBLANES * row_stride)
                y_ref[pl.ds(row0, SUBLANES, stride=row_stride), :] = nat
        return carry

    lax.fori_loop(0, nj * nkk, scatter_body, 0, unroll=RELAYOUT_UNROLL)


def _s5_params(a_re, a_im, log_dt, b_re, b_im, c_re, c_im, nk):
    args = (a_re, a_im, log_dt, b_re, b_im, c_re, c_im)
    slowest = jnp.min(a_re * jnp.exp(log_dt)[:, None])
    return lax.cond(slowest > MIN_FACTORED_DECAY, functools.partial(_s5_params_factored, nk=nk),
                    functools.partial(_s5_params_direct, nk=nk), *args)


def _s5_params_factored(a_re, a_im, log_dt, b_re, b_im, c_re, c_im, *, nk):
    g, p = a_re.shape
    c = SSM_GROUP
    gpb = GROUPS_PER_BLOCK
    dt = jnp.exp(log_dt)[:, None]
    adt_r, adt_i = a_re * dt, a_im * dt

    def lam_pow(n):
        nf = jnp.asarray(n, F32)
        nf = (nf[None] if nf.ndim == 1 else nf)[:, :, None]
        mag = jnp.exp(adt_r[:, None, :] * nf)
        ang = adt_i[:, None, :] * nf
        return mag * jnp.cos(ang), mag * jnp.sin(ang)

    gi = jnp.arange(g)[:, None, None] % gpb
    step = ((jnp.arange(CHUNK // gpb)[None, :, None]) * gpb
            + (jnp.arange(gpb)[None, None, :] - gi) % gpb).reshape(g, CHUNK)

    l1r, l1i = lam_pow(jnp.ones((1,)))
    den = a_re * a_re + a_im * a_im
    nr, ni = l1r[:, 0] - 1.0, l1i[:, 0]
    qr = (nr * a_re + ni * a_im) / den
    qi = (ni * a_re - nr * a_im) / den
    bcr = (qr[..., None] * b_re - qi[..., None] * b_im).transpose(0, 2, 1)
    bci = (qr[..., None] * b_im + qi[..., None] * b_re).transpose(0, 2, 1)

    def times_b(pr, pi):
        pr, pi = pr[:, :, None, :], pi[:, :, None, :]
        return pr * bcr[:, None] - pi * bci[:, None], pr * bci[:, None] + pi * bcr[:, None]

    def times_c(pr, pi):
        pr, pi = pr[:, :, None, :], pi[:, :, None, :]
        return c_re[:, None] * pr - c_im[:, None] * pi, c_re[:, None] * pi + c_im[:, None] * pr

    rows = lambda parts: jnp.concatenate(parts, axis=-1).reshape(g, CHUNK * c, -1)
    ir, ii = times_b(*lam_pow(CHUNK - 1 - step))
    win = rows([ir, ii, ii, ir])
    clr, cli = times_c(*lam_pow(step + 1))
    wout_t = rows([clr, -cli])
    ar, ai = times_b(*lam_pow(-step))
    br, bi = times_c(*lam_pow(step))
    toep = jnp.einsum('gnp,gmp->gnm', rows([ar, ai]), rows([br, -bi]), precision=lax.Precision.HIGHEST)
    step_of_lane = jnp.repeat(step, c, axis=1)
    toep = jnp.where(step_of_lane[:, None, :] >= step_of_lane[:, :, None], toep, 0.0)
    return win.astype(BF16), toep.astype(BF16), wout_t.astype(BF16), _s5_scan_coefficients(lam_pow, g, p, nk)


def _s5_scan_coefficients(lam_pow, g, p, nk):
    n_steps = max(nk.bit_length() - 1, 0)
    powers = [CHUNK] + [CHUNK * j for j in range(CHUNKS_PER_SUPER)] + [SUPER * (1 << i) for i in range(n_steps)]
    ar, ai = lam_pow(jnp.array(powers, dtype=jnp.int32))
    c1 = jnp.concatenate([ar, ar, ar, ar], axis=2)
    c2 = jnp.concatenate([-ai, ai, ai, -ai], axis=2)
    coef = jnp.stack([c1, c2], axis=2).reshape(g, 2 * len(powers), 4 * p)
    return jnp.pad(coef, ((0, 0), (0, (-coef.shape[1]) % SUBLANES), (0, 0)))


def _s5_params_direct(a_re, a_im, log_dt, b_re, b_im, c_re, c_im, *, nk):
    g, p = a_re.shape
    c = SSM_GROUP
    hp = lax.Precision.HIGHEST
    dt = jnp.exp(log_dt)[:, None]
    adt_r, adt_i = a_re * dt, a_im * dt

    def lam_pow(n):
        nf = jnp.asarray(n, F32)[None, :, None]
        mag = jnp.exp(adt_r[:, None, :] * nf)
        ang = adt_i[:, None, :] * nf
        return mag * jnp.cos(ang), mag * jnp.sin(ang)

    lr, li = lam_pow(jnp.arange(CHUNK + 1))
    den = a_re * a_re + a_im * a_im
    nr, ni = lr[:, 1] - 1.0, li[:, 1]
    qr = (nr * a_re + ni * a_im) / den
    qi = (ni * a_re - nr * a_im) / den
    bbr = qr[..., None] * b_re - qi[..., None] * b_im
    bbi = qr[..., None] * b_im + qi[..., None] * b_re

    mr = lr[:, :CHUNK, :, None] * bbr[:, None] - li[:, :CHUNK, :, None] * bbi[:, None]
    mi = lr[:, :CHUNK, :, None] * bbi[:, None] + li[:, :CHUNK, :, None] * bbr[:, None]
    kern = (jnp.einsum('gcp,gtpd->gtcd', c_re, mr, precision=hp)
            - jnp.einsum('gcp,gtpd->gtcd', c_im, mi, precision=hp))
    toep = jnp.stack([jnp.pad(kern[:, :CHUNK - s], ((0, 0), (s, 0), (0, 0), (0, 0))) for s in range(CHUNK)],
                     axis=1)
    toep = toep.transpose(0, 1, 4, 2, 3).reshape(g, CHUNK * c, CHUNK * c)
    clr = c_re[:, None] * lr[:, 1:, None, :] - c_im[:, None] * li[:, 1:, None, :]
    cli = c_re[:, None] * li[:, 1:, None, :] + c_im[:, None] * lr[:, 1:, None, :]
    w_re = clr.transpose(0, 3, 1, 2).reshape(g, p, CHUNK * c)
    w_im = (-cli).transpose(0, 3, 1, 2).reshape(g, p, CHUNK * c)
    wout = jnp.concatenate([w_re, w_im], axis=1)
    pr, pi = lr[:, :CHUNK][:, ::-1], li[:, :CHUNK][:, ::-1]
    ir = pr[..., None] * bbr[:, None] - pi[..., None] * bbi[:, None]
    ii = pr[..., None] * bbi[:, None] + pi[..., None] * bbr[:, None]
    ir = ir.transpose(0, 1, 3, 2).reshape(g, CHUNK * c, p)
    ii = ii.transpose(0, 1, 3, 2).reshape(g, CHUNK * c, p)
    win = jnp.concatenate([ir, ii, ii, ir], axis=2)

    gi = jnp.arange(g)[:, None, None] % GROUPS_PER_BLOCK
    si = jnp.arange(CHUNK)[None, :, None]
    ci = jnp.arange(c)[None, None, :]
    lane_of = ((si // GROUPS_PER_BLOCK) * LANES + ((gi + si) % GROUPS_PER_BLOCK) * c + ci).reshape(g, CHUNK * c)
    perm = (lane_of[:, None, :] == jnp.arange(CHUNK * c)[None, :, None]).astype(BF16)
    win = jnp.einsum('gln,gnk->glk', perm, win.astype(BF16))
    toep = jnp.einsum('gln,gnk->glk', perm, toep.astype(BF16))
    toep = jnp.einsum('grn,gln->grl', toep, perm)
    wout_t = jnp.einsum('gln,gpn->glp', perm, wout.astype(BF16))
    return win.astype(BF16), toep.astype(BF16), wout_t.astype(BF16), _s5_scan_coefficients(lam_pow, g, p, nk)


def _s5(u, win, toep, wout_t, coef):
    b, s, dssm = u.shape
    nk = s // SUPER
    nj = CHUNKS_PER_SUPER
    w = CHUNK * SSM_GROUP
    gpb = GROUPS_PER_BLOCK
    nblk = dssm // LANES
    wspec = lambda a: pl.BlockSpec((gpb,) + a.shape[1:], lambda bi, li: (li, 0, 0))
    return pl.pallas_call(
        functools.partial(_s5_kernel, nk=nk),
        grid=(b, nblk),
        in_specs=[pl.BlockSpec((None, s, LANES), lambda bi, li: (bi, 0, li)), wspec(win), wspec(toep),
                  wspec(wout_t), wspec(coef)],
        out_specs=pl.BlockSpec((None, s, LANES), lambda bi, li: (bi, 0, li)),
        out_shape=jax.ShapeDtypeStruct((b, s, dssm), F32),
        scratch_shapes=[pltpu.VMEM((gpb, nj, nk, w), F32)] * 2,
        compiler_params=pltpu.CompilerParams(dimension_semantics=("arbitrary", "arbitrary"),
                                             vmem_limit_bytes=VMEM_LIMIT),
        name="s5",
    )(u, win, toep, wout_t, coef)


def _route(logits):
    e_all = logits[0:N_EXPERTS]
    gl = logits[N_EXPERTS:N_EXPERTS + SUBLANES]
    tm = logits.shape[1]
    ridx = lax.broadcasted_iota(I32, (SUBLANES, tm), 0)
    ge = jnp.exp(gl - jnp.max(gl, axis=0, keepdims=True))
    gp = ge / jnp.sum(ge, axis=0, keepdims=True)
    g_top = jnp.max(gp, axis=0, keepdims=True)
    g_sel = jnp.min(jnp.where(gp == g_top, ridx, SUBLANES), axis=0, keepdims=True)
    e_in = e_all[(N_EXPERT_GROUPS - 1) * SUBLANES:]
    for gi in range(N_EXPERT_GROUPS - 2, -1, -1):
        e_in = jnp.where(g_sel == gi, e_all[gi * SUBLANES:(gi + 1) * SUBLANES], e_in)
    ee = jnp.exp(e_in - jnp.max(e_in, axis=0, keepdims=True))
    ep = ee / jnp.sum(ee, axis=0, keepdims=True)
    v1 = jnp.max(ep, axis=0, keepdims=True)
    i1 = jnp.min(jnp.where(ep == v1, ridx, SUBLANES), axis=0, keepdims=True)
    ep2 = jnp.where(ridx == i1, -1.0, ep)
    v2 = jnp.max(ep2, axis=0, keepdims=True)
    i2 = jnp.min(jnp.where(ep2 == v2, ridx, SUBLANES), axis=0, keepdims=True)
    den = v1 + v2
    w1 = g_top * v1 / den
    w2 = g_top * v2 / den
    e1 = g_sel * EXPERTS_PER_GROUP + i1
    e2 = g_sel * EXPERTS_PER_GROUP + i2
    ids = jnp.where(ridx == 0, e1, jnp.where(ridx == 1, e2, 0))
    wts = jnp.where(ridx == 0, w1, jnp.where(ridx == 1, w2, 0.0))
    return ids, wts


def _mix_kernel(x_ref, ya_ref, ys_ref, u_ref, g1_ref, wgate_ref, dsk_ref, wglu_ref, wpa_ref, wps_ref,
                wout_ref, g2_ref, wrh_ref, wrl_ref, br_ref, x1_ref, xp_ref, logits_ref):
    d = x_ref.shape[1]
    rows = x_ref.shape[0] // MIX_CHAINS
    for chain in range(MIX_CHAINS):
        r = slice(chain * rows, (chain + 1) * rows)
        x = x_ref[r, :]
        xb = _rmsnorm_rows(x, g1_ref[...]).astype(BF16)
        gates = _dot(xb, wgate_ref[...])
        y = ys_ref[r, :].astype(F32) + dsk_ref[...] * u_ref[r, :].astype(F32)
        y = y * (0.5 * (1.0 + jnp.tanh(math.sqrt(2.0 / math.pi) * (y + 0.044715 * (y * y * y)))))
        y = y * _sigmoid(_dot(y.astype(BF16), wglu_ref[...]))
        mixed = (_sigmoid(gates[:, :d]) * _dot(ya_ref[r, :], wpa_ref[...])
                 + _sigmoid(gates[:, d:]) * _dot(y.astype(BF16), wps_ref[...]))
        x1 = x + _dot(mixed.astype(BF16), wout_ref[...])
        x1_ref[r, :] = x1
        xn = _rmsnorm_rows(x1, g2_ref[...])
        xp_ref[r, :] = pltpu.pack_elementwise([xn[:, :d // 2], xn[:, d // 2:]], packed_dtype=BF16)
        xh = xn.astype(BF16)
        xl = (xn - xh.astype(F32)).astype(BF16)
        wrh = wrh_ref[...]
        logits_ref[:, r] = _nt_dot(wrh, xh) + _nt_dot(wrl_ref[...], xh) + _nt_dot(wrh, xl) + br_ref[...]


def _mix(x2, ya, ys, u, g1, wgate, dsk, wglu, wpa, wps, wout, g2, wrh, wrl, br):
    t, d = x2.shape
    tm = TM_MIX
    full = lambda a: pl.BlockSpec(a.shape, lambda i: (0,) * a.ndim)
    row = lambda width: pl.BlockSpec((tm, width), lambda i: (i, 0))
    return pl.pallas_call(
        _mix_kernel,
        grid=(t // tm,),
        in_specs=[row(d), row(ya.shape[1]), row(ys.shape[1]), row(u.shape[1]), full(g1), full(wgate), full(dsk),
                  full(wglu), full(wpa),
                  full(wps), full(wout), full(g2), full(wrh), full(wrl), full(br)],
        out_specs=[row(d), row(d // 2), pl.BlockSpec((ROUTER_ROWS, tm), lambda i: (0, i))],
        out_shape=[jax.ShapeDtypeStruct((t, d), F32), jax.ShapeDtypeStruct((t, d // 2), jnp.uint32),
                   jax.ShapeDtypeStruct((ROUTER_ROWS, t), F32)],
        compiler_params=pltpu.CompilerParams(dimension_semantics=("arbitrary",), vmem_limit_bytes=VMEM_LIMIT),
        name="mix",
    )(x2, ya, ys, u, g1, wgate, dsk, wglu, wpa, wps, wout, g2, wrh, wrl, br)


def _route_kernel(logits_ref, ids_ref, wts_ref):
    ids, wts = _route(logits_ref[...])
    ids_ref[...] = ids
    wts_ref[...] = wts


def _route_call(logits):
    t = logits.shape[1]
    tl = ROUTE_LANES
    col = pl.BlockSpec((SUBLANES, tl), lambda i: (0, i))
    return pl.pallas_call(
        _route_kernel,
        grid=(t // tl,),
        in_specs=[pl.BlockSpec((ROUTER_ROWS, tl), lambda i: (0, i))],
        out_specs=[col, col],
        out_shape=[jax.ShapeDtypeStruct((SUBLANES, t), I32), jax.ShapeDtypeStruct((SUBLANES, t), F32)],
        compiler_params=pltpu.CompilerParams(dimension_semantics=("arbitrary",)),
        name="route",
    )(logits)


def _meta_kernel(ids_ref, tri_ref, dest_ref, tile_ref, *, tile_rows):
    nk, nc, c = ids_ref.shape
    ne = N_EXPERTS
    erow = lax.broadcasted_iota(I32, (ne, c), 0)
    ones = jnp.ones((c, LANES), BF16)

    def onehot(k, ci):
        mask = erow == ids_ref[k, pl.ds(ci, 1), :]
        return mask, jnp.where(mask, 1.0, 0.0).astype(BF16)

    def count_body(n, acc):
        return acc + _dot(onehot(n // nc, n % nc)[1], ones)

    cnt = lax.fori_loop(0, nk * nc, count_body, jnp.zeros((ne, LANES), F32))
    ntiles = jnp.floor((cnt + (tile_rows - 1)) * (1.0 / tile_rows))
    lower = jnp.where(lax.broadcasted_iota(I32, (ne, ne), 1) < lax.broadcasted_iota(I32, (ne, ne), 0), 1.0, 0.0)
    start_tiles = _dot(lower.astype(BF16), ntiles.astype(BF16))
    base = start_tiles * tile_rows

    tri = tri_ref[...]

    def dest_body(n, carry):
        k, ci = n // nc, n % nc
        mask, oh = onehot(k, ci)
        prefix = _dot(oh, tri)
        slot = _lane_tile(base + carry, c // LANES) + prefix - 1.0
        dest = jnp.sum(jnp.where(mask, slot, 0.0), axis=0, keepdims=True)
        dest_ref[k, pl.ds(ci, 1), :] = dest.astype(I32)
        return carry + _dot(oh, ones)

    lax.fori_loop(0, nk * nc, dest_body, jnp.zeros((ne, LANES), F32))

    nt_lanes = tile_ref.shape[1]
    end_tiles = _lane_tile(start_tiles + ntiles, nt_lanes // LANES)
    tidx = lax.broadcasted_iota(I32, (ne, nt_lanes), 1).astype(F32)
    texp = jnp.sum(jnp.where(tidx >= end_tiles, 1.0, 0.0), axis=0, keepdims=True)
    valid = jnp.where(texp < ne, 1, 0)
    texp = jnp.minimum(texp, ne - 1.0).astype(I32)
    ridx = lax.broadcasted_iota(I32, tile_ref.shape, 0)
    tile_ref[...] = jnp.where(ridx == 0, texp, jnp.where(ridx == 1, valid, 0))


def _meta(ids2, tri, n_tiles):
    nk, t = ids2.shape
    c = SORT_CHUNK
    nt_lanes = pl.cdiv(n_tiles, LANES) * LANES
    ids3 = ids2.reshape(nk, t // c, c)
    dest, tile = pl.pallas_call(
        functools.partial(_meta_kernel, tile_rows=TM_EXPERT),
        out_shape=[jax.ShapeDtypeStruct(ids3.shape, I32), jax.ShapeDtypeStruct((SUBLANES, nt_lanes), I32)],
        compiler_params=pltpu.CompilerParams(vmem_limit_bytes=VMEM_LIMIT),
        name="meta",
    )(ids3, tri)
    return dest.reshape(nk * t), tile[0, :n_tiles], tile[1, :n_tiles]


def _row_copy(src_ref, src_row, dst_ref, dst_row, sem):
    return pltpu.make_async_copy(src_ref.at[pl.ds(src_row, 1)], dst_ref.at[pl.ds(dst_row, 1)], sem)


def _dispatch_kernel(dest_ref, xp_ref, xs_in_ref, xs_ref, sem, *, n_tokens):
    del xs_in_ref
    tm = xp_ref.shape[0]
    base = pl.program_id(0) * tm

    def body(r, carry):
        for k in range(2):
            _row_copy(xp_ref, r, xs_ref, dest_ref[k * n_tokens + base + r], sem).start()
        return carry

    lax.fori_loop(0, tm, body, 0, unroll=ROW_DMA_UNROLL)
    for k in range(2):
        pltpu.make_async_copy(xp_ref, xs_ref.at[pl.ds(0, tm)], sem).wait()


def _zero_fill_kernel(o_ref):
    o_ref[...] = jnp.zeros(o_ref.shape, o_ref.dtype)


def _zeros(n_rows, width, dtype):
    rows = n_rows // pl.cdiv(n_rows, ZERO_FILL_ROWS)
    assert n_rows % rows == 0 and rows % SUBLANES == 0
    return pl.pallas_call(
        _zero_fill_kernel,
        grid=(n_rows // rows,),
        out_specs=pl.BlockSpec((rows, width), lambda i: (i, 0)),
        out_shape=jax.ShapeDtypeStruct((n_rows, width), dtype),
        compiler_params=pltpu.CompilerParams(dimension_semantics=("arbitrary",)),
        name="zero_fill",
    )()


def _dispatch(dest, xp, n_slots):
    t, w = xp.shape
    tm = TM_DISPATCH
    xs0 = _zeros(n_slots, w, xp.dtype)
    return pl.pallas_call(
        functools.partial(_dispatch_kernel, n_tokens=t),
        grid_spec=pltpu.PrefetchScalarGridSpec(
            num_scalar_prefetch=1,
            grid=(t // tm,),
            in_specs=[pl.BlockSpec((tm, w), lambda i, d: (i, 0)), pl.BlockSpec(memory_space=pl.ANY)],
            out_specs=pl.BlockSpec(memory_space=pl.ANY),
            scratch_shapes=[pltpu.SemaphoreType.DMA(())]),
        out_shape=jax.ShapeDtypeStruct(xs0.shape, xp.dtype),
        input_output_aliases={2: 0},
        compiler_params=pltpu.CompilerParams(dimension_semantics=("arbitrary",), has_side_effects=True),
        name="dispatch",
    )(dest, xp, xs0)


def _unpack_rows(packed):
    lo = pltpu.unpack_elementwise(packed, index=0, packed_dtype=BF16, unpacked_dtype=F32)
    hi = pltpu.unpack_elementwise(packed, index=1, packed_dtype=BF16, unpacked_dtype=F32)
    return jnp.concatenate([lo, hi], axis=1)


def _expert_kernel(te_ref, tv_ref, nv_ref, xs_ref, wg_ref, wu_ref, wd_ref, ys_ref, wg_bf, wu_bf, wd_bf):
    del nv_ref
    i = pl.program_id(0)
    half = xs_ref.shape[1]

    @pl.when((i == 0) | (te_ref[i] != te_ref[jnp.maximum(i - 1, 0)]))
    def _():
        wg_bf[...] = wg_ref[...].astype(BF16)
        wu_bf[...] = wu_ref[...].astype(BF16)
        wd_bf[...] = wd_ref[...].astype(BF16)

    @pl.when(tv_ref[i] > 0)
    def _():
        x = _unpack_rows(xs_ref[...]).astype(BF16)
        hg = _dot(x, wg_bf[...])
        hu = _dot(x, wu_bf[...])
        h = (hg * _sigmoid(hg) * hu).astype(BF16)
        y = _dot(h, wd_bf[...])
        ys_ref[...] = pltpu.pack_elementwise([y[:, :half], y[:, half:]], packed_dtype=BF16)

    @pl.when(tv_ref[i] == 0)
    def _():
        zero = jnp.zeros(ys_ref.shape, F32)
        ys_ref[...] = pltpu.pack_elementwise([zero, zero], packed_dtype=BF16)


def _experts(tile_expert, tile_valid, xs, wg, wu, wd):
    n_slots, w = xs.shape
    tm = TM_EXPERT
    n_tiles = n_slots // tm
    n_valid = jnp.sum(tile_valid).astype(I32).reshape(1)
    wspec = lambda a: pl.BlockSpec((None,) + a.shape[1:], lambda i, te, tv, nv: (te[i], 0, 0))
    rows_in = pl.BlockSpec((tm, w), lambda i, te, tv, nv: (jnp.minimum(i, nv[0] - 1), 0))
    rows_out = pl.BlockSpec((tm, w), lambda i, te, tv, nv: (i, 0))
    return pl.pallas_call(
        _expert_kernel,
        grid_spec=pltpu.PrefetchScalarGridSpec(
            num_scalar_prefetch=3,
            grid=(n_tiles,),
            in_specs=[rows_in, wspec(wg), wspec(wu), wspec(wd)],
            out_specs=rows_out,
            scratch_shapes=[pltpu.VMEM(a.shape[1:], BF16) for a in (wg, wu, wd)]),
        out_shape=jax.ShapeDtypeStruct(xs.shape, xs.dtype),
        compiler_params=pltpu.CompilerParams(dimension_semantics=("arbitrary",), vmem_limit_bytes=VMEM_LIMIT),
        name="experts",
    )(tile_expert, tile_valid, n_valid, xs, wg, wu, wd)


def _combine_kernel(dest_ref, x1_ref, w0_ref, w1_ref, ys_ref, out_ref, buf, sem, *, n_tokens):
    tm, d = x1_ref.shape
    base = pl.program_id(0) * tm

    def body(r, carry):
        for k in range(2):
            _row_copy(ys_ref, dest_ref[k * n_tokens + base + r], buf.at[k], r, sem).start()
        return carry

    lax.fori_loop(0, tm, body, 0, unroll=ROW_DMA_UNROLL)
    for k in range(2):
        pltpu.make_async_copy(ys_ref.at[pl.ds(0, tm)], buf.at[k], sem).wait()
    reps = d // LANES
    out_ref[...] = (x1_ref[...]
                    + _lane_tile(w0_ref[...], reps) * _unpack_rows(buf[0])
                    + _lane_tile(w1_ref[...], reps) * _unpack_rows(buf[1]))


def _combine(dest, x1, w0, w1, ys):
    t, d = x1.shape
    tm = TM_COMBINE
    row = lambda width: pl.BlockSpec((tm, width), lambda i, dref: (i, 0))
    return pl.pallas_call(
        functools.partial(_combine_kernel, n_tokens=t),
        grid_spec=pltpu.PrefetchScalarGridSpec(
            num_scalar_prefetch=1,
            grid=(t // tm,),
            in_specs=[row(d), row(LANES), row(LANES), pl.BlockSpec(memory_space=pl.ANY)],
            out_specs=row(d),
            scratch_shapes=[pltpu.VMEM((2, tm, ys.shape[-1]), ys.dtype), pltpu.SemaphoreType.DMA(())]),
        out_shape=jax.ShapeDtypeStruct((t, d), x1.dtype),
        compiler_params=pltpu.CompilerParams(dimension_semantics=("arbitrary",), vmem_limit_bytes=VMEM_LIMIT),
        name="combine",
    )(dest, x1, w0, w1, ys)


def _lower_tri(n):
    return jnp.where(jnp.arange(n)[:, None] >= jnp.arange(n)[None, :], 1.0, 0.0).astype(BF16)


def _upper_tri(n):
    return jnp.where(jnp.arange(n)[:, None] <= jnp.arange(n)[None, :], 1.0, 0.0).astype(BF16)


def _split_bf16(w):
    hi = w.astype(BF16)
    lo = (w - hi.astype(F32)).astype(BF16)
    return hi, lo


def _layer(x, norm_mix_g, w_in, b_forget, q_norm_g, k_norm_g, ssm_A_re, ssm_A_im, ssm_log_dt, ssm_B_re, ssm_B_im,
           ssm_C_re, ssm_C_im, ssm_D, w_glu, w_proj_attn, w_proj_ssm, w_out, norm_ffn_g, w_router_group,
           b_router_group, w_router_expert, b_router_expert, w_expert_gate, w_expert_up, w_expert_down):
    b, s, d = x.shape
    t = b * s
    da = N_HEADS * HEAD_DIM
    dssm = ssm_D.shape[0]
    nk = s // SUPER
    assert s % SUPER == 0 and nk % SUBLANES == 0 and nk & (nk - 1) == 0 and s % TQ_ATTN == 0 and t % TM_PROJ == 0
    assert TM_PROJ == TS_ATTN
    x2 = x.reshape(t, d)

    o_f = 3 * da
    o_u = o_f + N_HEADS
    o_g = o_u + dssm
    w_main = jnp.concatenate([w_in[:, :o_f], w_in[:, o_u:o_g]], axis=1).astype(BF16)
    wf = jnp.repeat(w_in[:, o_f:o_u], FORGET_COPIES_STRIDE, axis=1).astype(BF16)
    bf = jnp.repeat(b_forget, FORGET_COPIES_STRIDE)[None, :]
    w_gates = w_in[:, o_g:].astype(BF16)
    qg = (jnp.tile(q_norm_g, N_HEADS) * (HEAD_DIM ** -0.5 * LOG2E))[None, :]
    kg = jnp.tile(k_norm_g, N_HEADS)[None, :]
    head_of = jnp.arange(da) // HEAD_DIM
    bd = jnp.where(head_of[:, None] == head_of[None, :], 1.0 / HEAD_DIM, 0.0).astype(BF16)
    logit_bound = 1.02 * HEAD_DIM ** 0.5 * jnp.max(jnp.abs(q_norm_g)) * jnp.max(jnp.abs(k_norm_g))
    plain_ok = LOG2E * logit_bound <= PLAIN_SOFTMAX_MAX_LOG2
    shift = jnp.full((1, LANES), logit_bound, F32)
    hw = N_HEADS * LANES
    lane_in_head = jnp.arange(hw) % LANES
    ones = jnp.stack([
        (lane_in_head >= BIAS_K_LANE) & (lane_in_head < BIAS_K_LANE + 3),
        (lane_in_head >= BIAS_Q_LANE) & (lane_in_head < BIAS_Q_LANE + 3),
        lane_in_head == HEAD_DIM]
        + [jnp.zeros((hw,), bool)] * (SUBLANES - 3)).astype(F32)
    prow = jnp.arange(LANES)
    head, copy = prow // FORGET_COPIES_STRIDE, prow % FORGET_COPIES_STRIDE
    target = jnp.where(copy < 3, head * LANES + BIAS_Q_LANE + copy, hw + head * LANES + BIAS_K_LANE + copy - 3)
    place = ((jnp.arange(2 * hw)[None, :] == target[:, None]) & (copy < 6)[:, None]).astype(BF16)

    q, k, v, u, fedge = _proj(x2, norm_mix_g[None, :], w_main, wf, bf, qg, kg, bd, _lower_tri(TM_PROJ), place, ones,
                              shift, seq=s)
    y_attn = _attention(q.reshape(b, s, hw), k.reshape(b, s, hw), v.reshape(b, s, hw), fedge, plain_ok)

    y_ssm = _s5(u.reshape(b, s, dssm),
                *_s5_params(ssm_A_re, ssm_A_im, ssm_log_dt, ssm_B_re, ssm_B_im, ssm_C_re, ssm_C_im, s // SUPER))

    wr = jnp.concatenate([w_router_expert.T, w_router_group.T,
                          jnp.zeros((ROUTER_ROWS - N_EXPERTS - N_EXPERT_GROUPS, d), F32)], axis=0)
    br = jnp.concatenate([b_router_expert, b_router_group,
                          jnp.full((SUBLANES - N_EXPERT_GROUPS,), NEG_INF, F32),
                          jnp.zeros((ROUTER_ROWS - N_EXPERTS - SUBLANES,), F32)])[:, None]
    x1, xp, logits = _mix(x2, y_attn.reshape(t, da), y_ssm.reshape(t, dssm), u, norm_mix_g[None, :], w_gates,
                          ssm_D[None, :], w_glu.astype(BF16), w_proj_attn.astype(BF16), w_proj_ssm.astype(BF16),
                          w_out.astype(BF16), norm_ffn_g[None, :], *_split_bf16(wr), br)
    ids, wts = _route_call(logits)

    n_tiles = (2 * t) // TM_EXPERT + N_EXPERTS
    dest, tile_expert, tile_valid = _meta(ids[:2], _upper_tri(SORT_CHUNK), n_tiles)
    xs = _dispatch(dest, xp, n_tiles * TM_EXPERT)
    ys = _experts(tile_expert, tile_valid, xs, w_expert_gate, w_expert_up, w_expert_down)
    w0 = jnp.broadcast_to(wts[0][:, None], (t, LANES))
    w1 = jnp.broadcast_to(wts[1][:, None], (t, LANES))
    out = _combine(dest, x1, w0, w1, ys)
    return out.reshape(b, s, d)


def kernel(x, norm_mix_g, w_in, b_forget, q_norm_g, k_norm_g, ssm_A_re, ssm_A_im, ssm_log_dt, ssm_B_re, ssm_B_im,
           ssm_C_re, ssm_C_im, ssm_D, w_glu, w_proj_attn, w_proj_ssm, w_out, norm_ffn_g, w_router_group,
           b_router_group, w_router_expert, b_router_expert, w_expert_gate, w_expert_up, w_expert_down):
    layer_params = (norm_mix_g, w_in, b_forget, q_norm_g, k_norm_g, ssm_A_re, ssm_A_im, ssm_log_dt, ssm_B_re,
                    ssm_B_im, ssm_C_re, ssm_C_im, ssm_D, w_glu, w_proj_attn, w_proj_ssm, w_out, norm_ffn_g,
                    w_router_group, b_router_group, w_router_expert, b_router_expert, w_expert_gate, w_expert_up,
                    w_expert_down)
    for layer in range(norm_mix_g.shape[0]):
        x = _layer(x, *[p[layer] for p in layer_params])
    return x
```

```python
import functools
import math

import jax
import jax.numpy as jnp
from jax import lax
from jax.experimental import pallas as pl
from jax.experimental.pallas import tpu as pltpu

F32 = jnp.float32
BF16 = jnp.bfloat16
I32 = jnp.int32

LANES = 128
SUBLANES = 8
MXU_DIM = 256

N_HEADS = 8
HEAD_DIM = 64
SSM_GROUP = 16
SSM_STATE = 64
N_EXPERT_GROUPS = 4
EXPERTS_PER_GROUP = 8
N_EXPERTS = N_EXPERT_GROUPS * EXPERTS_PER_GROUP
EPS = 1e-6
NEG_INF = -1e30
LOG2E = math.log2(math.e)
PLAIN_SOFTMAX_MAX_LOG2 = 60.0
UNDERFLOW_LOG2 = 160.0
BIAS_K_LANE = HEAD_DIM
BIAS_Q_LANE = HEAD_DIM + 3
FORGET_COPIES_STRIDE = LANES // N_HEADS

CHUNK = MXU_DIM // SSM_GROUP
CHUNKS_PER_SUPER = 8
SUPER = CHUNK * CHUNKS_PER_SUPER
GROUPS_PER_BLOCK = LANES // SSM_GROUP
RELAYOUT_UNROLL = 8
MIN_FACTORED_DECAY = -4.0

TM_PROJ = 512
TQ_ATTN = 2048
TS_ATTN = 512
HEADS_PER_STEP = 2
TM_MIX = 512
MIX_CHAINS = 1
TM_EXPERT = 512
TM_DISPATCH = 1024
TM_COMBINE = 1024
SORT_CHUNK = 1024
ZERO_FILL_ROWS = 2048
ROW_DMA_UNROLL = 16
ROUTE_LANES = 2048
ROUTER_ROWS = 48

VMEM_LIMIT = 48 * 1024 * 1024


def _nt_dot(a, b):
    return lax.dot_general(a, b, (((1,), (1,)), ((), ())), preferred_element_type=F32)


def _dot(a, b):
    return jnp.dot(a, b, preferred_element_type=F32)


def _lane_tile(x, n):
    return x if n == 1 else jnp.concatenate([x] * n, axis=1)


def _rmsnorm_rows(x, g):
    ms = jnp.mean(x * x, axis=-1, keepdims=True)
    return x * lax.rsqrt(ms + EPS) * g


def _sigmoid(x):
    return 1.0 / (1.0 + jnp.exp(-x))


def _split3(x):
    hi = x.astype(BF16)
    r1 = x - hi.astype(F32)
    mid = r1.astype(BF16)
    lo = (r1 - mid.astype(F32)).astype(BF16)
    return [hi, mid, lo]


def _expand_heads(z):
    lane = lax.broadcasted_iota(I32, (z.shape[0], LANES), 1)
    blocks = []
    for p in range(z.shape[1] // LANES):
        blk = z[:, p * LANES:(p + 1) * LANES]
        blocks.append(jnp.where(lane < HEAD_DIM, blk, 0.0))
        blocks.append(jnp.where(lane < HEAD_DIM, pltpu.roll(blk, HEAD_DIM, axis=1), 0.0))
    return jnp.concatenate(blocks, axis=1)


def _proj_kernel(x_ref, g_ref, w_ref, wf_ref, bf_ref, qg_ref, kg_ref, bd_ref, ltri_ref, place_ref,
                 ones_ref, shift_ref, q_ref, k_ref, v_ref, u_ref, fedge_ref, carry_ref, *, tiles_per_seq):
    i = pl.program_id(0)

    @pl.when(i % tiles_per_seq == 0)
    def _():
        carry_ref[...] = jnp.zeros_like(carry_ref)

    tm = x_ref.shape[0]
    da = N_HEADS * HEAD_DIM
    hw = N_HEADS * LANES
    xb = _rmsnorm_rows(x_ref[...], g_ref[...]).astype(BF16)
    z = _dot(xb, w_ref[...])
    zq, zk = z[:, :da], z[:, da:2 * da]
    bd = bd_ref[...]
    msq = _dot((zq * zq).astype(BF16), bd)
    msk = _dot((zk * zk).astype(BF16), bd)
    qn = zq * lax.rsqrt(msq + EPS) * qg_ref[...]
    kn = zk * lax.rsqrt(msk + EPS) * kg_ref[...]
    u_ref[...] = z[:, 3 * da:]

    a = _dot(xb, wf_ref[...]) + bf_ref[...]
    logf = jnp.minimum(a, 0.0) - jnp.log(1.0 + jnp.exp(-jnp.abs(a)))
    cs = _dot(ltri_ref[...], jnp.concatenate(_split3(logf), axis=1))
    cum = cs[:, :LANES] + cs[:, LANES:2 * LANES] + cs[:, 2 * LANES:] + carry_ref[0:1, :]
    carry_ref[...] = jnp.broadcast_to(cum[tm - 1:tm, :], carry_ref.shape)
    edge_row = lax.broadcasted_iota(I32, fedge_ref.shape, 0)
    fedge_ref[...] = jnp.where(edge_row == 0, cum[0:1, :], jnp.where(edge_row == 1, cum[tm - 1:tm, :], 0.0))
    copy = lax.broadcasted_iota(I32, cum.shape, 1) % FORGET_COPIES_STRIDE
    bias = jnp.where(copy < 3, (cum - shift_ref[...]) * LOG2E, cum * (-LOG2E))
    hi, mid, lo = [p.astype(F32) for p in _split3(bias)]
    piece = jnp.where(copy % 3 == 0, hi, jnp.where(copy % 3 == 1, mid, lo)).astype(BF16)
    extras = _dot(piece, place_ref[...])
    q_ref[...] = (_expand_heads(qn) + extras[:, :hw] + ones_ref[0:1, :]).astype(BF16)
    k_ref[...] = (_expand_heads(kn) + extras[:, hw:] + ones_ref[1:2, :]).astype(BF16)
    v_ref[...] = (_expand_heads(z[:, 2 * da:3 * da]) + ones_ref[2:3, :]).astype(BF16)


def _proj(x2, norm_g, w_main, wf, b_f, qg, kg, bd, ltri, place, ones, shift, *, seq):
    t, d = x2.shape
    tm = TM_PROJ
    hw = N_HEADS * LANES
    du = w_main.shape[1] - 3 * N_HEADS * HEAD_DIM
    tiles_per_seq = seq // tm
    full = lambda a: pl.BlockSpec(a.shape, lambda i: (0,) * a.ndim)
    row = lambda width: pl.BlockSpec((tm, width), lambda i: (i, 0))
    consts = (norm_g, w_main, wf, b_f, qg, kg, bd, ltri, place, ones, shift)
    return pl.pallas_call(
        functools.partial(_proj_kernel, tiles_per_seq=tiles_per_seq),
        grid=(t // tm,),
        in_specs=[row(d)] + [full(a) for a in consts],
        out_specs=[row(hw), row(hw), row(hw), row(du), pl.BlockSpec((None, SUBLANES, LANES), lambda i: (i, 0, 0))],
        out_shape=[jax.ShapeDtypeStruct((t, hw), BF16)] * 3
        + [jax.ShapeDtypeStruct((t, du), F32), jax.ShapeDtypeStruct((t // tm, SUBLANES, LANES), F32)],
        scratch_shapes=[pltpu.VMEM((SUBLANES, LANES), F32)],
        compiler_params=pltpu.CompilerParams(dimension_semantics=("arbitrary",), vmem_limit_bytes=VMEM_LIMIT),
        name="proj",
    )(x2, *consts)


def _attn_kernel(first_ref, q_ref, k_ref, v_ref, o_ref, acc_scr, m_scr, *, ts, safe):
    i = pl.program_id(2)
    nsub = q_ref.shape[0] // ts
    acc_scr[...] = jnp.zeros(acc_scr.shape, F32)
    if safe:
        m_scr[...] = jnp.full(m_scr.shape, NEG_INF, F32)

    def block(hd, sub, j, masked):
        off = pl.multiple_of(j * ts, ts)
        lanes = slice(hd * LANES, (hd + 1) * LANES)
        chain = hd * nsub + sub
        s = _nt_dot(q_ref[sub * ts:(sub + 1) * ts, lanes], k_ref[pl.ds(off, ts), lanes])
        if masked:
            rows = lax.broadcasted_iota(I32, (ts, ts), 0)
            cols = lax.broadcasted_iota(I32, (ts, ts), 1)
            s = jnp.where(cols <= rows, s, NEG_INF)
        vblk = v_ref[pl.ds(off, ts), lanes]
        if safe:
            m_prev = m_scr[chain]
            m_new = jnp.maximum(m_prev, jnp.max(s, axis=1, keepdims=True))
            p = jnp.exp2(s - _lane_tile(m_new, ts // LANES)).astype(BF16)
            acc_scr[chain] = jnp.exp2(m_prev - m_new) * acc_scr[chain] + _dot(p, vblk)
            m_scr[chain] = m_new
        else:
            acc_scr[chain] += _dot(jnp.exp2(s).astype(BF16), vblk)

    def body(heads, j, carry):
        for hd in heads:
            for sub in range(nsub):
                block(hd, sub, j, False)
        return carry

    step = (pl.program_id(0) * pl.num_programs(1) + pl.program_id(1)) * pl.num_programs(2) + i
    starts = [first_ref[step * HEADS_PER_STEP + hd] for hd in range(HEADS_PER_STEP)]
    shared = functools.reduce(jnp.maximum, starts)
    for hd in range(HEADS_PER_STEP):
        lax.fori_loop(starts[hd], shared, functools.partial(body, (hd,)), 0)
    lax.fori_loop(shared, nsub * i, functools.partial(body, tuple(range(HEADS_PER_STEP))), 0)
    for jj in range(nsub):
        for hd in range(HEADS_PER_STEP):
            for sub in range(jj, nsub):
                block(hd, sub, nsub * i + jj, sub == jj)
    lane = lax.broadcasted_iota(I32, (ts, LANES), 1)
    for sub in range(nsub):
        outs = []
        for hd in range(HEADS_PER_STEP):
            acc = acc_scr[hd * nsub + sub]
            outs.append(acc / acc[:, HEAD_DIM:HEAD_DIM + 1])
        both = jnp.where(lane < HEAD_DIM, outs[0], pltpu.roll(outs[1], HEAD_DIM, axis=1))
        o_ref[sub * ts:(sub + 1) * ts, :] = both.astype(o_ref.dtype)


def _first_kv_block(fedge, b, s, tq, ts):
    nblk = s // ts
    f_first = fedge[:, 0, ::FORGET_COPIES_STRIDE].reshape(b, nblk, N_HEADS).transpose(0, 2, 1)
    f_last = fedge[:, 1, ::FORGET_COPIES_STRIDE].reshape(b, nblk, N_HEADS).transpose(0, 2, 1)
    tile_start = f_first[:, :, ::tq // ts]
    bound = LOG2E * (tile_start[:, :, :, None] - f_last[:, :, None, :])
    first = jnp.sum(bound < -UNDERFLOW_LOG2, axis=-1).astype(I32)
    first = first.reshape(b, N_HEADS // HEADS_PER_STEP, HEADS_PER_STEP, -1).transpose(0, 1, 3, 2)
    return first.reshape(-1)


def _attention(q, k, v, fedge, plain_ok):
    b, s, hw = q.shape
    tq, ts = TQ_ATTN, TS_ATTN
    nq = s // tq
    steps = N_HEADS // HEADS_PER_STEP
    width = HEADS_PER_STEP * LANES
    chains = HEADS_PER_STEP * tq // ts

    def call(safe):
        first = jnp.zeros((b * N_HEADS * nq,), I32) if safe else _first_kv_block(fedge, b, s, tq, ts)
        return pl.pallas_call(
            functools.partial(_attn_kernel, ts=ts, safe=safe),
            grid_spec=pltpu.PrefetchScalarGridSpec(
                num_scalar_prefetch=1,
                grid=(b, steps, nq),
                in_specs=[pl.BlockSpec((None, tq, width), lambda bi, h, i, f: (bi, i, h)),
                          pl.BlockSpec((None, s, width), lambda bi, h, i, f: (bi, 0, h)),
                          pl.BlockSpec((None, s, width), lambda bi, h, i, f: (bi, 0, h))],
                out_specs=pl.BlockSpec((None, tq, LANES), lambda bi, h, i, f: (bi, i, h)),
                scratch_shapes=[pltpu.VMEM((chains, ts, LANES), F32)] * 2),
            out_shape=jax.ShapeDtypeStruct((b, s, steps * LANES), BF16),
            compiler_params=pltpu.CompilerParams(
                dimension_semantics=("arbitrary", "arbitrary", "arbitrary"), vmem_limit_bytes=VMEM_LIMIT),
            name="attn_safe" if safe else "attn",
        )(first, q, k, v)

    return lax.cond(plain_ok, lambda: call(False), lambda: call(True))


def _window_select(pieces, first_group):
    window = lax.broadcasted_iota(I32, pieces[0].shape, 1) // SSM_GROUP
    out = pieces[0]
    for m in range(1, GROUPS_PER_BLOCK):
        out = jnp.where(window == (first_group + m) % GROUPS_PER_BLOCK, pieces[m], out)
    return out


def _s5_kernel(u_ref, win_ref, toep_ref, wout_ref, coef_ref, y_ref, ub_scr, yb_scr, *, nk):
    nj = CHUNKS_PER_SUPER
    gpb = GROUPS_PER_BLOCK
    w = CHUNK * SSM_GROUP
    half = w // 2
    nkk = nk // SUBLANES
    row_stride = SUPER

    def gather_body(it, carry):
        j, kk = it // nkk, it % nkk
        for sp in range(CHUNK // gpb):
            rot = []
            for m in range(gpb):
                row0 = j * CHUNK + sp * gpb + m + kk * (SUBLANES * row_stride)
                src = u_ref[pl.ds(row0, SUBLANES, stride=row_stride), :]
                rot.append(src if m == 0 else pltpu.roll(src, m * SSM_GROUP, axis=1))
            for gl in range(gpb):
                ub_scr[gl, j, pl.ds(kk * SUBLANES, SUBLANES), sp * LANES:(sp + 1) * LANES] = _window_select(rot, gl)
        return carry

    lax.fori_loop(0, nj * nkk, gather_body, 0, unroll=RELAYOUT_UNROLL)

    def swap(val):
        return jnp.concatenate([val[:, half:], val[:, :half]], axis=1)

    def group_body(gl, carry):
        def cmul(val, idx):
            return coef_ref[gl, idx:idx + 1, :] * val + coef_ref[gl, idx + 1:idx + 2, :] * swap(val)

        u = ub_scr[gl].reshape(nj * nk, w).astype(BF16)
        s1 = _dot(u, win_ref[gl]).reshape(nj, nk, w)
        e = jnp.zeros((nk, w), F32)
        local = []
        for j in range(nj):
            local.append(e)
            e = cmul(e, 0) + s1[j]
        kidx = lax.broadcasted_iota(I32, (nk, w), 0)
        x = e
        step, d = 0, 1
        while d < nk:
            shifted = jnp.where(kidx >= d, pltpu.roll(x, d, axis=0), 0.0)
            x = x + cmul(shifted, 2 + 2 * nj + 2 * step)
            step, d = step + 1, d * 2
        x_start = jnp.where(kidx >= 1, pltpu.roll(x, 1, axis=0), 0.0)
        starts = [local[j] + cmul(x_start, 2 + 2 * j) for j in range(nj)]
        p = jnp.concatenate([st[:, :half] for st in starts], axis=0)
        hi = p.astype(BF16)
        lo = (p - hi.astype(F32)).astype(BF16)
        wout_t = wout_ref[gl]
        y = _dot(u, toep_ref[gl]) + _nt_dot(hi, wout_t) + _nt_dot(lo, wout_t)
        yb_scr[gl] = y.reshape(nj, nk, w)
        return carry

    lax.fori_loop(0, gpb, group_body, 0)

    def scatter_body(it, carry):
        j, kk = it // nkk, it % nkk
        for tp in range(CHUNK // gpb):
            src = [yb_scr[gl, j, pl.ds(kk * SUBLANES, SUBLANES), tp * LANES:(tp + 1) * LANES] for gl in range(gpb)]
            for m in range(gpb):
                by_window = [src[(wi - m) % gpb] for wi in range(gpb)]
                window = lax.broadcasted_iota(I32, by_window[0].shape, 1) // SSM_GROUP
                merged = by_window[0]
                for wi in range(1, gpb):
                    merged = jnp.where(window == wi, by_window[wi], merged)
                nat = merged if m == 0 else pltpu.roll(merged, LANES - m * SSM_GROUP, axis=1)
                row0 = j * CHUNK + tp * gpb + m + kk * (SUBLANES * row_stride)
                y_ref[pl.ds(row0, SUBLANES, stride=row_stride), :] = nat
        return carry

    lax.fori_loop(0, nj * nkk, scatter_body, 0, unroll=RELAYOUT_UNROLL)


def _s5_params(a_re, a_im, log_dt, b_re, b_im, c_re, c_im, nk):
    args = (a_re, a_im, log_dt, b_re, b_im, c_re, c_im)
    slowest = jnp.min(a_re * jnp.exp(log_dt)[:, None])
    return lax.cond(slowest > MIN_FACTORED_DECAY, functools.partial(_s5_params_factored, nk=nk),
                    functools.partial(_s5_params_direct, nk=nk), *args)


def _s5_params_factored(a_re, a_im, log_dt, b_re, b_im, c_re, c_im, *, nk):
    g, p = a_re.shape
    c = SSM_GROUP
    gpb = GROUPS_PER_BLOCK
    dt = jnp.exp(log_dt)[:, None]
    adt_r, adt_i = a_re * dt, a_im * dt

    def lam_pow(n):
        nf = jnp.asarray(n, F32)
        nf = (nf[None] if nf.ndim == 1 else nf)[:, :, None]
        mag = jnp.exp(adt_r[:, None, :] * nf)
        ang = adt_i[:, None, :] * nf
        return mag * jnp.cos(ang), mag * jnp.sin(ang)

    gi = jnp.arange(g)[:, None, None] % gpb
    step = ((jnp.arange(CHUNK // gpb)[None, :, None]) * gpb
            + (jnp.arange(gpb)[None, None, :] - gi) % gpb).reshape(g, CHUNK)

    l1r, l1i = lam_pow(jnp.ones((1,)))
    den = a_re * a_re + a_im * a_im
    nr, ni = l1r[:, 0] - 1.0, l1i[:, 0]
    qr = (nr * a_re + ni * a_im) / den
    qi = (ni * a_re - nr * a_im) / den
    bcr = (qr[..., None] * b_re - qi[..., None] * b_im).transpose(0, 2, 1)
    bci = (qr[..., None] * b_im + qi[..., None] * b_re).transpose(0, 2, 1)

    def times_b(pr, pi):
        pr, pi = pr[:, :, None, :], pi[:, :, None, :]
        return pr * bcr[:, None] - pi * bci[:, None], pr * bci[:, None] + pi * bcr[:, None]

    def times_c(pr, pi):
        pr, pi = pr[:, :, None, :], pi[:, :, None, :]
        return c_re[:, None] * pr - c_im[:, None] * pi, c_re[:, None] * pi + c_im[:, None] * pr

    rows = lambda parts: jnp.concatenate(parts, axis=-1).reshape(g, CHUNK * c, -1)
    ir, ii = times_b(*lam_pow(CHUNK - 1 - step))
    win = rows([ir, ii, ii, ir])
    clr, cli = times_c(*lam_pow(step + 1))
    wout_t = rows([clr, -cli])
    ar, ai = times_b(*lam_pow(-step))
    br, bi = times_c(*lam_pow(step))
    toep = jnp.einsum('gnp,gmp->gnm', rows([ar, ai]), rows([br, -bi]), precision=lax.Precision.HIGHEST)
    step_of_lane = jnp.repeat(step, c, axis=1)
    toep = jnp.where(step_of_lane[:, None, :] >= step_of_lane[:, :, None], toep, 0.0)
    return win.astype(BF16), toep.astype(BF16), wout_t.astype(BF16), _s5_scan_coefficients(lam_pow, g, p, nk)


def _s5_scan_coefficients(lam_pow, g, p, nk):
    n_steps = max(nk.bit_length() - 1, 0)
    powers = [CHUNK] + [CHUNK * j for j in range(CHUNKS_PER_SUPER)] + [SUPER * (1 << i) for i in range(n_steps)]
    ar, ai = lam_pow(jnp.array(powers, dtype=jnp.int32))
    c1 = jnp.concatenate([ar, ar, ar, ar], axis=2)
    c2 = jnp.concatenate([-ai, ai, ai, -ai], axis=2)
    coef = jnp.stack([c1, c2], axis=2).reshape(g, 2 * len(powers), 4 * p)
    return jnp.pad(coef, ((0, 0), (0, (-coef.shape[1]) % SUBLANES), (0, 0)))


def _s5_params_direct(a_re, a_im, log_dt, b_re, b_im, c_re, c_im, *, nk):
    g, p = a_re.shape
    c = SSM_GROUP
    hp = lax.Precision.HIGHEST
    dt = jnp.exp(log_dt)[:, None]
    adt_r, adt_i = a_re * dt, a_im * dt

    def lam_pow(n):
        nf = jnp.asarray(n, F32)[None, :, None]
        mag = jnp.exp(adt_r[:, None, :] * nf)
        ang = adt_i[:, None, :] * nf
        return mag * jnp.cos(ang), mag * jnp.sin(ang)

    lr, li = lam_pow(jnp.arange(CHUNK + 1))
    den = a_re * a_re + a_im * a_im
    nr, ni = lr[:, 1] - 1.0, li[:, 1]
    qr = (nr * a_re + ni * a_im) / den
    qi = (ni * a_re - nr * a_im) / den
    bbr = qr[..., None] * b_re - qi[..., None] * b_im
    bbi = qr[..., None] * b_im + qi[..., None] * b_re

    mr = lr[:, :CHUNK, :, None] * bbr[:, None] - li[:, :CHUNK, :, None] * bbi[:, None]
    mi = lr[:, :CHUNK, :, None] * bbi[:, None] + li[:, :CHUNK, :, None] * bbr[:, None]
    kern = (jnp.einsum('gcp,gtpd->gtcd', c_re, mr, precision=hp)
            - jnp.einsum('gcp,gtpd->gtcd', c_im, mi, precision=hp))
    toep = jnp.stack([jnp.pad(kern[:, :CHUNK - s], ((0, 0), (s, 0), (0, 0), (0, 0))) for s in range(CHUNK)],
                     axis=1)
    toep = toep.transpose(0, 1, 4, 2, 3).reshape(g, CHUNK * c, CHUNK * c)
    clr = c_re[:, None] * lr[:, 1:, None, :] - c_im[:, None] * li[:, 1:, None, :]
    cli = c_re[:, None] * li[:, 1:, None, :] + c_im[:, None] * lr[:, 1:, None, :]
    w_re = clr.transpose(0, 3, 1, 2).reshape(g, p, CHUNK * c)
    w_im = (-cli).transpose(0, 3, 1, 2).reshape(g, p, CHUNK * c)
    wout = jnp.concatenate([w_re, w_im], axis=1)
    pr, pi = lr[:, :CHUNK][:, ::-1], li[:, :CHUNK][:, ::-1]
    ir = pr[..., None] * bbr[:, None] - pi[..., None] * bbi[:, None]
    ii = pr[..., None] * bbi[:, None] + pi[..., None] * bbr[:, None]
    ir = ir.transpose(0, 1, 3, 2).reshape(g, CHUNK * c, p)
    ii = ii.transpose(0, 1, 3, 2).reshape(g, CHUNK * c, p)
    win = jnp.concatenate([ir, ii, ii, ir], axis=2)

    gi = jnp.arange(g)[:, None, None] % GROUPS_PER_BLOCK
    si = jnp.arange(CHUNK)[None, :, None]
    ci = jnp.arange(c)[None, None, :]
    lane_of = ((si // GROUPS_PER_BLOCK) * LANES + ((gi + si) % GROUPS_PER_BLOCK) * c + ci).reshape(g, CHUNK * c)
    perm = (lane_of[:, None, :] == jnp.arange(CHUNK * c)[None, :, None]).astype(BF16)
    win = jnp.einsum('gln,gnk->glk', perm, win.astype(BF16))
    toep = jnp.einsum('gln,gnk->glk', perm, toep.astype(BF16))
    toep = jnp.einsum('grn,gln->grl', toep, perm)
    wout_t = jnp.einsum('gln,gpn->glp', perm, wout.astype(BF16))
    return win.astype(BF16), toep.astype(BF16), wout_t.astype(BF16), _s5_scan_coefficients(lam_pow, g, p, nk)


def _s5(u, win, toep, wout_t, coef):
    b, s, dssm = u.shape
    nk = s // SUPER
    nj = CHUNKS_PER_SUPER
    w = CHUNK * SSM_GROUP
    gpb = GROUPS_PER_BLOCK
    nblk = dssm // LANES
    wspec = lambda a: pl.BlockSpec((gpb,) + a.shape[1:], lambda bi, li: (li, 0, 0))
    return pl.pallas_call(
        functools.partial(_s5_kernel, nk=nk),
        grid=(b, nblk),
        in_specs=[pl.BlockSpec((None, s, LANES), lambda bi, li: (bi, 0, li)), wspec(win), wspec(toep),
                  wspec(wout_t), wspec(coef)],
        out_specs=pl.BlockSpec((None, s, LANES), lambda bi, li: (bi, 0, li)),
        out_shape=jax.ShapeDtypeStruct((b, s, dssm), F32),
        scratch_shapes=[pltpu.VMEM((gpb, nj, nk, w), F32)] * 2,
        compiler_params=pltpu.CompilerParams(dimension_semantics=("arbitrary", "arbitrary"),
                                             vmem_limit_bytes=VMEM_LIMIT),
        name="s5",
    )(u, win, toep, wout_t, coef)


def _route(logits):
    e_all = logits[0:N_EXPERTS]
    gl = logits[N_EXPERTS:N_EXPERTS + SUBLANES]
    tm = logits.shape[1]
    ridx = lax.broadcasted_iota(I32, (SUBLANES, tm), 0)
    ge = jnp.exp(gl - jnp.max(gl, axis=0, keepdims=True))
    gp = ge / jnp.sum(ge, axis=0, keepdims=True)
    g_top = jnp.max(gp, axis=0, keepdims=True)
    g_sel = jnp.min(jnp.where(gp == g_top, ridx, SUBLANES), axis=0, keepdims=True)
    e_in = e_all[(N_EXPERT_GROUPS - 1) * SUBLANES:]
    for gi in range(N_EXPERT_GROUPS - 2, -1, -1):
        e_in = jnp.where(g_sel == gi, e_all[gi * SUBLANES:(gi + 1) * SUBLANES], e_in)
    ee = jnp.exp(e_in - jnp.max(e_in, axis=0, keepdims=True))
    ep = ee / jnp.sum(ee, axis=0, keepdims=True)
    v1 = jnp.max(ep, axis=0, keepdims=True)
    i1 = jnp.min(jnp.where(ep == v1, ridx, SUBLANES), axis=0, keepdims=True)
    ep2 = jnp.where(ridx == i1, -1.0, ep)
    v2 = jnp.max(ep2, axis=0, keepdims=True)
    i2 = jnp.min(jnp.where(ep2 == v2, ridx, SUBLANES), axis=0, keepdims=True)
    den = v1 + v2
    w1 = g_top * v1 / den
    w2 = g_top * v2 / den
    e1 = g_sel * EXPERTS_PER_GROUP + i1
    e2 = g_sel * EXPERTS_PER_GROUP + i2
    ids = jnp.where(ridx == 0, e1, jnp.where(ridx == 1, e2, 0))
    wts = jnp.where(ridx == 0, w1, jnp.where(ridx == 1, w2, 0.0))
    return ids, wts


def _mix_kernel(x_ref, ya_ref, ys_ref, u_ref, g1_ref, wgate_ref, dsk_ref, wglu_ref, wpa_ref, wps_ref,
                wout_ref, g2_ref, wrh_ref, wrl_ref, br_ref, x1_ref, xp_ref, logits_ref):
    d = x_ref.shape[1]
    rows = x_ref.shape[0] // MIX_CHAINS
    for chain in range(MIX_CHAINS):
        r = slice(chain * rows, (chain + 1) * rows)
        x = x_ref[r, :]
        xb = _rmsnorm_rows(x, g1_ref[...]).astype(BF16)
        gates = _dot(xb, wgate_ref[...])
        y = ys_ref[r, :].astype(F32) + dsk_ref[...] * u_ref[r, :].astype(F32)
        y = y * (0.5 * (1.0 + jnp.tanh(math.sqrt(2.0 / math.pi) * (y + 0.044715 * (y * y * y)))))
        y = y * _sigmoid(_dot(y.astype(BF16), wglu_ref[...]))
        mixed = (_sigmoid(gates[:, :d]) * _dot(ya_ref[r, :], wpa_ref[...])
                 + _sigmoid(gates[:, d:]) * _dot(y.astype(BF16), wps_ref[...]))
        x1 = x + _dot(mixed.astype(BF16), wout_ref[...])
        x1_ref[r, :] = x1
        xn = _rmsnorm_rows(x1, g2_ref[...])
        xp_ref[r, :] = pltpu.pack_elementwise([xn[:, :d // 2], xn[:, d // 2:]], packed_dtype=BF16)
        xh = xn.astype(BF16)
        xl = (xn - xh.astype(F32)).astype(BF16)
        wrh = wrh_ref[...]
        logits_ref[:, r] = _nt_dot(wrh, xh) + _nt_dot(wrl_ref[...], xh) + _nt_dot(wrh, xl) + br_ref[...]


def _mix(x2, ya, ys, u, g1, wgate, dsk, wglu, wpa, wps, wout, g2, wrh, wrl, br):
    t, d = x2.shape
    tm = TM_MIX
    full = lambda a: pl.BlockSpec(a.shape, lambda i: (0,) * a.ndim)
    row = lambda width: pl.BlockSpec((tm, width), lambda i: (i, 0))
    return pl.pallas_call(
        _mix_kernel,
        grid=(t // tm,),
        in_specs=[row(d), row(ya.shape[1]), row(ys.shape[1]), row(u.shape[1]), full(g1), full(wgate), full(dsk),
                  full(wglu), full(wpa),
                  full(wps), full(wout), full(g2), full(wrh), full(wrl), full(br)],
        out_specs=[row(d), row(d // 2), pl.BlockSpec((ROUTER_ROWS, tm), lambda i: (0, i))],
        out_shape=[jax.ShapeDtypeStruct((t, d), F32), jax.ShapeDtypeStruct((t, d // 2), jnp.uint32),
                   jax.ShapeDtypeStruct((ROUTER_ROWS, t), F32)],
        compiler_params=pltpu.CompilerParams(dimension_semantics=("arbitrary",), vmem_limit_bytes=VMEM_LIMIT),
        name="mix",
    )(x2, ya, ys, u, g1, wgate, dsk, wglu, wpa, wps, wout, g2, wrh, wrl, br)


def _route_kernel(logits_ref, ids_ref, wts_ref):
    ids, wts = _route(logits_ref[...])
    ids_ref[...] = ids
    wts_ref[...] = wts


def _route_call(logits):
    t = logits.shape[1]
    tl = ROUTE_LANES
    col = pl.BlockSpec((SUBLANES, tl), lambda i: (0, i))
    return pl.pallas_call(
        _route_kernel,
        grid=(t // tl,),
        in_specs=[pl.BlockSpec((ROUTER_ROWS, tl), lambda i: (0, i))],
        out_specs=[col, col],
        out_shape=[jax.ShapeDtypeStruct((SUBLANES, t), I32), jax.ShapeDtypeStruct((SUBLANES, t), F32)],
        compiler_params=pltpu.CompilerParams(dimension_semantics=("arbitrary",)),
        name="route",
    )(logits)


def _meta_kernel(ids_ref, tri_ref, dest_ref, tile_ref, *, tile_rows):
    nk, nc, c = ids_ref.shape
    ne = N_EXPERTS
    erow = lax.broadcasted_iota(I32, (ne, c), 0)
    ones = jnp.ones((c, LANES), BF16)

    def onehot(k, ci):
        mask = erow == ids_ref[k, pl.ds(ci, 1), :]
        return mask, jnp.where(mask, 1.0, 0.0).astype(BF16)

    def count_body(n, acc):
        return acc + _dot(onehot(n // nc, n % nc)[1], ones)

    cnt = lax.fori_loop(0, nk * nc, count_body, jnp.zeros((ne, LANES), F32))
    ntiles = jnp.floor((cnt + (tile_rows - 1)) * (1.0 / tile_rows))
    lower = jnp.where(lax.broadcasted_iota(I32, (ne, ne), 1) < lax.broadcasted_iota(I32, (ne, ne), 0), 1.0, 0.0)
    start_tiles = _dot(lower.astype(BF16), ntiles.astype(BF16))
    base = start_tiles * tile_rows

    tri = tri_ref[...]

    def dest_body(n, carry):
        k, ci = n // nc, n % nc
        mask, oh = onehot(k, ci)
        prefix = _dot(oh, tri)
        slot = _lane_tile(base + carry, c // LANES) + prefix - 1.0
        dest = jnp.sum(jnp.where(mask, slot, 0.0), axis=0, keepdims=True)
        dest_ref[k, pl.ds(ci, 1), :] = dest.astype(I32)
        return carry + _dot(oh, ones)

    lax.fori_loop(0, nk * nc, dest_body, jnp.zeros((ne, LANES), F32))

    nt_lanes = tile_ref.shape[1]
    end_tiles = _lane_tile(start_tiles + ntiles, nt_lanes // LANES)
    tidx = lax.broadcasted_iota(I32, (ne, nt_lanes), 1).astype(F32)
    texp = jnp.sum(jnp.where(tidx >= end_tiles, 1.0, 0.0), axis=0, keepdims=True)
    valid = jnp.where(texp < ne, 1, 0)
    texp = jnp.minimum(texp, ne - 1.0).astype(I32)
    ridx = lax.broadcasted_iota(I32, tile_ref.shape, 0)
    tile_ref[...] = jnp.where(ridx == 0, texp, jnp.where(ridx == 1, valid, 0))


def _meta(ids2, tri, n_tiles):
    nk, t = ids2.shape
    c = SORT_CHUNK
    nt_lanes = pl.cdiv(n_tiles, LANES) * LANES
    ids3 = ids2.reshape(nk, t // c, c)
    dest, tile = pl.pallas_call(
        functools.partial(_meta_kernel, tile_rows=TM_EXPERT),
        out_shape=[jax.ShapeDtypeStruct(ids3.shape, I32), jax.ShapeDtypeStruct((SUBLANES, nt_lanes), I32)],
        compiler_params=pltpu.CompilerParams(vmem_limit_bytes=VMEM_LIMIT),
        name="meta",
    )(ids3, tri)
    return dest.reshape(nk * t), tile[0, :n_tiles], tile[1, :n_tiles]


def _row_copy(src_ref, src_row, dst_ref, dst_row, sem):
    return pltpu.make_async_copy(src_ref.at[pl.ds(src_row, 1)], dst_ref.at[pl.ds(dst_row, 1)], sem)


def _dispatch_kernel(dest_ref, xp_ref, xs_in_ref, xs_ref, sem, *, n_tokens):
    del xs_in_ref
    tm = xp_ref.shape[0]
    base = pl.program_id(0) * tm

    def body(r, carry):
        for k in range(2):
            _row_copy(xp_ref, r, xs_ref, dest_ref[k * n_tokens + base + r], sem).start()
        return carry

    lax.fori_loop(0, tm, body, 0, unroll=ROW_DMA_UNROLL)
    for k in range(2):
        pltpu.make_async_copy(xp_ref, xs_ref.at[pl.ds(0, tm)], sem).wait()


def _zero_fill_kernel(o_ref):
    o_ref[...] = jnp.zeros(o_ref.shape, o_ref.dtype)


def _zeros(n_rows, width, dtype):
    rows = n_rows // pl.cdiv(n_rows, ZERO_FILL_ROWS)
    assert n_rows % rows == 0 and rows % SUBLANES == 0
    return pl.pallas_call(
        _zero_fill_kernel,
        grid=(n_rows // rows,),
        out_specs=pl.BlockSpec((rows, width), lambda i: (i, 0)),
        out_shape=jax.ShapeDtypeStruct((n_rows, width), dtype),
        compiler_params=pltpu.CompilerParams(dimension_semantics=("arbitrary",)),
        name="zero_fill",
    )()


def _dispatch(dest, xp, n_slots):
    t, w = xp.shape
    tm = TM_DISPATCH
    xs0 = _zeros(n_slots, w, xp.dtype)
    return pl.pallas_call(
        functools.partial(_dispatch_kernel, n_tokens=t),
        grid_spec=pltpu.PrefetchScalarGridSpec(
            num_scalar_prefetch=1,
            grid=(t // tm,),
            in_specs=[pl.BlockSpec((tm, w), lambda i, d: (i, 0)), pl.BlockSpec(memory_space=pl.ANY)],
            out_specs=pl.BlockSpec(memory_space=pl.ANY),
            scratch_shapes=[pltpu.SemaphoreType.DMA(())]),
        out_shape=jax.ShapeDtypeStruct(xs0.shape, xp.dtype),
        input_output_aliases={2: 0},
        compiler_params=pltpu.CompilerParams(dimension_semantics=("arbitrary",), has_side_effects=True),
        name="dispatch",
    )(dest, xp, xs0)


def _unpack_rows(packed):
    lo = pltpu.unpack_elementwise(packed, index=0, packed_dtype=BF16, unpacked_dtype=F32)
    hi = pltpu.unpack_elementwise(packed, index=1, packed_dtype=BF16, unpacked_dtype=F32)
    return jnp.concatenate([lo, hi], axis=1)


def _expert_kernel(te_ref, tv_ref, nv_ref, nxt_ref, xs_ref, wg_hbm, wu_hbm, wd_hbm, ys_ref,
                   wg_bf, wu_bf, wd_bf, wg_f32, wu_f32, wd_f32, slot_ref, sem):
    del nv_ref
    i = pl.program_id(0)
    half = xs_ref.shape[1]

    def weight_copies(expert, slot):
        return [pltpu.make_async_copy(hbm.at[expert], buf.at[slot], sem.at[slot])
                for hbm, buf in ((wg_hbm, wg_f32), (wu_hbm, wu_f32), (wd_hbm, wd_f32))]

    @pl.when(i == 0)
    def _():
        slot_ref[0] = 0
        for copy in weight_copies(te_ref[0], 0):
            copy.start()

    @pl.when((i == 0) | (te_ref[i] != te_ref[jnp.maximum(i - 1, 0)]))
    def _():
        slot = slot_ref[0]
        for copy in weight_copies(te_ref[i], slot):
            copy.wait()
        wg_bf[...] = wg_f32[slot].astype(BF16)
        wu_bf[...] = wu_f32[slot].astype(BF16)
        wd_bf[...] = wd_f32[slot].astype(BF16)
        slot_ref[0] = 1 - slot

        @pl.when(nxt_ref[i] >= 0)
        def _():
            for copy in weight_copies(nxt_ref[i], 1 - slot):
                copy.start()

    @pl.when(tv_ref[i] > 0)
    def _():
        x = _unpack_rows(xs_ref[...]).astype(BF16)
        hg = _dot(x, wg_bf[...])
        hu = _dot(x, wu_bf[...])
        h = (hg * _sigmoid(hg) * hu).astype(BF16)
        y = _dot(h, wd_bf[...])
        ys_ref[...] = pltpu.pack_elementwise([y[:, :half], y[:, half:]], packed_dtype=BF16)

    @pl.when(tv_ref[i] == 0)
    def _():
        zero = jnp.zeros(ys_ref.shape, F32)
        ys_ref[...] = pltpu.pack_elementwise([zero, zero], packed_dtype=BF16)


def _experts(tile_expert, tile_valid, xs, wg, wu, wd):
    n_slots, w = xs.shape
    tm = TM_EXPERT
    n_tiles = n_slots // tm
    n_valid = jnp.sum(tile_valid).astype(I32).reshape(1)
    tile_expert = jnp.where(tile_valid > 0, tile_expert, tile_expert[n_valid[0] - 1])
    later = (tile_expert[None, :] > tile_expert[:, None]) & (tile_valid[None, :] > 0)
    next_expert = jnp.min(jnp.where(later, tile_expert[None, :], N_EXPERTS), axis=1)
    next_expert = jnp.where(next_expert < N_EXPERTS, next_expert, -1).astype(I32)
    rows_in = pl.BlockSpec((tm, w), lambda i, te, tv, nv, nx: (jnp.minimum(i, nv[0] - 1), 0))
    rows_out = pl.BlockSpec((tm, w), lambda i, te, tv, nv, nx: (i, 0))
    hbm = pl.BlockSpec(memory_space=pl.ANY)
    weights = (wg, wu, wd)
    return pl.pallas_call(
        _expert_kernel,
        grid_spec=pltpu.PrefetchScalarGridSpec(
            num_scalar_prefetch=4,
            grid=(n_tiles,),
            in_specs=[rows_in, hbm, hbm, hbm],
            out_specs=rows_out,
            scratch_shapes=[pltpu.VMEM(a.shape[1:], BF16) for a in weights]
            + [pltpu.VMEM((2,) + a.shape[1:], a.dtype) for a in weights]
            + [pltpu.SMEM((1,), I32), pltpu.SemaphoreType.DMA((2,))]),
        out_shape=jax.ShapeDtypeStruct(xs.shape, xs.dtype),
        compiler_params=pltpu.CompilerParams(dimension_semantics=("arbitrary",), vmem_limit_bytes=VMEM_LIMIT),
        name="experts",
    )(tile_expert, tile_valid, n_valid, next_expert, xs, wg, wu, wd)


def _combine_kernel(dest_ref, x1_ref, w0_ref, w1_ref, ys_ref, out_ref, buf, sem, *, n_tokens):
    tm, d = x1_ref.shape
    base = pl.program_id(0) * tm

    def body(r, carry):
        for k in range(2):
            _row_copy(ys_ref, dest_ref[k * n_tokens + base + r], buf.at[k], r, sem).start()
        return carry

    lax.fori_loop(0, tm, body, 0, unroll=ROW_DMA_UNROLL)
    for k in range(2):
        pltpu.make_async_copy(ys_ref.at[pl.ds(0, tm)], buf.at[k], sem).wait()
    reps = d // LANES
    out_ref[...] = (x1_ref[...]
                    + _lane_tile(w0_ref[...], reps) * _unpack_rows(buf[0])
                    + _lane_tile(w1_ref[...], reps) * _unpack_rows(buf[1]))


def _combine(dest, x1, w0, w1, ys):
    t, d = x1.shape
    tm = TM_COMBINE
    row = lambda width: pl.BlockSpec((tm, width), lambda i, dref: (i, 0))
    return pl.pallas_call(
        functools.partial(_combine_kernel, n_tokens=t),
        grid_spec=pltpu.PrefetchScalarGridSpec(
            num_scalar_prefetch=1,
            grid=(t // tm,),
            in_specs=[row(d), row(LANES), row(LANES), pl.BlockSpec(memory_space=pl.ANY)],
            out_specs=row(d),
            scratch_shapes=[pltpu.VMEM((2, tm, ys.shape[-1]), ys.dtype), pltpu.SemaphoreType.DMA(())]),
        out_shape=jax.ShapeDtypeStruct((t, d), x1.dtype),
        compiler_params=pltpu.CompilerParams(dimension_semantics=("arbitrary",), vmem_limit_bytes=VMEM_LIMIT),
        name="combine",
    )(dest, x1, w0, w1, ys)


def _lower_tri(n):
    return jnp.where(jnp.arange(n)[:, None] >= jnp.arange(n)[None, :], 1.0, 0.0).astype(BF16)


def _upper_tri(n):
    return jnp.where(jnp.arange(n)[:, None] <= jnp.arange(n)[None, :], 1.0, 0.0).astype(BF16)


def _split_bf16(w):
    hi = w.astype(BF16)
    lo = (w - hi.astype(F32)).astype(BF16)
    return hi, lo


def _layer(x, norm_mix_g, w_in, b_forget, q_norm_g, k_norm_g, ssm_A_re, ssm_A_im, ssm_log_dt, ssm_B_re, ssm_B_im,
           ssm_C_re, ssm_C_im, ssm_D, w_glu, w_proj_attn, w_proj_ssm, w_out, norm_ffn_g, w_router_group,
           b_router_group, w_router_expert, b_router_expert, w_expert_gate, w_expert_up, w_expert_down):
    b, s, d = x.shape
    t = b * s
    da = N_HEADS * HEAD_DIM
    dssm = ssm_D.shape[0]
    nk = s // SUPER
    assert s % SUPER == 0 and nk % SUBLANES == 0 and nk & (nk - 1) == 0 and s % TQ_ATTN == 0 and t % TM_PROJ == 0
    assert TM_PROJ == TS_ATTN
    x2 = x.reshape(t, d)

    o_f = 3 * da
    o_u = o_f + N_HEADS
    o_g = o_u + dssm
    w_main = jnp.concatenate([w_in[:, :o_f], w_in[:, o_u:o_g]], axis=1).astype(BF16)
    wf = jnp.repeat(w_in[:, o_f:o_u], FORGET_COPIES_STRIDE, axis=1).astype(BF16)
    bf = jnp.repeat(b_forget, FORGET_COPIES_STRIDE)[None, :]
    w_gates = w_in[:, o_g:].astype(BF16)
    qg = (jnp.tile(q_norm_g, N_HEADS) * (HEAD_DIM ** -0.5 * LOG2E))[None, :]
    kg = jnp.tile(k_norm_g, N_HEADS)[None, :]
    head_of = jnp.arange(da) // HEAD_DIM
    bd = jnp.where(head_of[:, None] == head_of[None, :], 1.0 / HEAD_DIM, 0.0).astype(BF16)
    logit_bound = 1.02 * HEAD_DIM ** 0.5 * jnp.max(jnp.abs(q_norm_g)) * jnp.max(jnp.abs(k_norm_g))
    plain_ok = LOG2E * logit_bound <= PLAIN_SOFTMAX_MAX_LOG2
    shift = jnp.full((1, LANES), logit_bound, F32)
    hw = N_HEADS * LANES
    lane_in_head = jnp.arange(hw) % LANES
    ones = jnp.stack([
        (lane_in_head >= BIAS_K_LANE) & (lane_in_head < BIAS_K_LANE + 3),
        (lane_in_head >= BIAS_Q_LANE) & (lane_in_head < BIAS_Q_LANE + 3),
        lane_in_head == HEAD_DIM]
        + [jnp.zeros((hw,), bool)] * (SUBLANES - 3)).astype(F32)
    prow = jnp.arange(LANES)
    head, copy = prow // FORGET_COPIES_STRIDE, prow % FORGET_COPIES_STRIDE
    target = jnp.where(copy < 3, head * LANES + BIAS_Q_LANE + copy, hw + head * LANES + BIAS_K_LANE + copy - 3)
    place = ((jnp.arange(2 * hw)[None, :] == target[:, None]) & (copy < 6)[:, None]).astype(BF16)

    q, k, v, u, fedge = _proj(x2, norm_mix_g[None, :], w_main, wf, bf, qg, kg, bd, _lower_tri(TM_PROJ), place, ones,
                              shift, seq=s)
    y_attn = _attention(q.reshape(b, s, hw), k.reshape(b, s, hw), v.reshape(b, s, hw), fedge, plain_ok)

    y_ssm = _s5(u.reshape(b, s, dssm),
                *_s5_params(ssm_A_re, ssm_A_im, ssm_log_dt, ssm_B_re, ssm_B_im, ssm_C_re, ssm_C_im, s // SUPER))

    wr = jnp.concatenate([w_router_expert.T, w_router_group.T,
                          jnp.zeros((ROUTER_ROWS - N_EXPERTS - N_EXPERT_GROUPS, d), F32)], axis=0)
    br = jnp.concatenate([b_router_expert, b_router_group,
                          jnp.full((SUBLANES - N_EXPERT_GROUPS,), NEG_INF, F32),
                          jnp.zeros((ROUTER_ROWS - N_EXPERTS - SUBLANES,), F32)])[:, None]
    x1, xp, logits = _mix(x2, y_attn.reshape(t, da), y_ssm.reshape(t, dssm), u, norm_mix_g[None, :], w_gates,
                          ssm_D[None, :], w_glu.astype(BF16), w_proj_attn.astype(BF16), w_proj_ssm.astype(BF16),
                          w_out.astype(BF16), norm_ffn_g[None, :], *_split_bf16(wr), br)
    ids, wts = _route_call(logits)

    n_tiles = (2 * t) // TM_EXPERT + N_EXPERTS
    dest, tile_expert, tile_valid = _meta(ids[:2], _upper_tri(SORT_CHUNK), n_tiles)
    xs = _dispatch(dest, xp, n_tiles * TM_EXPERT)
    ys = _experts(tile_expert, tile_valid, xs, w_expert_gate, w_expert_up, w_expert_down)
    w0 = jnp.broadcast_to(wts[0][:, None], (t, LANES))
    w1 = jnp.broadcast_to(wts[1][:, None], (t, LANES))
    out = _combine(dest, x1, w0, w1, ys)
    return out.reshape(b, s, d)


def kernel(x, norm_mix_g, w_in, b_forget, q_norm_g, k_norm_g, ssm_A_re, ssm_A_im, ssm_log_dt, ssm_B_re, ssm_B_im,
           ssm_C_re, ssm_C_im, ssm_D, w_glu, w_proj_attn, w_proj_ssm, w_out, norm_ffn_g, w_router_group,
           b_router_group, w_router_expert, b_router_expert, w_expert_gate, w_expert_up, w_expert_down):
    layer_params = (norm_mix_g, w_in, b_forget, q_norm_g, k_norm_g, ssm_A_re, ssm_A_im, ssm_log_dt, ssm_B_re,
                    ssm_B_im, ssm_C_re, ssm_C_im, ssm_D, w_glu, w_proj_attn, w_proj_ssm, w_out, norm_ffn_g,
                    w_router_group, b_router_group, w_router_expert, b_router_expert, w_expert_gate, w_expert_up,
                    w_expert_down)
    for layer in range(norm_mix_g.shape[0]):
        x = _layer(x, *[p[layer] for p in layer_params])
    return x
```

```python
import functools
import math

import jax
import jax.numpy as jnp
from jax import lax
from jax.experimental import pallas as pl
from jax.experimental.pallas import tpu as pltpu

F32 = jnp.float32
BF16 = jnp.bfloat16
I32 = jnp.int32

LANES = 128
SUBLANES = 8
MXU_DIM = 256

N_HEADS = 8
HEAD_DIM = 64
SSM_GROUP = 16
SSM_STATE = 64
N_EXPERT_GROUPS = 4
EXPERTS_PER_GROUP = 8
N_EXPERTS = N_EXPERT_GROUPS * EXPERTS_PER_GROUP
EPS = 1e-6
NEG_INF = -1e30
LOG2E = math.log2(math.e)
PLAIN_SOFTMAX_MAX_LOG2 = 60.0
UNDERFLOW_LOG2 = 160.0
BIAS_K_LANE = HEAD_DIM
BIAS_Q_LANE = HEAD_DIM + 3
FORGET_COPIES_STRIDE = LANES // N_HEADS

CHUNK = MXU_DIM // SSM_GROUP
CHUNKS_PER_SUPER = 8
SUPER = CHUNK * CHUNKS_PER_SUPER
GROUPS_PER_BLOCK = LANES // SSM_GROUP
RELAYOUT_UNROLL = 8
MIN_FACTORED_DECAY = -4.0

TM_PROJ = 512
TQ_ATTN = 2048
TS_ATTN = 512
HEADS_PER_STEP = 2
TM_MIX = 512
MIX_CHAINS = 1
TM_EXPERT = 512
TM_DISPATCH = 1024
TM_COMBINE = 1024
SORT_CHUNK = 1024
ROW_DMA_UNROLL = 16
ROUTE_LANES = 2048
ROUTER_ROWS = 48

VMEM_LIMIT = 48 * 1024 * 1024


def _nt_dot(a, b):
    return lax.dot_general(a, b, (((1,), (1,)), ((), ())), preferred_element_type=F32)


def _dot(a, b):
    return jnp.dot(a, b, preferred_element_type=F32)


def _lane_tile(x, n):
    return x if n == 1 else jnp.concatenate([x] * n, axis=1)


def _rmsnorm_rows(x, g):
    ms = jnp.mean(x * x, axis=-1, keepdims=True)
    return x * lax.rsqrt(ms + EPS) * g


def _sigmoid(x):
    return 1.0 / (1.0 + jnp.exp(-x))


def _split3(x):
    hi = x.astype(BF16)
    r1 = x - hi.astype(F32)
    mid = r1.astype(BF16)
    lo = (r1 - mid.astype(F32)).astype(BF16)
    return [hi, mid, lo]


def _expand_heads(z):
    lane = lax.broadcasted_iota(I32, (z.shape[0], LANES), 1)
    blocks = []
    for p in range(z.shape[1] // LANES):
        blk = z[:, p * LANES:(p + 1) * LANES]
        blocks.append(jnp.where(lane < HEAD_DIM, blk, 0.0))
        blocks.append(jnp.where(lane < HEAD_DIM, pltpu.roll(blk, HEAD_DIM, axis=1), 0.0))
    return jnp.concatenate(blocks, axis=1)


def _proj_kernel(x_ref, g_ref, w_ref, wf_ref, bf_ref, qg_ref, kg_ref, bd_ref, ltri_ref, place_ref,
                 ones_ref, shift_ref, q_ref, k_ref, v_ref, u_ref, fedge_ref, carry_ref, *, tiles_per_seq):
    i = pl.program_id(0)

    @pl.when(i % tiles_per_seq == 0)
    def _():
        carry_ref[...] = jnp.zeros_like(carry_ref)

    tm = x_ref.shape[0]
    da = N_HEADS * HEAD_DIM
    hw = N_HEADS * LANES
    xb = _rmsnorm_rows(x_ref[...], g_ref[...]).astype(BF16)
    z = _dot(xb, w_ref[...])
    zq, zk = z[:, :da], z[:, da:2 * da]
    bd = bd_ref[...]
    msq = _dot((zq * zq).astype(BF16), bd)
    msk = _dot((zk * zk).astype(BF16), bd)
    qn = zq * lax.rsqrt(msq + EPS) * qg_ref[...]
    kn = zk * lax.rsqrt(msk + EPS) * kg_ref[...]
    u_ref[...] = z[:, 3 * da:]

    a = _dot(xb, wf_ref[...]) + bf_ref[...]
    logf = jnp.minimum(a, 0.0) - jnp.log(1.0 + jnp.exp(-jnp.abs(a)))
    cs = _dot(ltri_ref[...], jnp.concatenate(_split3(logf), axis=1))
    cum = cs[:, :LANES] + cs[:, LANES:2 * LANES] + cs[:, 2 * LANES:] + carry_ref[0:1, :]
    carry_ref[...] = jnp.broadcast_to(cum[tm - 1:tm, :], carry_ref.shape)
    edge_row = lax.broadcasted_iota(I32, fedge_ref.shape, 0)
    fedge_ref[...] = jnp.where(edge_row == 0, cum[0:1, :], jnp.where(edge_row == 1, cum[tm - 1:tm, :], 0.0))
    copy = lax.broadcasted_iota(I32, cum.shape, 1) % FORGET_COPIES_STRIDE
    bias = jnp.where(copy < 3, (cum - shift_ref[...]) * LOG2E, cum * (-LOG2E))
    hi, mid, lo = [p.astype(F32) for p in _split3(bias)]
    piece = jnp.where(copy % 3 == 0, hi, jnp.where(copy % 3 == 1, mid, lo)).astype(BF16)
    extras = _dot(piece, place_ref[...])
    q_ref[...] = (_expand_heads(qn) + extras[:, :hw] + ones_ref[0:1, :]).astype(BF16)
    k_ref[...] = (_expand_heads(kn) + extras[:, hw:] + ones_ref[1:2, :]).astype(BF16)
    v_ref[...] = (_expand_heads(z[:, 2 * da:3 * da]) + ones_ref[2:3, :]).astype(BF16)


def _proj(x2, norm_g, w_main, wf, b_f, qg, kg, bd, ltri, place, ones, shift, *, seq):
    t, d = x2.shape
    tm = TM_PROJ
    hw = N_HEADS * LANES
    du = w_main.shape[1] - 3 * N_HEADS * HEAD_DIM
    tiles_per_seq = seq // tm
    full = lambda a: pl.BlockSpec(a.shape, lambda i: (0,) * a.ndim)
    row = lambda width: pl.BlockSpec((tm, width), lambda i: (i, 0))
    consts = (norm_g, w_main, wf, b_f, qg, kg, bd, ltri, place, ones, shift)
    return pl.pallas_call(
        functools.partial(_proj_kernel, tiles_per_seq=tiles_per_seq),
        grid=(t // tm,),
        in_specs=[row(d)] + [full(a) for a in consts],
        out_specs=[row(hw), row(hw), row(hw), row(du), pl.BlockSpec((None, SUBLANES, LANES), lambda i: (i, 0, 0))],
        out_shape=[jax.ShapeDtypeStruct((t, hw), BF16)] * 3
        + [jax.ShapeDtypeStruct((t, du), F32), jax.ShapeDtypeStruct((t // tm, SUBLANES, LANES), F32)],
        scratch_shapes=[pltpu.VMEM((SUBLANES, LANES), F32)],
        compiler_params=pltpu.CompilerParams(dimension_semantics=("arbitrary",), vmem_limit_bytes=VMEM_LIMIT),
        name="proj",
    )(x2, *consts)


def _attn_kernel(first_ref, q_ref, k_ref, v_ref, o_ref, acc_scr, m_scr, *, ts, safe):
    i = pl.program_id(2)
    nsub = q_ref.shape[0] // ts
    acc_scr[...] = jnp.zeros(acc_scr.shape, F32)
    if safe:
        m_scr[...] = jnp.full(m_scr.shape, NEG_INF, F32)

    def block(hd, sub, j, masked):
        off = pl.multiple_of(j * ts, ts)
        lanes = slice(hd * LANES, (hd + 1) * LANES)
        chain = hd * nsub + sub
        s = _nt_dot(q_ref[sub * ts:(sub + 1) * ts, lanes], k_ref[pl.ds(off, ts), lanes])
        if masked:
            rows = lax.broadcasted_iota(I32, (ts, ts), 0)
            cols = lax.broadcasted_iota(I32, (ts, ts), 1)
            s = jnp.where(cols <= rows, s, NEG_INF)
        vblk = v_ref[pl.ds(off, ts), lanes]
        if safe:
            m_prev = m_scr[chain]
            m_new = jnp.maximum(m_prev, jnp.max(s, axis=1, keepdims=True))
            p = jnp.exp2(s - _lane_tile(m_new, ts // LANES)).astype(BF16)
            acc_scr[chain] = jnp.exp2(m_prev - m_new) * acc_scr[chain] + _dot(p, vblk)
            m_scr[chain] = m_new
        else:
            acc_scr[chain] += _dot(jnp.exp2(s).astype(BF16), vblk)

    def body(heads, j, carry):
        for hd in heads:
            for sub in range(nsub):
                block(hd, sub, j, False)
        return carry

    step = (pl.program_id(0) * pl.num_programs(1) + pl.program_id(1)) * pl.num_programs(2) + i
    starts = [first_ref[step * HEADS_PER_STEP + hd] for hd in range(HEADS_PER_STEP)]
    shared = functools.reduce(jnp.maximum, starts)
    for hd in range(HEADS_PER_STEP):
        lax.fori_loop(starts[hd], shared, functools.partial(body, (hd,)), 0)
    lax.fori_loop(shared, nsub * i, functools.partial(body, tuple(range(HEADS_PER_STEP))), 0)
    for jj in range(nsub):
        for hd in range(HEADS_PER_STEP):
            for sub in range(jj, nsub):
                block(hd, sub, nsub * i + jj, sub == jj)
    lane = lax.broadcasted_iota(I32, (ts, LANES), 1)
    for sub in range(nsub):
        outs = []
        for hd in range(HEADS_PER_STEP):
            acc = acc_scr[hd * nsub + sub]
            outs.append(acc / acc[:, HEAD_DIM:HEAD_DIM + 1])
        both = jnp.where(lane < HEAD_DIM, outs[0], pltpu.roll(outs[1], HEAD_DIM, axis=1))
        o_ref[sub * ts:(sub + 1) * ts, :] = both.astype(o_ref.dtype)


def _first_kv_block(fedge, b, s, tq, ts):
    nblk = s // ts
    f_first = fedge[:, 0, ::FORGET_COPIES_STRIDE].reshape(b, nblk, N_HEADS).transpose(0, 2, 1)
    f_last = fedge[:, 1, ::FORGET_COPIES_STRIDE].reshape(b, nblk, N_HEADS).transpose(0, 2, 1)
    tile_start = f_first[:, :, ::tq // ts]
    bound = LOG2E * (tile_start[:, :, :, None] - f_last[:, :, None, :])
    first = jnp.sum(bound < -UNDERFLOW_LOG2, axis=-1).astype(I32)
    first = first.reshape(b, N_HEADS // HEADS_PER_STEP, HEADS_PER_STEP, -1).transpose(0, 1, 3, 2)
    return first.reshape(-1)


def _attention(q, k, v, fedge, plain_ok):
    b, s, hw = q.shape
    tq, ts = TQ_ATTN, TS_ATTN
    nq = s // tq
    steps = N_HEADS // HEADS_PER_STEP
    width = HEADS_PER_STEP * LANES
    chains = HEADS_PER_STEP * tq // ts

    def call(safe):
        first = jnp.zeros((b * N_HEADS * nq,), I32) if safe else _first_kv_block(fedge, b, s, tq, ts)
        return pl.pallas_call(
            functools.partial(_attn_kernel, ts=ts, safe=safe),
            grid_spec=pltpu.PrefetchScalarGridSpec(
                num_scalar_prefetch=1,
                grid=(b, steps, nq),
                in_specs=[pl.BlockSpec((None, tq, width), lambda bi, h, i, f: (bi, i, h)),
                          pl.BlockSpec((None, s, width), lambda bi, h, i, f: (bi, 0, h)),
                          pl.BlockSpec((None, s, width), lambda bi, h, i, f: (bi, 0, h))],
                out_specs=pl.BlockSpec((None, tq, LANES), lambda bi, h, i, f: (bi, i, h)),
                scratch_shapes=[pltpu.VMEM((chains, ts, LANES), F32)] * 2),
            out_shape=jax.ShapeDtypeStruct((b, s, steps * LANES), BF16),
            compiler_params=pltpu.CompilerParams(
                dimension_semantics=("arbitrary", "arbitrary", "arbitrary"), vmem_limit_bytes=VMEM_LIMIT),
            name="attn_safe" if safe else "attn",
        )(first, q, k, v)

    return lax.cond(plain_ok, lambda: call(False), lambda: call(True))


def _window_select(pieces, first_group):
    window = lax.broadcasted_iota(I32, pieces[0].shape, 1) // SSM_GROUP
    out = pieces[0]
    for m in range(1, GROUPS_PER_BLOCK):
        out = jnp.where(window == (first_group + m) % GROUPS_PER_BLOCK, pieces[m], out)
    return out


def _s5_kernel(u_ref, win_ref, toep_ref, wout_ref, coef_ref, y_ref, ub_scr, yb_scr, *, nk):
    nj = CHUNKS_PER_SUPER
    gpb = GROUPS_PER_BLOCK
    w = CHUNK * SSM_GROUP
    half = w // 2
    nkk = nk // SUBLANES
    row_stride = SUPER

    def gather_body(it, carry):
        j, kk = it // nkk, it % nkk
        for sp in range(CHUNK // gpb):
            rot = []
            for m in range(gpb):
                row0 = j * CHUNK + sp * gpb + m + kk * (SUBLANES * row_stride)
                src = u_ref[pl.ds(row0, SUBLANES, stride=row_stride), :]
                rot.append(src if m == 0 else pltpu.roll(src, m * SSM_GROUP, axis=1))
            for gl in range(gpb):
                ub_scr[gl, j, pl.ds(kk * SUBLANES, SUBLANES), sp * LANES:(sp + 1) * LANES] = _window_select(rot, gl)
        return carry

    lax.fori_loop(0, nj * nkk, gather_body, 0, unroll=RELAYOUT_UNROLL)

    def swap(val):
        return jnp.concatenate([val[:, half:], val[:, :half]], axis=1)

    def group_body(gl, carry):
        def cmul(val, idx):
            return coef_ref[gl, idx:idx + 1, :] * val + coef_ref[gl, idx + 1:idx + 2, :] * swap(val)

        u = ub_scr[gl].reshape(nj * nk, w).astype(BF16)
        s1 = _dot(u, win_ref[gl]).reshape(nj, nk, w)
        e = jnp.zeros((nk, w), F32)
        local = []
        for j in range(nj):
            local.append(e)
            e = cmul(e, 0) + s1[j]
        kidx = lax.broadcasted_iota(I32, (nk, w), 0)
        x = e
        step, d = 0, 1
        while d < nk:
            shifted = jnp.where(kidx >= d, pltpu.roll(x, d, axis=0), 0.0)
            x = x + cmul(shifted, 2 + 2 * nj + 2 * step)
            step, d = step + 1, d * 2
        x_start = jnp.where(kidx >= 1, pltpu.roll(x, 1, axis=0), 0.0)
        starts = [local[j] + cmul(x_start, 2 + 2 * j) for j in range(nj)]
        p = jnp.concatenate([st[:, :half] for st in starts], axis=0)
        hi = p.astype(BF16)
        lo = (p - hi.astype(F32)).astype(BF16)
        wout_t = wout_ref[gl]
        y = _dot(u, toep_ref[gl]) + _nt_dot(hi, wout_t) + _nt_dot(lo, wout_t)
        yb_scr[gl] = y.reshape(nj, nk, w)
        return carry

    lax.fori_loop(0, gpb, group_body, 0)

    def scatter_body(it, carry):
        j, kk = it // nkk, it % nkk
        for tp in range(CHUNK // gpb):
            src = [yb_scr[gl, j, pl.ds(kk * SUBLANES, SUBLANES), tp * LANES:(tp + 1) * LANES] for gl in range(gpb)]
            for m in range(gpb):
                by_window = [src[(wi - m) % gpb] for wi in range(gpb)]
                window = lax.broadcasted_iota(I32, by_window[0].shape, 1) // SSM_GROUP
                merged = by_window[0]
                for wi in range(1, gpb):
                    merged = jnp.where(window == wi, by_window[wi], merged)
                nat = merged if m == 0 else pltpu.roll(merged, LANES - m * SSM_GROUP, axis=1)
                row0 = j * CHUNK + tp * gpb + m + kk * (SUBLANES * row_stride)
                y_ref[pl.ds(row0, SUBLANES, stride=row_stride), :] = nat
        return carry

    lax.fori_loop(0, nj * nkk, scatter_body, 0, unroll=RELAYOUT_UNROLL)


def _s5_params(a_re, a_im, log_dt, b_re, b_im, c_re, c_im, nk):
    args = (a_re, a_im, log_dt, b_re, b_im, c_re, c_im)
    slowest = jnp.min(a_re * jnp.exp(log_dt)[:, None])
    return lax.cond(slowest > MIN_FACTORED_DECAY, functools.partial(_s5_params_factored, nk=nk),
                    functools.partial(_s5_params_direct, nk=nk), *args)


def _s5_params_factored(a_re, a_im, log_dt, b_re, b_im, c_re, c_im, *, nk):
    g, p = a_re.shape
    c = SSM_GROUP
    gpb = GROUPS_PER_BLOCK
    dt = jnp.exp(log_dt)[:, None]
    adt_r, adt_i = a_re * dt, a_im * dt

    def lam_pow(n):
        nf = jnp.asarray(n, F32)
        nf = (nf[None] if nf.ndim == 1 else nf)[:, :, None]
        mag = jnp.exp(adt_r[:, None, :] * nf)
        ang = adt_i[:, None, :] * nf
        return mag * jnp.cos(ang), mag * jnp.sin(ang)

    gi = jnp.arange(g)[:, None, None] % gpb
    step = ((jnp.arange(CHUNK // gpb)[None, :, None]) * gpb
            + (jnp.arange(gpb)[None, None, :] - gi) % gpb).reshape(g, CHUNK)

    l1r, l1i = lam_pow(jnp.ones((1,)))
    den = a_re * a_re + a_im * a_im
    nr, ni = l1r[:, 0] - 1.0, l1i[:, 0]
    qr = (nr * a_re + ni * a_im) / den
    qi = (ni * a_re - nr * a_im) / den
    bcr = (qr[..., None] * b_re - qi[..., None] * b_im).transpose(0, 2, 1)
    bci = (qr[..., None] * b_im + qi[..., None] * b_re).transpose(0, 2, 1)

    def times_b(pr, pi):
        pr, pi = pr[:, :, None, :], pi[:, :, None, :]
        return pr * bcr[:, None] - pi * bci[:, None], pr * bci[:, None] + pi * bcr[:, None]

    def times_c(pr, pi):
        pr, pi = pr[:, :, None, :], pi[:, :, None, :]
        return c_re[:, None] * pr - c_im[:, None] * pi, c_re[:, None] * pi + c_im[:, None] * pr

    rows = lambda parts: jnp.concatenate(parts, axis=-1).reshape(g, CHUNK * c, -1)
    ir, ii = times_b(*lam_pow(CHUNK - 1 - step))
    win = rows([ir, ii, ii, ir])
    clr, cli = times_c(*lam_pow(step + 1))
    wout_t = rows([clr, -cli])
    ar, ai = times_b(*lam_pow(-step))
    br, bi = times_c(*lam_pow(step))
    toep = jnp.einsum('gnp,gmp->gnm', rows([ar, ai]), rows([br, -bi]), precision=lax.Precision.HIGHEST)
    step_of_lane = jnp.repeat(step, c, axis=1)
    toep = jnp.where(step_of_lane[:, None, :] >= step_of_lane[:, :, None], toep, 0.0)
    return win.astype(BF16), toep.astype(BF16), wout_t.astype(BF16), _s5_scan_coefficients(lam_pow, g, p, nk)


def _s5_scan_coefficients(lam_pow, g, p, nk):
    n_steps = max(nk.bit_length() - 1, 0)
    powers = [CHUNK] + [CHUNK * j for j in range(CHUNKS_PER_SUPER)] + [SUPER * (1 << i) for i in range(n_steps)]
    ar, ai = lam_pow(jnp.array(powers, dtype=jnp.int32))
    c1 = jnp.concatenate([ar, ar, ar, ar], axis=2)
    c2 = jnp.concatenate([-ai, ai, ai, -ai], axis=2)
    coef = jnp.stack([c1, c2], axis=2).reshape(g, 2 * len(powers), 4 * p)
    return jnp.pad(coef, ((0, 0), (0, (-coef.shape[1]) % SUBLANES), (0, 0)))


def _s5_params_direct(a_re, a_im, log_dt, b_re, b_im, c_re, c_im, *, nk):
    g, p = a_re.shape
    c = SSM_GROUP
    hp = lax.Precision.HIGHEST
    dt = jnp.exp(log_dt)[:, None]
    adt_r, adt_i = a_re * dt, a_im * dt

    def lam_pow(n):
        nf = jnp.asarray(n, F32)[None, :, None]
        mag = jnp.exp(adt_r[:, None, :] * nf)
        ang = adt_i[:, None, :] * nf
        return mag * jnp.cos(ang), mag * jnp.sin(ang)

    lr, li = lam_pow(jnp.arange(CHUNK + 1))
    den = a_re * a_re + a_im * a_im
    nr, ni = lr[:, 1] - 1.0, li[:, 1]
    qr = (nr * a_re + ni * a_im) / den
    qi = (ni * a_re - nr * a_im) / den
    bbr = qr[..., None] * b_re - qi[..., None] * b_im
    bbi = qr[..., None] * b_im + qi[..., None] * b_re

    mr = lr[:, :CHUNK, :, None] * bbr[:, None] - li[:, :CHUNK, :, None] * bbi[:, None]
    mi = lr[:, :CHUNK, :, None] * bbi[:, None] + li[:, :CHUNK, :, None] * bbr[:, None]
    kern = (jnp.einsum('gcp,gtpd->gtcd', c_re, mr, precision=hp)
            - jnp.einsum('gcp,gtpd->gtcd', c_im, mi, precision=hp))
    toep = jnp.stack([jnp.pad(kern[:, :CHUNK - s], ((0, 0), (s, 0), (0, 0), (0, 0))) for s in range(CHUNK)],
                     axis=1)
    toep = toep.transpose(0, 1, 4, 2, 3).reshape(g, CHUNK * c, CHUNK * c)
    clr = c_re[:, None] * lr[:, 1:, None, :] - c_im[:, None] * li[:, 1:, None, :]
    cli = c_re[:, None] * li[:, 1:, None, :] + c_im[:, None] * lr[:, 1:, None, :]
    w_re = clr.transpose(0, 3, 1, 2).reshape(g, p, CHUNK * c)
    w_im = (-cli).transpose(0, 3, 1, 2).reshape(g, p, CHUNK * c)
    wout = jnp.concatenate([w_re, w_im], axis=1)
    pr, pi = lr[:, :CHUNK][:, ::-1], li[:, :CHUNK][:, ::-1]
    ir = pr[..., None] * bbr[:, None] - pi[..., None] * bbi[:, None]
    ii = pr[..., None] * bbi[:, None] + pi[..., None] * bbr[:, None]
    ir = ir.transpose(0, 1, 3, 2).reshape(g, CHUNK * c, p)
    ii = ii.transpose(0, 1, 3, 2).reshape(g, CHUNK * c, p)
    win = jnp.concatenate([ir, ii, ii, ir], axis=2)

    gi = jnp.arange(g)[:, None, None] % GROUPS_PER_BLOCK
    si = jnp.arange(CHUNK)[None, :, None]
    ci = jnp.arange(c)[None, None, :]
    lane_of = ((si // GROUPS_PER_BLOCK) * LANES + ((gi + si) % GROUPS_PER_BLOCK) * c + ci).reshape(g, CHUNK * c)
    perm = (lane_of[:, None, :] == jnp.arange(CHUNK * c)[None, :, None]).astype(BF16)
    win = jnp.einsum('gln,gnk->glk', perm, win.astype(BF16))
    toep = jnp.einsum('gln,gnk->glk', perm, toep.astype(BF16))
    toep = jnp.einsum('grn,gln->grl', toep, perm)
    wout_t = jnp.einsum('gln,gpn->glp', perm, wout.astype(BF16))
    return win.astype(BF16), toep.astype(BF16), wout_t.astype(BF16), _s5_scan_coefficients(lam_pow, g, p, nk)


def _s5(u, win, toep, wout_t, coef):
    b, s, dssm = u.shape
    nk = s // SUPER
    nj = CHUNKS_PER_SUPER
    w = CHUNK * SSM_GROUP
    gpb = GROUPS_PER_BLOCK
    nblk = dssm // LANES
    wspec = lambda a: pl.BlockSpec((gpb,) + a.shape[1:], lambda bi, li: (li, 0, 0))
    return pl.pallas_call(
        functools.partial(_s5_kernel, nk=nk),
        grid=(b, nblk),
        in_specs=[pl.BlockSpec((None, s, LANES), lambda bi, li: (bi, 0, li)), wspec(win), wspec(toep),
                  wspec(wout_t), wspec(coef)],
        out_specs=pl.BlockSpec((None, s, LANES), lambda bi, li: (bi, 0, li)),
        out_shape=jax.ShapeDtypeStruct((b, s, dssm), F32),
        scratch_shapes=[pltpu.VMEM((gpb, nj, nk, w), F32)] * 2,
        compiler_params=pltpu.CompilerParams(dimension_semantics=("arbitrary", "arbitrary"),
                                             vmem_limit_bytes=VMEM_LIMIT),
        name="s5",
    )(u, win, toep, wout_t, coef)


def _route(logits):
    e_all = logits[0:N_EXPERTS]
    gl = logits[N_EXPERTS:N_EXPERTS + SUBLANES]
    tm = logits.shape[1]
    ridx = lax.broadcasted_iota(I32, (SUBLANES, tm), 0)
    ge = jnp.exp(gl - jnp.max(gl, axis=0, keepdims=True))
    gp = ge / jnp.sum(ge, axis=0, keepdims=True)
    g_top = jnp.max(gp, axis=0, keepdims=True)
    g_sel = jnp.min(jnp.where(gp == g_top, ridx, SUBLANES), axis=0, keepdims=True)
    e_in = e_all[(N_EXPERT_GROUPS - 1) * SUBLANES:]
    for gi in range(N_EXPERT_GROUPS - 2, -1, -1):
        e_in = jnp.where(g_sel == gi, e_all[gi * SUBLANES:(gi + 1) * SUBLANES], e_in)
    ee = jnp.exp(e_in - jnp.max(e_in, axis=0, keepdims=True))
    ep = ee / jnp.sum(ee, axis=0, keepdims=True)
    v1 = jnp.max(ep, axis=0, keepdims=True)
    i1 = jnp.min(jnp.where(ep == v1, ridx, SUBLANES), axis=0, keepdims=True)
    ep2 = jnp.where(ridx == i1, -1.0, ep)
    v2 = jnp.max(ep2, axis=0, keepdims=True)
    i2 = jnp.min(jnp.where(ep2 == v2, ridx, SUBLANES), axis=0, keepdims=True)
    den = v1 + v2
    w1 = g_top * v1 / den
    w2 = g_top * v2 / den
    e1 = g_sel * EXPERTS_PER_GROUP + i1
    e2 = g_sel * EXPERTS_PER_GROUP + i2
    ids = jnp.where(ridx == 0, e1, jnp.where(ridx == 1, e2, 0))
    wts = jnp.where(ridx == 0, w1, jnp.where(ridx == 1, w2, 0.0))
    return ids, wts


def _mix_kernel(x_ref, ya_ref, ys_ref, u_ref, g1_ref, wgate_ref, dsk_ref, wglu_ref, wpa_ref, wps_ref,
                wout_ref, g2_ref, wrh_ref, wrl_ref, br_ref, x1_ref, xp_ref, logits_ref, slots_ref, zero_scr, zsem):
    d = x_ref.shape[1]
    rows = x_ref.shape[0] // MIX_CHAINS
    i = pl.program_id(0)

    @pl.when(i == 0)
    def _():
        zero_scr[...] = jnp.zeros(zero_scr.shape, zero_scr.dtype)

    zrows = zero_scr.shape[0]
    zero_copy = pltpu.make_async_copy(zero_scr, slots_ref.at[pl.ds(pl.multiple_of(i * zrows, SUBLANES), zrows)], zsem)
    zero_copy.start()
    for chain in range(MIX_CHAINS):
        r = slice(chain * rows, (chain + 1) * rows)
        x = x_ref[r, :]
        xb = _rmsnorm_rows(x, g1_ref[...]).astype(BF16)
        gates = _dot(xb, wgate_ref[...])
        y = ys_ref[r, :].astype(F32) + dsk_ref[...] * u_ref[r, :].astype(F32)
        y = y * (0.5 * (1.0 + jnp.tanh(math.sqrt(2.0 / math.pi) * (y + 0.044715 * (y * y * y)))))
        y = y * _sigmoid(_dot(y.astype(BF16), wglu_ref[...]))
        mixed = (_sigmoid(gates[:, :d]) * _dot(ya_ref[r, :], wpa_ref[...])
                 + _sigmoid(gates[:, d:]) * _dot(y.astype(BF16), wps_ref[...]))
        x1 = x + _dot(mixed.astype(BF16), wout_ref[...])
        x1_ref[r, :] = x1
        xn = _rmsnorm_rows(x1, g2_ref[...])
        xp_ref[r, :] = pltpu.pack_elementwise([xn[:, :d // 2], xn[:, d // 2:]], packed_dtype=BF16)
        xh = xn.astype(BF16)
        xl = (xn - xh.astype(F32)).astype(BF16)
        wrh = wrh_ref[...]
        logits_ref[:, r] = _nt_dot(wrh, xh) + _nt_dot(wrl_ref[...], xh) + _nt_dot(wrh, xl) + br_ref[...]
    zero_copy.wait()


def _mix(x2, ya, ys, u, g1, wgate, dsk, wglu, wpa, wps, wout, g2, wrh, wrl, br, *, n_slots):
    t, d = x2.shape
    tm = TM_MIX
    steps = t // tm
    assert n_slots % steps == 0 and (n_slots // steps) % SUBLANES == 0
    full = lambda a: pl.BlockSpec(a.shape, lambda i: (0,) * a.ndim)
    row = lambda width: pl.BlockSpec((tm, width), lambda i: (i, 0))
    return pl.pallas_call(
        _mix_kernel,
        grid=(steps,),
        in_specs=[row(d), row(ya.shape[1]), row(ys.shape[1]), row(u.shape[1]), full(g1), full(wgate), full(dsk),
                  full(wglu), full(wpa),
                  full(wps), full(wout), full(g2), full(wrh), full(wrl), full(br)],
        out_specs=[row(d), row(d // 2), pl.BlockSpec((ROUTER_ROWS, tm), lambda i: (0, i)),
                   pl.BlockSpec(memory_space=pl.ANY)],
        out_shape=[jax.ShapeDtypeStruct((t, d), F32), jax.ShapeDtypeStruct((t, d // 2), jnp.uint32),
                   jax.ShapeDtypeStruct((ROUTER_ROWS, t), F32), jax.ShapeDtypeStruct((n_slots, d // 2), jnp.uint32)],
        scratch_shapes=[pltpu.VMEM((n_slots // steps, d // 2), jnp.uint32), pltpu.SemaphoreType.DMA(())],
        compiler_params=pltpu.CompilerParams(dimension_semantics=("arbitrary",), vmem_limit_bytes=VMEM_LIMIT),
        name="mix",
    )(x2, ya, ys, u, g1, wgate, dsk, wglu, wpa, wps, wout, g2, wrh, wrl, br)


def _route_kernel(logits_ref, ids_ref, wts_ref):
    ids, wts = _route(logits_ref[...])
    ids_ref[...] = ids
    wts_ref[...] = wts


def _route_call(logits):
    t = logits.shape[1]
    tl = ROUTE_LANES
    col = pl.BlockSpec((SUBLANES, tl), lambda i: (0, i))
    return pl.pallas_call(
        _route_kernel,
        grid=(t // tl,),
        in_specs=[pl.BlockSpec((ROUTER_ROWS, tl), lambda i: (0, i))],
        out_specs=[col, col],
        out_shape=[jax.ShapeDtypeStruct((SUBLANES, t), I32), jax.ShapeDtypeStruct((SUBLANES, t), F32)],
        compiler_params=pltpu.CompilerParams(dimension_semantics=("arbitrary",)),
        name="route",
    )(logits)


def _meta_kernel(ids_ref, tri_ref, dest_ref, tile_ref, *, tile_rows):
    nk, nc, c = ids_ref.shape
    ne = N_EXPERTS
    erow = lax.broadcasted_iota(I32, (ne, c), 0)
    ones = jnp.ones((c, LANES), BF16)

    def onehot(k, ci):
        mask = erow == ids_ref[k, pl.ds(ci, 1), :]
        return mask, jnp.where(mask, 1.0, 0.0).astype(BF16)

    def count_body(n, acc):
        return acc + _dot(onehot(n // nc, n % nc)[1], ones)

    cnt = lax.fori_loop(0, nk * nc, count_body, jnp.zeros((ne, LANES), F32))
    ntiles = jnp.floor((cnt + (tile_rows - 1)) * (1.0 / tile_rows))
    lower = jnp.where(lax.broadcasted_iota(I32, (ne, ne), 1) < lax.broadcasted_iota(I32, (ne, ne), 0), 1.0, 0.0)
    start_tiles = _dot(lower.astype(BF16), ntiles.astype(BF16))
    base = start_tiles * tile_rows

    tri = tri_ref[...]

    def dest_body(n, carry):
        k, ci = n // nc, n % nc
        mask, oh = onehot(k, ci)
        prefix = _dot(oh, tri)
        slot = _lane_tile(base + carry, c // LANES) + prefix - 1.0
        dest = jnp.sum(jnp.where(mask, slot, 0.0), axis=0, keepdims=True)
        dest_ref[k, pl.ds(ci, 1), :] = dest.astype(I32)
        return carry + _dot(oh, ones)

    lax.fori_loop(0, nk * nc, dest_body, jnp.zeros((ne, LANES), F32))

    nt_lanes = tile_ref.shape[1]
    end_tiles = _lane_tile(start_tiles + ntiles, nt_lanes // LANES)
    tidx = lax.broadcasted_iota(I32, (ne, nt_lanes), 1).astype(F32)
    texp = jnp.sum(jnp.where(tidx >= end_tiles, 1.0, 0.0), axis=0, keepdims=True)
    valid = jnp.where(texp < ne, 1, 0)
    texp = jnp.minimum(texp, ne - 1.0).astype(I32)
    ridx = lax.broadcasted_iota(I32, tile_ref.shape, 0)
    tile_ref[...] = jnp.where(ridx == 0, texp, jnp.where(ridx == 1, valid, 0))


def _meta(ids2, tri, n_tiles):
    nk, t = ids2.shape
    c = SORT_CHUNK
    nt_lanes = pl.cdiv(n_tiles, LANES) * LANES
    ids3 = ids2.reshape(nk, t // c, c)
    dest, tile = pl.pallas_call(
        functools.partial(_meta_kernel, tile_rows=TM_EXPERT),
        out_shape=[jax.ShapeDtypeStruct(ids3.shape, I32), jax.ShapeDtypeStruct((SUBLANES, nt_lanes), I32)],
        compiler_params=pltpu.CompilerParams(vmem_limit_bytes=VMEM_LIMIT),
        name="meta",
    )(ids3, tri)
    return dest.reshape(nk * t), tile[0, :n_tiles], tile[1, :n_tiles]


def _row_copy(src_ref, src_row, dst_ref, dst_row, sem):
    return pltpu.make_async_copy(src_ref.at[pl.ds(src_row, 1)], dst_ref.at[pl.ds(dst_row, 1)], sem)


def _dispatch_kernel(dest_ref, xp_ref, xs_in_ref, xs_ref, sem, *, n_tokens):
    del xs_in_ref
    tm = xp_ref.shape[0]
    base = pl.program_id(0) * tm

    def body(r, carry):
        for k in range(2):
            _row_copy(xp_ref, r, xs_ref, dest_ref[k * n_tokens + base + r], sem).start()
        return carry

    lax.fori_loop(0, tm, body, 0, unroll=ROW_DMA_UNROLL)
    for k in range(2):
        pltpu.make_async_copy(xp_ref, xs_ref.at[pl.ds(0, tm)], sem).wait()


def _dispatch(dest, xp, xs0):
    t, w = xp.shape
    tm = TM_DISPATCH
    return pl.pallas_call(
        functools.partial(_dispatch_kernel, n_tokens=t),
        grid_spec=pltpu.PrefetchScalarGridSpec(
            num_scalar_prefetch=1,
            grid=(t // tm,),
            in_specs=[pl.BlockSpec((tm, w), lambda i, d: (i, 0)), pl.BlockSpec(memory_space=pl.ANY)],
            out_specs=pl.BlockSpec(memory_space=pl.ANY),
            scratch_shapes=[pltpu.SemaphoreType.DMA(())]),
        out_shape=jax.ShapeDtypeStruct(xs0.shape, xp.dtype),
        input_output_aliases={2: 0},
        compiler_params=pltpu.CompilerParams(dimension_semantics=("arbitrary",), has_side_effects=True),
        name="dispatch",
    )(dest, xp, xs0)


def _unpack_rows(packed):
    lo = pltpu.unpack_elementwise(packed, index=0, packed_dtype=BF16, unpacked_dtype=F32)
    hi = pltpu.unpack_elementwise(packed, index=1, packed_dtype=BF16, unpacked_dtype=F32)
    return jnp.concatenate([lo, hi], axis=1)


def _expert_kernel(te_ref, tv_ref, nv_ref, nxt_ref, xs_ref, wg_hbm, wu_hbm, wd_hbm, ys_ref,
                   wg_bf, wu_bf, wd_bf, wg_f32, wu_f32, wd_f32, slot_ref, sem):
    del nv_ref
    i = pl.program_id(0)
    half = xs_ref.shape[1]

    def weight_copies(expert, slot):
        return [pltpu.make_async_copy(hbm.at[expert], buf.at[slot], sem.at[slot])
                for hbm, buf in ((wg_hbm, wg_f32), (wu_hbm, wu_f32), (wd_hbm, wd_f32))]

    @pl.when(i == 0)
    def _():
        slot_ref[0] = 0
        for copy in weight_copies(te_ref[0], 0):
            copy.start()

    @pl.when((i == 0) | (te_ref[i] != te_ref[jnp.maximum(i - 1, 0)]))
    def _():
        slot = slot_ref[0]
        for copy in weight_copies(te_ref[i], slot):
            copy.wait()
        wg_bf[...] = wg_f32[slot].astype(BF16)
        wu_bf[...] = wu_f32[slot].astype(BF16)
        wd_bf[...] = wd_f32[slot].astype(BF16)
        slot_ref[0] = 1 - slot

        @pl.when(nxt_ref[i] >= 0)
        def _():
            for copy in weight_copies(nxt_ref[i], 1 - slot):
                copy.start()

    @pl.when(tv_ref[i] > 0)
    def _():
        x = _unpack_rows(xs_ref[...]).astype(BF16)
        hg = _dot(x, wg_bf[...])
        hu = _dot(x, wu_bf[...])
        h = (hg * _sigmoid(hg) * hu).astype(BF16)
        y = _dot(h, wd_bf[...])
        ys_ref[...] = pltpu.pack_elementwise([y[:, :half], y[:, half:]], packed_dtype=BF16)

    @pl.when(tv_ref[i] == 0)
    def _():
        zero = jnp.zeros(ys_ref.shape, F32)
        ys_ref[...] = pltpu.pack_elementwise([zero, zero], packed_dtype=BF16)


def _experts(tile_expert, tile_valid, xs, wg, wu, wd):
    n_slots, w = xs.shape
    tm = TM_EXPERT
    n_tiles = n_slots // tm
    n_valid = jnp.sum(tile_valid).astype(I32).reshape(1)
    tile_expert = jnp.where(tile_valid > 0, tile_expert, tile_expert[n_valid[0] - 1])
    later = (tile_expert[None, :] > tile_expert[:, None]) & (tile_valid[None, :] > 0)
    next_expert = jnp.min(jnp.where(later, tile_expert[None, :], N_EXPERTS), axis=1)
    next_expert = jnp.where(next_expert < N_EXPERTS, next_expert, -1).astype(I32)
    rows_in = pl.BlockSpec((tm, w), lambda i, te, tv, nv, nx: (jnp.minimum(i, nv[0] - 1), 0))
    rows_out = pl.BlockSpec((tm, w), lambda i, te, tv, nv, nx: (i, 0))
    hbm = pl.BlockSpec(memory_space=pl.ANY)
    weights = (wg, wu, wd)
    return pl.pallas_call(
        _expert_kernel,
        grid_spec=pltpu.PrefetchScalarGridSpec(
            num_scalar_prefetch=4,
            grid=(n_tiles,),
            in_specs=[rows_in, hbm, hbm, hbm],
            out_specs=rows_out,
            scratch_shapes=[pltpu.VMEM(a.shape[1:], BF16) for a in weights]
            + [pltpu.VMEM((2,) + a.shape[1:], a.dtype) for a in weights]
            + [pltpu.SMEM((1,), I32), pltpu.SemaphoreType.DMA((2,))]),
        out_shape=jax.ShapeDtypeStruct(xs.shape, xs.dtype),
        compiler_params=pltpu.CompilerParams(dimension_semantics=("arbitrary",), vmem_limit_bytes=VMEM_LIMIT),
        name="experts",
    )(tile_expert, tile_valid, n_valid, next_expert, xs, wg, wu, wd)


def _combine_kernel(dest_ref, x1_ref, w0_ref, w1_ref, ys_ref, out_ref, buf, sem, *, n_tokens):
    tm, d = x1_ref.shape
    base = pl.program_id(0) * tm

    def body(r, carry):
        for k in range(2):
            _row_copy(ys_ref, dest_ref[k * n_tokens + base + r], buf.at[k], r, sem).start()
        return carry

    lax.fori_loop(0, tm, body, 0, unroll=ROW_DMA_UNROLL)
    for k in range(2):
        pltpu.make_async_copy(ys_ref.at[pl.ds(0, tm)], buf.at[k], sem).wait()
    reps = d // LANES
    out_ref[...] = (x1_ref[...]
                    + _lane_tile(w0_ref[...], reps) * _unpack_rows(buf[0])
                    + _lane_tile(w1_ref[...], reps) * _unpack_rows(buf[1]))


def _combine(dest, x1, w0, w1, ys):
    t, d = x1.shape
    tm = TM_COMBINE
    row = lambda width: pl.BlockSpec((tm, width), lambda i, dref: (i, 0))
    return pl.pallas_call(
        functools.partial(_combine_kernel, n_tokens=t),
        grid_spec=pltpu.PrefetchScalarGridSpec(
            num_scalar_prefetch=1,
            grid=(t // tm,),
            in_specs=[row(d), row(LANES), row(LANES), pl.BlockSpec(memory_space=pl.ANY)],
            out_specs=row(d),
            scratch_shapes=[pltpu.VMEM((2, tm, ys.shape[-1]), ys.dtype), pltpu.SemaphoreType.DMA(())]),
        out_shape=jax.ShapeDtypeStruct((t, d), x1.dtype),
        compiler_params=pltpu.CompilerParams(dimension_semantics=("arbitrary",), vmem_limit_bytes=VMEM_LIMIT),
        name="combine",
    )(dest, x1, w0, w1, ys)


def _lower_tri(n):
    return jnp.where(jnp.arange(n)[:, None] >= jnp.arange(n)[None, :], 1.0, 0.0).astype(BF16)


def _upper_tri(n):
    return jnp.where(jnp.arange(n)[:, None] <= jnp.arange(n)[None, :], 1.0, 0.0).astype(BF16)


def _split_bf16(w):
    hi = w.astype(BF16)
    lo = (w - hi.astype(F32)).astype(BF16)
    return hi, lo


def _layer(x, norm_mix_g, w_in, b_forget, q_norm_g, k_norm_g, ssm_A_re, ssm_A_im, ssm_log_dt, ssm_B_re, ssm_B_im,
           ssm_C_re, ssm_C_im, ssm_D, w_glu, w_proj_attn, w_proj_ssm, w_out, norm_ffn_g, w_router_group,
           b_router_group, w_router_expert, b_router_expert, w_expert_gate, w_expert_up, w_expert_down):
    b, s, d = x.shape
    t = b * s
    da = N_HEADS * HEAD_DIM
    dssm = ssm_D.shape[0]
    nk = s // SUPER
    assert s % SUPER == 0 and nk % SUBLANES == 0 and nk & (nk - 1) == 0 and s % TQ_ATTN == 0 and t % TM_PROJ == 0
    assert TM_PROJ == TS_ATTN
    x2 = x.reshape(t, d)

    o_f = 3 * da
    o_u = o_f + N_HEADS
    o_g = o_u + dssm
    w_main = jnp.concatenate([w_in[:, :o_f], w_in[:, o_u:o_g]], axis=1).astype(BF16)
    wf = jnp.repeat(w_in[:, o_f:o_u], FORGET_COPIES_STRIDE, axis=1).astype(BF16)
    bf = jnp.repeat(b_forget, FORGET_COPIES_STRIDE)[None, :]
    w_gates = w_in[:, o_g:].astype(BF16)
    qg = (jnp.tile(q_norm_g, N_HEADS) * (HEAD_DIM ** -0.5 * LOG2E))[None, :]
    kg = jnp.tile(k_norm_g, N_HEADS)[None, :]
    head_of = jnp.arange(da) // HEAD_DIM
    bd = jnp.where(head_of[:, None] == head_of[None, :], 1.0 / HEAD_DIM, 0.0).astype(BF16)
    logit_bound = 1.02 * HEAD_DIM ** 0.5 * jnp.max(jnp.abs(q_norm_g)) * jnp.max(jnp.abs(k_norm_g))
    plain_ok = LOG2E * logit_bound <= PLAIN_SOFTMAX_MAX_LOG2
    shift = jnp.full((1, LANES), logit_bound, F32)
    hw = N_HEADS * LANES
    lane_in_head = jnp.arange(hw) % LANES
    ones = jnp.stack([
        (lane_in_head >= BIAS_K_LANE) & (lane_in_head < BIAS_K_LANE + 3),
        (lane_in_head >= BIAS_Q_LANE) & (lane_in_head < BIAS_Q_LANE + 3),
        lane_in_head == HEAD_DIM]
        + [jnp.zeros((hw,), bool)] * (SUBLANES - 3)).astype(F32)
    prow = jnp.arange(LANES)
    head, copy = prow // FORGET_COPIES_STRIDE, prow % FORGET_COPIES_STRIDE
    target = jnp.where(copy < 3, head * LANES + BIAS_Q_LANE + copy, hw + head * LANES + BIAS_K_LANE + copy - 3)
    place = ((jnp.arange(2 * hw)[None, :] == target[:, None]) & (copy < 6)[:, None]).astype(BF16)

    q, k, v, u, fedge = _proj(x2, norm_mix_g[None, :], w_main, wf, bf, qg, kg, bd, _lower_tri(TM_PROJ), place, ones,
                              shift, seq=s)
    y_attn = _attention(q.reshape(b, s, hw), k.reshape(b, s, hw), v.reshape(b, s, hw), fedge, plain_ok)

    y_ssm = _s5(u.reshape(b, s, dssm),
                *_s5_params(ssm_A_re, ssm_A_im, ssm_log_dt, ssm_B_re, ssm_B_im, ssm_C_re, ssm_C_im, s // SUPER))

    wr = jnp.concatenate([w_router_expert.T, w_router_group.T,
                          jnp.zeros((ROUTER_ROWS - N_EXPERTS - N_EXPERT_GROUPS, d), F32)], axis=0)
    br = jnp.concatenate([b_router_expert, b_router_group,
                          jnp.full((SUBLANES - N_EXPERT_GROUPS,), NEG_INF, F32),
                          jnp.zeros((ROUTER_ROWS - N_EXPERTS - SUBLANES,), F32)])[:, None]
    n_tiles = (2 * t) // TM_EXPERT + N_EXPERTS
    x1, xp, logits, xs0 = _mix(x2, y_attn.reshape(t, da), y_ssm.reshape(t, dssm), u, norm_mix_g[None, :], w_gates,
                               ssm_D[None, :], w_glu.astype(BF16), w_proj_attn.astype(BF16),
                               w_proj_ssm.astype(BF16), w_out.astype(BF16), norm_ffn_g[None, :], *_split_bf16(wr),
                               br, n_slots=n_tiles * TM_EXPERT)
    ids, wts = _route_call(logits)

    dest, tile_expert, tile_valid = _meta(ids[:2], _upper_tri(SORT_CHUNK), n_tiles)
    xs = _dispatch(dest, xp, xs0)
    ys = _experts(tile_expert, tile_valid, xs, w_expert_gate, w_expert_up, w_expert_down)
    w0 = jnp.broadcast_to(wts[0][:, None], (t, LANES))
    w1 = jnp.broadcast_to(wts[1][:, None], (t, LANES))
    out = _combine(dest, x1, w0, w1, ys)
    return out.reshape(b, s, d)


def kernel(x, norm_mix_g, w_in, b_forget, q_norm_g, k_norm_g, ssm_A_re, ssm_A_im, ssm_log_dt, ssm_B_re, ssm_B_im,
           ssm_C_re, ssm_C_im, ssm_D, w_glu, w_proj_attn, w_proj_ssm, w_out, norm_ffn_g, w_router_group,
           b_router_group, w_router_expert, b_router_expert, w_expert_gate, w_expert_up, w_expert_down):
    layer_params = (norm_mix_g, w_in, b_forget, q_norm_g, k_norm_g, ssm_A_re, ssm_A_im, ssm_log_dt, ssm_B_re,
                    ssm_B_im, ssm_C_re, ssm_C_im, ssm_D, w_glu, w_proj_attn, w_proj_ssm, w_out, norm_ffn_g,
                    w_router_group, b_router_group, w_router_expert, b_router_expert, w_expert_gate, w_expert_up,
                    w_expert_down)
    for layer in range(norm_mix_g.shape[0]):
        x = _layer(x, *[p[layer] for p in layer_params])
    return x
```

```python
import functools
import math

import jax
import jax.numpy as jnp
from jax import lax
from jax.experimental import pallas as pl
from jax.experimental.pallas import tpu as pltpu

F32 = jnp.float32
BF16 = jnp.bfloat16
I32 = jnp.int32

LANES = 128
SUBLANES = 8
MXU_DIM = 256

N_HEADS = 8
HEAD_DIM = 64
SSM_GROUP = 16
SSM_STATE = 64
N_EXPERT_GROUPS = 4
EXPERTS_PER_GROUP = 8
N_EXPERTS = N_EXPERT_GROUPS * EXPERTS_PER_GROUP
EPS = 1e-6
NEG_INF = -1e30
LOG2E = math.log2(math.e)
PLAIN_SOFTMAX_MAX_LOG2 = 60.0
UNDERFLOW_LOG2 = 160.0
BIAS_K_LANE = HEAD_DIM
BIAS_Q_LANE = HEAD_DIM + 3
FORGET_COPIES_STRIDE = LANES // N_HEADS

CHUNK = MXU_DIM // SSM_GROUP
CHUNKS_PER_SUPER = 8
SUPER = CHUNK * CHUNKS_PER_SUPER
GROUPS_PER_BLOCK = LANES // SSM_GROUP
RELAYOUT_UNROLL = 16
GROUP_UNROLL = 4
MIN_FACTORED_DECAY = -4.0

TM_PROJ = 512
TQ_ATTN = 2048
TS_ATTN = 512
HEADS_PER_STEP = 2
TM_MIX = 512
MIX_CHAINS = 1
TM_EXPERT = 512
TM_DISPATCH = 1024
TM_COMBINE = 1024
SORT_CHUNK = 1024
ROW_DMA_UNROLL = 16
ROUTE_LANES = 2048
ROUTER_ROWS = 48

VMEM_LIMIT = 48 * 1024 * 1024


def _nt_dot(a, b):
    return lax.dot_general(a, b, (((1,), (1,)), ((), ())), preferred_element_type=F32)


def _dot(a, b):
    return jnp.dot(a, b, preferred_element_type=F32)


def _lane_tile(x, n):
    return x if n == 1 else jnp.concatenate([x] * n, axis=1)


def _rmsnorm_rows(x, g):
    ms = jnp.mean(x * x, axis=-1, keepdims=True)
    return x * lax.rsqrt(ms + EPS) * g


def _sigmoid(x):
    return 1.0 / (1.0 + jnp.exp(-x))


def _split3(x):
    hi = x.astype(BF16)
    r1 = x - hi.astype(F32)
    mid = r1.astype(BF16)
    lo = (r1 - mid.astype(F32)).astype(BF16)
    return [hi, mid, lo]


def _expand_heads(z):
    lane = lax.broadcasted_iota(I32, (z.shape[0], LANES), 1)
    blocks = []
    for p in range(z.shape[1] // LANES):
        blk = z[:, p * LANES:(p + 1) * LANES]
        blocks.append(jnp.where(lane < HEAD_DIM, blk, 0.0))
        blocks.append(jnp.where(lane < HEAD_DIM, pltpu.roll(blk, HEAD_DIM, axis=1), 0.0))
    return jnp.concatenate(blocks, axis=1)


def _proj_kernel(x_ref, g_ref, w_ref, wf_ref, bf_ref, qg_ref, kg_ref, bd_ref, ltri_ref, place_ref,
                 ones_ref, shift_ref, q_ref, k_ref, v_ref, u_ref, fedge_ref, carry_ref, *, tiles_per_seq):
    i = pl.program_id(0)

    @pl.when(i % tiles_per_seq == 0)
    def _():
        carry_ref[...] = jnp.zeros_like(carry_ref)

    tm = x_ref.shape[0]
    da = N_HEADS * HEAD_DIM
    hw = N_HEADS * LANES
    xb = _rmsnorm_rows(x_ref[...], g_ref[...]).astype(BF16)
    z = _dot(xb, w_ref[...])
    zq, zk = z[:, :da], z[:, da:2 * da]
    bd = bd_ref[...]
    msq = _dot((zq * zq).astype(BF16), bd)
    msk = _dot((zk * zk).astype(BF16), bd)
    qn = zq * lax.rsqrt(msq + EPS) * qg_ref[...]
    kn = zk * lax.rsqrt(msk + EPS) * kg_ref[...]
    u_ref[...] = z[:, 3 * da:]

    a = _dot(xb, wf_ref[...]) + bf_ref[...]
    logf = jnp.minimum(a, 0.0) - jnp.log(1.0 + jnp.exp(-jnp.abs(a)))
    cs = _dot(ltri_ref[...], jnp.concatenate(_split3(logf), axis=1))
    cum = cs[:, :LANES] + cs[:, LANES:2 * LANES] + cs[:, 2 * LANES:] + carry_ref[0:1, :]
    carry_ref[...] = jnp.broadcast_to(cum[tm - 1:tm, :], carry_ref.shape)
    edge_row = lax.broadcasted_iota(I32, fedge_ref.shape, 0)
    fedge_ref[...] = jnp.where(edge_row == 0, cum[0:1, :], jnp.where(edge_row == 1, cum[tm - 1:tm, :], 0.0))
    copy = lax.broadcasted_iota(I32, cum.shape, 1) % FORGET_COPIES_STRIDE
    bias = jnp.where(copy < 3, (cum - shift_ref[...]) * LOG2E, cum * (-LOG2E))
    hi, mid, lo = [p.astype(F32) for p in _split3(bias)]
    piece = jnp.where(copy % 3 == 0, hi, jnp.where(copy % 3 == 1, mid, lo)).astype(BF16)
    extras = _dot(piece, place_ref[...])
    q_ref[...] = (_expand_heads(qn) + extras[:, :hw] + ones_ref[0:1, :]).astype(BF16)
    k_ref[...] = (_expand_heads(kn) + extras[:, hw:] + ones_ref[1:2, :]).astype(BF16)
    v_ref[...] = (_expand_heads(z[:, 2 * da:3 * da]) + ones_ref[2:3, :]).astype(BF16)


def _proj(x2, norm_g, w_main, wf, b_f, qg, kg, bd, ltri, place, ones, shift, *, seq):
    t, d = x2.shape
    tm = TM_PROJ
    hw = N_HEADS * LANES
    du = w_main.shape[1] - 3 * N_HEADS * HEAD_DIM
    tiles_per_seq = seq // tm
    full = lambda a: pl.BlockSpec(a.shape, lambda i: (0,) * a.ndim)
    row = lambda width: pl.BlockSpec((tm, width), lambda i: (i, 0))
    consts = (norm_g, w_main, wf, b_f, qg, kg, bd, ltri, place, ones, shift)
    return pl.pallas_call(
        functools.partial(_proj_kernel, tiles_per_seq=tiles_per_seq),
        grid=(t // tm,),
        in_specs=[row(d)] + [full(a) for a in consts],
        out_specs=[row(hw), row(hw), row(hw), row(du), pl.BlockSpec((None, SUBLANES, LANES), lambda i: (i, 0, 0))],
        out_shape=[jax.ShapeDtypeStruct((t, hw), BF16)] * 3
        + [jax.ShapeDtypeStruct((t, du), F32), jax.ShapeDtypeStruct((t // tm, SUBLANES, LANES), F32)],
        scratch_shapes=[pltpu.VMEM((SUBLANES, LANES), F32)],
        compiler_params=pltpu.CompilerParams(dimension_semantics=("arbitrary",), vmem_limit_bytes=VMEM_LIMIT),
        name="proj",
    )(x2, *consts)


def _attn_kernel(first_ref, q_ref, k_ref, v_ref, o_ref, acc_scr, m_scr, *, ts, safe):
    i = pl.program_id(2)
    nsub = q_ref.shape[0] // ts
    acc_scr[...] = jnp.zeros(acc_scr.shape, F32)
    if safe:
        m_scr[...] = jnp.full(m_scr.shape, NEG_INF, F32)

    def block(hd, sub, j, masked):
        off = pl.multiple_of(j * ts, ts)
        lanes = slice(hd * LANES, (hd + 1) * LANES)
        chain = hd * nsub + sub
        s = _nt_dot(q_ref[sub * ts:(sub + 1) * ts, lanes], k_ref[pl.ds(off, ts), lanes])
        if masked:
            rows = lax.broadcasted_iota(I32, (ts, ts), 0)
            cols = lax.broadcasted_iota(I32, (ts, ts), 1)
            s = jnp.where(cols <= rows, s, NEG_INF)
        vblk = v_ref[pl.ds(off, ts), lanes]
        if safe:
            m_prev = m_scr[chain]
            m_new = jnp.maximum(m_prev, jnp.max(s, axis=1, keepdims=True))
            p = jnp.exp2(s - _lane_tile(m_new, ts // LANES)).astype(BF16)
            acc_scr[chain] = jnp.exp2(m_prev - m_new) * acc_scr[chain] + _dot(p, vblk)
            m_scr[chain] = m_new
        else:
            acc_scr[chain] += _dot(jnp.exp2(s).astype(BF16), vblk)

    def body(heads, j, carry):
        for hd in heads:
            for sub in range(nsub):
                block(hd, sub, j, False)
        return carry

    step = (pl.program_id(0) * pl.num_programs(1) + pl.program_id(1)) * pl.num_programs(2) + i
    starts = [first_ref[step * HEADS_PER_STEP + hd] for hd in range(HEADS_PER_STEP)]
    shared = functools.reduce(jnp.maximum, starts)
    for hd in range(HEADS_PER_STEP):
        lax.fori_loop(starts[hd], shared, functools.partial(body, (hd,)), 0)
    lax.fori_loop(shared, nsub * i, functools.partial(body, tuple(range(HEADS_PER_STEP))), 0)
    for jj in range(nsub):
        for hd in range(HEADS_PER_STEP):
            for sub in range(jj, nsub):
                block(hd, sub, nsub * i + jj, sub == jj)
    lane = lax.broadcasted_iota(I32, (ts, LANES), 1)
    for sub in range(nsub):
        outs = []
        for hd in range(HEADS_PER_STEP):
            acc = acc_scr[hd * nsub + sub]
            outs.append(acc / acc[:, HEAD_DIM:HEAD_DIM + 1])
        both = jnp.where(lane < HEAD_DIM, outs[0], pltpu.roll(outs[1], HEAD_DIM, axis=1))
        o_ref[sub * ts:(sub + 1) * ts, :] = both.astype(o_ref.dtype)


def _first_kv_block(fedge, b, s, tq, ts):
    nblk = s // ts
    f_first = fedge[:, 0, ::FORGET_COPIES_STRIDE].reshape(b, nblk, N_HEADS).transpose(0, 2, 1)
    f_last = fedge[:, 1, ::FORGET_COPIES_STRIDE].reshape(b, nblk, N_HEADS).transpose(0, 2, 1)
    tile_start = f_first[:, :, ::tq // ts]
    bound = LOG2E * (tile_start[:, :, :, None] - f_last[:, :, None, :])
    first = jnp.sum(bound < -UNDERFLOW_LOG2, axis=-1).astype(I32)
    first = first.reshape(b, N_HEADS // HEADS_PER_STEP, HEADS_PER_STEP, -1).transpose(0, 1, 3, 2)
    return first.reshape(-1)


def _attention(q, k, v, fedge, plain_ok):
    b, s, hw = q.shape
    tq, ts = TQ_ATTN, TS_ATTN
    nq = s // tq
    steps = N_HEADS // HEADS_PER_STEP
    width = HEADS_PER_STEP * LANES
    chains = HEADS_PER_STEP * tq // ts

    def call(safe):
        first = jnp.zeros((b * N_HEADS * nq,), I32) if safe else _first_kv_block(fedge, b, s, tq, ts)
        return pl.pallas_call(
            functools.partial(_attn_kernel, ts=ts, safe=safe),
            grid_spec=pltpu.PrefetchScalarGridSpec(
                num_scalar_prefetch=1,
                grid=(b, steps, nq),
                in_specs=[pl.BlockSpec((None, tq, width), lambda bi, h, i, f: (bi, i, h)),
                          pl.BlockSpec((None, s, width), lambda bi, h, i, f: (bi, 0, h)),
                          pl.BlockSpec((None, s, width), lambda bi, h, i, f: (bi, 0, h))],
                out_specs=pl.BlockSpec((None, tq, LANES), lambda bi, h, i, f: (bi, i, h)),
                scratch_shapes=[pltpu.VMEM((chains, ts, LANES), F32)] * 2),
            out_shape=jax.ShapeDtypeStruct((b, s, steps * LANES), BF16),
            compiler_params=pltpu.CompilerParams(
                dimension_semantics=("arbitrary", "arbitrary", "arbitrary"), vmem_limit_bytes=VMEM_LIMIT),
            name="attn_safe" if safe else "attn",
        )(first, q, k, v)

    return lax.cond(plain_ok, lambda: call(False), lambda: call(True))


def _window_select(pieces, first_group):
    window = lax.broadcasted_iota(I32, pieces[0].shape, 1) // SSM_GROUP
    out = pieces[0]
    for m in range(1, GROUPS_PER_BLOCK):
        out = jnp.where(window == (first_group + m) % GROUPS_PER_BLOCK, pieces[m], out)
    return out


def _s5_kernel(u_ref, win_ref, toep_ref, wout_ref, coef_ref, y_ref, ub_scr, yb_scr, *, nk):
    nj = CHUNKS_PER_SUPER
    gpb = GROUPS_PER_BLOCK
    w = CHUNK * SSM_GROUP
    half = w // 2
    nkk = nk // SUBLANES
    row_stride = SUPER

    def gather_body(it, carry):
        j, kk = it // nkk, it % nkk
        for sp in range(CHUNK // gpb):
            rot = []
            for m in range(gpb):
                row0 = j * CHUNK + sp * gpb + m + kk * (SUBLANES * row_stride)
                src = u_ref[pl.ds(row0, SUBLANES, stride=row_stride), :]
                rot.append(src if m == 0 else pltpu.roll(src, m * SSM_GROUP, axis=1))
            for gl in range(gpb):
                ub_scr[gl, j, pl.ds(kk * SUBLANES, SUBLANES), sp * LANES:(sp + 1) * LANES] = _window_select(rot, gl)
        return carry

    lax.fori_loop(0, nj * nkk, gather_body, 0, unroll=RELAYOUT_UNROLL)

    def swap(val):
        return jnp.concatenate([val[:, half:], val[:, :half]], axis=1)

    def group_body(gl, carry):
        def cmul(val, idx):
            return coef_ref[gl, idx:idx + 1, :] * val + coef_ref[gl, idx + 1:idx + 2, :] * swap(val)

        u = ub_scr[gl].reshape(nj * nk, w).astype(BF16)
        s1 = _dot(u, win_ref[gl]).reshape(nj, nk, w)
        e = jnp.zeros((nk, w), F32)
        local = []
        for j in range(nj):
            local.append(e)
            e = cmul(e, 0) + s1[j]
        kidx = lax.broadcasted_iota(I32, (nk, w), 0)
        x = e
        step, d = 0, 1
        while d < nk:
            shifted = jnp.where(kidx >= d, pltpu.roll(x, d, axis=0), 0.0)
            x = x + cmul(shifted, 2 + 2 * nj + 2 * step)
            step, d = step + 1, d * 2
        x_start = jnp.where(kidx >= 1, pltpu.roll(x, 1, axis=0), 0.0)
        starts = [local[j] + cmul(x_start, 2 + 2 * j) for j in range(nj)]
        p = jnp.concatenate([st[:, :half] for st in starts], axis=0)
        hi = p.astype(BF16)
        lo = (p - hi.astype(F32)).astype(BF16)
        wout_t = wout_ref[gl]
        y = _dot(u, toep_ref[gl]) + _nt_dot(hi, wout_t) + _nt_dot(lo, wout_t)
        yb_scr[gl] = y.reshape(nj, nk, w)
        return carry

    lax.fori_loop(0, gpb, group_body, 0, unroll=GROUP_UNROLL)

    def scatter_body(it, carry):
        j, kk = it // nkk, it % nkk
        for tp in range(CHUNK // gpb):
            src = [yb_scr[gl, j, pl.ds(kk * SUBLANES, SUBLANES), tp * LANES:(tp + 1) * LANES] for gl in range(gpb)]
            for m in range(gpb):
                by_window = [src[(wi - m) % gpb] for wi in range(gpb)]
                window = lax.broadcasted_iota(I32, by_window[0].shape, 1) // SSM_GROUP
                merged = by_window[0]
                for wi in range(1, gpb):
                    merged = jnp.where(window == wi, by_window[wi], merged)
                nat = merged if m == 0 else pltpu.roll(merged, LANES - m * SSM_GROUP, axis=1)
                row0 = j * CHUNK + tp * gpb + m + kk * (SUBLANES * row_stride)
                y_ref[pl.ds(row0, SUBLANES, stride=row_stride), :] = nat
        return carry

    lax.fori_loop(0, nj * nkk, scatter_body, 0, unroll=RELAYOUT_UNROLL)


def _s5_params(a_re, a_im, log_dt, b_re, b_im, c_re, c_im, nk):
    args = (a_re, a_im, log_dt, b_re, b_im, c_re, c_im)
    slowest = jnp.min(a_re * jnp.exp(log_dt)[:, None])
    return lax.cond(slowest > MIN_FACTORED_DECAY, functools.partial(_s5_params_factored, nk=nk),
                    functools.partial(_s5_params_direct, nk=nk), *args)


def _s5_params_factored(a_re, a_im, log_dt, b_re, b_im, c_re, c_im, *, nk):
    g, p = a_re.shape
    c = SSM_GROUP
    gpb = GROUPS_PER_BLOCK
    dt = jnp.exp(log_dt)[:, None]
    adt_r, adt_i = a_re * dt, a_im * dt

    def lam_pow(n):
        nf = jnp.asarray(n, F32)
        nf = (nf[None] if nf.ndim == 1 else nf)[:, :, None]
        mag = jnp.exp(adt_r[:, None, :] * nf)
        ang = adt_i[:, None, :] * nf
        return mag * jnp.cos(ang), mag * jnp.sin(ang)

    gi = jnp.arange(g)[:, None, None] % gpb
    step = ((jnp.arange(CHUNK // gpb)[None, :, None]) * gpb
            + (jnp.arange(gpb)[None, None, :] - gi) % gpb).reshape(g, CHUNK)

    l1r, l1i = lam_pow(jnp.ones((1,)))
    den = a_re * a_re + a_im * a_im
    nr, ni = l1r[:, 0] - 1.0, l1i[:, 0]
    qr = (nr * a_re + ni * a_im) / den
    qi = (ni * a_re - nr * a_im) / den
    bcr = (qr[..., None] * b_re - qi[..., None] * b_im).transpose(0, 2, 1)
    bci = (qr[..., None] * b_im + qi[..., None] * b_re).transpose(0, 2, 1)

    def times_b(pr, pi):
        pr, pi = pr[:, :, None, :], pi[:, :, None, :]
        return pr * bcr[:, None] - pi * bci[:, None], pr * bci[:, None] + pi * bcr[:, None]

    def times_c(pr, pi):
        pr, pi = pr[:, :, None, :], pi[:, :, None, :]
        return c_re[:, None] * pr - c_im[:, None] * pi, c_re[:, None] * pi + c_im[:, None] * pr

    rows = lambda parts: jnp.concatenate(parts, axis=-1).reshape(g, CHUNK * c, -1)
    ir, ii = times_b(*lam_pow(CHUNK - 1 - step))
    win = rows([ir, ii, ii, ir])
    clr, cli = times_c(*lam_pow(step + 1))
    wout_t = rows([clr, -cli])
    ar, ai = times_b(*lam_pow(-step))
    br, bi = times_c(*lam_pow(step))
    toep = jnp.einsum('gnp,gmp->gnm', rows([ar, ai]), rows([br, -bi]), precision=lax.Precision.HIGHEST)
    step_of_lane = jnp.repeat(step, c, axis=1)
    toep = jnp.where(step_of_lane[:, None, :] >= step_of_lane[:, :, None], toep, 0.0)
    return win.astype(BF16), toep.astype(BF16), wout_t.astype(BF16), _s5_scan_coefficients(lam_pow, g, p, nk)


def _s5_scan_coefficients(lam_pow, g, p, nk):
    n_steps = max(nk.bit_length() - 1, 0)
    powers = [CHUNK] + [CHUNK * j for j in range(CHUNKS_PER_SUPER)] + [SUPER * (1 << i) for i in range(n_steps)]
    ar, ai = lam_pow(jnp.array(powers, dtype=jnp.int32))
    c1 = jnp.concatenate([ar, ar, ar, ar], axis=2)
    c2 = jnp.concatenate([-ai, ai, ai, -ai], axis=2)
    coef = jnp.stack([c1, c2], axis=2).reshape(g, 2 * len(powers), 4 * p)
    return jnp.pad(coef, ((0, 0), (0, (-coef.shape[1]) % SUBLANES), (0, 0)))


def _s5_params_direct(a_re, a_im, log_dt, b_re, b_im, c_re, c_im, *, nk):
    g, p = a_re.shape
    c = SSM_GROUP
    hp = lax.Precision.HIGHEST
    dt = jnp.exp(log_dt)[:, None]
    adt_r, adt_i = a_re * dt, a_im * dt

    def lam_pow(n):
        nf = jnp.asarray(n, F32)[None, :, None]
        mag = jnp.exp(adt_r[:, None, :] * nf)
        ang = adt_i[:, None, :] * nf
        return mag * jnp.cos(ang), mag * jnp.sin(ang)

    lr, li = lam_pow(jnp.arange(CHUNK + 1))
    den = a_re * a_re + a_im * a_im
    nr, ni = lr[:, 1] - 1.0, li[:, 1]
    qr = (nr * a_re + ni * a_im) / den
    qi = (ni * a_re - nr * a_im) / den
    bbr = qr[..., None] * b_re - qi[..., None] * b_im
    bbi = qr[..., None] * b_im + qi[..., None] * b_re

    mr = lr[:, :CHUNK, :, None] * bbr[:, None] - li[:, :CHUNK, :, None] * bbi[:, None]
    mi = lr[:, :CHUNK, :, None] * bbi[:, None] + li[:, :CHUNK, :, None] * bbr[:, None]
    kern = (jnp.einsum('gcp,gtpd->gtcd', c_re, mr, precision=hp)
            - jnp.einsum('gcp,gtpd->gtcd', c_im, mi, precision=hp))
    toep = jnp.stack([jnp.pad(kern[:, :CHUNK - s], ((0, 0), (s, 0), (0, 0), (0, 0))) for s in range(CHUNK)],
                     axis=1)
    toep = toep.transpose(0, 1, 4, 2, 3).reshape(g, CHUNK * c, CHUNK * c)
    clr = c_re[:, None] * lr[:, 1:, None, :] - c_im[:, None] * li[:, 1:, None, :]
    cli = c_re[:, None] * li[:, 1:, None, :] + c_im[:, None] * lr[:, 1:, None, :]
    w_re = clr.transpose(0, 3, 1, 2).reshape(g, p, CHUNK * c)
    w_im = (-cli).transpose(0, 3, 1, 2).reshape(g, p, CHUNK * c)
    wout = jnp.concatenate([w_re, w_im], axis=1)
    pr, pi = lr[:, :CHUNK][:, ::-1], li[:, :CHUNK][:, ::-1]
    ir = pr[..., None] * bbr[:, None] - pi[..., None] * bbi[:, None]
    ii = pr[..., None] * bbi[:, None] + pi[..., None] * bbr[:, None]
    ir = ir.transpose(0, 1, 3, 2).reshape(g, CHUNK * c, p)
    ii = ii.transpose(0, 1, 3, 2).reshape(g, CHUNK * c, p)
    win = jnp.concatenate([ir, ii, ii, ir], axis=2)

    gi = jnp.arange(g)[:, None, None] % GROUPS_PER_BLOCK
    si = jnp.arange(CHUNK)[None, :, None]
    ci = jnp.arange(c)[None, None, :]
    lane_of = ((si // GROUPS_PER_BLOCK) * LANES + ((gi + si) % GROUPS_PER_BLOCK) * c + ci).reshape(g, CHUNK * c)
    perm = (lane_of[:, None, :] == jnp.arange(CHUNK * c)[None, :, None]).astype(BF16)
    win = jnp.einsum('gln,gnk->glk', perm, win.astype(BF16))
    toep = jnp.einsum('gln,gnk->glk', perm, toep.astype(BF16))
    toep = jnp.einsum('grn,gln->grl', toep, perm)
    wout_t = jnp.einsum('gln,gpn->glp', perm, wout.astype(BF16))
    return win.astype(BF16), toep.astype(BF16), wout_t.astype(BF16), _s5_scan_coefficients(lam_pow, g, p, nk)


def _s5(u, win, toep, wout_t, coef):
    b, s, dssm = u.shape
    nk = s // SUPER
    nj = CHUNKS_PER_SUPER
    w = CHUNK * SSM_GROUP
    gpb = GROUPS_PER_BLOCK
    nblk = dssm // LANES
    wspec = lambda a: pl.BlockSpec((gpb,) + a.shape[1:], lambda bi, li: (li, 0, 0))
    return pl.pallas_call(
        functools.partial(_s5_kernel, nk=nk),
        grid=(b, nblk),
        in_specs=[pl.BlockSpec((None, s, LANES), lambda bi, li: (bi, 0, li)), wspec(win), wspec(toep),
                  wspec(wout_t), wspec(coef)],
        out_specs=pl.BlockSpec((None, s, LANES), lambda bi, li: (bi, 0, li)),
        out_shape=jax.ShapeDtypeStruct((b, s, dssm), F32),
        scratch_shapes=[pltpu.VMEM((gpb, nj, nk, w), F32)] * 2,
        compiler_params=pltpu.CompilerParams(dimension_semantics=("arbitrary", "arbitrary"),
                                             vmem_limit_bytes=VMEM_LIMIT),
        name="s5",
    )(u, win, toep, wout_t, coef)


def _route(logits):
    e_all = logits[0:N_EXPERTS]
    gl = logits[N_EXPERTS:N_EXPERTS + SUBLANES]
    tm = logits.shape[1]
    ridx = lax.broadcasted_iota(I32, (SUBLANES, tm), 0)
    ge = jnp.exp(gl - jnp.max(gl, axis=0, keepdims=True))
    gp = ge / jnp.sum(ge, axis=0, keepdims=True)
    g_top = jnp.max(gp, axis=0, keepdims=True)
    g_sel = jnp.min(jnp.where(gp == g_top, ridx, SUBLANES), axis=0, keepdims=True)
    e_in = e_all[(N_EXPERT_GROUPS - 1) * SUBLANES:]
    for gi in range(N_EXPERT_GROUPS - 2, -1, -1):
        e_in = jnp.where(g_sel == gi, e_all[gi * SUBLANES:(gi + 1) * SUBLANES], e_in)
    ee = jnp.exp(e_in - jnp.max(e_in, axis=0, keepdims=True))
    ep = ee / jnp.sum(ee, axis=0, keepdims=True)
    v1 = jnp.max(ep, axis=0, keepdims=True)
    i1 = jnp.min(jnp.where(ep == v1, ridx, SUBLANES), axis=0, keepdims=True)
    ep2 = jnp.where(ridx == i1, -1.0, ep)
    v2 = jnp.max(ep2, axis=0, keepdims=True)
    i2 = jnp.min(jnp.where(ep2 == v2, ridx, SUBLANES), axis=0, keepdims=True)
    den = v1 + v2
    w1 = g_top * v1 / den
    w2 = g_top * v2 / den
    e1 = g_sel * EXPERTS_PER_GROUP + i1
    e2 = g_sel * EXPERTS_PER_GROUP + i2
    ids = jnp.where(ridx == 0, e1, jnp.where(ridx == 1, e2, 0))
    wts = jnp.where(ridx == 0, w1, jnp.where(ridx == 1, w2, 0.0))
    return ids, wts


def _mix_kernel(x_ref, ya_ref, ys_ref, u_ref, g1_ref, wgate_ref, dsk_ref, wglu_ref, wpa_ref, wps_ref,
                wout_ref, g2_ref, wrh_ref, wrl_ref, br_ref, x1_ref, xp_ref, logits_ref, slots_ref, zero_scr, zsem):
    d = x_ref.shape[1]
    rows = x_ref.shape[0] // MIX_CHAINS
    i = pl.program_id(0)

    @pl.when(i == 0)
    def _():
        zero_scr[...] = jnp.zeros(zero_scr.shape, zero_scr.dtype)

    zrows = zero_scr.shape[0]
    zero_copy = pltpu.make_async_copy(zero_scr, slots_ref.at[pl.ds(pl.multiple_of(i * zrows, SUBLANES), zrows)], zsem)
    zero_copy.start()
    for chain in range(MIX_CHAINS):
        r = slice(chain * rows, (chain + 1) * rows)
        x = x_ref[r, :]
        xb = _rmsnorm_rows(x, g1_ref[...]).astype(BF16)
        gates = _dot(xb, wgate_ref[...])
        y = ys_ref[r, :].astype(F32) + dsk_ref[...] * u_ref[r, :].astype(F32)
        y = y * (0.5 * (1.0 + jnp.tanh(math.sqrt(2.0 / math.pi) * (y + 0.044715 * (y * y * y)))))
        y = y * _sigmoid(_dot(y.astype(BF16), wglu_ref[...]))
        mixed = (_sigmoid(gates[:, :d]) * _dot(ya_ref[r, :], wpa_ref[...])
                 + _sigmoid(gates[:, d:]) * _dot(y.astype(BF16), wps_ref[...]))
        x1 = x + _dot(mixed.astype(BF16), wout_ref[...])
        x1_ref[r, :] = x1
        xn = _rmsnorm_rows(x1, g2_ref[...])
        xp_ref[r, :] = pltpu.pack_elementwise([xn[:, :d // 2], xn[:, d // 2:]], packed_dtype=BF16)
        xh = xn.astype(BF16)
        xl = (xn - xh.astype(F32)).astype(BF16)
        wrh = wrh_ref[...]
        logits_ref[:, r] = _nt_dot(wrh, xh) + _nt_dot(wrl_ref[...], xh) + _nt_dot(wrh, xl) + br_ref[...]
    zero_copy.wait()


def _mix(x2, ya, ys, u, g1, wgate, dsk, wglu, wpa, wps, wout, g2, wrh, wrl, br, *, n_slots):
    t, d = x2.shape
    tm = TM_MIX
    steps = t // tm
    assert n_slots % steps == 0 and (n_slots // steps) % SUBLANES == 0
    full = lambda a: pl.BlockSpec(a.shape, lambda i: (0,) * a.ndim)
    row = lambda width: pl.BlockSpec((tm, width), lambda i: (i, 0))
    return pl.pallas_call(
        _mix_kernel,
        grid=(steps,),
        in_specs=[row(d), row(ya.shape[1]), row(ys.shape[1]), row(u.shape[1]), full(g1), full(wgate), full(dsk),
                  full(wglu), full(wpa),
                  full(wps), full(wout), full(g2), full(wrh), full(wrl), full(br)],
        out_specs=[row(d), row(d // 2), pl.BlockSpec((ROUTER_ROWS, tm), lambda i: (0, i)),
                   pl.BlockSpec(memory_space=pl.ANY)],
        out_shape=[jax.ShapeDtypeStruct((t, d), F32), jax.ShapeDtypeStruct((t, d // 2), jnp.uint32),
                   jax.ShapeDtypeStruct((ROUTER_ROWS, t), F32), jax.ShapeDtypeStruct((n_slots, d // 2), jnp.uint32)],
        scratch_shapes=[pltpu.VMEM((n_slots // steps, d // 2), jnp.uint32), pltpu.SemaphoreType.DMA(())],
        compiler_params=pltpu.CompilerParams(dimension_semantics=("arbitrary",), vmem_limit_bytes=VMEM_LIMIT),
        name="mix",
    )(x2, ya, ys, u, g1, wgate, dsk, wglu, wpa, wps, wout, g2, wrh, wrl, br)


def _route_kernel(logits_ref, ids_ref, wts_ref):
    ids, wts = _route(logits_ref[...])
    ids_ref[...] = ids
    wts_ref[...] = wts


def _route_call(logits):
    t = logits.shape[1]
    tl = ROUTE_LANES
    col = pl.BlockSpec((SUBLANES, tl), lambda i: (0, i))
    return pl.pallas_call(
        _route_kernel,
        grid=(t // tl,),
        in_specs=[pl.BlockSpec((ROUTER_ROWS, tl), lambda i: (0, i))],
        out_specs=[col, col],
        out_shape=[jax.ShapeDtypeStruct((SUBLANES, t), I32), jax.ShapeDtypeStruct((SUBLANES, t), F32)],
        compiler_params=pltpu.CompilerParams(dimension_semantics=("arbitrary",)),
        name="route",
    )(logits)


def _meta_kernel(ids_ref, tri_ref, dest_ref, tile_ref, *, tile_rows):
    nk, nc, c = ids_ref.shape
    ne = N_EXPERTS
    erow = lax.broadcasted_iota(I32, (ne, c), 0)
    ones = jnp.ones((c, LANES), BF16)

    def onehot(k, ci):
        mask = erow == ids_ref[k, pl.ds(ci, 1), :]
        return mask, jnp.where(mask, 1.0, 0.0).astype(BF16)

    def count_body(n, acc):
        return acc + _dot(onehot(n // nc, n % nc)[1], ones)

    cnt = lax.fori_loop(0, nk * nc, count_body, jnp.zeros((ne, LANES), F32))
    ntiles = jnp.floor((cnt + (tile_rows - 1)) * (1.0 / tile_rows))
    lower = jnp.where(lax.broadcasted_iota(I32, (ne, ne), 1) < lax.broadcasted_iota(I32, (ne, ne), 0), 1.0, 0.0)
    start_tiles = _dot(lower.astype(BF16), ntiles.astype(BF16))
    base = start_tiles * tile_rows

    tri = tri_ref[...]

    def dest_body(n, carry):
        k, ci = n // nc, n % nc
        mask, oh = onehot(k, ci)
        prefix = _dot(oh, tri)
        slot = _lane_tile(base + carry, c // LANES) + prefix - 1.0
        dest = jnp.sum(jnp.where(mask, slot, 0.0), axis=0, keepdims=True)
        dest_ref[k, pl.ds(ci, 1), :] = dest.astype(I32)
        return carry + _dot(oh, ones)

    lax.fori_loop(0, nk * nc, dest_body, jnp.zeros((ne, LANES), F32))

    nt_lanes = tile_ref.shape[1]
    end_tiles = _lane_tile(start_tiles + ntiles, nt_lanes // LANES)
    tidx = lax.broadcasted_iota(I32, (ne, nt_lanes), 1).astype(F32)
    texp = jnp.sum(jnp.where(tidx >= end_tiles, 1.0, 0.0), axis=0, keepdims=True)
    valid = jnp.where(texp < ne, 1, 0)
    texp = jnp.minimum(texp, ne - 1.0).astype(I32)
    ridx = lax.broadcasted_iota(I32, tile_ref.shape, 0)
    tile_ref[...] = jnp.where(ridx == 0, texp, jnp.where(ridx == 1, valid, 0))


def _meta(ids2, tri, n_tiles):
    nk, t = ids2.shape
    c = SORT_CHUNK
    nt_lanes = pl.cdiv(n_tiles, LANES) * LANES
    ids3 = ids2.reshape(nk, t // c, c)
    dest, tile = pl.pallas_call(
        functools.partial(_meta_kernel, tile_rows=TM_EXPERT),
        out_shape=[jax.ShapeDtypeStruct(ids3.shape, I32), jax.ShapeDtypeStruct((SUBLANES, nt_lanes), I32)],
        compiler_params=pltpu.CompilerParams(vmem_limit_bytes=VMEM_LIMIT),
        name="meta",
    )(ids3, tri)
    return dest.reshape(nk * t), tile[0, :n_tiles], tile[1, :n_tiles]


def _row_copy(src_ref, src_row, dst_ref, dst_row, sem):
    return pltpu.make_async_copy(src_ref.at[pl.ds(src_row, 1)], dst_ref.at[pl.ds(dst_row, 1)], sem)


def _dispatch_kernel(dest_ref, xp_ref, xs_in_ref, xs_ref, sem, *, n_tokens):
    del xs_in_ref
    tm = xp_ref.shape[0]
    base = pl.program_id(0) * tm

    def body(r, carry):
        for k in range(2):
            _row_copy(xp_ref, r, xs_ref, dest_ref[k * n_tokens + base + r], sem).start()
        return carry

    lax.fori_loop(0, tm, body, 0, unroll=ROW_DMA_UNROLL)
    for k in range(2):
        pltpu.make_async_copy(xp_ref, xs_ref.at[pl.ds(0, tm)], sem).wait()


def _dispatch(dest, xp, xs0):
    t, w = xp.shape
    tm = TM_DISPATCH
    return pl.pallas_call(
        functools.partial(_dispatch_kernel, n_tokens=t),
        grid_spec=pltpu.PrefetchScalarGridSpec(
            num_scalar_prefetch=1,
            grid=(t // tm,),
            in_specs=[pl.BlockSpec((tm, w), lambda i, d: (i, 0)), pl.BlockSpec(memory_space=pl.ANY)],
            out_specs=pl.BlockSpec(memory_space=pl.ANY),
            scratch_shapes=[pltpu.SemaphoreType.DMA(())]),
        out_shape=jax.ShapeDtypeStruct(xs0.shape, xp.dtype),
        input_output_aliases={2: 0},
        compiler_params=pltpu.CompilerParams(dimension_semantics=("arbitrary",), has_side_effects=True),
        name="dispatch",
    )(dest, xp, xs0)


def _unpack_rows(packed):
    lo = pltpu.unpack_elementwise(packed, index=0, packed_dtype=BF16, unpacked_dtype=F32)
    hi = pltpu.unpack_elementwise(packed, index=1, packed_dtype=BF16, unpacked_dtype=F32)
    return jnp.concatenate([lo, hi], axis=1)


def _expert_kernel(te_ref, tv_ref, nv_ref, nxt_ref, xs_ref, wg_hbm, wu_hbm, wd_hbm, ys_ref,
                   wg_bf, wu_bf, wd_bf, wg_f32, wu_f32, wd_f32, slot_ref, sem):
    del nv_ref
    i = pl.program_id(0)
    half = xs_ref.shape[1]

    def weight_copies(expert, slot):
        return [pltpu.make_async_copy(hbm.at[expert], buf.at[slot], sem.at[slot])
                for hbm, buf in ((wg_hbm, wg_f32), (wu_hbm, wu_f32), (wd_hbm, wd_f32))]

    @pl.when(i == 0)
    def _():
        slot_ref[0] = 0
        for copy in weight_copies(te_ref[0], 0):
            copy.start()

    @pl.when((i == 0) | (te_ref[i] != te_ref[jnp.maximum(i - 1, 0)]))
    def _():
        slot = slot_ref[0]
        for copy in weight_copies(te_ref[i], slot):
            copy.wait()
        wg_bf[...] = wg_f32[slot].astype(BF16)
        wu_bf[...] = wu_f32[slot].astype(BF16)
        wd_bf[...] = wd_f32[slot].astype(BF16)
        slot_ref[0] = 1 - slot

        @pl.when(nxt_ref[i] >= 0)
        def _():
            for copy in weight_copies(nxt_ref[i], 1 - slot):
                copy.start()

    @pl.when(tv_ref[i] > 0)
    def _():
        x = _unpack_rows(xs_ref[...]).astype(BF16)
        hg = _dot(x, wg_bf[...])
        hu = _dot(x, wu_bf[...])
        h = (hg * _sigmoid(hg) * hu).astype(BF16)
        y = _dot(h, wd_bf[...])
        ys_ref[...] = pltpu.pack_elementwise([y[:, :half], y[:, half:]], packed_dtype=BF16)

    @pl.when(tv_ref[i] == 0)
    def _():
        zero = jnp.zeros(ys_ref.shape, F32)
        ys_ref[...] = pltpu.pack_elementwise([zero, zero], packed_dtype=BF16)


def _experts(tile_expert, tile_valid, xs, wg, wu, wd):
    n_slots, w = xs.shape
    tm = TM_EXPERT
    n_tiles = n_slots // tm
    n_valid = jnp.sum(tile_valid).astype(I32).reshape(1)
    tile_expert = jnp.where(tile_valid > 0, tile_expert, tile_expert[n_valid[0] - 1])
    later = (tile_expert[None, :] > tile_expert[:, None]) & (tile_valid[None, :] > 0)
    next_expert = jnp.min(jnp.where(later, tile_expert[None, :], N_EXPERTS), axis=1)
    next_expert = jnp.where(next_expert < N_EXPERTS, next_expert, -1).astype(I32)
    rows_in = pl.BlockSpec((tm, w), lambda i, te, tv, nv, nx: (jnp.minimum(i, nv[0] - 1), 0))
    rows_out = pl.BlockSpec((tm, w), lambda i, te, tv, nv, nx: (i, 0))
    hbm = pl.BlockSpec(memory_space=pl.ANY)
    weights = (wg, wu, wd)
    return pl.pallas_call(
        _expert_kernel,
        grid_spec=pltpu.PrefetchScalarGridSpec(
            num_scalar_prefetch=4,
            grid=(n_tiles,),
            in_specs=[rows_in, hbm, hbm, hbm],
            out_specs=rows_out,
            scratch_shapes=[pltpu.VMEM(a.shape[1:], BF16) for a in weights]
            + [pltpu.VMEM((2,) + a.shape[1:], a.dtype) for a in weights]
            + [pltpu.SMEM((1,), I32), pltpu.SemaphoreType.DMA((2,))]),
        out_shape=jax.ShapeDtypeStruct(xs.shape, xs.dtype),
        compiler_params=pltpu.CompilerParams(dimension_semantics=("arbitrary",), vmem_limit_bytes=VMEM_LIMIT),
        name="experts",
    )(tile_expert, tile_valid, n_valid, next_expert, xs, wg, wu, wd)


def _combine_kernel(dest_ref, x1_ref, w0_ref, w1_ref, ys_ref, out_ref, buf, sem, *, n_tokens):
    tm, d = x1_ref.shape
    base = pl.program_id(0) * tm

    def body(r, carry):
        for k in range(2):
            _row_copy(ys_ref, dest_ref[k * n_tokens + base + r], buf.at[k], r, sem).start()
        return carry

    lax.fori_loop(0, tm, body, 0, unroll=ROW_DMA_UNROLL)
    for k in range(2):
        pltpu.make_async_copy(ys_ref.at[pl.ds(0, tm)], buf.at[k], sem).wait()
    reps = d // LANES
    out_ref[...] = (x1_ref[...]
                    + _lane_tile(w0_ref[...], reps) * _unpack_rows(buf[0])
                    + _lane_tile(w1_ref[...], reps) * _unpack_rows(buf[1]))


def _combine(dest, x1, w0, w1, ys):
    t, d = x1.shape
    tm = TM_COMBINE
    row = lambda width: pl.BlockSpec((tm, width), lambda i, dref: (i, 0))
    return pl.pallas_call(
        functools.partial(_combine_kernel, n_tokens=t),
        grid_spec=pltpu.PrefetchScalarGridSpec(
            num_scalar_prefetch=1,
            grid=(t // tm,),
            in_specs=[row(d), row(LANES), row(LANES), pl.BlockSpec(memory_space=pl.ANY)],
            out_specs=row(d),
            scratch_shapes=[pltpu.VMEM((2, tm, ys.shape[-1]), ys.dtype), pltpu.SemaphoreType.DMA(())]),
        out_shape=jax.ShapeDtypeStruct((t, d), x1.dtype),
        compiler_params=pltpu.CompilerParams(dimension_semantics=("arbitrary",), vmem_limit_bytes=VMEM_LIMIT),
        name="combine",
    )(dest, x1, w0, w1, ys)


def _lower_tri(n):
    return jnp.where(jnp.arange(n)[:, None] >= jnp.arange(n)[None, :], 1.0, 0.0).astype(BF16)


def _upper_tri(n):
    return jnp.where(jnp.arange(n)[:, None] <= jnp.arange(n)[None, :], 1.0, 0.0).astype(BF16)


def _split_bf16(w):
    hi = w.astype(BF16)
    lo = (w - hi.astype(F32)).astype(BF16)
    return hi, lo


def _layer(x, norm_mix_g, w_in, b_forget, q_norm_g, k_norm_g, ssm_A_re, ssm_A_im, ssm_log_dt, ssm_B_re, ssm_B_im,
           ssm_C_re, ssm_C_im, ssm_D, w_glu, w_proj_attn, w_proj_ssm, w_out, norm_ffn_g, w_router_group,
           b_router_group, w_router_expert, b_router_expert, w_expert_gate, w_expert_up, w_expert_down):
    b, s, d = x.shape
    t = b * s
    da = N_HEADS * HEAD_DIM
    dssm = ssm_D.shape[0]
    nk = s // SUPER
    assert s % SUPER == 0 and nk % SUBLANES == 0 and nk & (nk - 1) == 0 and s % TQ_ATTN == 0 and t % TM_PROJ == 0
    assert TM_PROJ == TS_ATTN
    x2 = x.reshape(t, d)

    o_f = 3 * da
    o_u = o_f + N_HEADS
    o_g = o_u + dssm
    w_main = jnp.concatenate([w_in[:, :o_f], w_in[:, o_u:o_g]], axis=1).astype(BF16)
    wf = jnp.repeat(w_in[:, o_f:o_u], FORGET_COPIES_STRIDE, axis=1).astype(BF16)
    bf = jnp.repeat(b_forget, FORGET_COPIES_STRIDE)[None, :]
    w_gates = w_in[:, o_g:].astype(BF16)
    qg = (jnp.tile(q_norm_g, N_HEADS) * (HEAD_DIM ** -0.5 * LOG2E))[None, :]
    kg = jnp.tile(k_norm_g, N_HEADS)[None, :]
    head_of = jnp.arange(da) // HEAD_DIM
    bd = jnp.where(head_of[:, None] == head_of[None, :], 1.0 / HEAD_DIM, 0.0).astype(BF16)
    logit_bound = 1.02 * HEAD_DIM ** 0.5 * jnp.max(jnp.abs(q_norm_g)) * jnp.max(jnp.abs(k_norm_g))
    plain_ok = LOG2E * logit_bound <= PLAIN_SOFTMAX_MAX_LOG2
    shift = jnp.full((1, LANES), logit_bound, F32)
    hw = N_HEADS * LANES
    lane_in_head = jnp.arange(hw) % LANES
    ones = jnp.stack([
        (lane_in_head >= BIAS_K_LANE) & (lane_in_head < BIAS_K_LANE + 3),
        (lane_in_head >= BIAS_Q_LANE) & (lane_in_head < BIAS_Q_LANE + 3),
        lane_in_head == HEAD_DIM]
        + [jnp.zeros((hw,), bool)] * (SUBLANES - 3)).astype(F32)
    prow = jnp.arange(LANES)
    head, copy = prow // FORGET_COPIES_STRIDE, prow % FORGET_COPIES_STRIDE
    target = jnp.where(copy < 3, head * LANES + BIAS_Q_LANE + copy, hw + head * LANES + BIAS_K_LANE + copy - 3)
    place = ((jnp.arange(2 * hw)[None, :] == target[:, None]) & (copy < 6)[:, None]).astype(BF16)

    q, k, v, u, fedge = _proj(x2, norm_mix_g[None, :], w_main, wf, bf, qg, kg, bd, _lower_tri(TM_PROJ), place, ones,
                              shift, seq=s)
    y_attn = _attention(q.reshape(b, s, hw), k.reshape(b, s, hw), v.reshape(b, s, hw), fedge, plain_ok)

    y_ssm = _s5(u.reshape(b, s, dssm),
                *_s5_params(ssm_A_re, ssm_A_im, ssm_log_dt, ssm_B_re, ssm_B_im, ssm_C_re, ssm_C_im, s // SUPER))

    wr = jnp.concatenate([w_router_expert.T, w_router_group.T,
                          jnp.zeros((ROUTER_ROWS - N_EXPERTS - N_EXPERT_GROUPS, d), F32)], axis=0)
    br = jnp.concatenate([b_router_expert, b_router_group,
                          jnp.full((SUBLANES - N_EXPERT_GROUPS,), NEG_INF, F32),
                          jnp.zeros((ROUTER_ROWS - N_EXPERTS - SUBLANES,), F32)])[:, None]
    n_tiles = (2 * t) // TM_EXPERT + N_EXPERTS
    x1, xp, logits, xs0 = _mix(x2, y_attn.reshape(t, da), y_ssm.reshape(t, dssm), u, norm_mix_g[None, :], w_gates,
                               ssm_D[None, :], w_glu.astype(BF16), w_proj_attn.astype(BF16),
                               w_proj_ssm.astype(BF16), w_out.astype(BF16), norm_ffn_g[None, :], *_split_bf16(wr),
                               br, n_slots=n_tiles * TM_EXPERT)
    ids, wts = _route_call(logits)

    dest, tile_expert, tile_valid = _meta(ids[:2], _upper_tri(SORT_CHUNK), n_tiles)
    xs = _dispatch(dest, xp, xs0)
    ys = _experts(tile_expert, tile_valid, xs, w_expert_gate, w_expert_up, w_expert_down)
    w0 = jnp.broadcast_to(wts[0][:, None], (t, LANES))
    w1 = jnp.broadcast_to(wts[1][:, None], (t, LANES))
    out = _combine(dest, x1, w0, w1, ys)
    return out.reshape(b, s, d)


def kernel(x, norm_mix_g, w_in, b_forget, q_norm_g, k_norm_g, ssm_A_re, ssm_A_im, ssm_log_dt, ssm_B_re, ssm_B_im,
           ssm_C_re, ssm_C_im, ssm_D, w_glu, w_proj_attn, w_proj_ssm, w_out, norm_ffn_g, w_router_group,
           b_router_group, w_router_expert, b_router_expert, w_expert_gate, w_expert_up, w_expert_down):
    layer_params = (norm_mix_g, w_in, b_forget, q_norm_g, k_norm_g, ssm_A_re, ssm_A_im, ssm_log_dt, ssm_B_re,
                    ssm_B_im, ssm_C_re, ssm_C_im, ssm_D, w_glu, w_proj_attn, w_proj_ssm, w_out, norm_ffn_g,
                    w_router_group, b_router_group, w_router_expert, b_router_expert, w_expert_gate, w_expert_up,
                    w_expert_down)
    for layer in range(norm_mix_g.shape[0]):
        x = _layer(x, *[p[layer] for p in layer_params])
    return x
```

```python
import functools
import math

import jax
import jax.numpy as jnp
from jax import lax
from jax.experimental import pallas as pl
from jax.experimental.pallas import tpu as pltpu

F32 = jnp.float32
BF16 = jnp.bfloat16
I32 = jnp.int32

LANES = 128
SUBLANES = 8
MXU_DIM = 256

N_HEADS = 8
HEAD_DIM = 64
SSM_GROUP = 16
SSM_STATE = 64
N_EXPERT_GROUPS = 4
EXPERTS_PER_GROUP = 8
N_EXPERTS = N_EXPERT_GROUPS * EXPERTS_PER_GROUP
EPS = 1e-6
NEG_INF = -1e30
LOG2E = math.log2(math.e)
PLAIN_SOFTMAX_MAX_LOG2 = 60.0
UNDERFLOW_LOG2 = 160.0
BIAS_K_LANE = HEAD_DIM
BIAS_Q_LANE = HEAD_DIM + 3
FORGET_COPIES_STRIDE = LANES // N_HEADS

CHUNK = MXU_DIM // SSM_GROUP
CHUNKS_PER_SUPER = 8
SUPER = CHUNK * CHUNKS_PER_SUPER
GROUPS_PER_BLOCK = LANES // SSM_GROUP
RELAYOUT_UNROLL = 16
GROUP_UNROLL = 4
MIN_FACTORED_DECAY = -4.0

TM_PROJ = 512
TQ_ATTN = 2048
TS_ATTN = 512
HEADS_PER_STEP = 2
TM_MIX = 512
MIX_CHAINS = 1
TM_EXPERT = 512
TM_DISPATCH = 1024
TM_COMBINE = 1024
SORT_CHUNK = 1024
ROW_DMA_UNROLL = 16
ROUTE_LANES = 2048
ROUTER_ROWS = 48

VMEM_LIMIT = 48 * 1024 * 1024


def _nt_dot(a, b):
    return lax.dot_general(a, b, (((1,), (1,)), ((), ())), preferred_element_type=F32)


def _dot(a, b):
    return jnp.dot(a, b, preferred_element_type=F32)


def _lane_tile(x, n):
    return x if n == 1 else jnp.concatenate([x] * n, axis=1)


def _rmsnorm_rows(x, g):
    ms = jnp.mean(x * x, axis=-1, keepdims=True)
    return x * lax.rsqrt(ms + EPS) * g


def _sigmoid(x):
    return 1.0 / (1.0 + jnp.exp(-x))


def _split3(x):
    hi = x.astype(BF16)
    r1 = x - hi.astype(F32)
    mid = r1.astype(BF16)
    lo = (r1 - mid.astype(F32)).astype(BF16)
    return [hi, mid, lo]


def _expand_heads(z):
    lane = lax.broadcasted_iota(I32, (z.shape[0], LANES), 1)
    blocks = []
    for p in range(z.shape[1] // LANES):
        blk = z[:, p * LANES:(p + 1) * LANES]
        blocks.append(jnp.where(lane < HEAD_DIM, blk, 0.0))
        blocks.append(jnp.where(lane < HEAD_DIM, pltpu.roll(blk, HEAD_DIM, axis=1), 0.0))
    return jnp.concatenate(blocks, axis=1)


def _proj_kernel(x_ref, g_ref, w_ref, wf_ref, bf_ref, qg_ref, kg_ref, bd_ref, ltri_ref, place_ref,
                 ones_ref, shift_ref, q_ref, k_ref, v_ref, u_ref, fedge_ref, carry_ref, *, tiles_per_seq):
    i = pl.program_id(0)

    @pl.when(i % tiles_per_seq == 0)
    def _():
        carry_ref[...] = jnp.zeros_like(carry_ref)

    tm = x_ref.shape[0]
    da = N_HEADS * HEAD_DIM
    hw = N_HEADS * LANES
    xb = _rmsnorm_rows(x_ref[...], g_ref[...]).astype(BF16)
    z = _dot(xb, w_ref[...])
    zq, zk = z[:, :da], z[:, da:2 * da]
    bd = bd_ref[...]
    msq = _dot((zq * zq).astype(BF16), bd)
    msk = _dot((zk * zk).astype(BF16), bd)
    qn = zq * lax.rsqrt(msq + EPS) * qg_ref[...]
    kn = zk * lax.rsqrt(msk + EPS) * kg_ref[...]
    u_ref[...] = z[:, 3 * da:]

    a = _dot(xb, wf_ref[...]) + bf_ref[...]
    logf = jnp.minimum(a, 0.0) - jnp.log(1.0 + jnp.exp(-jnp.abs(a)))
    cs = _dot(ltri_ref[...], jnp.concatenate(_split3(logf), axis=1))
    cum = cs[:, :LANES] + cs[:, LANES:2 * LANES] + cs[:, 2 * LANES:] + carry_ref[0:1, :]
    carry_ref[...] = jnp.broadcast_to(cum[tm - 1:tm, :], carry_ref.shape)
    edge_row = lax.broadcasted_iota(I32, fedge_ref.shape, 0)
    fedge_ref[...] = jnp.where(edge_row == 0, cum[0:1, :], jnp.where(edge_row == 1, cum[tm - 1:tm, :], 0.0))
    copy = lax.broadcasted_iota(I32, cum.shape, 1) % FORGET_COPIES_STRIDE
    bias = jnp.where(copy < 3, (cum - shift_ref[...]) * LOG2E, cum * (-LOG2E))
    hi, mid, lo = [p.astype(F32) for p in _split3(bias)]
    piece = jnp.where(copy % 3 == 0, hi, jnp.where(copy % 3 == 1, mid, lo)).astype(BF16)
    extras = _dot(piece, place_ref[...])
    q_ref[...] = (_expand_heads(qn) + extras[:, :hw] + ones_ref[0:1, :]).astype(BF16)
    k_ref[...] = (_expand_heads(kn) + extras[:, hw:] + ones_ref[1:2, :]).astype(BF16)
    v_ref[...] = (_expand_heads(z[:, 2 * da:3 * da]) + ones_ref[2:3, :]).astype(BF16)


def _proj(x2, norm_g, w_main, wf, b_f, qg, kg, bd, ltri, place, ones, shift, *, seq):
    t, d = x2.shape
    tm = TM_PROJ
    hw = N_HEADS * LANES
    du = w_main.shape[1] - 3 * N_HEADS * HEAD_DIM
    tiles_per_seq = seq // tm
    full = lambda a: pl.BlockSpec(a.shape, lambda i: (0,) * a.ndim)
    row = lambda width: pl.BlockSpec((tm, width), lambda i: (i, 0))
    consts = (norm_g, w_main, wf, b_f, qg, kg, bd, ltri, place, ones, shift)
    return pl.pallas_call(
        functools.partial(_proj_kernel, tiles_per_seq=tiles_per_seq),
        grid=(t // tm,),
        in_specs=[row(d)] + [full(a) for a in consts],
        out_specs=[row(hw), row(hw), row(hw), row(du), pl.BlockSpec((None, SUBLANES, LANES), lambda i: (i, 0, 0))],
        out_shape=[jax.ShapeDtypeStruct((t, hw), BF16)] * 3
        + [jax.ShapeDtypeStruct((t, du), F32), jax.ShapeDtypeStruct((t // tm, SUBLANES, LANES), F32)],
        scratch_shapes=[pltpu.VMEM((SUBLANES, LANES), F32)],
        compiler_params=pltpu.CompilerParams(dimension_semantics=("arbitrary",), vmem_limit_bytes=VMEM_LIMIT),
        name="proj",
    )(x2, *consts)


def _attn_kernel(first_ref, q_ref, k_ref, v_ref, o_ref, acc_scr, m_scr, *, ts, safe):
    i = pl.program_id(2)
    nsub = q_ref.shape[0] // ts
    acc_scr[...] = jnp.zeros(acc_scr.shape, F32)
    if safe:
        m_scr[...] = jnp.full(m_scr.shape, NEG_INF, F32)

    def block(hd, sub, j, masked):
        off = pl.multiple_of(j * ts, ts)
        lanes = slice(hd * LANES, (hd + 1) * LANES)
        chain = hd * nsub + sub
        s = _nt_dot(q_ref[sub * ts:(sub + 1) * ts, lanes], k_ref[pl.ds(off, ts), lanes])
        if masked:
            rows = lax.broadcasted_iota(I32, (ts, ts), 0)
            cols = lax.broadcasted_iota(I32, (ts, ts), 1)
            s = jnp.where(cols <= rows, s, NEG_INF)
        vblk = v_ref[pl.ds(off, ts), lanes]
        if safe:
            m_prev = m_scr[chain]
            m_new = jnp.maximum(m_prev, jnp.max(s, axis=1, keepdims=True))
            p = jnp.exp2(s - _lane_tile(m_new, ts // LANES)).astype(BF16)
            acc_scr[chain] = jnp.exp2(m_prev - m_new) * acc_scr[chain] + _dot(p, vblk)
            m_scr[chain] = m_new
        else:
            acc_scr[chain] += _dot(jnp.exp2(s).astype(BF16), vblk)

    def body(heads, j, carry):
        for hd in heads:
            for sub in range(nsub):
                block(hd, sub, j, False)
        return carry

    step = (pl.program_id(0) * pl.num_programs(1) + pl.program_id(1)) * pl.num_programs(2) + i
    starts = [first_ref[step * HEADS_PER_STEP + hd] for hd in range(HEADS_PER_STEP)]
    shared = functools.reduce(jnp.maximum, starts)
    for hd in range(HEADS_PER_STEP):
        lax.fori_loop(starts[hd], shared, functools.partial(body, (hd,)), 0)
    lax.fori_loop(shared, nsub * i, functools.partial(body, tuple(range(HEADS_PER_STEP))), 0)
    for jj in range(nsub):
        for hd in range(HEADS_PER_STEP):
            for sub in range(jj, nsub):
                block(hd, sub, nsub * i + jj, sub == jj)
    lane = lax.broadcasted_iota(I32, (ts, LANES), 1)
    for sub in range(nsub):
        outs = []
        for hd in range(HEADS_PER_STEP):
            acc = acc_scr[hd * nsub + sub]
            outs.append(acc / acc[:, HEAD_DIM:HEAD_DIM + 1])
        both = jnp.where(lane < HEAD_DIM, outs[0], pltpu.roll(outs[1], HEAD_DIM, axis=1))
        o_ref[sub * ts:(sub + 1) * ts, :] = both.astype(o_ref.dtype)


def _first_kv_block(fedge, b, s, tq, ts):
    nblk = s // ts
    f_first = fedge[:, 0, ::FORGET_COPIES_STRIDE].reshape(b, nblk, N_HEADS).transpose(0, 2, 1)
    f_last = fedge[:, 1, ::FORGET_COPIES_STRIDE].reshape(b, nblk, N_HEADS).transpose(0, 2, 1)
    tile_start = f_first[:, :, ::tq // ts]
    bound = LOG2E * (tile_start[:, :, :, None] - f_last[:, :, None, :])
    first = jnp.sum(bound < -UNDERFLOW_LOG2, axis=-1).astype(I32)
    first = first.reshape(b, N_HEADS // HEADS_PER_STEP, HEADS_PER_STEP, -1).transpose(0, 1, 3, 2)
    return first.reshape(-1)


def _attention(q, k, v, fedge, plain_ok):
    b, s, hw = q.shape
    tq, ts = TQ_ATTN, TS_ATTN
    nq = s // tq
    steps = N_HEADS // HEADS_PER_STEP
    width = HEADS_PER_STEP * LANES
    chains = HEADS_PER_STEP * tq // ts

    def call(safe):
        first = jnp.zeros((b * N_HEADS * nq,), I32) if safe else _first_kv_block(fedge, b, s, tq, ts)
        return pl.pallas_call(
            functools.partial(_attn_kernel, ts=ts, safe=safe),
            grid_spec=pltpu.PrefetchScalarGridSpec(
                num_scalar_prefetch=1,
                grid=(b, steps, nq),
                in_specs=[pl.BlockSpec((None, tq, width), lambda bi, h, i, f: (bi, i, h)),
                          pl.BlockSpec((None, s, width), lambda bi, h, i, f: (bi, 0, h)),
                          pl.BlockSpec((None, s, width), lambda bi, h, i, f: (bi, 0, h))],
                out_specs=pl.BlockSpec((None, tq, LANES), lambda bi, h, i, f: (bi, i, h)),
                scratch_shapes=[pltpu.VMEM((chains, ts, LANES), F32)] * 2),
            out_shape=jax.ShapeDtypeStruct((b, s, steps * LANES), BF16),
            compiler_params=pltpu.CompilerParams(
                dimension_semantics=("arbitrary", "arbitrary", "arbitrary"), vmem_limit_bytes=VMEM_LIMIT),
            name="attn_safe" if safe else "attn",
        )(first, q, k, v)

    return lax.cond(plain_ok, lambda: call(False), lambda: call(True))


def _window_select(pieces, first_group):
    window = lax.broadcasted_iota(I32, pieces[0].shape, 1) // SSM_GROUP
    out = pieces[0]
    for m in range(1, GROUPS_PER_BLOCK):
        out = jnp.where(window == (first_group + m) % GROUPS_PER_BLOCK, pieces[m], out)
    return out


def _s5_kernel(u_ref, win_ref, toep_ref, wout_ref, coef_ref, y_ref, ub_scr, yb_scr, *, nk):
    nj = CHUNKS_PER_SUPER
    gpb = GROUPS_PER_BLOCK
    w = CHUNK * SSM_GROUP
    half = w // 2
    nkk = nk // SUBLANES
    row_stride = SUPER

    def gather_body(it, carry):
        j, kk = it // nkk, it % nkk
        for sp in range(CHUNK // gpb):
            rot = []
            for m in range(gpb):
                row0 = j * CHUNK + sp * gpb + m + kk * (SUBLANES * row_stride)
                src = u_ref[pl.ds(row0, SUBLANES, stride=row_stride), :]
                rot.append(src if m == 0 else pltpu.roll(src, m * SSM_GROUP, axis=1))
            for gl in range(gpb):
                ub_scr[gl, j, pl.ds(kk * SUBLANES, SUBLANES), sp * LANES:(sp + 1) * LANES] = _window_select(rot, gl)
        return carry

    lax.fori_loop(0, nj * nkk, gather_body, 0, unroll=RELAYOUT_UNROLL)

    def swap(val):
        return jnp.concatenate([val[:, half:], val[:, :half]], axis=1)

    def group_body(gl, carry):
        def cmul(val, idx):
            return coef_ref[gl, idx:idx + 1, :] * val + coef_ref[gl, idx + 1:idx + 2, :] * swap(val)

        u = ub_scr[gl].reshape(nj * nk, w).astype(BF16)
        s1 = _dot(u, win_ref[gl]).reshape(nj, nk, w)
        e = jnp.zeros((nk, w), F32)
        local = []
        for j in range(nj):
            local.append(e)
            e = cmul(e, 0) + s1[j]
        kidx = lax.broadcasted_iota(I32, (nk, w), 0)
        x = e
        step, d = 0, 1
        while d < nk:
            shifted = jnp.where(kidx >= d, pltpu.roll(x, d, axis=0), 0.0)
            x = x + cmul(shifted, 2 + 2 * nj + 2 * step)
            step, d = step + 1, d * 2
        x_start = jnp.where(kidx >= 1, pltpu.roll(x, 1, axis=0), 0.0)
        starts = [local[j] + cmul(x_start, 2 + 2 * j) for j in range(nj)]
        p = jnp.concatenate([st[:, :half] for st in starts], axis=0)
        hi = p.astype(BF16)
        lo = (p - hi.astype(F32)).astype(BF16)
        wout_t = wout_ref[gl]
        y = _dot(u, toep_ref[gl]) + _nt_dot(hi, wout_t) + _nt_dot(lo, wout_t)
        yb_scr[gl] = y.reshape(nj, nk, w)
        return carry

    lax.fori_loop(0, gpb, group_body, 0, unroll=GROUP_UNROLL)

    def scatter_body(it, carry):
        j, kk = it // nkk, it % nkk
        for tp in range(CHUNK // gpb):
            src = [yb_scr[gl, j, pl.ds(kk * SUBLANES, SUBLANES), tp * LANES:(tp + 1) * LANES] for gl in range(gpb)]
            for m in range(gpb):
                by_window = [src[(wi - m) % gpb] for wi in range(gpb)]
                window = lax.broadcasted_iota(I32, by_window[0].shape, 1) // SSM_GROUP
                merged = by_window[0]
                for wi in range(1, gpb):
                    merged = jnp.where(window == wi, by_window[wi], merged)
                nat = merged if m == 0 else pltpu.roll(merged, LANES - m * SSM_GROUP, axis=1)
                row0 = j * CHUNK + tp * gpb + m + kk * (SUBLANES * row_stride)
                y_ref[pl.ds(row0, SUBLANES, stride=row_stride), :] = nat
        return carry

    lax.fori_loop(0, nj * nkk, scatter_body, 0, unroll=RELAYOUT_UNROLL)


def _s5_params(a_re, a_im, log_dt, b_re, b_im, c_re, c_im, nk):
    args = (a_re, a_im, log_dt, b_re, b_im, c_re, c_im)
    slowest = jnp.min(a_re * jnp.exp(log_dt)[:, None])
    return lax.cond(slowest > MIN_FACTORED_DECAY, functools.partial(_s5_params_factored, nk=nk),
                    functools.partial(_s5_params_direct, nk=nk), *args)


def _s5_params_factored(a_re, a_im, log_dt, b_re, b_im, c_re, c_im, *, nk):
    g, p = a_re.shape
    c = SSM_GROUP
    gpb = GROUPS_PER_BLOCK
    dt = jnp.exp(log_dt)[:, None]
    adt_r, adt_i = a_re * dt, a_im * dt

    def lam_pow(n):
        nf = jnp.asarray(n, F32)
        nf = (nf[None] if nf.ndim == 1 else nf)[:, :, None]
        mag = jnp.exp(adt_r[:, None, :] * nf)
        ang = adt_i[:, None, :] * nf
        return mag * jnp.cos(ang), mag * jnp.sin(ang)

    gi = jnp.arange(g)[:, None, None] % gpb
    step = ((jnp.arange(CHUNK // gpb)[None, :, None]) * gpb
            + (jnp.arange(gpb)[None, None, :] - gi) % gpb).reshape(g, CHUNK)

    l1r, l1i = lam_pow(jnp.ones((1,)))
    den = a_re * a_re + a_im * a_im
    nr, ni = l1r[:, 0] - 1.0, l1i[:, 0]
    qr = (nr * a_re + ni * a_im) / den
    qi = (ni * a_re - nr * a_im) / den
    bcr = (qr[..., None] * b_re - qi[..., None] * b_im).transpose(0, 2, 1)
    bci = (qr[..., None] * b_im + qi[..., None] * b_re).transpose(0, 2, 1)

    def times_b(pr, pi):
        pr, pi = pr[:, :, None, :], pi[:, :, None, :]
        return pr * bcr[:, None] - pi * bci[:, None], pr * bci[:, None] + pi * bcr[:, None]

    def times_c(pr, pi):
        pr, pi = pr[:, :, None, :], pi[:, :, None, :]
        return c_re[:, None] * pr - c_im[:, None] * pi, c_re[:, None] * pi + c_im[:, None] * pr

    rows = lambda parts: jnp.concatenate(parts, axis=-1).reshape(g, CHUNK * c, -1)
    ir, ii = times_b(*lam_pow(CHUNK - 1 - step))
    win = rows([ir, ii, ii, ir])
    clr, cli = times_c(*lam_pow(step + 1))
    wout_t = rows([clr, -cli])
    ar, ai = times_b(*lam_pow(-step))
    br, bi = times_c(*lam_pow(step))
    toep = jnp.einsum('gnp,gmp->gnm', rows([ar, ai]), rows([br, -bi]), precision=lax.Precision.HIGHEST)
    step_of_lane = jnp.repeat(step, c, axis=1)
    toep = jnp.where(step_of_lane[:, None, :] >= step_of_lane[:, :, None], toep, 0.0)
    return win.astype(BF16), toep.astype(BF16), wout_t.astype(BF16), _s5_scan_coefficients(lam_pow, g, p, nk)


def _s5_scan_coefficients(lam_pow, g, p, nk):
    n_steps = max(nk.bit_length() - 1, 0)
    powers = [CHUNK] + [CHUNK * j for j in range(CHUNKS_PER_SUPER)] + [SUPER * (1 << i) for i in range(n_steps)]
    ar, ai = lam_pow(jnp.array(powers, dtype=jnp.int32))
    c1 = jnp.concatenate([ar, ar, ar, ar], axis=2)
    c2 = jnp.concatenate([-ai, ai, ai, -ai], axis=2)
    coef = jnp.stack([c1, c2], axis=2).reshape(g, 2 * len(powers), 4 * p)
    return jnp.pad(coef, ((0, 0), (0, (-coef.shape[1]) % SUBLANES), (0, 0)))


def _s5_params_direct(a_re, a_im, log_dt, b_re, b_im, c_re, c_im, *, nk):
    g, p = a_re.shape
    c = SSM_GROUP
    hp = lax.Precision.HIGHEST
    dt = jnp.exp(log_dt)[:, None]
    adt_r, adt_i = a_re * dt, a_im * dt

    def lam_pow(n):
        nf = jnp.asarray(n, F32)[None, :, None]
        mag = jnp.exp(adt_r[:, None, :] * nf)
        ang = adt_i[:, None, :] * nf
        return mag * jnp.cos(ang), mag * jnp.sin(ang)

    lr, li = lam_pow(jnp.arange(CHUNK + 1))
    den = a_re * a_re + a_im * a_im
    nr, ni = lr[:, 1] - 1.0, li[:, 1]
    qr = (nr * a_re + ni * a_im) / den
    qi = (ni * a_re - nr * a_im) / den
    bbr = qr[..., None] * b_re - qi[..., None] * b_im
    bbi = qr[..., None] * b_im + qi[..., None] * b_re

    mr = lr[:, :CHUNK, :, None] * bbr[:, None] - li[:, :CHUNK, :, None] * bbi[:, None]
    mi = lr[:, :CHUNK, :, None] * bbi[:, None] + li[:, :CHUNK, :, None] * bbr[:, None]
    kern = (jnp.einsum('gcp,gtpd->gtcd', c_re, mr, precision=hp)
            - jnp.einsum('gcp,gtpd->gtcd', c_im, mi, precision=hp))
    toep = jnp.stack([jnp.pad(kern[:, :CHUNK - s], ((0, 0), (s, 0), (0, 0), (0, 0))) for s in range(CHUNK)],
                     axis=1)
    toep = toep.transpose(0, 1, 4, 2, 3).reshape(g, CHUNK * c, CHUNK * c)
    clr = c_re[:, None] * lr[:, 1:, None, :] - c_im[:, None] * li[:, 1:, None, :]
    cli = c_re[:, None] * li[:, 1:, None, :] + c_im[:, None] * lr[:, 1:, None, :]
    w_re = clr.transpose(0, 3, 1, 2).reshape(g, p, CHUNK * c)
    w_im = (-cli).transpose(0, 3, 1, 2).reshape(g, p, CHUNK * c)
    wout = jnp.concatenate([w_re, w_im], axis=1)
    pr, pi = lr[:, :CHUNK][:, ::-1], li[:, :CHUNK][:, ::-1]
    ir = pr[..., None] * bbr[:, None] - pi[..., None] * bbi[:, None]
    ii = pr[..., None] * bbi[:, None] + pi[..., None] * bbr[:, None]
    ir = ir.transpose(0, 1, 3, 2).reshape(g, CHUNK * c, p)
    ii = ii.transpose(0, 1, 3, 2).reshape(g, CHUNK * c, p)
    win = jnp.concatenate([ir, ii, ii, ir], axis=2)

    gi = jnp.arange(g)[:, None, None] % GROUPS_PER_BLOCK
    si = jnp.arange(CHUNK)[None, :, None]
    ci = jnp.arange(c)[None, None, :]
    lane_of = ((si // GROUPS_PER_BLOCK) * LANES + ((gi + si) % GROUPS_PER_BLOCK) * c + ci).reshape(g, CHUNK * c)
    perm = (lane_of[:, None, :] == jnp.arange(CHUNK * c)[None, :, None]).astype(BF16)
    win = jnp.einsum('gln,gnk->glk', perm, win.astype(BF16))
    toep = jnp.einsum('gln,gnk->glk', perm, toep.astype(BF16))
    toep = jnp.einsum('grn,gln->grl', toep, perm)
    wout_t = jnp.einsum('gln,gpn->glp', perm, wout.astype(BF16))
    return win.astype(BF16), toep.astype(BF16), wout_t.astype(BF16), _s5_scan_coefficients(lam_pow, g, p, nk)


def _s5(u, win, toep, wout_t, coef):
    b, s, dssm = u.shape
    nk = s // SUPER
    nj = CHUNKS_PER_SUPER
    w = CHUNK * SSM_GROUP
    gpb = GROUPS_PER_BLOCK
    nblk = dssm // LANES
    wspec = lambda a: pl.BlockSpec((gpb,) + a.shape[1:], lambda bi, li: (li, 0, 0))
    return pl.pallas_call(
        functools.partial(_s5_kernel, nk=nk),
        grid=(b, nblk),
        in_specs=[pl.BlockSpec((None, s, LANES), lambda bi, li: (bi, 0, li)), wspec(win), wspec(toep),
                  wspec(wout_t), wspec(coef)],
        out_specs=pl.BlockSpec((None, s, LANES), lambda bi, li: (bi, 0, li)),
        out_shape=jax.ShapeDtypeStruct((b, s, dssm), F32),
        scratch_shapes=[pltpu.VMEM((gpb, nj, nk, w), F32)] * 2,
        compiler_params=pltpu.CompilerParams(dimension_semantics=("arbitrary", "arbitrary"),
                                             vmem_limit_bytes=VMEM_LIMIT),
        name="s5",
    )(u, win, toep, wout_t, coef)


def _route(logits):
    e_all = logits[0:N_EXPERTS]
    gl = logits[N_EXPERTS:N_EXPERTS + SUBLANES]
    tm = logits.shape[1]
    ridx = lax.broadcasted_iota(I32, (SUBLANES, tm), 0)
    ge = jnp.exp(gl - jnp.max(gl, axis=0, keepdims=True))
    gp = ge / jnp.sum(ge, axis=0, keepdims=True)
    g_top = jnp.max(gp, axis=0, keepdims=True)
    g_sel = jnp.min(jnp.where(gp == g_top, ridx, SUBLANES), axis=0, keepdims=True)
    e_in = e_all[(N_EXPERT_GROUPS - 1) * SUBLANES:]
    for gi in range(N_EXPERT_GROUPS - 2, -1, -1):
        e_in = jnp.where(g_sel == gi, e_all[gi * SUBLANES:(gi + 1) * SUBLANES], e_in)
    ee = jnp.exp(e_in - jnp.max(e_in, axis=0, keepdims=True))
    ep = ee / jnp.sum(ee, axis=0, keepdims=True)
    v1 = jnp.max(ep, axis=0, keepdims=True)
    i1 = jnp.min(jnp.where(ep == v1, ridx, SUBLANES), axis=0, keepdims=True)
    ep2 = jnp.where(ridx == i1, -1.0, ep)
    v2 = jnp.max(ep2, axis=0, keepdims=True)
    i2 = jnp.min(jnp.where(ep2 == v2, ridx, SUBLANES), axis=0, keepdims=True)
    den = v1 + v2
    w1 = g_top * v1 / den
    w2 = g_top * v2 / den
    e1 = g_sel * EXPERTS_PER_GROUP + i1
    e2 = g_sel * EXPERTS_PER_GROUP + i2
    ids = jnp.where(ridx == 0, e1, jnp.where(ridx == 1, e2, 0))
    wts = jnp.where(ridx == 0, w1, jnp.where(ridx == 1, w2, 0.0))
    return ids, wts


def _mix_kernel(x_ref, ya_ref, ys_ref, u_ref, g1_ref, wgate_ref, dsk_ref, wglu_ref, wpa_ref, wps_ref,
                wout_ref, g2_ref, wrh_ref, wrl_ref, br_ref, x1_ref, xp_ref, logits_ref, slots_ref, zero_scr, zsem):
    d = x_ref.shape[1]
    rows = x_ref.shape[0] // MIX_CHAINS
    i = pl.program_id(0)

    @pl.when(i == 0)
    def _():
        zero_scr[...] = jnp.zeros(zero_scr.shape, zero_scr.dtype)

    zrows = zero_scr.shape[0]
    zero_copy = pltpu.make_async_copy(zero_scr, slots_ref.at[pl.ds(pl.multiple_of(i * zrows, SUBLANES), zrows)], zsem)
    zero_copy.start()
    for chain in range(MIX_CHAINS):
        r = slice(chain * rows, (chain + 1) * rows)
        x = x_ref[r, :]
        xb = _rmsnorm_rows(x, g1_ref[...]).astype(BF16)
        gates = _dot(xb, wgate_ref[...])
        y = ys_ref[r, :].astype(F32) + dsk_ref[...] * u_ref[r, :].astype(F32)
        y = y * (0.5 * (1.0 + jnp.tanh(math.sqrt(2.0 / math.pi) * (y + 0.044715 * (y * y * y)))))
        y = y * _sigmoid(_dot(y.astype(BF16), wglu_ref[...]))
        mixed = (_sigmoid(gates[:, :d]) * _dot(ya_ref[r, :], wpa_ref[...])
                 + _sigmoid(gates[:, d:]) * _dot(y.astype(BF16), wps_ref[...]))
        x1 = x + _dot(mixed.astype(BF16), wout_ref[...])
        x1_ref[r, :] = x1
        xn = _rmsnorm_rows(x1, g2_ref[...])
        xp_ref[r, :] = pltpu.pack_elementwise([xn[:, :d // 2], xn[:, d // 2:]], packed_dtype=BF16)
        xh = xn.astype(BF16)
        xl = (xn - xh.astype(F32)).astype(BF16)
        wrh = wrh_ref[...]
        logits_ref[:, r] = _nt_dot(wrh, xh) + _nt_dot(wrl_ref[...], xh) + _nt_dot(wrh, xl) + br_ref[...]
    zero_copy.wait()


def _mix(x2, ya, ys, u, g1, wgate, dsk, wglu, wpa, wps, wout, g2, wrh, wrl, br, *, n_slots):
    t, d = x2.shape
    tm = TM_MIX
    steps = t // tm
    assert n_slots % steps == 0 and (n_slots // steps) % SUBLANES == 0
    full = lambda a: pl.BlockSpec(a.shape, lambda i: (0,) * a.ndim)
    row = lambda width: pl.BlockSpec((tm, width), lambda i: (i, 0))
    return pl.pallas_call(
        _mix_kernel,
        grid=(steps,),
        in_specs=[row(d), row(ya.shape[1]), row(ys.shape[1]), row(u.shape[1]), full(g1), full(wgate), full(dsk),
                  full(wglu), full(wpa),
                  full(wps), full(wout), full(g2), full(wrh), full(wrl), full(br)],
        out_specs=[row(d), row(d // 2), pl.BlockSpec((ROUTER_ROWS, tm), lambda i: (0, i)),
                   pl.BlockSpec(memory_space=pl.ANY)],
        out_shape=[jax.ShapeDtypeStruct((t, d), F32), jax.ShapeDtypeStruct((t, d // 2), jnp.uint32),
                   jax.ShapeDtypeStruct((ROUTER_ROWS, t), F32), jax.ShapeDtypeStruct((n_slots, d // 2), jnp.uint32)],
        scratch_shapes=[pltpu.VMEM((n_slots // steps, d // 2), jnp.uint32), pltpu.SemaphoreType.DMA(())],
        compiler_params=pltpu.CompilerParams(dimension_semantics=("arbitrary",), vmem_limit_bytes=VMEM_LIMIT),
        name="mix",
    )(x2, ya, ys, u, g1, wgate, dsk, wglu, wpa, wps, wout, g2, wrh, wrl, br)


def _route_kernel(logits_ref, ids_ref, wts_ref):
    ids, wts = _route(logits_ref[...])
    ids_ref[...] = ids
    wts_ref[...] = wts


def _route_call(logits):
    t = logits.shape[1]
    tl = ROUTE_LANES
    col = pl.BlockSpec((SUBLANES, tl), lambda i: (0, i))
    return pl.pallas_call(
        _route_kernel,
        grid=(t // tl,),
        in_specs=[pl.BlockSpec((ROUTER_ROWS, tl), lambda i: (0, i))],
        out_specs=[col, col],
        out_shape=[jax.ShapeDtypeStruct((SUBLANES, t), I32), jax.ShapeDtypeStruct((SUBLANES, t), F32)],
        compiler_params=pltpu.CompilerParams(dimension_semantics=("arbitrary",)),
        name="route",
    )(logits)


def _meta_kernel(ids_ref, tri_ref, dest_ref, tile_ref, *, tile_rows):
    nk, nc, c = ids_ref.shape
    ne = N_EXPERTS
    erow = lax.broadcasted_iota(I32, (ne, c), 0)
    ones = jnp.ones((c, LANES), BF16)

    def onehot(k, ci):
        mask = erow == ids_ref[k, pl.ds(ci, 1), :]
        return mask, jnp.where(mask, 1.0, 0.0).astype(BF16)

    def count_body(n, acc):
        return acc + _dot(onehot(n // nc, n % nc)[1], ones)

    cnt = lax.fori_loop(0, nk * nc, count_body, jnp.zeros((ne, LANES), F32))
    ntiles = jnp.floor((cnt + (tile_rows - 1)) * (1.0 / tile_rows))
    lower = jnp.where(lax.broadcasted_iota(I32, (ne, ne), 1) < lax.broadcasted_iota(I32, (ne, ne), 0), 1.0, 0.0)
    start_tiles = _dot(lower.astype(BF16), ntiles.astype(BF16))
    base = start_tiles * tile_rows

    tri = tri_ref[...]

    def dest_body(n, carry):
        k, ci = n // nc, n % nc
        mask, oh = onehot(k, ci)
        prefix = _dot(oh, tri)
        slot = _lane_tile(base + carry, c // LANES) + prefix - 1.0
        dest = jnp.sum(jnp.where(mask, slot, 0.0), axis=0, keepdims=True)
        dest_ref[k, pl.ds(ci, 1), :] = dest.astype(I32)
        return carry + _dot(oh, ones)

    lax.fori_loop(0, nk * nc, dest_body, jnp.zeros((ne, LANES), F32))

    nt_lanes = tile_ref.shape[1]
    end_tiles = _lane_tile(start_tiles + ntiles, nt_lanes // LANES)
    tidx = lax.broadcasted_iota(I32, (ne, nt_lanes), 1).astype(F32)
    texp = jnp.sum(jnp.where(tidx >= end_tiles, 1.0, 0.0), axis=0, keepdims=True)
    valid = jnp.where(texp < ne, 1, 0)
    texp = jnp.minimum(texp, ne - 1.0).astype(I32)
    ridx = lax.broadcasted_iota(I32, tile_ref.shape, 0)
    tile_ref[...] = jnp.where(ridx == 0, texp, jnp.where(ridx == 1, valid, 0))


def _meta(ids2, tri, n_tiles):
    nk, t = ids2.shape
    c = SORT_CHUNK
    nt_lanes = pl.cdiv(n_tiles, LANES) * LANES
    ids3 = ids2.reshape(nk, t // c, c)
    dest, tile = pl.pallas_call(
        functools.partial(_meta_kernel, tile_rows=TM_EXPERT),
        out_shape=[jax.ShapeDtypeStruct(ids3.shape, I32), jax.ShapeDtypeStruct((SUBLANES, nt_lanes), I32)],
        compiler_params=pltpu.CompilerParams(vmem_limit_bytes=VMEM_LIMIT),
        name="meta",
    )(ids3, tri)
    return dest.reshape(nk * t), tile[0, :n_tiles], tile[1, :n_tiles]


def _row_copy(src_ref, src_row, dst_ref, dst_row, sem):
    return pltpu.make_async_copy(src_ref.at[pl.ds(src_row, 1)], dst_ref.at[pl.ds(dst_row, 1)], sem)


def _dispatch_kernel(dest_ref, xp_ref, xs_in_ref, xs_ref, sem, *, n_tokens):
    del xs_in_ref
    tm = xp_ref.shape[0]
    base = pl.program_id(0) * tm

    def body(r, carry):
        for k in range(2):
            _row_copy(xp_ref, r, xs_ref, dest_ref[k * n_tokens + base + r], sem).start(priority=k)
        return carry

    lax.fori_loop(0, tm, body, 0, unroll=ROW_DMA_UNROLL)
    for k in range(2):
        pltpu.make_async_copy(xp_ref, xs_ref.at[pl.ds(0, tm)], sem).wait()


def _dispatch(dest, xp, xs0):
    t, w = xp.shape
    tm = TM_DISPATCH
    return pl.pallas_call(
        functools.partial(_dispatch_kernel, n_tokens=t),
        grid_spec=pltpu.PrefetchScalarGridSpec(
            num_scalar_prefetch=1,
            grid=(t // tm,),
            in_specs=[pl.BlockSpec((tm, w), lambda i, d: (i, 0)), pl.BlockSpec(memory_space=pl.ANY)],
            out_specs=pl.BlockSpec(memory_space=pl.ANY),
            scratch_shapes=[pltpu.SemaphoreType.DMA(())]),
        out_shape=jax.ShapeDtypeStruct(xs0.shape, xp.dtype),
        input_output_aliases={2: 0},
        compiler_params=pltpu.CompilerParams(dimension_semantics=("arbitrary",), has_side_effects=True),
        name="dispatch",
    )(dest, xp, xs0)


def _unpack_rows(packed):
    lo = pltpu.unpack_elementwise(packed, index=0, packed_dtype=BF16, unpacked_dtype=F32)
    hi = pltpu.unpack_elementwise(packed, index=1, packed_dtype=BF16, unpacked_dtype=F32)
    return jnp.concatenate([lo, hi], axis=1)


def _expert_kernel(te_ref, tv_ref, nv_ref, nxt_ref, xs_ref, wg_hbm, wu_hbm, wd_hbm, ys_ref,
                   wg_bf, wu_bf, wd_bf, wg_f32, wu_f32, wd_f32, slot_ref, sem):
    del nv_ref
    i = pl.program_id(0)
    half = xs_ref.shape[1]

    def weight_copies(expert, slot):
        return [pltpu.make_async_copy(hbm.at[expert], buf.at[slot], sem.at[slot])
                for hbm, buf in ((wg_hbm, wg_f32), (wu_hbm, wu_f32), (wd_hbm, wd_f32))]

    @pl.when(i == 0)
    def _():
        slot_ref[0] = 0
        for copy in weight_copies(te_ref[0], 0):
            copy.start()

    @pl.when((i == 0) | (te_ref[i] != te_ref[jnp.maximum(i - 1, 0)]))
    def _():
        slot = slot_ref[0]
        for copy in weight_copies(te_ref[i], slot):
            copy.wait()
        wg_bf[...] = wg_f32[slot].astype(BF16)
        wu_bf[...] = wu_f32[slot].astype(BF16)
        wd_bf[...] = wd_f32[slot].astype(BF16)
        slot_ref[0] = 1 - slot

        @pl.when(nxt_ref[i] >= 0)
        def _():
            for copy in weight_copies(nxt_ref[i], 1 - slot):
                copy.start()

    @pl.when(tv_ref[i] > 0)
    def _():
        x = _unpack_rows(xs_ref[...]).astype(BF16)
        hg = _dot(x, wg_bf[...])
        hu = _dot(x, wu_bf[...])
        h = (hg * _sigmoid(hg) * hu).astype(BF16)
        y = _dot(h, wd_bf[...])
        ys_ref[...] = pltpu.pack_elementwise([y[:, :half], y[:, half:]], packed_dtype=BF16)

    @pl.when(tv_ref[i] == 0)
    def _():
        zero = jnp.zeros(ys_ref.shape, F32)
        ys_ref[...] = pltpu.pack_elementwise([zero, zero], packed_dtype=BF16)


def _experts(tile_expert, tile_valid, xs, wg, wu, wd):
    n_slots, w = xs.shape
    tm = TM_EXPERT
    n_tiles = n_slots // tm
    n_valid = jnp.sum(tile_valid).astype(I32).reshape(1)
    tile_expert = jnp.where(tile_valid > 0, tile_expert, tile_expert[n_valid[0] - 1])
    later = (tile_expert[None, :] > tile_expert[:, None]) & (tile_valid[None, :] > 0)
    next_expert = jnp.min(jnp.where(later, tile_expert[None, :], N_EXPERTS), axis=1)
    next_expert = jnp.where(next_expert < N_EXPERTS, next_expert, -1).astype(I32)
    rows_in = pl.BlockSpec((tm, w), lambda i, te, tv, nv, nx: (jnp.minimum(i, nv[0] - 1), 0))
    rows_out = pl.BlockSpec((tm, w), lambda i, te, tv, nv, nx: (i, 0))
    hbm = pl.BlockSpec(memory_space=pl.ANY)
    weights = (wg, wu, wd)
    return pl.pallas_call(
        _expert_kernel,
        grid_spec=pltpu.PrefetchScalarGridSpec(
            num_scalar_prefetch=4,
            grid=(n_tiles,),
            in_specs=[rows_in, hbm, hbm, hbm],
            out_specs=rows_out,
            scratch_shapes=[pltpu.VMEM(a.shape[1:], BF16) for a in weights]
            + [pltpu.VMEM((2,) + a.shape[1:], a.dtype) for a in weights]
            + [pltpu.SMEM((1,), I32), pltpu.SemaphoreType.DMA((2,))]),
        out_shape=jax.ShapeDtypeStruct(xs.shape, xs.dtype),
        compiler_params=pltpu.CompilerParams(dimension_semantics=("arbitrary",), vmem_limit_bytes=VMEM_LIMIT),
        name="experts",
    )(tile_expert, tile_valid, n_valid, next_expert, xs, wg, wu, wd)


def _combine_kernel(dest_ref, x1_ref, w0_ref, w1_ref, ys_ref, out_ref, buf, sem, *, n_tokens):
    tm, d = x1_ref.shape
    base = pl.program_id(0) * tm

    def body(r, carry):
        for k in range(2):
            _row_copy(ys_ref, dest_ref[k * n_tokens + base + r], buf.at[k], r, sem).start(priority=k)
        return carry

    lax.fori_loop(0, tm, body, 0, unroll=ROW_DMA_UNROLL)
    for k in range(2):
        pltpu.make_async_copy(ys_ref.at[pl.ds(0, tm)], buf.at[k], sem).wait()
    reps = d // LANES
    out_ref[...] = (x1_ref[...]
                    + _lane_tile(w0_ref[...], reps) * _unpack_rows(buf[0])
                    + _lane_tile(w1_ref[...], reps) * _unpack_rows(buf[1]))


def _combine(dest, x1, w0, w1, ys):
    t, d = x1.shape
    tm = TM_COMBINE
    row = lambda width: pl.BlockSpec((tm, width), lambda i, dref: (i, 0))
    return pl.pallas_call(
        functools.partial(_combine_kernel, n_tokens=t),
        grid_spec=pltpu.PrefetchScalarGridSpec(
            num_scalar_prefetch=1,
            grid=(t // tm,),
            in_specs=[row(d), row(LANES), row(LANES), pl.BlockSpec(memory_space=pl.ANY)],
            out_specs=row(d),
            scratch_shapes=[pltpu.VMEM((2, tm, ys.shape[-1]), ys.dtype), pltpu.SemaphoreType.DMA(())]),
        out_shape=jax.ShapeDtypeStruct((t, d), x1.dtype),
        compiler_params=pltpu.CompilerParams(dimension_semantics=("arbitrary",), vmem_limit_bytes=VMEM_LIMIT),
        name="combine",
    )(dest, x1, w0, w1, ys)


def _lower_tri(n):
    return jnp.where(jnp.arange(n)[:, None] >= jnp.arange(n)[None, :], 1.0, 0.0).astype(BF16)


def _upper_tri(n):
    return jnp.where(jnp.arange(n)[:, None] <= jnp.arange(n)[None, :], 1.0, 0.0).astype(BF16)


def _split_bf16(w):
    hi = w.astype(BF16)
    lo = (w - hi.astype(F32)).astype(BF16)
    return hi, lo


def _layer(x, norm_mix_g, w_in, b_forget, q_norm_g, k_norm_g, ssm_A_re, ssm_A_im, ssm_log_dt, ssm_B_re, ssm_B_im,
           ssm_C_re, ssm_C_im, ssm_D, w_glu, w_proj_attn, w_proj_ssm, w_out, norm_ffn_g, w_router_group,
           b_router_group, w_router_expert, b_router_expert, w_expert_gate, w_expert_up, w_expert_down):
    b, s, d = x.shape
    t = b * s
    da = N_HEADS * HEAD_DIM
    dssm = ssm_D.shape[0]
    nk = s // SUPER
    assert s % SUPER == 0 and nk % SUBLANES == 0 and nk & (nk - 1) == 0 and s % TQ_ATTN == 0 and t % TM_PROJ == 0
    assert TM_PROJ == TS_ATTN
    x2 = x.reshape(t, d)

    o_f = 3 * da
    o_u = o_f + N_HEADS
    o_g = o_u + dssm
    w_main = jnp.concatenate([w_in[:, :o_f], w_in[:, o_u:o_g]], axis=1).astype(BF16)
    wf = jnp.repeat(w_in[:, o_f:o_u], FORGET_COPIES_STRIDE, axis=1).astype(BF16)
    bf = jnp.repeat(b_forget, FORGET_COPIES_STRIDE)[None, :]
    w_gates = w_in[:, o_g:].astype(BF16)
    qg = (jnp.tile(q_norm_g, N_HEADS) * (HEAD_DIM ** -0.5 * LOG2E))[None, :]
    kg = jnp.tile(k_norm_g, N_HEADS)[None, :]
    head_of = jnp.arange(da) // HEAD_DIM
    bd = jnp.where(head_of[:, None] == head_of[None, :], 1.0 / HEAD_DIM, 0.0).astype(BF16)
    logit_bound = 1.02 * HEAD_DIM ** 0.5 * jnp.max(jnp.abs(q_norm_g)) * jnp.max(jnp.abs(k_norm_g))
    plain_ok = LOG2E * logit_bound <= PLAIN_SOFTMAX_MAX_LOG2
    shift = jnp.full((1, LANES), logit_bound, F32)
    hw = N_HEADS * LANES
    lane_in_head = jnp.arange(hw) % LANES
    ones = jnp.stack([
        (lane_in_head >= BIAS_K_LANE) & (lane_in_head < BIAS_K_LANE + 3),
        (lane_in_head >= BIAS_Q_LANE) & (lane_in_head < BIAS_Q_LANE + 3),
        lane_in_head == HEAD_DIM]
        + [jnp.zeros((hw,), bool)] * (SUBLANES - 3)).astype(F32)
    prow = jnp.arange(LANES)
    head, copy = prow // FORGET_COPIES_STRIDE, prow % FORGET_COPIES_STRIDE
    target = jnp.where(copy < 3, head * LANES + BIAS_Q_LANE + copy, hw + head * LANES + BIAS_K_LANE + copy - 3)
    place = ((jnp.arange(2 * hw)[None, :] == target[:, None]) & (copy < 6)[:, None]).astype(BF16)

    q, k, v, u, fedge = _proj(x2, norm_mix_g[None, :], w_main, wf, bf, qg, kg, bd, _lower_tri(TM_PROJ), place, ones,
                              shift, seq=s)
    y_attn = _attention(q.reshape(b, s, hw), k.reshape(b, s, hw), v.reshape(b, s, hw), fedge, plain_ok)

    y_ssm = _s5(u.reshape(b, s, dssm),
                *_s5_params(ssm_A_re, ssm_A_im, ssm_log_dt, ssm_B_re, ssm_B_im, ssm_C_re, ssm_C_im, s // SUPER))

    wr = jnp.concatenate([w_router_expert.T, w_router_group.T,
                          jnp.zeros((ROUTER_ROWS - N_EXPERTS - N_EXPERT_GROUPS, d), F32)], axis=0)
    br = jnp.concatenate([b_router_expert, b_router_group,
                          jnp.full((SUBLANES - N_EXPERT_GROUPS,), NEG_INF, F32),
                          jnp.zeros((ROUTER_ROWS - N_EXPERTS - SUBLANES,), F32)])[:, None]
    n_tiles = (2 * t) // TM_EXPERT + N_EXPERTS
    x1, xp, logits, xs0 = _mix(x2, y_attn.reshape(t, da), y_ssm.reshape(t, dssm), u, norm_mix_g[None, :], w_gates,
                               ssm_D[None, :], w_glu.astype(BF16), w_proj_attn.astype(BF16),
                               w_proj_ssm.astype(BF16), w_out.astype(BF16), norm_ffn_g[None, :], *_split_bf16(wr),
                               br, n_slots=n_tiles * TM_EXPERT)
    ids, wts = _route_call(logits)

    dest, tile_expert, tile_valid = _meta(ids[:2], _upper_tri(SORT_CHUNK), n_tiles)
    xs = _dispatch(dest, xp, xs0)
    ys = _experts(tile_expert, tile_valid, xs, w_expert_gate, w_expert_up, w_expert_down)
    w0 = jnp.broadcast_to(wts[0][:, None], (t, LANES))
    w1 = jnp.broadcast_to(wts[1][:, None], (t, LANES))
    out = _combine(dest, x1, w0, w1, ys)
    return out.reshape(b, s, d)


def kernel(x, norm_mix_g, w_in, b_forget, q_norm_g, k_norm_g, ssm_A_re, ssm_A_im, ssm_log_dt, ssm_B_re, ssm_B_im,
           ssm_C_re, ssm_C_im, ssm_D, w_glu, w_proj_attn, w_proj_ssm, w_out, norm_ffn_g, w_router_group,
           b_router_group, w_router_expert, b_router_expert, w_expert_gate, w_expert_up, w_expert_down):
    layer_params = (norm_mix_g, w_in, b_forget, q_norm_g, k_norm_g, ssm_A_re, ssm_A_im, ssm_log_dt, ssm_B_re,
                    ssm_B_im, ssm_C_re, ssm_C_im, ssm_D, w_glu, w_proj_attn, w_proj_ssm, w_out, norm_ffn_g,
                    w_router_group, b_router_group, w_router_expert, b_router_expert, w_expert_gate, w_expert_up,
                    w_expert_down)
    for layer in range(norm_mix_g.shape[0]):
        x = _layer(x, *[p[layer] for p in layer_params])
    return x
```

```python
import functools
import math

import jax
import jax.numpy as jnp
from jax import lax
from jax.experimental import pallas as pl
from jax.experimental.pallas import tpu as pltpu

F32 = jnp.float32
BF16 = jnp.bfloat16
I32 = jnp.int32

LANES = 128
SUBLANES = 8
MXU_DIM = 256

N_HEADS = 8
HEAD_DIM = 64
SSM_GROUP = 16
SSM_STATE = 64
N_EXPERT_GROUPS = 4
EXPERTS_PER_GROUP = 8
N_EXPERTS = N_EXPERT_GROUPS * EXPERTS_PER_GROUP
EPS = 1e-6
NEG_INF = -1e30
LOG2E = math.log2(math.e)
PLAIN_SOFTMAX_MAX_LOG2 = 60.0
UNDERFLOW_LOG2 = 160.0
BIAS_K_LANE = HEAD_DIM
BIAS_Q_LANE = HEAD_DIM + 3
FORGET_COPIES_STRIDE = LANES // N_HEADS

CHUNK = MXU_DIM // SSM_GROUP
CHUNKS_PER_SUPER = 8
SUPER = CHUNK * CHUNKS_PER_SUPER
GROUPS_PER_BLOCK = LANES // SSM_GROUP
RELAYOUT_UNROLL = 16
GROUP_UNROLL = 4
MIN_FACTORED_DECAY = -4.0

TM_PROJ = 512
TQ_ATTN = 2048
TS_ATTN = 512
HEADS_PER_STEP = 2
TM_MIX = 512
MIX_CHAINS = 1
TM_EXPERT = 512
TM_DISPATCH = 1024
TM_COMBINE = 1024
SORT_CHUNK = 1024
ROW_DMA_UNROLL = 16
ROUTE_LANES = 2048
ROUTER_ROWS = 48

VMEM_LIMIT = 48 * 1024 * 1024


def _nt_dot(a, b):
    return lax.dot_general(a, b, (((1,), (1,)), ((), ())), preferred_element_type=F32)


def _dot(a, b):
    return jnp.dot(a, b, preferred_element_type=F32)


def _lane_tile(x, n):
    return x if n == 1 else jnp.concatenate([x] * n, axis=1)


def _rmsnorm_rows(x, g):
    ms = jnp.mean(x * x, axis=-1, keepdims=True)
    return x * lax.rsqrt(ms + EPS) * g


def _sigmoid(x):
    return 1.0 / (1.0 + jnp.exp(-x))


def _split3(x):
    hi = x.astype(BF16)
    r1 = x - hi.astype(F32)
    mid = r1.astype(BF16)
    lo = (r1 - mid.astype(F32)).astype(BF16)
    return [hi, mid, lo]


def _expand_heads(z):
    lane = lax.broadcasted_iota(I32, (z.shape[0], LANES), 1)
    blocks = []
    for p in range(z.shape[1] // LANES):
        blk = z[:, p * LANES:(p + 1) * LANES]
        blocks.append(jnp.where(lane < HEAD_DIM, blk, 0.0))
        blocks.append(jnp.where(lane < HEAD_DIM, pltpu.roll(blk, HEAD_DIM, axis=1), 0.0))
    return jnp.concatenate(blocks, axis=1)


def _proj_kernel(x_ref, g_ref, w_ref, wf_ref, bf_ref, qg_ref, kg_ref, bd_ref, ltri_ref, place_ref,
                 ones_ref, shift_ref, q_ref, k_ref, v_ref, u_ref, fedge_ref, carry_ref, *, tiles_per_seq):
    i = pl.program_id(0)

    @pl.when(i % tiles_per_seq == 0)
    def _():
        carry_ref[...] = jnp.zeros_like(carry_ref)

    tm = x_ref.shape[0]
    da = N_HEADS * HEAD_DIM
    hw = N_HEADS * LANES
    xb = _rmsnorm_rows(x_ref[...], g_ref[...]).astype(BF16)
    z = _dot(xb, w_ref[...])
    zq, zk = z[:, :da], z[:, da:2 * da]
    bd = bd_ref[...]
    msq = _dot((zq * zq).astype(BF16), bd)
    msk = _dot((zk * zk).astype(BF16), bd)
    qn = zq * lax.rsqrt(msq + EPS) * qg_ref[...]
    kn = zk * lax.rsqrt(msk + EPS) * kg_ref[...]
    u_ref[...] = z[:, 3 * da:]

    a = _dot(xb, wf_ref[...]) + bf_ref[...]
    logf = jnp.minimum(a, 0.0) - jnp.log(1.0 + jnp.exp(-jnp.abs(a)))
    cs = _dot(ltri_ref[...], jnp.concatenate(_split3(logf), axis=1))
    cum = cs[:, :LANES] + cs[:, LANES:2 * LANES] + cs[:, 2 * LANES:] + carry_ref[0:1, :]
    carry_ref[...] = jnp.broadcast_to(cum[tm - 1:tm, :], carry_ref.shape)
    edge_row = lax.broadcasted_iota(I32, fedge_ref.shape, 0)
    fedge_ref[...] = jnp.where(edge_row == 0, cum[0:1, :], jnp.where(edge_row == 1, cum[tm - 1:tm, :], 0.0))
    copy = lax.broadcasted_iota(I32, cum.shape, 1) % FORGET_COPIES_STRIDE
    bias = jnp.where(copy < 3, (cum - shift_ref[...]) * LOG2E, cum * (-LOG2E))
    hi, mid, lo = [p.astype(F32) for p in _split3(bias)]
    piece = jnp.where(copy % 3 == 0, hi, jnp.where(copy % 3 == 1, mid, lo)).astype(BF16)
    extras = _dot(piece, place_ref[...])
    q_ref[...] = (_expand_heads(qn) + extras[:, :hw] + ones_ref[0:1, :]).astype(BF16)
    k_ref[...] = (_expand_heads(kn) + extras[:, hw:] + ones_ref[1:2, :]).astype(BF16)
    v_ref[...] = (_expand_heads(z[:, 2 * da:3 * da]) + ones_ref[2:3, :]).astype(BF16)


def _proj(x2, norm_g, w_main, wf, b_f, qg, kg, bd, ltri, place, ones, shift, *, seq):
    t, d = x2.shape
    tm = TM_PROJ
    hw = N_HEADS * LANES
    du = w_main.shape[1] - 3 * N_HEADS * HEAD_DIM
    tiles_per_seq = seq // tm
    full = lambda a: pl.BlockSpec(a.shape, lambda i: (0,) * a.ndim)
    row = lambda width: pl.BlockSpec((tm, width), lambda i: (i, 0))
    consts = (norm_g, w_main, wf, b_f, qg, kg, bd, ltri, place, ones, shift)
    return pl.pallas_call(
        functools.partial(_proj_kernel, tiles_per_seq=tiles_per_seq),
        grid=(t // tm,),
        in_specs=[row(d)] + [full(a) for a in consts],
        out_specs=[row(hw), row(hw), row(hw), row(du), pl.BlockSpec((None, SUBLANES, LANES), lambda i: (i, 0, 0))],
        out_shape=[jax.ShapeDtypeStruct((t, hw), BF16)] * 3
        + [jax.ShapeDtypeStruct((t, du), F32), jax.ShapeDtypeStruct((t // tm, SUBLANES, LANES), F32)],
        scratch_shapes=[pltpu.VMEM((SUBLANES, LANES), F32)],
        compiler_params=pltpu.CompilerParams(dimension_semantics=("arbitrary",), vmem_limit_bytes=VMEM_LIMIT),
        name="proj",
    )(x2, *consts)


def _attn_kernel(first_ref, q_ref, k_ref, v_ref, o_ref, acc_scr, m_scr, *, ts, safe):
    i = pl.program_id(2)
    nsub = q_ref.shape[0] // ts
    acc_scr[...] = jnp.zeros(acc_scr.shape, F32)
    if safe:
        m_scr[...] = jnp.full(m_scr.shape, NEG_INF, F32)

    def block(hd, sub, j, masked):
        off = pl.multiple_of(j * ts, ts)
        lanes = slice(hd * LANES, (hd + 1) * LANES)
        chain = hd * nsub + sub
        s = _nt_dot(q_ref[sub * ts:(sub + 1) * ts, lanes], k_ref[pl.ds(off, ts), lanes])
        if masked:
            rows = lax.broadcasted_iota(I32, (ts, ts), 0)
            cols = lax.broadcasted_iota(I32, (ts, ts), 1)
            s = jnp.where(cols <= rows, s, NEG_INF)
        vblk = v_ref[pl.ds(off, ts), lanes]
        if safe:
            m_prev = m_scr[chain]
            m_new = jnp.maximum(m_prev, jnp.max(s, axis=1, keepdims=True))
            p = jnp.exp2(s - _lane_tile(m_new, ts // LANES)).astype(BF16)
            acc_scr[chain] = jnp.exp2(m_prev - m_new) * acc_scr[chain] + _dot(p, vblk)
            m_scr[chain] = m_new
        else:
            acc_scr[chain] += _dot(jnp.exp2(s).astype(BF16), vblk)

    def body(heads, j, carry):
        for hd in heads:
            for sub in range(nsub):
                block(hd, sub, j, False)
        return carry

    step = (pl.program_id(0) * pl.num_programs(1) + pl.program_id(1)) * pl.num_programs(2) + i
    starts = [first_ref[step * HEADS_PER_STEP + hd] for hd in range(HEADS_PER_STEP)]
    shared = functools.reduce(jnp.maximum, starts)
    for hd in range(HEADS_PER_STEP):
        lax.fori_loop(starts[hd], shared, functools.partial(body, (hd,)), 0)
    lax.fori_loop(shared, nsub * i, functools.partial(body, tuple(range(HEADS_PER_STEP))), 0)
    for jj in range(nsub):
        for hd in range(HEADS_PER_STEP):
            for sub in range(jj, nsub):
                block(hd, sub, nsub * i + jj, sub == jj)
    lane = lax.broadcasted_iota(I32, (ts, LANES), 1)
    for sub in range(nsub):
        outs = []
        for hd in range(HEADS_PER_STEP):
            acc = acc_scr[hd * nsub + sub]
            outs.append(acc / acc[:, HEAD_DIM:HEAD_DIM + 1])
        both = jnp.where(lane < HEAD_DIM, outs[0], pltpu.roll(outs[1], HEAD_DIM, axis=1))
        o_ref[sub * ts:(sub + 1) * ts, :] = both.astype(o_ref.dtype)


def _first_kv_block(fedge, b, s, tq, ts):
    nblk = s // ts
    f_first = fedge[:, 0, ::FORGET_COPIES_STRIDE].reshape(b, nblk, N_HEADS).transpose(0, 2, 1)
    f_last = fedge[:, 1, ::FORGET_COPIES_STRIDE].reshape(b, nblk, N_HEADS).transpose(0, 2, 1)
    tile_start = f_first[:, :, ::tq // ts]
    bound = LOG2E * (tile_start[:, :, :, None] - f_last[:, :, None, :])
    first = jnp.sum(bound < -UNDERFLOW_LOG2, axis=-1).astype(I32)
    first = first.reshape(b, N_HEADS // HEADS_PER_STEP, HEADS_PER_STEP, -1).transpose(0, 1, 3, 2)
    return first.reshape(-1)


def _attention(q, k, v, fedge, plain_ok):
    b, s, hw = q.shape
    tq, ts = TQ_ATTN, TS_ATTN
    nq = s // tq
    steps = N_HEADS // HEADS_PER_STEP
    width = HEADS_PER_STEP * LANES
    chains = HEADS_PER_STEP * tq // ts

    def call(safe):
        first = jnp.zeros((b * N_HEADS * nq,), I32) if safe else _first_kv_block(fedge, b, s, tq, ts)
        return pl.pallas_call(
            functools.partial(_attn_kernel, ts=ts, safe=safe),
            grid_spec=pltpu.PrefetchScalarGridSpec(
                num_scalar_prefetch=1,
                grid=(b, steps, nq),
                in_specs=[pl.BlockSpec((None, tq, width), lambda bi, h, i, f: (bi, i, h)),
                          pl.BlockSpec((None, s, width), lambda bi, h, i, f: (bi, 0, h)),
                          pl.BlockSpec((None, s, width), lambda bi, h, i, f: (bi, 0, h))],
                out_specs=pl.BlockSpec((None, tq, LANES), lambda bi, h, i, f: (bi, i, h)),
                scratch_shapes=[pltpu.VMEM((chains, ts, LANES), F32)] * 2),
            out_shape=jax.ShapeDtypeStruct((b, s, steps * LANES), BF16),
            compiler_params=pltpu.CompilerParams(
                dimension_semantics=("arbitrary", "arbitrary", "arbitrary"), vmem_limit_bytes=VMEM_LIMIT),
            name="attn_safe" if safe else "attn",
        )(first, q, k, v)

    return lax.cond(plain_ok, lambda: call(False), lambda: call(True))


def _window_select(pieces, first_group):
    window = lax.broadcasted_iota(I32, pieces[0].shape, 1) // SSM_GROUP
    out = pieces[0]
    for m in range(1, GROUPS_PER_BLOCK):
        out = jnp.where(window == (first_group + m) % GROUPS_PER_BLOCK, pieces[m], out)
    return out


def _s5_kernel(u_ref, win_ref, toep_ref, wout_ref, coef_ref, y_ref, ub_scr, yb_scr, *, nk):
    nj = CHUNKS_PER_SUPER
    gpb = GROUPS_PER_BLOCK
    w = CHUNK * SSM_GROUP
    half = w // 2
    nkk = nk // SUBLANES
    row_stride = SUPER

    def gather_body(it, carry):
        j, kk = it // nkk, it % nkk
        for sp in range(CHUNK // gpb):
            rot = []
            for m in range(gpb):
                row0 = j * CHUNK + sp * gpb + m + kk * (SUBLANES * row_stride)
                src = u_ref[pl.ds(row0, SUBLANES, stride=row_stride), :]
                rot.append(src if m == 0 else pltpu.roll(src, m * SSM_GROUP, axis=1))
            for gl in range(gpb):
                ub_scr[gl, j, pl.ds(kk * SUBLANES, SUBLANES), sp * LANES:(sp + 1) * LANES] = _window_select(rot, gl)
        return carry

    lax.fori_loop(0, nj * nkk, gather_body, 0, unroll=RELAYOUT_UNROLL)

    def swap(val):
        return jnp.concatenate([val[:, half:], val[:, :half]], axis=1)

    def group_body(gl, carry):
        def cmul(val, idx):
            return coef_ref[gl, idx:idx + 1, :] * val + coef_ref[gl, idx + 1:idx + 2, :] * swap(val)

        u = ub_scr[gl].reshape(nj * nk, w).astype(BF16)
        s1 = _dot(u, win_ref[gl]).reshape(nj, nk, w)
        e = jnp.zeros((nk, w), F32)
        local = []
        for j in range(nj):
            local.append(e)
            e = cmul(e, 0) + s1[j]
        kidx = lax.broadcasted_iota(I32, (nk, w), 0)
        x = e
        step, d = 0, 1
        while d < nk:
            shifted = jnp.where(kidx >= d, pltpu.roll(x, d, axis=0), 0.0)
            x = x + cmul(shifted, 2 + 2 * nj + 2 * step)
            step, d = step + 1, d * 2
        x_start = jnp.where(kidx >= 1, pltpu.roll(x, 1, axis=0), 0.0)
        starts = [local[j] + cmul(x_start, 2 + 2 * j) for j in range(nj)]
        p = jnp.concatenate([st[:, :half] for st in starts], axis=0)
        hi = p.astype(BF16)
        lo = (p - hi.astype(F32)).astype(BF16)
        wout_t = wout_ref[gl]
        y = _dot(u, toep_ref[gl]) + _nt_dot(hi, wout_t) + _nt_dot(lo, wout_t)
        yb_scr[gl] = y.reshape(nj, nk, w)
        return carry

    lax.fori_loop(0, gpb, group_body, 0, unroll=GROUP_UNROLL)

    def scatter_body(it, carry):
        j, kk = it // nkk, it % nkk
        for tp in range(CHUNK // gpb):
            src = [yb_scr[gl, j, pl.ds(kk * SUBLANES, SUBLANES), tp * LANES:(tp + 1) * LANES] for gl in range(gpb)]
            for m in range(gpb):
                by_window = [src[(wi - m) % gpb] for wi in range(gpb)]
                window = lax.broadcasted_iota(I32, by_window[0].shape, 1) // SSM_GROUP
                merged = by_window[0]
                for wi in range(1, gpb):
                    merged = jnp.where(window == wi, by_window[wi], merged)
                nat = merged if m == 0 else pltpu.roll(merged, LANES - m * SSM_GROUP, axis=1)
                row0 = j * CHUNK + tp * gpb + m + kk * (SUBLANES * row_stride)
                y_ref[pl.ds(row0, SUBLANES, stride=row_stride), :] = nat
        return carry

    lax.fori_loop(0, nj * nkk, scatter_body, 0, unroll=RELAYOUT_UNROLL)


def _s5_params(a_re, a_im, log_dt, b_re, b_im, c_re, c_im, nk):
    args = (a_re, a_im, log_dt, b_re, b_im, c_re, c_im)
    slowest = jnp.min(a_re * jnp.exp(log_dt)[:, None])
    return lax.cond(slowest > MIN_FACTORED_DECAY, functools.partial(_s5_params_factored, nk=nk),
                    functools.partial(_s5_params_direct, nk=nk), *args)


def _s5_params_factored(a_re, a_im, log_dt, b_re, b_im, c_re, c_im, *, nk):
    g, p = a_re.shape
    c = SSM_GROUP
    gpb = GROUPS_PER_BLOCK
    dt = jnp.exp(log_dt)[:, None]
    adt_r, adt_i = a_re * dt, a_im * dt

    def lam_pow(n):
        nf = jnp.asarray(n, F32)
        nf = (nf[None] if nf.ndim == 1 else nf)[:, :, None]
        mag = jnp.exp(adt_r[:, None, :] * nf)
        ang = adt_i[:, None, :] * nf
        return mag * jnp.cos(ang), mag * jnp.sin(ang)

    gi = jnp.arange(g)[:, None, None] % gpb
    step = ((jnp.arange(CHUNK // gpb)[None, :, None]) * gpb
            + (jnp.arange(gpb)[None, None, :] - gi) % gpb).reshape(g, CHUNK)

    l1r, l1i = lam_pow(jnp.ones((1,)))
    den = a_re * a_re + a_im * a_im
    nr, ni = l1r[:, 0] - 1.0, l1i[:, 0]
    qr = (nr * a_re + ni * a_im) / den
    qi = (ni * a_re - nr * a_im) / den
    bcr = (qr[..., None] * b_re - qi[..., None] * b_im).transpose(0, 2, 1)
    bci = (qr[..., None] * b_im + qi[..., None] * b_re).transpose(0, 2, 1)

    def times_b(pr, pi):
        pr, pi = pr[:, :, None, :], pi[:, :, None, :]
        return pr * bcr[:, None] - pi * bci[:, None], pr * bci[:, None] + pi * bcr[:, None]

    def times_c(pr, pi):
        pr, pi = pr[:, :, None, :], pi[:, :, None, :]
        return c_re[:, None] * pr - c_im[:, None] * pi, c_re[:, None] * pi + c_im[:, None] * pr

    rows = lambda parts: jnp.concatenate(parts, axis=-1).reshape(g, CHUNK * c, -1)
    ir, ii = times_b(*lam_pow(CHUNK - 1 - step))
    win = rows([ir, ii, ii, ir])
    clr, cli = times_c(*lam_pow(step + 1))
    wout_t = rows([clr, -cli])
    ar, ai = times_b(*lam_pow(-step))
    br, bi = times_c(*lam_pow(step))
    toep = jnp.einsum('gnp,gmp->gnm', rows([ar, ai]), rows([br, -bi]), precision=lax.Precision.HIGHEST)
    step_of_lane = jnp.repeat(step, c, axis=1)
    toep = jnp.where(step_of_lane[:, None, :] >= step_of_lane[:, :, None], toep, 0.0)
    return win.astype(BF16), toep.astype(BF16), wout_t.astype(BF16), _s5_scan_coefficients(lam_pow, g, p, nk)


def _s5_scan_coefficients(lam_pow, g, p, nk):
    n_steps = max(nk.bit_length() - 1, 0)
    powers = [CHUNK] + [CHUNK * j for j in range(CHUNKS_PER_SUPER)] + [SUPER * (1 << i) for i in range(n_steps)]
    ar, ai = lam_pow(jnp.array(powers, dtype=jnp.int32))
    c1 = jnp.concatenate([ar, ar, ar, ar], axis=2)
    c2 = jnp.concatenate([-ai, ai, ai, -ai], axis=2)
    coef = jnp.stack([c1, c2], axis=2).reshape(g, 2 * len(powers), 4 * p)
    return jnp.pad(coef, ((0, 0), (0, (-coef.shape[1]) % SUBLANES), (0, 0)))


def _s5_params_direct(a_re, a_im, log_dt, b_re, b_im, c_re, c_im, *, nk):
    g, p = a_re.shape
    c = SSM_GROUP
    hp = lax.Precision.HIGHEST
    dt = jnp.exp(log_dt)[:, None]
    adt_r, adt_i = a_re * dt, a_im * dt

    def lam_pow(n):
        nf = jnp.asarray(n, F32)[None, :, None]
        mag = jnp.exp(adt_r[:, None, :] * nf)
        ang = adt_i[:, None, :] * nf
        return mag * jnp.cos(ang), mag * jnp.sin(ang)

    lr, li = lam_pow(jnp.arange(CHUNK + 1))
    den = a_re * a_re + a_im * a_im
    nr, ni = lr[:, 1] - 1.0, li[:, 1]
    qr = (nr * a_re + ni * a_im) / den
    qi = (ni * a_re - nr * a_im) / den
    bbr = qr[..., None] * b_re - qi[..., None] * b_im
    bbi = qr[..., None] * b_im + qi[..., None] * b_re

    mr = lr[:, :CHUNK, :, None] * bbr[:, None] - li[:, :CHUNK, :, None] * bbi[:, None]
    mi = lr[:, :CHUNK, :, None] * bbi[:, None] + li[:, :CHUNK, :, None] * bbr[:, None]
    kern = (jnp.einsum('gcp,gtpd->gtcd', c_re, mr, precision=hp)
            - jnp.einsum('gcp,gtpd->gtcd', c_im, mi, precision=hp))
    toep = jnp.stack([jnp.pad(kern[:, :CHUNK - s], ((0, 0), (s, 0), (0, 0), (0, 0))) for s in range(CHUNK)],
                     axis=1)
    toep = toep.transpose(0, 1, 4, 2, 3).reshape(g, CHUNK * c, CHUNK * c)
    clr = c_re[:, None] * lr[:, 1:, None, :] - c_im[:, None] * li[:, 1:, None, :]
    cli = c_re[:, None] * li[:, 1:, None, :] + c_im[:, None] * lr[:, 1:, None, :]
    w_re = clr.transpose(0, 3, 1, 2).reshape(g, p, CHUNK * c)
    w_im = (-cli).transpose(0, 3, 1, 2).reshape(g, p, CHUNK * c)
    wout = jnp.concatenate([w_re, w_im], axis=1)
    pr, pi = lr[:, :CHUNK][:, ::-1], li[:, :CHUNK][:, ::-1]
    ir = pr[..., None] * bbr[:, None] - pi[..., None] * bbi[:, None]
    ii = pr[..., None] * bbi[:, None] + pi[..., None] * bbr[:, None]
    ir = ir.transpose(0, 1, 3, 2).reshape(g, CHUNK * c, p)
    ii = ii.transpose(0, 1, 3, 2).reshape(g, CHUNK * c, p)
    win = jnp.concatenate([ir, ii, ii, ir], axis=2)

    gi = jnp.arange(g)[:, None, None] % GROUPS_PER_BLOCK
    si = jnp.arange(CHUNK)[None, :, None]
    ci = jnp.arange(c)[None, None, :]
    lane_of = ((si // GROUPS_PER_BLOCK) * LANES + ((gi + si) % GROUPS_PER_BLOCK) * c + ci).reshape(g, CHUNK * c)
    perm = (lane_of[:, None, :] == jnp.arange(CHUNK * c)[None, :, None]).astype(BF16)
    win = jnp.einsum('gln,gnk->glk', perm, win.astype(BF16))
    toep = jnp.einsum('gln,gnk->glk', perm, toep.astype(BF16))
    toep = jnp.einsum('grn,gln->grl', toep, perm)
    wout_t = jnp.einsum('gln,gpn->glp', perm, wout.astype(BF16))
    return win.astype(BF16), toep.astype(BF16), wout_t.astype(BF16), _s5_scan_coefficients(lam_pow, g, p, nk)


def _s5(u, win, toep, wout_t, coef):
    b, s, dssm = u.shape
    nk = s // SUPER
    nj = CHUNKS_PER_SUPER
    w = CHUNK * SSM_GROUP
    gpb = GROUPS_PER_BLOCK
    nblk = dssm // LANES
    wspec = lambda a: pl.BlockSpec((gpb,) + a.shape[1:], lambda bi, li: (li, 0, 0))
    return pl.pallas_call(
        functools.partial(_s5_kernel, nk=nk),
        grid=(b, nblk),
        in_specs=[pl.BlockSpec((None, s, LANES), lambda bi, li: (bi, 0, li)), wspec(win), wspec(toep),
                  wspec(wout_t), wspec(coef)],
        out_specs=pl.BlockSpec((None, s, LANES), lambda bi, li: (bi, 0, li)),
        out_shape=jax.ShapeDtypeStruct((b, s, dssm), F32),
        scratch_shapes=[pltpu.VMEM((gpb, nj, nk, w), F32)] * 2,
        compiler_params=pltpu.CompilerParams(dimension_semantics=("arbitrary", "arbitrary"),
                                             vmem_limit_bytes=VMEM_LIMIT),
        name="s5",
    )(u, win, toep, wout_t, coef)


def _route(logits):
    e_all = logits[0:N_EXPERTS]
    gl = logits[N_EXPERTS:N_EXPERTS + SUBLANES]
    tm = logits.shape[1]
    ridx = lax.broadcasted_iota(I32, (SUBLANES, tm), 0)
    ge = jnp.exp(gl - jnp.max(gl, axis=0, keepdims=True))
    gp = ge / jnp.sum(ge, axis=0, keepdims=True)
    g_top = jnp.max(gp, axis=0, keepdims=True)
    g_sel = jnp.min(jnp.where(gp == g_top, ridx, SUBLANES), axis=0, keepdims=True)
    e_in = e_all[(N_EXPERT_GROUPS - 1) * SUBLANES:]
    for gi in range(N_EXPERT_GROUPS - 2, -1, -1):
        e_in = jnp.where(g_sel == gi, e_all[gi * SUBLANES:(gi + 1) * SUBLANES], e_in)
    ee = jnp.exp(e_in - jnp.max(e_in, axis=0, keepdims=True))
    ep = ee / jnp.sum(ee, axis=0, keepdims=True)
    v1 = jnp.max(ep, axis=0, keepdims=True)
    i1 = jnp.min(jnp.where(ep == v1, ridx, SUBLANES), axis=0, keepdims=True)
    ep2 = jnp.where(ridx == i1, -1.0, ep)
    v2 = jnp.max(ep2, axis=0, keepdims=True)
    i2 = jnp.min(jnp.where(ep2 == v2, ridx, SUBLANES), axis=0, keepdims=True)
    den = v1 + v2
    w1 = g_top * v1 / den
    w2 = g_top * v2 / den
    e1 = g_sel * EXPERTS_PER_GROUP + i1
    e2 = g_sel * EXPERTS_PER_GROUP + i2
    ids = jnp.where(ridx == 0, e1, jnp.where(ridx == 1, e2, 0))
    wts = jnp.where(ridx == 0, w1, jnp.where(ridx == 1, w2, 0.0))
    return ids, wts


def _mix_kernel(x_ref, ya_ref, ys_ref, u_ref, g1_ref, wgate_ref, dsk_ref, wglu_ref, wpa_ref, wps_ref,
                wout_ref, g2_ref, wrh_ref, wrl_ref, br_ref, x1_ref, xp_ref, logits_ref, slots_ref, zero_scr, zsem):
    d = x_ref.shape[1]
    rows = x_ref.shape[0] // MIX_CHAINS
    i = pl.program_id(0)

    @pl.when(i == 0)
    def _():
        zero_scr[...] = jnp.zeros(zero_scr.shape, zero_scr.dtype)

    zrows = zero_scr.shape[0]
    zero_copy = pltpu.make_async_copy(zero_scr, slots_ref.at[pl.ds(pl.multiple_of(i * zrows, SUBLANES), zrows)], zsem)
    zero_copy.start()
    for chain in range(MIX_CHAINS):
        r = slice(chain * rows, (chain + 1) * rows)
        x = x_ref[r, :]
        xb = _rmsnorm_rows(x, g1_ref[...]).astype(BF16)
        gates = _dot(xb, wgate_ref[...])
        y = ys_ref[r, :].astype(F32) + dsk_ref[...] * u_ref[r, :].astype(F32)
        y = y * (0.5 * (1.0 + jnp.tanh(math.sqrt(2.0 / math.pi) * (y + 0.044715 * (y * y * y)))))
        y = y * _sigmoid(_dot(y.astype(BF16), wglu_ref[...]))
        mixed = (_sigmoid(gates[:, :d]) * _dot(ya_ref[r, :], wpa_ref[...])
                 + _sigmoid(gates[:, d:]) * _dot(y.astype(BF16), wps_ref[...]))
        x1 = x + _dot(mixed.astype(BF16), wout_ref[...])
        x1_ref[r, :] = x1
        xn = _rmsnorm_rows(x1, g2_ref[...])
        xp_ref[r, :] = pltpu.pack_elementwise([xn[:, :d // 2], xn[:, d // 2:]], packed_dtype=BF16)
        xh = xn.astype(BF16)
        xl = (xn - xh.astype(F32)).astype(BF16)
        wrh = wrh_ref[...]
        logits_ref[:, r] = _nt_dot(wrh, xh) + _nt_dot(wrl_ref[...], xh) + _nt_dot(wrh, xl) + br_ref[...]
    zero_copy.wait()


def _mix(x2, ya, ys, u, g1, wgate, dsk, wglu, wpa, wps, wout, g2, wrh, wrl, br, *, n_slots):
    t, d = x2.shape
    tm = TM_MIX
    steps = t // tm
    assert n_slots % steps == 0 and (n_slots // steps) % SUBLANES == 0
    full = lambda a: pl.BlockSpec(a.shape, lambda i: (0,) * a.ndim)
    row = lambda width: pl.BlockSpec((tm, width), lambda i: (i, 0))
    return pl.pallas_call(
        _mix_kernel,
        grid=(steps,),
        in_specs=[row(d), row(ya.shape[1]), row(ys.shape[1]), row(u.shape[1]), full(g1), full(wgate), full(dsk),
                  full(wglu), full(wpa),
                  full(wps), full(wout), full(g2), full(wrh), full(wrl), full(br)],
        out_specs=[row(d), row(d // 2), pl.BlockSpec((ROUTER_ROWS, tm), lambda i: (0, i)),
                   pl.BlockSpec(memory_space=pl.ANY)],
        out_shape=[jax.ShapeDtypeStruct((t, d), F32), jax.ShapeDtypeStruct((t, d // 2), jnp.uint32),
                   jax.ShapeDtypeStruct((ROUTER_ROWS, t), F32), jax.ShapeDtypeStruct((n_slots, d // 2), jnp.uint32)],
        scratch_shapes=[pltpu.VMEM((n_slots // steps, d // 2), jnp.uint32), pltpu.SemaphoreType.DMA(())],
        compiler_params=pltpu.CompilerParams(dimension_semantics=("arbitrary",), vmem_limit_bytes=VMEM_LIMIT),
        name="mix",
    )(x2, ya, ys, u, g1, wgate, dsk, wglu, wpa, wps, wout, g2, wrh, wrl, br)


def _route_kernel(logits_ref, ids_ref, wts_ref):
    ids, wts = _route(logits_ref[...])
    ids_ref[...] = ids
    wts_ref[...] = wts


def _route_call(logits):
    t = logits.shape[1]
    tl = ROUTE_LANES
    col = pl.BlockSpec((SUBLANES, tl), lambda i: (0, i))
    return pl.pallas_call(
        _route_kernel,
        grid=(t // tl,),
        in_specs=[pl.BlockSpec((ROUTER_ROWS, tl), lambda i: (0, i))],
        out_specs=[col, col],
        out_shape=[jax.ShapeDtypeStruct((SUBLANES, t), I32), jax.ShapeDtypeStruct((SUBLANES, t), F32)],
        compiler_params=pltpu.CompilerParams(dimension_semantics=("arbitrary",)),
        name="route",
    )(logits)


def _meta_kernel(ids_ref, tri_ref, dest_ref, tile_ref, *, tile_rows):
    nk, nc, c = ids_ref.shape
    ne = N_EXPERTS
    erow = lax.broadcasted_iota(I32, (ne, c), 0)
    ones = jnp.ones((c, LANES), BF16)

    def onehot(k, ci):
        mask = erow == ids_ref[k, pl.ds(ci, 1), :]
        return mask, jnp.where(mask, 1.0, 0.0).astype(BF16)

    def count_body(n, acc):
        return acc + _dot(onehot(n // nc, n % nc)[1], ones)

    cnt = lax.fori_loop(0, nk * nc, count_body, jnp.zeros((ne, LANES), F32))
    ntiles = jnp.floor((cnt + (tile_rows - 1)) * (1.0 / tile_rows))
    lower = jnp.where(lax.broadcasted_iota(I32, (ne, ne), 1) < lax.broadcasted_iota(I32, (ne, ne), 0), 1.0, 0.0)
    start_tiles = _dot(lower.astype(BF16), ntiles.astype(BF16))
    base = start_tiles * tile_rows

    tri = tri_ref[...]

    def dest_body(n, carry):
        k, ci = n // nc, n % nc
        mask, oh = onehot(k, ci)
        prefix = _dot(oh, tri)
        slot = _lane_tile(base + carry, c // LANES) + prefix - 1.0
        dest = jnp.sum(jnp.where(mask, slot, 0.0), axis=0, keepdims=True)
        dest_ref[k, pl.ds(ci, 1), :] = dest.astype(I32)
        return carry + _dot(oh, ones)

    lax.fori_loop(0, nk * nc, dest_body, jnp.zeros((ne, LANES), F32))

    nt_lanes = tile_ref.shape[1]
    end_tiles = _lane_tile(start_tiles + ntiles, nt_lanes // LANES)
    tidx = lax.broadcasted_iota(I32, (ne, nt_lanes), 1).astype(F32)
    texp = jnp.sum(jnp.where(tidx >= end_tiles, 1.0, 0.0), axis=0, keepdims=True)
    valid = jnp.where(texp < ne, 1, 0)
    texp = jnp.minimum(texp, ne - 1.0).astype(I32)
    ridx = lax.broadcasted_iota(I32, tile_ref.shape, 0)
    tile_ref[...] = jnp.where(ridx == 0, texp, jnp.where(ridx == 1, valid, 0))


def _meta(ids2, tri, n_tiles):
    nk, t = ids2.shape
    c = SORT_CHUNK
    nt_lanes = pl.cdiv(n_tiles, LANES) * LANES
    ids3 = ids2.reshape(nk, t // c, c)
    dest, tile = pl.pallas_call(
        functools.partial(_meta_kernel, tile_rows=TM_EXPERT),
        out_shape=[jax.ShapeDtypeStruct(ids3.shape, I32), jax.ShapeDtypeStruct((SUBLANES, nt_lanes), I32)],
        compiler_params=pltpu.CompilerParams(vmem_limit_bytes=VMEM_LIMIT),
        name="meta",
    )(ids3, tri)
    return dest.reshape(nk * t), tile[0, :n_tiles], tile[1, :n_tiles]


def _row_copy(src_ref, src_row, dst_ref, dst_row, sem):
    return pltpu.make_async_copy(src_ref.at[pl.ds(src_row, 1)], dst_ref.at[pl.ds(dst_row, 1)], sem)


def _dispatch_kernel(dest_ref, xp_ref, xs_in_ref, xs_ref, sem, *, n_tokens):
    del xs_in_ref
    tm = xp_ref.shape[0]
    base = pl.program_id(0) * tm

    def body(r, carry):
        for k in range(2):
            _row_copy(xp_ref, r, xs_ref, dest_ref[k * n_tokens + base + r], sem).start(priority=k)
        return carry

    lax.fori_loop(0, tm, body, 0, unroll=ROW_DMA_UNROLL)
    for k in range(2):
        pltpu.make_async_copy(xp_ref, xs_ref.at[pl.ds(0, tm)], sem).wait()


def _dispatch(dest, xp, xs0):
    t, w = xp.shape
    tm = TM_DISPATCH
    return pl.pallas_call(
        functools.partial(_dispatch_kernel, n_tokens=t),
        grid_spec=pltpu.PrefetchScalarGridSpec(
            num_scalar_prefetch=1,
            grid=(t // tm,),
            in_specs=[pl.BlockSpec((tm, w), lambda i, d: (i, 0)), pl.BlockSpec(memory_space=pl.ANY)],
            out_specs=pl.BlockSpec(memory_space=pl.ANY),
            scratch_shapes=[pltpu.SemaphoreType.DMA(())]),
        out_shape=jax.ShapeDtypeStruct(xs0.shape, xp.dtype),
        input_output_aliases={2: 0},
        compiler_params=pltpu.CompilerParams(dimension_semantics=("arbitrary",), has_side_effects=True),
        name="dispatch",
    )(dest, xp, xs0)


def _unpack_rows(packed):
    lo = pltpu.unpack_elementwise(packed, index=0, packed_dtype=BF16, unpacked_dtype=F32)
    hi = pltpu.unpack_elementwise(packed, index=1, packed_dtype=BF16, unpacked_dtype=F32)
    return jnp.concatenate([lo, hi], axis=1)


def _expert_kernel(te_ref, tv_ref, nv_ref, nxt_ref, xs_ref, wg_hbm, wu_hbm, wd_hbm, ys_ref,
                   wg_bf, wu_bf, wd_bf, wg_f32, wu_f32, wd_f32, slot_ref, sem):
    del nv_ref
    i = pl.program_id(0)
    half = xs_ref.shape[1]

    def weight_copies(expert, slot):
        return [pltpu.make_async_copy(hbm.at[expert], buf.at[slot], sem.at[slot])
                for hbm, buf in ((wg_hbm, wg_f32), (wu_hbm, wu_f32), (wd_hbm, wd_f32))]

    @pl.when(i == 0)
    def _():
        slot_ref[0] = 0
        for copy in weight_copies(te_ref[0], 0):
            copy.start()

    @pl.when((i == 0) | (te_ref[i] != te_ref[jnp.maximum(i - 1, 0)]))
    def _():
        slot = slot_ref[0]
        for copy in weight_copies(te_ref[i], slot):
            copy.wait()
        wg_bf[...] = wg_f32[slot].astype(BF16)
        wu_bf[...] = wu_f32[slot].astype(BF16)
        wd_bf[...] = wd_f32[slot].astype(BF16)
        slot_ref[0] = 1 - slot

        @pl.when(nxt_ref[i] >= 0)
        def _():
            for copy in weight_copies(nxt_ref[i], 1 - slot):
                copy.start()

    @pl.when(tv_ref[i] > 0)
    def _():
        x = _unpack_rows(xs_ref[...]).astype(BF16)
        hg = _dot(x, wg_bf[...])
        hu = _dot(x, wu_bf[...])
        h = (hg * _sigmoid(hg) * hu).astype(BF16)
        y = _dot(h, wd_bf[...])
        ys_ref[...] = pltpu.pack_elementwise([y[:, :half], y[:, half:]], packed_dtype=BF16)

    @pl.when(tv_ref[i] == 0)
    def _():
        zero = jnp.zeros(ys_ref.shape, F32)
        ys_ref[...] = pltpu.pack_elementwise([zero, zero], packed_dtype=BF16)


def _experts(tile_expert, tile_valid, xs, wg, wu, wd):
    n_slots, w = xs.shape
    tm = TM_EXPERT
    n_tiles = n_slots // tm
    n_valid = jnp.sum(tile_valid).astype(I32).reshape(1)
    tile_expert = jnp.where(tile_valid > 0, tile_expert, tile_expert[n_valid[0] - 1])
    later = (tile_expert[None, :] > tile_expert[:, None]) & (tile_valid[None, :] > 0)
    next_expert = jnp.min(jnp.where(later, tile_expert[None, :], N_EXPERTS), axis=1)
    next_expert = jnp.where(next_expert < N_EXPERTS, next_expert, -1).astype(I32)
    rows_in = pl.BlockSpec((tm, w), lambda i, te, tv, nv, nx: (jnp.minimum(i, nv[0] - 1), 0))
    rows_out = pl.BlockSpec((tm, w), lambda i, te, tv, nv, nx: (i, 0))
    hbm = pl.BlockSpec(memory_space=pl.ANY)
    weights = (wg, wu, wd)
    return pl.pallas_call(
        _expert_kernel,
        grid_spec=pltpu.PrefetchScalarGridSpec(
            num_scalar_prefetch=4,
            grid=(n_tiles,),
            in_specs=[rows_in, hbm, hbm, hbm],
            out_specs=rows_out,
            scratch_shapes=[pltpu.VMEM(a.shape[1:], BF16) for a in weights]
            + [pltpu.VMEM((2,) + a.shape[1:], a.dtype) for a in weights]
            + [pltpu.SMEM((1,), I32), pltpu.SemaphoreType.DMA((2,))]),
        out_shape=jax.ShapeDtypeStruct(xs.shape, xs.dtype),
        compiler_params=pltpu.CompilerParams(dimension_semantics=("arbitrary",), vmem_limit_bytes=VMEM_LIMIT),
        name="experts",
    )(tile_expert, tile_valid, n_valid, next_expert, xs, wg, wu, wd)


def _combine_kernel(dest_ref, x1_ref, w0_ref, w1_ref, ys_ref, out_ref, buf, sem, *, n_tokens):
    tm, d = x1_ref.shape
    i = pl.program_id(0)
    slot = i % 2

    def issue(step, into):
        def body(r, carry):
            for k in range(2):
                src_row = dest_ref[k * n_tokens + step * tm + r]
                _row_copy(ys_ref, src_row, buf.at[into, k], r, sem.at[into]).start(priority=k)
            return carry

        lax.fori_loop(0, tm, body, 0, unroll=ROW_DMA_UNROLL)

    @pl.when(i == 0)
    def _():
        issue(0, 0)

    @pl.when(i + 1 < pl.num_programs(0))
    def _():
        issue(i + 1, 1 - slot)

    for k in range(2):
        pltpu.make_async_copy(ys_ref.at[pl.ds(0, tm)], buf.at[slot, k], sem.at[slot]).wait()
    reps = d // LANES
    out_ref[...] = (x1_ref[...]
                    + _lane_tile(w0_ref[...], reps) * _unpack_rows(buf[slot, 0])
                    + _lane_tile(w1_ref[...], reps) * _unpack_rows(buf[slot, 1]))


def _combine(dest, x1, w0, w1, ys):
    t, d = x1.shape
    tm = TM_COMBINE
    row = lambda width: pl.BlockSpec((tm, width), lambda i, dref: (i, 0))
    return pl.pallas_call(
        functools.partial(_combine_kernel, n_tokens=t),
        grid_spec=pltpu.PrefetchScalarGridSpec(
            num_scalar_prefetch=1,
            grid=(t // tm,),
            in_specs=[row(d), row(LANES), row(LANES), pl.BlockSpec(memory_space=pl.ANY)],
            out_specs=row(d),
            scratch_shapes=[pltpu.VMEM((2, 2, tm, ys.shape[-1]), ys.dtype), pltpu.SemaphoreType.DMA((2,))]),
        out_shape=jax.ShapeDtypeStruct((t, d), x1.dtype),
        compiler_params=pltpu.CompilerParams(dimension_semantics=("arbitrary",), vmem_limit_bytes=VMEM_LIMIT),
        name="combine",
    )(dest, x1, w0, w1, ys)


def _lower_tri(n):
    return jnp.where(jnp.arange(n)[:, None] >= jnp.arange(n)[None, :], 1.0, 0.0).astype(BF16)


def _upper_tri(n):
    return jnp.where(jnp.arange(n)[:, None] <= jnp.arange(n)[None, :], 1.0, 0.0).astype(BF16)


def _split_bf16(w):
    hi = w.astype(BF16)
    lo = (w - hi.astype(F32)).astype(BF16)
    return hi, lo


def _layer(x, norm_mix_g, w_in, b_forget, q_norm_g, k_norm_g, ssm_A_re, ssm_A_im, ssm_log_dt, ssm_B_re, ssm_B_im,
           ssm_C_re, ssm_C_im, ssm_D, w_glu, w_proj_attn, w_proj_ssm, w_out, norm_ffn_g, w_router_group,
           b_router_group, w_router_expert, b_router_expert, w_expert_gate, w_expert_up, w_expert_down):
    b, s, d = x.shape
    t = b * s
    da = N_HEADS * HEAD_DIM
    dssm = ssm_D.shape[0]
    nk = s // SUPER
    assert s % SUPER == 0 and nk % SUBLANES == 0 and nk & (nk - 1) == 0 and s % TQ_ATTN == 0 and t % TM_PROJ == 0
    assert TM_PROJ == TS_ATTN
    x2 = x.reshape(t, d)

    o_f = 3 * da
    o_u = o_f + N_HEADS
    o_g = o_u + dssm
    w_main = jnp.concatenate([w_in[:, :o_f], w_in[:, o_u:o_g]], axis=1).astype(BF16)
    wf = jnp.repeat(w_in[:, o_f:o_u], FORGET_COPIES_STRIDE, axis=1).astype(BF16)
    bf = jnp.repeat(b_forget, FORGET_COPIES_STRIDE)[None, :]
    w_gates = w_in[:, o_g:].astype(BF16)
    qg = (jnp.tile(q_norm_g, N_HEADS) * (HEAD_DIM ** -0.5 * LOG2E))[None, :]
    kg = jnp.tile(k_norm_g, N_HEADS)[None, :]
    head_of = jnp.arange(da) // HEAD_DIM
    bd = jnp.where(head_of[:, None] == head_of[None, :], 1.0 / HEAD_DIM, 0.0).astype(BF16)
    logit_bound = 1.02 * HEAD_DIM ** 0.5 * jnp.max(jnp.abs(q_norm_g)) * jnp.max(jnp.abs(k_norm_g))
    plain_ok = LOG2E * logit_bound <= PLAIN_SOFTMAX_MAX_LOG2
    shift = jnp.full((1, LANES), logit_bound, F32)
    hw = N_HEADS * LANES
    lane_in_head = jnp.arange(hw) % LANES
    ones = jnp.stack([
        (lane_in_head >= BIAS_K_LANE) & (lane_in_head < BIAS_K_LANE + 3),
        (lane_in_head >= BIAS_Q_LANE) & (lane_in_head < BIAS_Q_LANE + 3),
        lane_in_head == HEAD_DIM]
        + [jnp.zeros((hw,), bool)] * (SUBLANES - 3)).astype(F32)
    prow = jnp.arange(LANES)
    head, copy = prow // FORGET_COPIES_STRIDE, prow % FORGET_COPIES_STRIDE
    target = jnp.where(copy < 3, head * LANES + BIAS_Q_LANE + copy, hw + head * LANES + BIAS_K_LANE + copy - 3)
    place = ((jnp.arange(2 * hw)[None, :] == target[:, None]) & (copy < 6)[:, None]).astype(BF16)

    q, k, v, u, fedge = _proj(x2, norm_mix_g[None, :], w_main, wf, bf, qg, kg, bd, _lower_tri(TM_PROJ), place, ones,
                              shift, seq=s)
    y_attn = _attention(q.reshape(b, s, hw), k.reshape(b, s, hw), v.reshape(b, s, hw), fedge, plain_ok)

    y_ssm = _s5(u.reshape(b, s, dssm),
                *_s5_params(ssm_A_re, ssm_A_im, ssm_log_dt, ssm_B_re, ssm_B_im, ssm_C_re, ssm_C_im, s // SUPER))

    wr = jnp.concatenate([w_router_expert.T, w_router_group.T,
                          jnp.zeros((ROUTER_ROWS - N_EXPERTS - N_EXPERT_GROUPS, d), F32)], axis=0)
    br = jnp.concatenate([b_router_expert, b_router_group,
                          jnp.full((SUBLANES - N_EXPERT_GROUPS,), NEG_INF, F32),
                          jnp.zeros((ROUTER_ROWS - N_EXPERTS - SUBLANES,), F32)])[:, None]
    n_tiles = (2 * t) // TM_EXPERT + N_EXPERTS
    x1, xp, logits, xs0 = _mix(x2, y_attn.reshape(t, da), y_ssm.reshape(t, dssm), u, norm_mix_g[None, :], w_gates,
                               ssm_D[None, :], w_glu.astype(BF16), w_proj_attn.astype(BF16),
                               w_proj_ssm.astype(BF16), w_out.astype(BF16), norm_ffn_g[None, :], *_split_bf16(wr),
                               br, n_slots=n_tiles * TM_EXPERT)
    ids, wts = _route_call(logits)

    dest, tile_expert, tile_valid = _meta(ids[:2], _upper_tri(SORT_CHUNK), n_tiles)
    xs = _dispatch(dest, xp, xs0)
    ys = _experts(tile_expert, tile_valid, xs, w_expert_gate, w_expert_up, w_expert_down)
    w0 = jnp.broadcast_to(wts[0][:, None], (t, LANES))
    w1 = jnp.broadcast_to(wts[1][:, None], (t, LANES))
    out = _combine(dest, x1, w0, w1, ys)
    return out.reshape(b, s, d)


def kernel(x, norm_mix_g, w_in, b_forget, q_norm_g, k_norm_g, ssm_A_re, ssm_A_im, ssm_log_dt, ssm_B_re, ssm_B_im,
           ssm_C_re, ssm_C_im, ssm_D, w_glu, w_proj_attn, w_proj_ssm, w_out, norm_ffn_g, w_router_group,
           b_router_group, w_router_expert, b_router_expert, w_expert_gate, w_expert_up, w_expert_down):
    layer_params = (norm_mix_g, w_in, b_forget, q_norm_g, k_norm_g, ssm_A_re, ssm_A_im, ssm_log_dt, ssm_B_re,
                    ssm_B_im, ssm_C_re, ssm_C_im, ssm_D, w_glu, w_proj_attn, w_proj_ssm, w_out, norm_ffn_g,
                    w_router_group, b_router_group, w_router_expert, b_router_expert, w_expert_gate, w_expert_up,
                    w_expert_down)
    for layer in range(norm_mix_g.shape[0]):
        x = _layer(x, *[p[layer] for p in layer_params])
    return x
```
